```python
import math
import jax
import jax.numpy as jnp
from jax import lax
import numpy as np

D_MODEL = 2048
BATCH = 16
SEQ = 256
DEPTH = 2
DEC_BATCH = 4
DEC_SEQ = 1024
PAST_LEN = 256

GRID_W = 64
HEAD_DIM = 64
GROUP_W = D_MODEL // 4
SWA_HEADS = GROUP_W // HEAD_DIM
SWA_KV_HEADS = 2
SWA_GROUP = SWA_HEADS // SWA_KV_HEADS
SWA_WINDOW = 128
SWA_BLOCK = 128
NA_HEADS = GROUP_W // HEAD_DIM
NA_KH = 8
NA_KW = 16
DIFF_HEADS = GROUP_W // (2 * HEAD_DIM)
DIFF_VDIM = 2 * HEAD_DIM
LRU_WIDTH = GROUP_W
LRU_BLOCKS = 8
LRU_BLOCK_W = LRU_WIDTH // LRU_BLOCKS
LRU_C = 8.0
CONV_W = 4
FFN_HIDDEN = (-(-8 * D_MODEL // 3) + 255) // 256 * 256
Q_BLOCK = 128
ROPE_BASE = 10000.0
NORM_EPS = 1e-6
NEG_INF = -1e30
MOD_CHUNKS = 6
PROJ_SIZES = (SWA_HEADS * HEAD_DIM, SWA_KV_HEADS * HEAD_DIM, SWA_KV_HEADS * HEAD_DIM,
              NA_HEADS * HEAD_DIM, NA_HEADS * HEAD_DIM, NA_HEADS * HEAD_DIM,
              DIFF_HEADS * 2 * HEAD_DIM, DIFF_HEADS * 2 * HEAD_DIM, DIFF_HEADS * DIFF_VDIM,
              LRU_WIDTH, LRU_WIDTH)
PROJ_W = sum(PROJ_SIZES)
MIX_W = SWA_HEADS * HEAD_DIM + NA_HEADS * HEAD_DIM + DIFF_HEADS * DIFF_VDIM + LRU_WIDTH

kernel_name = 'hybrid_parallel_heads_flow_step'


def _rms(x, g):
    xf = x.astype(jnp.float32)
    y = xf * lax.rsqrt(jnp.mean(xf * xf, axis=-1, keepdims=True) + NORM_EPS)
    return (y * g.astype(jnp.float32)).astype(x.dtype)


def _modulation(cvec, w_mod, b_mod):
    m = jax.nn.silu(cvec) @ w_mod + b_mod
    return jnp.split(m, MOD_CHUNKS, axis=-1)


def _modnorm(x, g, shift, scale):
    return _rms(x, g) * (1 + scale) + shift


def _swiglu(h, wg, wu, wd):
    return (jax.nn.silu(h @ wg) * (h @ wu)) @ wd


def _axial_angles(n_tok):
    t = jnp.arange(n_tok)
    pos = jnp.stack([t // GRID_W, t % GRID_W], axis=-1).astype(jnp.float32)
    nf = HEAD_DIM // 4
    inv = ROPE_BASE ** (-jnp.arange(nf, dtype=jnp.float32) / nf)
    return pos[:, :, None] * inv


def _rope(x, ang):
    nf = HEAD_DIM // 4
    shp = x.shape
    bshape = (1, shp[1]) + (1,) * (x.ndim - 3) + (2, nf)
    cos = jnp.cos(ang).reshape(bshape)
    sin = jnp.sin(ang).reshape(bshape)
    xf = x.astype(jnp.float32).reshape(shp[:-1] + (2, 2, nf))
    x1, x2 = xf[..., 0, :], xf[..., 1, :]
    out = jnp.stack([x1 * cos - x2 * sin, x2 * cos + x1 * sin], axis=-2)
    return out.reshape(shp).astype(x.dtype)


def _project(h, w_in, qk_gain):
    p = h @ w_in
    b, t, _ = p.shape
    parts = []
    off = 0
    for size in PROJ_SIZES:
        parts.append(p[..., off:off + size])
        off += size
    sq, sk, sv, nq, nk, nv, dq, dk, dv, lx, lg = parts
    sq = _rms(sq.reshape(b, t, SWA_KV_HEADS, SWA_GROUP, HEAD_DIM), qk_gain[0, 0])
    sk = _rms(sk.reshape(b, t, SWA_KV_HEADS, HEAD_DIM), qk_gain[0, 1])
    sv = sv.reshape(b, t, SWA_KV_HEADS, HEAD_DIM)
    nq = _rms(nq.reshape(b, t, NA_HEADS, HEAD_DIM), qk_gain[1, 0])
    nk = _rms(nk.reshape(b, t, NA_HEADS, HEAD_DIM), qk_gain[1, 1])
    nv = nv.reshape(b, t, NA_HEADS, HEAD_DIM)
    dq = _rms(dq.reshape(b, t, DIFF_HEADS, 2, HEAD_DIM), qk_gain[2, 0])
    dk = _rms(dk.reshape(b, t, DIFF_HEADS, 2, HEAD_DIM), qk_gain[2, 1])
    dv = dv.reshape(b, t, DIFF_HEADS, DIFF_VDIM)
    return sq, sk, sv, nq, nk, nv, dq, dk, dv, lx, lg


def _dense_attn(q, k, v, sink):
    b, t, hk, g, d = q.shape
    nb = t // Q_BLOCK
    scale = d ** -0.5
    qb = jnp.moveaxis(q.reshape(b, nb, Q_BLOCK, hk, g, d), 1, 0)

    def one(qblk):
        s = jnp.einsum('bqhgd,bshd->bhgqs', qblk, k).astype(jnp.float32) * scale
        if sink is None:
            p = jax.nn.softmax(s, axis=-1)
        else:
            s_sink = jnp.broadcast_to(sink.astype(jnp.float32)[None, :, :, None, None], s.shape[:-1] + (1,))
            p = jax.nn.softmax(jnp.concatenate([s, s_sink], axis=-1), axis=-1)[..., :-1]
        return jnp.einsum('bhgqs,bshe->bqhge', p.astype(v.dtype), v)

    o = lax.map(one, qb)
    return jnp.moveaxis(o, 0, 1).reshape(b, t, hk, g, -1)


def _diff_attn(q, k, v, lam):
    b, t, h, _, d = q.shape
    nb = t // Q_BLOCK
    scale = d ** -0.5
    qb = jnp.moveaxis(q.reshape(b, nb, Q_BLOCK, h, 2, d), 1, 0)

    def one(qblk):
        s = jnp.einsum('bqhid,bshid->bihqs', qblk, k).astype(jnp.float32) * scale
        p = jax.nn.softmax(s, axis=-1)
        pd = p[:, 0] - lam * p[:, 1]
        return jnp.einsum('bhqs,bshe->bqhe', pd.astype(v.dtype), v)

    o = lax.map(one, qb)
    return jnp.moveaxis(o, 0, 1).reshape(b, t, h, -1)


def _lambda_init(layer):
    return 0.8 - 0.6 * math.exp(-0.3 * layer)


def _diff_lambda(lp, lam_init):
    lp = lp.astype(jnp.float32)
    return jnp.exp(jnp.sum(lp[0] * lp[1])) - jnp.exp(jnp.sum(lp[2] * lp[3])) + lam_init


def _diff_out(o, subln, lam_init):
    b, t, h, e = o.shape
    return (_rms(o, subln) * (1.0 - lam_init)).reshape(b, t, h * e)


def _swa_latent(q, k, v, k_ctx, v_ctx, sink):
    b, t, hk, g, d = q.shape
    nb = t // SWA_BLOCK
    span = 3 * SWA_BLOCK
    scale = d ** -0.5
    pad = ((0, 0), (SWA_BLOCK, SWA_BLOCK), (0, 0), (0, 0))
    idx = np.arange(nb)[:, None] * SWA_BLOCK + np.arange(span)[None, :]
    kb = jnp.pad(k, pad)[:, idx]
    vb = jnp.pad(v, pad)[:, idx]
    qb = q.reshape(b, nb, SWA_BLOCK, hk, g, d)
    s_loc = jnp.einsum('bnqhgd,bnkhd->bnhgqk', qb, kb).astype(jnp.float32) * scale
    qpos = np.arange(t).reshape(nb, SWA_BLOCK)
    kpos = idx - SWA_BLOCK
    valid = ((kpos[:, None, :] >= 0) & (kpos[:, None, :] < t)
             & (np.abs(qpos[:, :, None] - kpos[:, None, :]) <= SWA_WINDOW))
    s_loc = jnp.where(valid[None, :, None, None], s_loc, NEG_INF)
    s_ctx = jnp.einsum('bnqhgd,bshd->bnhgqs', qb, k_ctx).astype(jnp.float32) * scale
    s_sink = jnp.broadcast_to(sink.astype(jnp.float32)[None, None, :, :, None, None], s_loc.shape[:-1] + (1,))
    p = jax.nn.softmax(jnp.concatenate([s_loc, s_ctx, s_sink], axis=-1), axis=-1).astype(v.dtype)
    n_ctx = k_ctx.shape[1]
    o = (jnp.einsum('bnhgqk,bnkhe->bnqhge', p[..., :span], vb)
         + jnp.einsum('bnhgqs,bshe->bnqhge', p[..., span:span + n_ctx], v_ctx))
    return o.reshape(b, t, hk, g, -1)


def _na_latent(q, k, v, k_ctx, v_ctx, rpb):
    b, t, h, d = q.shape
    rows = t // GRID_W
    kh = min(NA_KH, rows)
    n_loc = kh * GRID_W
    scale = d ** -0.5
    r = np.arange(rows)
    row_idx = np.clip(r - kh // 2, 0, rows - kh)[:, None] + np.arange(kh)[None, :]
    cq = np.arange(GRID_W)
    col_start = np.clip(cq - NA_KW // 2, 0, GRID_W - NA_KW)
    kg = k.reshape(b, rows, GRID_W, h, d)[:, row_idx].reshape(b, rows, n_loc, h, d)
    vg = v.reshape(b, rows, GRID_W, h, d)[:, row_idx].reshape(b, rows, n_loc, h, d)
    qg = q.reshape(b, rows, GRID_W, h, d)
    s_loc = jnp.einsum('brqhd,brkhd->brhqk', qg, kg).astype(jnp.float32) * scale
    dr = row_idx - r[:, None] + (NA_KH - 1)
    dc = np.clip(cq[None, :] - cq[:, None] + (NA_KW - 1), 0, 2 * NA_KW - 2)
    bias = rpb[:, dr[:, None, :, None], dc[None, :, None, :]]
    bias = jnp.moveaxis(bias.reshape(h, rows, GRID_W, n_loc), 0, 1).astype(jnp.float32)
    col_ok = (cq[None, :] >= col_start[:, None]) & (cq[None, :] < col_start[:, None] + NA_KW)
    col_ok = np.tile(col_ok, (1, kh))
    s_loc = jnp.where(col_ok, s_loc + bias, NEG_INF)
    s_ctx = jnp.einsum('brqhd,bshd->brhqs', qg, k_ctx).astype(jnp.float32) * scale
    p = jax.nn.softmax(jnp.concatenate([s_loc, s_ctx], axis=-1), axis=-1).astype(v.dtype)
    o = (jnp.einsum('brhqk,brkhe->brqhe', p[..., :n_loc], vg)
         + jnp.einsum('brhqs,bshe->brqhe', p[..., n_loc:], v_ctx))
    return o.reshape(b, t, h, -1)


def _dwconv_centred(x, w, bias):
    n = x.shape[1]
    left = CONV_W // 2
    xp = jnp.pad(x, ((0, 0), (left, CONV_W - 1 - left), (0, 0)))
    y = bias
    for tap in range(CONV_W):
        y = y + w[tap] * xp[:, tap:tap + n]
    return y


def _lin_combine(left, right):
    a1, b1 = left
    a2, b2 = right
    return a1 * a2, a2 * b1 + b2


def _lru_scan(u, w_a, b_a, w_x, b_x, lam_l, h0, reverse):
    b, t, ch = u.shape
    uf = u.astype(jnp.float32)
    ub = uf.reshape(b, t, LRU_BLOCKS, LRU_BLOCK_W)
    r = jax.nn.sigmoid(jnp.einsum('btnc,ncd->btnd', ub, w_a.astype(jnp.float32)).reshape(b, t, ch) + b_a)
    i = jax.nn.sigmoid(jnp.einsum('btnc,ncd->btnd', ub, w_x.astype(jnp.float32)).reshape(b, t, ch) + b_x)
    log_a = -LRU_C * r * jax.nn.softplus(-lam_l.astype(jnp.float32))
    a = jnp.exp(log_a)
    bx = jnp.sqrt(-jnp.expm1(2.0 * log_a)) * (i * uf)
    edge = t - 1 if reverse else 0
    bx = bx.at[:, edge].add(a[:, edge] * h0.astype(jnp.float32))
    _, hs = lax.associative_scan(_lin_combine, (a, bx), axis=1, reverse=reverse)
    final = hs[:, 0] if reverse else hs[:, -1]
    return hs, final


def _lru_mixer(xb, gb, conv_w, conv_b, wa, ba, wx, bx, lam_l, h0):
    u = _dwconv_centred(xb, conv_w, conv_b)
    hf, sf = _lru_scan(u, wa[0], ba[0], wx[0], bx[0], lam_l[0], h0[:, 0], False)
    hb, sb = _lru_scan(u, wa[1], ba[1], wx[1], bx[1], lam_l[1], h0[:, 1], True)
    y = ((hf + hb) * jax.nn.gelu(gb.astype(jnp.float32))).astype(xb.dtype)
    return y, jnp.stack([sf, sb], axis=1).astype(xb.dtype)


def setup_inputs(seed: int = 0) -> dict:
    key = jax.random.key(seed)
    ks = iter(jax.random.split(key, 40))
    f32 = jnp.float32
    D = D_MODEL

    def nrm(shape, scale):
        return jax.random.normal(next(ks), shape, f32) * scale

    x_prompt = nrm((BATCH, SEQ, D), 1.0)
    x_sample = nrm((DEC_BATCH, DEC_SEQ, D), 1.0)
    cache_swa_k = nrm((DEC_BATCH, DEPTH, PAST_LEN, SWA_KV_HEADS, HEAD_DIM), 1.0)
    cache_swa_v = nrm((DEC_BATCH, DEPTH, PAST_LEN, SWA_KV_HEADS, HEAD_DIM), 1.0)
    cache_na_k = nrm((DEC_BATCH, DEPTH, PAST_LEN, NA_HEADS, HEAD_DIM), 1.0)
    cache_na_v = nrm((DEC_BATCH, DEPTH, PAST_LEN, NA_HEADS, HEAD_DIM), 1.0)
    cache_diff_k = nrm((DEC_BATCH, DEPTH, PAST_LEN, DIFF_HEADS, 2, HEAD_DIM), 1.0)
    cache_diff_v = nrm((DEC_BATCH, DEPTH, PAST_LEN, DIFF_HEADS, DIFF_VDIM), 1.0)
    state_lru = nrm((DEC_BATCH, DEPTH, 2, LRU_WIDTH), 0.5)
    c = nrm((DEC_BATCH, D), 1.0)
    c_ctx = nrm((D,), 1.0)
    norm_mix = 1.0 + nrm((DEPTH, D), 0.02)
    norm_ffn = 1.0 + nrm((DEPTH, D), 0.02)
    w_mod = nrm((DEPTH, D, MOD_CHUNKS * D), 0.5 * D ** -0.5)
    b_mod = nrm((DEPTH, MOD_CHUNKS * D), 0.02)
    w_in = nrm((DEPTH, D, PROJ_W), D ** -0.5)
    w_out = nrm((DEPTH, MIX_W, D), MIX_W ** -0.5)
    qk_gain = 1.0 + nrm((DEPTH, 3, 2, HEAD_DIM), 0.02)
    swa_sink = nrm((DEPTH, SWA_HEADS), 0.5)
    na_rpb = nrm((DEPTH, NA_HEADS, 2 * NA_KH - 1, 2 * NA_KW - 1), 0.1)
    diff_lambda = nrm((DEPTH, 4, HEAD_DIM), 0.1)
    diff_subln = 1.0 + nrm((DEPTH, DIFF_VDIM), 0.02)
    conv_w = nrm((DEPTH, CONV_W, LRU_WIDTH), CONV_W ** -0.5)
    conv_b = nrm((DEPTH, LRU_WIDTH), 0.02)
    lru_wa = nrm((DEPTH, 2, LRU_BLOCKS, LRU_BLOCK_W, LRU_BLOCK_W), LRU_BLOCK_W ** -0.5)
    lru_ba = nrm((DEPTH, 2, LRU_WIDTH), 0.02)
    lru_wx = nrm((DEPTH, 2, LRU_BLOCKS, LRU_BLOCK_W, LRU_BLOCK_W), LRU_BLOCK_W ** -0.5)
    lru_bx = nrm((DEPTH, 2, LRU_WIDTH), 0.02)
    a_c = jax.random.uniform(next(ks), (DEPTH, 2, LRU_WIDTH), f32, 0.9, 0.999)
    a_base = a_c ** (1.0 / LRU_C)
    lru_L = jnp.log(a_base) - jnp.log1p(-a_base)
    w_ffn_gate = nrm((DEPTH, D, FFN_HIDDEN), D ** -0.5)
    w_ffn_up = nrm((DEPTH, D, FFN_HIDDEN), D ** -0.5)
    w_ffn_down = nrm((DEPTH, FFN_HIDDEN, D), FFN_HIDDEN ** -0.5)
    return {'x_prompt': x_prompt, 'x_sample': x_sample,
            'cache_swa_k': cache_swa_k, 'cache_swa_v': cache_swa_v,
            'cache_na_k': cache_na_k, 'cache_na_v': cache_na_v,
            'cache_diff_k': cache_diff_k, 'cache_diff_v': cache_diff_v,
            'state_lru': state_lru, 'c': c, 'c_ctx': c_ctx,
            'norm_mix': norm_mix, 'norm_ffn': norm_ffn, 'w_mod': w_mod, 'b_mod': b_mod,
            'w_in': w_in, 'w_out': w_out, 'qk_gain': qk_gain, 'swa_sink': swa_sink,
            'na_rpb': na_rpb, 'diff_lambda': diff_lambda, 'diff_subln': diff_subln,
            'conv_w': conv_w, 'conv_b': conv_b, 'lru_wa': lru_wa, 'lru_ba': lru_ba,
            'lru_wx': lru_wx, 'lru_bx': lru_bx, 'lru_L': lru_L,
            'w_ffn_gate': w_ffn_gate, 'w_ffn_up': w_ffn_up, 'w_ffn_down': w_ffn_down}


def reference(x_prompt, x_sample, cache_swa_k, cache_swa_v, cache_na_k, cache_na_v,
              cache_diff_k, cache_diff_v, state_lru, c, c_ctx,
              norm_mix, norm_ffn, w_mod, b_mod, w_in, w_out, qk_gain, swa_sink,
              na_rpb, diff_lambda, diff_subln, conv_w, conv_b, lru_wa, lru_ba,
              lru_wx, lru_bx, lru_L, w_ffn_gate, w_ffn_up, w_ffn_down):
    xc = x_prompt
    bc, s_ctx, _ = xc.shape
    swa_k_l, swa_v_l, na_k_l, na_v_l, diff_k_l, diff_v_l, lru_s_l = [], [], [], [], [], [], []
    for l in range(DEPTH):
        lam_init = _lambda_init(l)
        sh1, sc1, g1, sh2, sc2, g2 = _modulation(c_ctx, w_mod[l], b_mod[l])
        h = _modnorm(xc, norm_mix[l], sh1, sc1)
        sq, sk, sv, nq, nk, nv, dq, dk, dv, lx, lg = _project(h, w_in[l], qk_gain[l])
        oa = _dense_attn(sq, sk, sv, swa_sink[l].reshape(SWA_KV_HEADS, SWA_GROUP))
        ob = _dense_attn(nq[:, :, :, None], nk, nv, None)
        lam = _diff_lambda(diff_lambda[l], lam_init)
        oc = _diff_out(_diff_attn(dq, dk, dv, lam), diff_subln[l], lam_init)
        h0 = jnp.zeros((bc, 2, LRU_WIDTH), xc.dtype)
        od, st = _lru_mixer(lx, lg, conv_w[l], conv_b[l], lru_wa[l], lru_ba[l],
                            lru_wx[l], lru_bx[l], lru_L[l], h0)
        mix = jnp.concatenate([oa.reshape(bc, s_ctx, -1), ob.reshape(bc, s_ctx, -1), oc, od], axis=-1)
        xc = xc + g1 * (mix @ w_out[l])
        xc = xc + g2 * _swiglu(_modnorm(xc, norm_ffn[l], sh2, sc2),
                               w_ffn_gate[l], w_ffn_up[l], w_ffn_down[l])
        swa_k_l.append(sk)
        swa_v_l.append(sv)
        na_k_l.append(nk)
        na_v_l.append(nv)
        diff_k_l.append(dk)
        diff_v_l.append(dv)
        lru_s_l.append(st)
    y_prompt = xc
    new_swa_k = jnp.stack(swa_k_l, axis=1)
    new_swa_v = jnp.stack(swa_v_l, axis=1)
    new_na_k = jnp.stack(na_k_l, axis=1)
    new_na_v = jnp.stack(na_v_l, axis=1)
    new_diff_k = jnp.stack(diff_k_l, axis=1)
    new_diff_v = jnp.stack(diff_v_l, axis=1)
    new_state_lru = jnp.stack(lru_s_l, axis=1)

    xs = x_sample
    bd, t_lat, _ = xs.shape
    ang = _axial_angles(t_lat)
    for l in range(DEPTH):
        lam_init = _lambda_init(l)
        sh1, sc1, g1, sh2, sc2, g2 = [m[:, None, :] for m in _modulation(c, w_mod[l], b_mod[l])]
        h = _modnorm(xs, norm_mix[l], sh1, sc1)
        sq, sk, sv, nq, nk, nv, dq, dk, dv, lx, lg = _project(h, w_in[l], qk_gain[l])
        oa = _swa_latent(_rope(sq, ang), _rope(sk, ang), sv, cache_swa_k[:, l], cache_swa_v[:, l],
                         swa_sink[l].reshape(SWA_KV_HEADS, SWA_GROUP))
        ob = _na_latent(nq, nk, nv, cache_na_k[:, l], cache_na_v[:, l], na_rpb[l])
        k_all = jnp.concatenate([_rope(dk, ang), cache_diff_k[:, l]], axis=1)
        v_all = jnp.concatenate([dv, cache_diff_v[:, l]], axis=1)
        lam = _diff_lambda(diff_lambda[l], lam_init)
        oc = _diff_out(_diff_attn(_rope(dq, ang), k_all, v_all, lam), diff_subln[l], lam_init)
        od, _ = _lru_mixer(lx, lg, conv_w[l], conv_b[l], lru_wa[l], lru_ba[l],
                           lru_wx[l], lru_bx[l], lru_L[l], state_lru[:, l])
        mix = jnp.concatenate([oa.reshape(bd, t_lat, -1), ob.reshape(bd, t_lat, -1), oc, od], axis=-1)
        xs = xs + g1 * (mix @ w_out[l])
        xs = xs + g2 * _swiglu(_modnorm(xs, norm_ffn[l], sh2, sc2),
                               w_ffn_gate[l], w_ffn_up[l], w_ffn_down[l])
    y_sample = xs
    return (y_prompt, y_sample, new_swa_k, new_swa_v, new_na_k, new_na_v,
            new_diff_k, new_diff_v, new_state_lru)
```

```python
import functools
import math

import jax
import jax.numpy as jnp
import numpy as np
from jax import lax
from jax.experimental import pallas as pl
from jax.experimental.pallas import tpu as pltpu

F32 = jnp.float32
BF16 = jnp.bfloat16

D_MODEL = 2048
DEPTH = 2
GRID_W = 64
HEAD_DIM = 64
GROUP_W = 512
SWA_HEADS = 8
SWA_KV_HEADS = 2
SWA_GROUP = 4
SWA_WINDOW = 128
NA_HEADS = 8
NA_KH = 8
NA_KW = 16
DIFF_HEADS = 4
DIFF_VDIM = 128
LRU_WIDTH = 512
LRU_BLOCKS = 8
LRU_BLOCK_W = 64
LRU_C = 8.0
CONV_W = 4
FFN_HIDDEN = 5632
ROPE_BASE = 10000.0
NORM_EPS = 1e-6
NEG_INF = -1e30
MOD_CHUNKS = 6
PROJ_W = 4864
QK_SCALE = HEAD_DIM ** -0.5

A_W, B_W, C_W, DD_W = 768, 1536, 1536, 1024
PROJ_TN = 256
A_TILES, B_TILES, C_TILES, DD_TILES = A_W // PROJ_TN, B_W // PROJ_TN, C_W // PROJ_TN, DD_W // PROJ_TN
N_PROJ_TILES = PROJ_W // PROJ_TN

V7X_VMEM_LIMIT = 56 * 1024 * 1024

NA_QT = 128
NA_WIN_ROWS = 10
NA_WIN = NA_WIN_ROWS * GRID_W


def _sigmoid(x):
    return 1.0 / (1.0 + jnp.exp(-x))


def _dot(a, b):
    return jnp.dot(a, b, preferred_element_type=F32)


def _dot_nt(a, b):
    return lax.dot_general(a, b, (((1,), (1,)), ((), ())), preferred_element_type=F32)


def _mod_kernel(c_ref, w_ref, b_ref, o_ref):
    cv = c_ref[...]
    s = cv * _sigmoid(cv)
    o_ref[0] = _dot(s.astype(BF16), w_ref[0].astype(BF16)) + b_ref[0]


def _modulation(cvecs, w_mod, b_mod):
    tn = 1024
    n = MOD_CHUNKS * D_MODEL
    return pl.pallas_call(
        _mod_kernel,
        out_shape=jax.ShapeDtypeStruct((DEPTH, 8, n), F32),
        grid=(DEPTH, n // tn),
        in_specs=[
            pl.BlockSpec((8, D_MODEL), lambda l, j: (0, 0)),
            pl.BlockSpec((1, D_MODEL, tn), lambda l, j: (l, 0, j)),
            pl.BlockSpec((1, 1, tn), lambda l, j: (l, 0, j)),
        ],
        out_specs=pl.BlockSpec((1, 8, tn), lambda l, j: (l, 0, j)),
        compiler_params=pltpu.CompilerParams(
            dimension_semantics=("arbitrary", "arbitrary"), vmem_limit_bytes=V7X_VMEM_LIMIT),
        name="modulation",
    )(cvecs, w_mod, b_mod.reshape(DEPTH, 1, n))


def _modnorm_rows(x_ref, nw_ref, sh_ref, sc_ref, h_scr, rows, chunk=128):
    def body(c, carry):
        r = pl.multiple_of(c * chunk, chunk)
        x = x_ref[pl.ds(r, chunk), :]
        ms = jnp.mean(x * x, axis=-1, keepdims=True)
        y = x * lax.rsqrt(ms + NORM_EPS) * nw_ref[...]
        h_scr[pl.ds(r, chunk), :] = (y * (1.0 + sc_ref[0]) + sh_ref[0]).astype(BF16)
        return carry
    lax.fori_loop(0, rows // chunk, body, 0)


def _inproj_kernel(*refs, tm, rope):
    if rope:
        (x_ref, sh_ref, sc_ref, nw_ref, w_ref, gain_ref, nflag_ref, hsum_ref,
         cos_ref, sin_ref, rflag_ref, a_ref, b_ref, c_ref, d_ref, h_scr, v_scr) = refs
    else:
        (x_ref, sh_ref, sc_ref, nw_ref, w_ref, gain_ref, nflag_ref, hsum_ref,
         a_ref, b_ref, c_ref, d_ref, h_scr, v_scr) = refs
    j = pl.program_id(1)

    @pl.when(j == 0)
    def _():
        _modnorm_rows(x_ref, nw_ref, sh_ref, sc_ref, h_scr, tm)

    v_scr[...] = _dot(h_scr[...], w_ref[...])

    is_norm = (j <= 6) | ((j >= 9) & (j <= 12))

    @pl.when(is_norm)
    def _():
        p = v_scr[...]
        ms = _dot((p * p).astype(BF16), hsum_ref[...])
        y = p * lax.rsqrt(ms + NORM_EPS) * gain_ref[...]
        y = jnp.where(nflag_ref[...] > 0.5, y, p)
        if rope:
            lane = lax.broadcasted_iota(jnp.int32, y.shape, 1)
            up = pltpu.roll(y, PROJ_TN - 16, 1)
            down = pltpu.roll(y, 16, 1)
            partner = jnp.where((lane & 31) < 16, up, down)
            yr = y * cos_ref[...] + partner * sin_ref[...]
            y = jnp.where(rflag_ref[...] > 0.5, yr, y)
        v_scr[...] = y

    @pl.when(j < A_TILES)
    def _():
        a_ref[...] = v_scr[...]

    @pl.when((j >= A_TILES) & (j < A_TILES + B_TILES))
    def _():
        b_ref[...] = v_scr[...]

    @pl.when((j >= A_TILES + B_TILES) & (j < A_TILES + B_TILES + C_TILES))
    def _():
        c_ref[...] = v_scr[...]

    @pl.when(j >= A_TILES + B_TILES + C_TILES)
    def _():
        d_ref[...] = v_scr[...]


def _in_projection(x, mods_l, mod_row, norm_w, w_bf, gain, nflag, hsum, rope_tabs, *, tm):
    n_tok = x.shape[0]
    rope = rope_tabs is not None
    off_b, off_c, off_d = A_TILES, A_TILES + B_TILES, A_TILES + B_TILES + C_TILES

    def clampj(j, off, nt):
        return jnp.clip(j - off, 0, nt - 1)

    in_specs = [
        pl.BlockSpec((tm, D_MODEL), lambda i, j: (i, 0)),
        pl.BlockSpec((1, 1, D_MODEL), lambda i, j: (mod_row(i) * MOD_CHUNKS + 0, 0, 0)),
        pl.BlockSpec((1, 1, D_MODEL), lambda i, j: (mod_row(i) * MOD_CHUNKS + 1, 0, 0)),
        pl.BlockSpec((1, D_MODEL), lambda i, j: (0, 0)),
        pl.BlockSpec((D_MODEL, PROJ_TN), lambda i, j: (0, j)),
        pl.BlockSpec((1, PROJ_TN), lambda i, j: (0, j)),
        pl.BlockSpec((1, PROJ_TN), lambda i, j: (0, j)),
        pl.BlockSpec((PROJ_TN, PROJ_TN), lambda i, j: (0, 0)),
    ]
    args = [x, mods_l, mods_l, norm_w.reshape(1, D_MODEL), w_bf, gain, nflag, hsum]
    if rope:
        cos_t, sin_t, rflag = rope_tabs
        in_specs += [
            pl.BlockSpec((tm, PROJ_TN), lambda i, j: (0, 0)),
            pl.BlockSpec((tm, PROJ_TN), lambda i, j: (0, 0)),
            pl.BlockSpec((1, PROJ_TN), lambda i, j: (0, j)),
        ]
        args += [cos_t, sin_t, rflag]
    out_specs = [
        pl.BlockSpec((tm, PROJ_TN), lambda i, j: (i, clampj(j, 0, A_TILES))),
        pl.BlockSpec((tm, PROJ_TN), lambda i, j: (i, clampj(j, off_b, B_TILES))),
        pl.BlockSpec((tm, PROJ_TN), lambda i, j: (i, clampj(j, off_c, C_TILES))),
        pl.BlockSpec((tm, PROJ_TN), lambda i, j: (i, clampj(j, off_d, DD_TILES))),
    ]
    out_shape = [jax.ShapeDtypeStruct((n_tok, w), F32) for w in (A_W, B_W, C_W, DD_W)]
    return pl.pallas_call(
        functools.partial(_inproj_kernel, tm=tm, rope=rope),
        out_shape=out_shape,
        grid=(n_tok // tm, N_PROJ_TILES),
        in_specs=in_specs,
        out_specs=out_specs,
        scratch_shapes=[pltpu.VMEM((tm, D_MODEL), BF16), pltpu.VMEM((tm, PROJ_TN), F32)],
        compiler_params=pltpu.CompilerParams(
            dimension_semantics=("arbitrary", "arbitrary"), vmem_limit_bytes=V7X_VMEM_LIMIT),
        name="in_projection_rope" if rope else "in_projection",
    )(*args)


def _softmax_pv(segs, extra=None):
    m = functools.reduce(jnp.maximum, [jnp.max(s, axis=-1, keepdims=True) for s, _ in segs])
    if extra is not None:
        m = jnp.maximum(m, extra)
    den = None
    o = None
    for s, v in segs:
        e = jnp.exp(s - m)
        de = jnp.sum(e, axis=-1, keepdims=True)
        oe = _dot(e.astype(BF16), v)
        den = de if den is None else den + de
        o = oe if o is None else o + oe
    if extra is not None:
        den = den + jnp.exp(extra - m)
    return o / den


def _probs(scores):
    m = functools.reduce(jnp.maximum, [jnp.max(s, axis=-1, keepdims=True) for s in scores])
    es = [jnp.exp(s - m) for s in scores]
    den = functools.reduce(lambda a, b: a + b, [jnp.sum(e, axis=-1, keepdims=True) for e in es])
    inv = 1.0 / den
    return [e * inv for e in es]


def _diff_lambda_val(dl_ref, lam_init):
    lp = dl_ref[...]
    s1 = jnp.sum(lp[0:1] * lp[1:2], axis=-1, keepdims=True)
    s2 = jnp.sum(lp[2:3] * lp[3:4], axis=-1, keepdims=True)
    return jnp.exp(s1) - jnp.exp(s2) + lam_init


def _diff_finish(o, subln_ref, lam_init):
    ms = jnp.mean(o * o, axis=-1, keepdims=True)
    return o * lax.rsqrt(ms + NORM_EPS) * subln_ref[...] * (1.0 - lam_init)


def _qcast(x):
    return (x * QK_SCALE).astype(BF16)


def _ctx_attn_kernel(sink_ref, a_ref, b_ref, c_ref, dl_ref, subln_ref, o_ref, *, t, lam_init):
    outs = []
    for hk in range(SWA_KV_HEADS):
        k = a_ref[0, :, 512 + hk * 64: 512 + (hk + 1) * 64].astype(BF16)
        v = a_ref[0, :, 640 + hk * 64: 640 + (hk + 1) * 64].astype(BF16)
        qs = jnp.concatenate(
            [_qcast(a_ref[0, :, (hk * SWA_GROUP + g) * 64: (hk * SWA_GROUP + g + 1) * 64])
             for g in range(SWA_GROUP)], axis=0)
        sink = jnp.concatenate(
            [jnp.full((t, 1), sink_ref[hk * SWA_GROUP + g], F32) for g in range(SWA_GROUP)], axis=0)
        o = _softmax_pv([(_dot_nt(qs, k), v)], extra=sink)
        outs += [o[g * t:(g + 1) * t] for g in range(SWA_GROUP)]
    o_ref[0, :, 0:GROUP_W] = jnp.concatenate(outs, axis=1).astype(BF16)
    outs = []
    for h in range(NA_HEADS):
        q = _qcast(b_ref[0, :, h * 64:(h + 1) * 64])
        k = b_ref[0, :, 512 + h * 64: 512 + (h + 1) * 64].astype(BF16)
        v = b_ref[0, :, 1024 + h * 64: 1024 + (h + 1) * 64].astype(BF16)
        outs.append(_softmax_pv([(_dot_nt(q, k), v)]))
    o_ref[0, :, GROUP_W:2 * GROUP_W] = jnp.concatenate(outs, axis=1).astype(BF16)
    lam = _diff_lambda_val(dl_ref, lam_init)
    for h in range(DIFF_HEADS):
        ps = []
        for i in range(2):
            q = _qcast(c_ref[0, :, h * 128 + i * 64: h * 128 + (i + 1) * 64])
            k = c_ref[0, :, 512 + h * 128 + i * 64: 512 + h * 128 + (i + 1) * 64].astype(BF16)
            ps.append(_probs([_dot_nt(q, k)])[0])
        v = c_ref[0, :, 1024 + h * 128: 1024 + (h + 1) * 128].astype(BF16)
        pd = ps[0] - lam * ps[1]
        o = _dot(pd.astype(BF16), v)
        o_ref[0, :, 1024 + h * 128: 1024 + (h + 1) * 128] = _diff_finish(o, subln_ref, lam_init).astype(BF16)


def _ctx_attention(a, b, c, sink, dlam, subln, lam_init, *, nb, t):
    a3, b3, c3 = a.reshape(nb, t, A_W), b.reshape(nb, t, B_W), c.reshape(nb, t, C_W)
    out = pl.pallas_call(
        functools.partial(_ctx_attn_kernel, t=t, lam_init=lam_init),
        out_shape=jax.ShapeDtypeStruct((nb, t, 3 * GROUP_W), BF16),
        grid=(nb,),
        in_specs=[
            pl.BlockSpec(memory_space=pltpu.SMEM),
            pl.BlockSpec((1, t, A_W), lambda i: (i, 0, 0)),
            pl.BlockSpec((1, t, B_W), lambda i: (i, 0, 0)),
            pl.BlockSpec((1, t, C_W), lambda i: (i, 0, 0)),
            pl.BlockSpec((4, HEAD_DIM), lambda i: (0, 0)),
            pl.BlockSpec((1, DIFF_VDIM), lambda i: (0, 0)),
        ],
        out_specs=pl.BlockSpec((1, t, 3 * GROUP_W), lambda i: (i, 0, 0)),
        compiler_params=pltpu.CompilerParams(
            dimension_semantics=("arbitrary",), vmem_limit_bytes=V7X_VMEM_LIMIT),
        name="ctx_attention",
    )(sink, a3, b3, c3, dlam, subln.reshape(1, DIFF_VDIM))
    return out.reshape(nb * t, 3 * GROUP_W)


def _swa_kernel(sink_ref, q_ref, kv_ref, ck_ref, cv_ref, o_ref, *, t, qb):
    n = pl.program_id(1)
    span = 3 * qb
    start = pl.multiple_of(jnp.clip((n - 1) * qb, 0, t - span), qb)
    row = (lax.broadcasted_iota(jnp.int32, (SWA_GROUP * qb, span), 0) & (qb - 1)) + n * qb
    col = lax.broadcasted_iota(jnp.int32, (SWA_GROUP * qb, span), 1) + start
    dist = row - col
    ok = (dist <= SWA_WINDOW) & (dist >= -SWA_WINDOW)
    outs = []
    for hk in range(SWA_KV_HEADS):
        kw = kv_ref[0, pl.ds(start, span), 512 + hk * 64: 512 + (hk + 1) * 64].astype(BF16)
        vw = kv_ref[0, pl.ds(start, span), 640 + hk * 64: 640 + (hk + 1) * 64].astype(BF16)
        ck = ck_ref[0, :, hk * 64:(hk + 1) * 64]
        cv = cv_ref[0, :, hk * 64:(hk + 1) * 64]
        qs = jnp.concatenate(
            [_qcast(q_ref[0, :, (hk * SWA_GROUP + g) * 64: (hk * SWA_GROUP + g + 1) * 64])
             for g in range(SWA_GROUP)], axis=0)
        sink = jnp.concatenate(
            [jnp.full((qb, 1), sink_ref[hk * SWA_GROUP + g], F32) for g in range(SWA_GROUP)], axis=0)
        s_loc = jnp.where(ok, _dot_nt(qs, kw), NEG_INF)
        s_ctx = _dot_nt(qs, ck)
        o = _softmax_pv([(s_loc, vw), (s_ctx, cv)], extra=sink)
        outs += [o[g * qb:(g + 1) * qb] for g in range(SWA_GROUP)]
    o_ref[0] = jnp.concatenate(outs, axis=1).astype(BF16)


def _swa_latent(a, ck, cv, sink, *, nb, t):
    qb = 128
    a3 = a.reshape(nb, t, A_W)
    out = pl.pallas_call(
        functools.partial(_swa_kernel, t=t, qb=qb),
        out_shape=jax.ShapeDtypeStruct((nb, t, GROUP_W), BF16),
        grid=(nb, t // qb),
        in_specs=[
            pl.BlockSpec(memory_space=pltpu.SMEM),
            pl.BlockSpec((1, qb, A_W), lambda b, n: (b, n, 0)),
            pl.BlockSpec((1, t, A_W), lambda b, n: (b, 0, 0)),
            pl.BlockSpec((1,) + ck.shape[1:], lambda b, n: (b, 0, 0)),
            pl.BlockSpec((1,) + cv.shape[1:], lambda b, n: (b, 0, 0)),
        ],
        out_specs=pl.BlockSpec((1, qb, GROUP_W), lambda b, n: (b, n, 0)),
        compiler_params=pltpu.CompilerParams(
            dimension_semantics=("arbitrary", "arbitrary"), vmem_limit_bytes=V7X_VMEM_LIMIT),
        name="swa_latent",
    )(sink, a3, a3, ck, cv)
    return out.reshape(nb * t, GROUP_W)


def _na_row_start(r, rows):
    kh = min(NA_KH, rows)
    return min(max(r - kh // 2, 0), rows - kh)


def _na_win_start(qt, rows):
    return min(max(_na_row_start(2 * qt, rows), 0), rows - NA_WIN_ROWS)


def _na_bias_kernel(rpb_ref, o_ref, *, rows):
    h = pl.program_id(0)
    n_dr, n_dc = 2 * NA_KH - 1, 2 * NA_KW - 1
    qi = lax.broadcasted_iota(jnp.int32, (GRID_W, GRID_W), 0)
    ki = lax.broadcasted_iota(jnp.int32, (GRID_W, GRID_W), 1)
    dc = jnp.clip(ki - qi + (NA_KW - 1), 0, n_dc - 1)
    cs = jnp.clip(qi - NA_KW // 2, 0, GRID_W - NA_KW)
    col_ok = (ki >= cs) & (ki < cs + NA_KW)
    neg = jnp.full((GRID_W, GRID_W), NEG_INF, F32)
    tabs = []
    for dr in range(n_dr):
        acc = jnp.zeros((GRID_W, GRID_W), F32)
        for c in range(n_dc):
            acc = jnp.where(dc == c, rpb_ref[(h * n_dr + dr) * n_dc + c], acc)
        tabs.append(jnp.where(col_ok, acc, neg))
    kh = min(NA_KH, rows)
    for qt in range(rows // 2):
        ws = _na_win_start(qt, rows)
        bands = []
        for qq in range(2):
            qr = 2 * qt + qq
            rs = _na_row_start(qr, rows)
            blks = []
            for kk in range(NA_WIN_ROWS):
                kr = ws + kk
                blks.append(tabs[kr - qr + NA_KH - 1] if rs <= kr < rs + kh else neg)
            bands.append(jnp.concatenate(blks, axis=1))
        o_ref[0, qt] = jnp.concatenate(bands, axis=0)


def _na_bias(rpb, rows):
    n_qt = rows // 2
    return pl.pallas_call(
        functools.partial(_na_bias_kernel, rows=rows),
        out_shape=jax.ShapeDtypeStruct((NA_HEADS, n_qt, NA_QT, NA_WIN), F32),
        grid=(NA_HEADS,),
        in_specs=[pl.BlockSpec(memory_space=pltpu.SMEM)],
        out_specs=pl.BlockSpec((1, n_qt, NA_QT, NA_WIN), lambda h: (h, 0, 0, 0)),
        compiler_params=pltpu.CompilerParams(
            dimension_semantics=("arbitrary",), vmem_limit_bytes=V7X_VMEM_LIMIT),
        name="na_bias",
    )(rpb.reshape(-1))


def _na_kernel(q_ref, kv_ref, ck_ref, cv_ref, bias_ref, o_ref, *, rows):
    qt = pl.program_id(1)
    ws = jnp.clip(jnp.clip(2 * qt - NA_KH // 2, 0, rows - NA_KH), 0, rows - NA_WIN_ROWS)
    start = pl.multiple_of(ws * GRID_W, GRID_W)
    outs = []
    for h in range(NA_HEADS):
        q = _qcast(q_ref[0, :, h * 64:(h + 1) * 64])
        kw = kv_ref[0, pl.ds(start, NA_WIN), 512 + h * 64: 512 + (h + 1) * 64].astype(BF16)
        vw = kv_ref[0, pl.ds(start, NA_WIN), 1024 + h * 64: 1024 + (h + 1) * 64].astype(BF16)
        ck = ck_ref[0, :, h * 64:(h + 1) * 64]
        cv = cv_ref[0, :, h * 64:(h + 1) * 64]
        s_loc = _dot_nt(q, kw) + bias_ref[h, 0]
        s_ctx = _dot_nt(q, ck)
        outs.append(_softmax_pv([(s_loc, vw), (s_ctx, cv)]))
    o_ref[0] = jnp.concatenate(outs, axis=1).astype(BF16)


def _na_latent(b, ck, cv, bias, *, nb, t):
    rows = t // GRID_W
    b3 = b.reshape(nb, t, B_W)
    out = pl.pallas_call(
        functools.partial(_na_kernel, rows=rows),
        out_shape=jax.ShapeDtypeStruct((nb, t, GROUP_W), BF16),
        grid=(nb, t // NA_QT),
        in_specs=[
            pl.BlockSpec((1, NA_QT, B_W), lambda i, n: (i, n, 0)),
            pl.BlockSpec((1, t, B_W), lambda i, n: (i, 0, 0)),
            pl.BlockSpec((1,) + ck.shape[1:], lambda i, n: (i, 0, 0)),
            pl.BlockSpec((1,) + cv.shape[1:], lambda i, n: (i, 0, 0)),
            pl.BlockSpec((NA_HEADS, 1, NA_QT, NA_WIN), lambda i, n: (0, n, 0, 0)),
        ],
        out_specs=pl.BlockSpec((1, NA_QT, GROUP_W), lambda i, n: (i, n, 0)),
        compiler_params=pltpu.CompilerParams(
            dimension_semantics=("arbitrary", "arbitrary"), vmem_limit_bytes=V7X_VMEM_LIMIT),
        name="na_latent",
    )(b3, b3, ck, cv, bias)
    return out.reshape(nb * t, GROUP_W)


def _diff_kernel(q_ref, kv_ref, ck_ref, cv_ref, dl_ref, subln_ref, o_ref, *, lam_init):
    lam = _diff_lambda_val(dl_ref, lam_init)
    for h in range(DIFF_HEADS):
        pl_, pc_ = [], []
        for i in range(2):
            lo = h * 128 + i * 64
            q = _qcast(q_ref[0, :, lo:lo + 64])
            k = kv_ref[0, :, 512 + lo: 512 + lo + 64].astype(BF16)
            ck = ck_ref[0, :, lo:lo + 64]
            p_loc, p_ctx = _probs([_dot_nt(q, k), _dot_nt(q, ck)])
            pl_.append(p_loc)
            pc_.append(p_ctx)
        v = kv_ref[0, :, 1024 + h * 128: 1024 + (h + 1) * 128].astype(BF16)
        cv = cv_ref[0, :, h * 128:(h + 1) * 128]
        pd_loc = pl_[0] - lam * pl_[1]
        pd_ctx = pc_[0] - lam * pc_[1]
        o = _dot(pd_loc.astype(BF16), v) + _dot(pd_ctx.astype(BF16), cv)
        o_ref[0, :, h * 128:(h + 1) * 128] = _diff_finish(o, subln_ref, lam_init).astype(BF16)


def _diff_latent(c, ck, cv, dlam, subln, lam_init, *, nb, t):
    tq = 256
    c3 = c.reshape(nb, t, C_W)
    out = pl.pallas_call(
        functools.partial(_diff_kernel, lam_init=lam_init),
        out_shape=jax.ShapeDtypeStruct((nb, t, GROUP_W), BF16),
        grid=(nb, t // tq),
        in_specs=[
            pl.BlockSpec((1, tq, C_W), lambda i, n: (i, n, 0)),
            pl.BlockSpec((1, t, C_W), lambda i, n: (i, 0, 0)),
            pl.BlockSpec((1,) + ck.shape[1:], lambda i, n: (i, 0, 0)),
            pl.BlockSpec((1,) + cv.shape[1:], lambda i, n: (i, 0, 0)),
            pl.BlockSpec((4, HEAD_DIM), lambda i, n: (0, 0)),
            pl.BlockSpec((1, DIFF_VDIM), lambda i, n: (0, 0)),
        ],
        out_specs=pl.BlockSpec((1, tq, GROUP_W), lambda i, n: (i, n, 0)),
        compiler_params=pltpu.CompilerParams(
            dimension_semantics=("arbitrary", "arbitrary"), vmem_limit_bytes=V7X_VMEM_LIMIT),
        name="diff_latent",
    )(c3, c3, ck, cv, dlam, subln.reshape(1, DIFF_VDIM))
    return out.reshape(nb * t, GROUP_W)


LRU_CT = 128


def _lru_kernel(x_ref, g_ref, cw_ref, cb_ref, w_ref, gb_ref, lam_ref, h0_ref, y_ref, st_ref, *, t):
    x = x_ref[0]
    row = lax.broadcasted_iota(jnp.int32, (t, LRU_CT), 0)
    u = cb_ref[...] + cw_ref[2:3, :] * x
    for tap, off in ((0, -2), (1, -1), (3, 1)):
        xs = pltpu.roll(x, (-off) % t, 0)
        ok = (row + off >= 0) & (row + off < t)
        u = u + cw_ref[tap:tap + 1, :] * jnp.where(ok, xs, 0.0)
    gates = _dot(u.astype(BF16), w_ref[0]) + gb_ref[0]
    total = None
    for d in range(2):
        r = _sigmoid(gates[:, (2 * d) * LRU_CT:(2 * d + 1) * LRU_CT])
        ig = _sigmoid(gates[:, (2 * d + 1) * LRU_CT:(2 * d + 2) * LRU_CT])
        nl = -lam_ref[d:d + 1, :]
        softplus = jnp.maximum(nl, 0.0) + jnp.log1p(jnp.exp(-jnp.abs(nl)))
        log_a = -LRU_C * r * softplus
        a = jnp.exp(log_a)
        bx = jnp.sqrt(jnp.tanh(-log_a) * (a * a + 1.0)) * (ig * u)
        edge = t - 1 if d == 1 else 0
        bx = jnp.where(row == edge, bx + a * h0_ref[0, d:d + 1, :], bx)
        s = 1
        while s < t:
            if d == 0:
                ok = row >= s
                a_sh = jnp.where(ok, pltpu.roll(a, s, 0), 1.0)
                b_sh = jnp.where(ok, pltpu.roll(bx, s, 0), 0.0)
            else:
                ok = row < t - s
                a_sh = jnp.where(ok, pltpu.roll(a, t - s, 0), 1.0)
                b_sh = jnp.where(ok, pltpu.roll(bx, t - s, 0), 0.0)
            bx = a * b_sh + bx
            a = a * a_sh
            s *= 2
        fin = t - 1 if d == 0 else 0
        st_ref[0, d:d + 1, :] = bx[fin:fin + 1, :]
        total = bx if total is None else total + bx
    g = g_ref[0]
    gelu = 0.5 * g * (1.0 + jnp.tanh(math.sqrt(2.0 / math.pi) * (g + 0.044715 * (g * g * g))))
    y_ref[0] = (total * gelu).astype(BF16)


def _lru_mixer(dd, conv_w, conv_b, w_gates, b_gates, lam, h0, *, nb, t):
    d3 = dd.reshape(nb, t, DD_W)
    nct = LRU_WIDTH // LRU_CT
    y, st = pl.pallas_call(
        functools.partial(_lru_kernel, t=t),
        out_shape=[jax.ShapeDtypeStruct((nb, t, LRU_WIDTH), BF16),
                   jax.ShapeDtypeStruct((nb, 2, LRU_WIDTH), F32)],
        grid=(nb, nct),
        in_specs=[
            pl.BlockSpec((1, t, LRU_CT), lambda i, c: (i, 0, c)),
            pl.BlockSpec((1, t, LRU_CT), lambda i, c: (i, 0, nct + c)),
            pl.BlockSpec((CONV_W, LRU_CT), lambda i, c: (0, c)),
            pl.BlockSpec((1, LRU_CT), lambda i, c: (0, c)),
            pl.BlockSpec((1, LRU_CT, 4 * LRU_CT), lambda i, c: (c, 0, 0)),
            pl.BlockSpec((1, 1, 4 * LRU_CT), lambda i, c: (c, 0, 0)),
            pl.BlockSpec((2, LRU_CT), lambda i, c: (0, c)),
            pl.BlockSpec((1, 2, LRU_CT), lambda i, c: (i, 0, c)),
        ],
        out_specs=[pl.BlockSpec((1, t, LRU_CT), lambda i, c: (i, 0, c)),
                   pl.BlockSpec((1, 2, LRU_CT), lambda i, c: (i, 0, c))],
        compiler_params=pltpu.CompilerParams(
            dimension_semantics=("arbitrary", "arbitrary"), vmem_limit_bytes=V7X_VMEM_LIMIT),
        name="lru_mixer",
    )(d3, d3, conv_w, conv_b.reshape(1, LRU_WIDTH), w_gates, b_gates, lam, h0)
    return y.reshape(nb * t, LRU_WIDTH), st


def _lru_gate_weights(wa, ba, wx, bx):
    nct = LRU_WIDTH // LRU_CT
    bpt = LRU_CT // LRU_BLOCK_W
    eye = jnp.eye(bpt, dtype=F32)

    def tile_w(w):
        w4 = w.reshape(nct, bpt, LRU_BLOCK_W, LRU_BLOCK_W)
        return jnp.einsum("cnij,nm->cnimj", w4, eye).reshape(nct, LRU_CT, LRU_CT)

    w = jnp.concatenate([tile_w(wa[0]), tile_w(wx[0]), tile_w(wa[1]), tile_w(wx[1])], axis=-1)
    b = jnp.concatenate([v.reshape(nct, 1, LRU_CT) for v in (ba[0], bx[0], ba[1], bx[1])], axis=-1)
    return w.astype(BF16), b


def _outproj_kernel(m_ref, d_ref, w_ref, x_ref, g_ref, o_ref):
    acc = _dot(m_ref[...], w_ref[0:3 * GROUP_W, :]) + _dot(d_ref[...], w_ref[3 * GROUP_W:, :])
    o_ref[...] = x_ref[...] + g_ref[0] * acc


def _out_projection(mix3, od, w_bf, x, mods_l, mod_row, *, tm):
    n_tok = x.shape[0]
    return pl.pallas_call(
        _outproj_kernel,
        out_shape=jax.ShapeDtypeStruct((n_tok, D_MODEL), F32),
        grid=(n_tok // tm,),
        in_specs=[
            pl.BlockSpec((tm, 3 * GROUP_W), lambda i: (i, 0)),
            pl.BlockSpec((tm, GROUP_W), lambda i: (i, 0)),
            pl.BlockSpec((D_MODEL, D_MODEL), lambda i: (0, 0)),
            pl.BlockSpec((tm, D_MODEL), lambda i: (i, 0)),
            pl.BlockSpec((1, 1, D_MODEL), lambda i: (mod_row(i) * MOD_CHUNKS + 2, 0, 0)),
        ],
        out_specs=pl.BlockSpec((tm, D_MODEL), lambda i: (i, 0)),
        compiler_params=pltpu.CompilerParams(
            dimension_semantics=("arbitrary",), vmem_limit_bytes=V7X_VMEM_LIMIT),
        name="out_projection",
    )(mix3, od, w_bf, x, mods_l)


def _ffn_kernel(x_ref, sh_ref, sc_ref, g_ref, nw_ref, wg_ref, wu_ref, wd_ref, o_ref, h_scr, acc_scr, *, tm):
    k = pl.program_id(1)

    @pl.when(k == 0)
    def _():
        _modnorm_rows(x_ref, nw_ref, sh_ref, sc_ref, h_scr, tm)
        acc_scr[...] = jnp.zeros_like(acc_scr)

    h = h_scr[...]
    g = _dot(h, wg_ref[...])
    u = _dot(h, wu_ref[...])
    act = (g * _sigmoid(g)) * u
    acc_scr[...] += _dot(act.astype(BF16), wd_ref[...])

    @pl.when(k == pl.num_programs(1) - 1)
    def _():
        o_ref[...] = x_ref[...] + g_ref[0] * acc_scr[...]


def _ffn(x, mods_l, mod_row, norm_w, wg, wu, wd, *, tm, th):
    n_tok = x.shape[0]
    return pl.pallas_call(
        functools.partial(_ffn_kernel, tm=tm),
        out_shape=jax.ShapeDtypeStruct((n_tok, D_MODEL), F32),
        grid=(n_tok // tm, FFN_HIDDEN // th),
        in_specs=[
            pl.BlockSpec((tm, D_MODEL), lambda i, k: (i, 0)),
            pl.BlockSpec((1, 1, D_MODEL), lambda i, k: (mod_row(i) * MOD_CHUNKS + 3, 0, 0)),
            pl.BlockSpec((1, 1, D_MODEL), lambda i, k: (mod_row(i) * MOD_CHUNKS + 4, 0, 0)),
            pl.BlockSpec((1, 1, D_MODEL), lambda i, k: (mod_row(i) * MOD_CHUNKS + 5, 0, 0)),
            pl.BlockSpec((1, D_MODEL), lambda i, k: (0, 0)),
            pl.BlockSpec((D_MODEL, th), lambda i, k: (0, k)),
            pl.BlockSpec((D_MODEL, th), lambda i, k: (0, k)),
            pl.BlockSpec((th, D_MODEL), lambda i, k: (k, 0)),
        ],
        out_specs=pl.BlockSpec((tm, D_MODEL), lambda i, k: (i, 0)),
        scratch_shapes=[pltpu.VMEM((tm, D_MODEL), BF16), pltpu.VMEM((tm, D_MODEL), F32)],
        compiler_params=pltpu.CompilerParams(
            dimension_semantics=("arbitrary", "arbitrary"), vmem_limit_bytes=V7X_VMEM_LIMIT),
        name="ffn",
    )(x, mods_l, mods_l, mods_l, norm_w.reshape(1, D_MODEL), wg, wu, wd)


def _rope_tables(t_lat):
    nf = HEAD_DIM // 4
    tok = jnp.arange(t_lat)
    pos = jnp.stack([tok // GRID_W, tok % GRID_W], axis=-1).astype(F32)
    inv = ROPE_BASE ** (-jnp.arange(nf, dtype=F32) / nf)
    ang = pos[:, :, None] * inv
    cos, sin = jnp.cos(ang), jnp.sin(ang)
    cos_h = jnp.concatenate([cos[:, 0], cos[:, 0], cos[:, 1], cos[:, 1]], axis=-1)
    sin_h = jnp.concatenate([-sin[:, 0], sin[:, 0], -sin[:, 1], sin[:, 1]], axis=-1)
    reps = PROJ_TN // HEAD_DIM
    return jnp.tile(cos_h, (1, reps)), jnp.tile(sin_h, (1, reps))


def _column_vectors(qk_gain_l):
    ones = lambda n: jnp.ones((n,), F32)
    zeros = lambda n: jnp.zeros((n,), F32)
    tile = lambda g, n: jnp.tile(g, n // HEAD_DIM)
    gain = jnp.concatenate([
        tile(qk_gain_l[0, 0], 512), tile(qk_gain_l[0, 1], 128), ones(128),
        tile(qk_gain_l[1, 0], 512), tile(qk_gain_l[1, 1], 512), ones(512),
        tile(qk_gain_l[2, 0], 512), tile(qk_gain_l[2, 1], 512), ones(512),
        ones(1024)])
    nflag = jnp.concatenate([ones(640), zeros(128), ones(1024), zeros(512), ones(1024), zeros(512), zeros(1024)])
    rflag = jnp.concatenate([ones(640), zeros(128), zeros(1536), ones(1024), zeros(512), zeros(1024)])
    return gain.reshape(1, PROJ_W), nflag.reshape(1, PROJ_W), rflag.reshape(1, PROJ_W)


def _lambda_init(layer):
    return 0.8 - 0.6 * math.exp(-0.3 * layer)


def kernel(x_prompt, x_sample, cache_swa_k, cache_swa_v, cache_na_k, cache_na_v, cache_diff_k, cache_diff_v, state_lru, c, c_ctx, norm_mix, norm_ffn, w_mod, b_mod, w_in, w_out, qk_gain, swa_sink, na_rpb, diff_lambda, diff_subln, conv_w, conv_b, lru_wa, lru_ba, lru_wx, lru_bx, lru_L, w_ffn_gate, w_ffn_up, w_ffn_down):
    bc, s_ctx, _ = x_prompt.shape
    bd, t_lat, _ = x_sample.shape
    n_ctx, n_lat = bc * s_ctx, bd * t_lat
    rows = t_lat // GRID_W

    cvecs = jnp.concatenate([c_ctx[None, :], c, jnp.zeros((8 - 1 - bd, D_MODEL), F32)], axis=0)
    mods = _modulation(cvecs, w_mod, b_mod).reshape(DEPTH, 8 * MOD_CHUNKS, 1, D_MODEL)

    hsum = jnp.asarray(np.kron(np.eye(PROJ_TN // HEAD_DIM, dtype=np.float32),
                               np.full((HEAD_DIM, HEAD_DIM), 1.0 / HEAD_DIM, np.float32)), BF16)
    cos_t, sin_t = _rope_tables(t_lat)

    tm_proj = 1024
    tm_mm = 512
    ctx_row_proj = lambda i: 0
    lat_row_proj = lambda i: 1 + (i * tm_proj) // t_lat
    ctx_row = lambda i: 0
    lat_row = lambda i: 1 + (i * tm_mm) // t_lat

    xc = x_prompt.reshape(n_ctx, D_MODEL)
    xs = x_sample.reshape(n_lat, D_MODEL)
    zero_state = jnp.zeros((bc, 2, LRU_WIDTH), F32)
    new_k = {name: [] for name in ("swa_k", "swa_v", "na_k", "na_v", "diff_k", "diff_v", "lru")}

    for l in range(DEPTH):
        lam_init = _lambda_init(l)
        mods_l = mods[l]
        w_in_bf = w_in[l].astype(BF16)
        w_out_bf = w_out[l].astype(BF16)
        wg_bf, wu_bf, wd_bf = w_ffn_gate[l].astype(BF16), w_ffn_up[l].astype(BF16), w_ffn_down[l].astype(BF16)
        gain, nflag, rflag = _column_vectors(qk_gain[l])
        w_gates, b_gates = _lru_gate_weights(lru_wa[l], lru_ba[l], lru_wx[l], lru_bx[l])

        a, b, cc, dd = _in_projection(xc, mods_l, ctx_row_proj, norm_mix[l], w_in_bf, gain, nflag, hsum,
                                      None, tm=tm_proj)
        mix3 = _ctx_attention(a, b, cc, swa_sink[l], diff_lambda[l], diff_subln[l], lam_init, nb=bc, t=s_ctx)
        od, st = _lru_mixer(dd, conv_w[l], conv_b[l], w_gates, b_gates, lru_L[l], zero_state, nb=bc, t=s_ctx)
        xc = _out_projection(mix3, od, w_out_bf, xc, mods_l, ctx_row, tm=tm_mm)
        xc = _ffn(xc, mods_l, ctx_row, norm_ffn[l], wg_bf, wu_bf, wd_bf, tm=tm_mm, th=512)
        a3, b3, c3 = a.reshape(bc, s_ctx, A_W), b.reshape(bc, s_ctx, B_W), cc.reshape(bc, s_ctx, C_W)
        new_k["swa_k"].append(a3[:, :, 512:640].reshape(bc, s_ctx, SWA_KV_HEADS, HEAD_DIM))
        new_k["swa_v"].append(a3[:, :, 640:768].reshape(bc, s_ctx, SWA_KV_HEADS, HEAD_DIM))
        new_k["na_k"].append(b3[:, :, 512:1024].reshape(bc, s_ctx, NA_HEADS, HEAD_DIM))
        new_k["na_v"].append(b3[:, :, 1024:1536].reshape(bc, s_ctx, NA_HEADS, HEAD_DIM))
        new_k["diff_k"].append(c3[:, :, 512:1024].reshape(bc, s_ctx, DIFF_HEADS, 2, HEAD_DIM))
        new_k["diff_v"].append(c3[:, :, 1024:1536].reshape(bc, s_ctx, DIFF_HEADS, DIFF_VDIM))
        new_k["lru"].append(st)

        a, b, cc, dd = _in_projection(xs, mods_l, lat_row_proj, norm_mix[l], w_in_bf, gain, nflag, hsum,
                                      (cos_t, sin_t, rflag), tm=tm_proj)
        p_ctx = cache_swa_k.shape[2]
        oa = _swa_latent(a, cache_swa_k[:, l].reshape(bd, p_ctx, -1).astype(BF16),
                         cache_swa_v[:, l].reshape(bd, p_ctx, -1).astype(BF16), swa_sink[l], nb=bd, t=t_lat)
        bias = _na_bias(na_rpb[l], rows)
        ob = _na_latent(b, cache_na_k[:, l].reshape(bd, p_ctx, -1).astype(BF16),
                        cache_na_v[:, l].reshape(bd, p_ctx, -1).astype(BF16), bias, nb=bd, t=t_lat)
        oc = _diff_latent(cc, cache_diff_k[:, l].reshape(bd, p_ctx, -1).astype(BF16),
                          cache_diff_v[:, l].reshape(bd, p_ctx, -1).astype(BF16),
                          diff_lambda[l], diff_subln[l], lam_init, nb=bd, t=t_lat)
        od, _ = _lru_mixer(dd, conv_w[l], conv_b[l], w_gates, b_gates, lru_L[l], state_lru[:, l], nb=bd, t=t_lat)
        mix3 = jnp.concatenate([oa, ob, oc], axis=-1)
        xs = _out_projection(mix3, od, w_out_bf, xs, mods_l, lat_row, tm=tm_mm)
        xs = _ffn(xs, mods_l, lat_row, norm_ffn[l], wg_bf, wu_bf, wd_bf, tm=tm_mm, th=512)

    stack = lambda name: jnp.stack(new_k[name], axis=1)
    return (xc.reshape(bc, s_ctx, D_MODEL), xs.reshape(bd, t_lat, D_MODEL),
            stack("swa_k"), stack("swa_v"), stack("na_k"), stack("na_v"),
            stack("diff_k"), stack("diff_v"), stack("lru"))
```

```python
import functools
import math

import jax
import jax.numpy as jnp
import numpy as np
from jax import lax
from jax.experimental import pallas as pl
from jax.experimental.pallas import tpu as pltpu

F32 = jnp.float32
BF16 = jnp.bfloat16

D_MODEL = 2048
DEPTH = 2
GRID_W = 64
HEAD_DIM = 64
GROUP_W = 512
SWA_HEADS = 8
SWA_KV_HEADS = 2
SWA_GROUP = 4
SWA_WINDOW = 128
NA_HEADS = 8
NA_KH = 8
NA_KW = 16
DIFF_HEADS = 4
DIFF_VDIM = 128
LRU_WIDTH = 512
LRU_BLOCKS = 8
LRU_BLOCK_W = 64
LRU_C = 8.0
CONV_W = 4
FFN_HIDDEN = 5632
ROPE_BASE = 10000.0
NORM_EPS = 1e-6
NEG_INF = -1e30
MOD_CHUNKS = 6
PROJ_W = 4864
QK_SCALE = HEAD_DIM ** -0.5

V7X_VMEM_LIMIT = 60 * 1024 * 1024

PROJ_HALF = 256
PROJ_TILE = 2 * PROJ_HALF
TILE_NAMES = ("lx", "lg", "nk", "dk", "nv", "dv", "skv", "sq", "nq", "dq")
N_TILES = len(TILE_NAMES)
_FIRST_BLOCK = {"sq": 0, "skv": 2, "nq": 3, "nk": 5, "nv": 7, "dq": 9, "dk": 11, "dv": 13, "lx": 15, "lg": 17}
CLS_PLAIN, CLS_NORM, CLS_ROPE, CLS_MIXED = 0, 1, 2, 3
_TILE_CLASS = {"lx": CLS_PLAIN, "lg": CLS_PLAIN, "nv": CLS_PLAIN, "dv": CLS_PLAIN,
               "nk": CLS_NORM, "nq": CLS_NORM, "dk": CLS_ROPE, "sq": CLS_ROPE, "dq": CLS_ROPE,
               "skv": CLS_MIXED}
P_FIRST = 2
P_TILES = N_TILES - P_FIRST
P_COL = {name: TILE_NAMES.index(name) - P_FIRST for name in TILE_NAMES[P_FIRST:]}
F_TILES_CTX = 7
F_TILES_LAT = 2

NA_QT = 128
NA_WIN_ROWS = 10
NA_WIN = NA_WIN_ROWS * GRID_W
LRU_CT = 128
ROW_CHUNK = 256


def _sigmoid(x):
    return 1.0 / (1.0 + jnp.exp(-x))


def _dot(a, b):
    return jnp.dot(a, b, preferred_element_type=F32)


def _dot_nt(a, b):
    return lax.dot_general(a, b, (((1,), (1,)), ((), ())), preferred_element_type=F32)


def _params(*sem):
    return pltpu.CompilerParams(dimension_semantics=sem, vmem_limit_bytes=V7X_VMEM_LIMIT)


def _mod_kernel(c_ref, w_ref, b_ref, o_ref):
    cv = c_ref[...]
    s = cv * _sigmoid(cv)
    o_ref[0] = _dot(s.astype(BF16), w_ref[0].astype(BF16)) + b_ref[0]


def _modulation(cvecs, w_mod, b_mod):
    tn = 1024
    n = MOD_CHUNKS * D_MODEL
    return pl.pallas_call(
        _mod_kernel,
        out_shape=jax.ShapeDtypeStruct((DEPTH, 8, n), F32),
        grid=(DEPTH, n // tn),
        in_specs=[
            pl.BlockSpec((8, D_MODEL), lambda l, j: (0, 0)),
            pl.BlockSpec((1, D_MODEL, tn), lambda l, j: (l, 0, j)),
            pl.BlockSpec((1, 1, tn), lambda l, j: (l, 0, j)),
        ],
        out_specs=pl.BlockSpec((1, 8, tn), lambda l, j: (l, 0, j)),
        compiler_params=_params("arbitrary", "arbitrary"),
        name="modulation",
    )(cvecs, w_mod, b_mod.reshape(DEPTH, 1, n))


def _modnorm_rows(x_ref, nw_ref, sh_ref, sc_ref, h_scr, rows, chunk=128):
    def body(c, carry):
        r = pl.multiple_of(c * chunk, chunk)
        x = x_ref[pl.ds(r, chunk), :]
        ms = jnp.mean(x * x, axis=-1, keepdims=True)
        y = x * lax.rsqrt(ms + NORM_EPS) * nw_ref[...]
        h_scr[pl.ds(r, chunk), :] = (y * (1.0 + sc_ref[0]) + sh_ref[0]).astype(BF16)
        return carry
    lax.fori_loop(0, rows // chunk, body, 0)


def _inproj_kernel(*refs, tm, rope):
    if rope:
        (tab_ref, x_ref, sh_ref, sc_ref, nw_ref, wlo_ref, whi_ref, gain_ref, hsum_ref,
         cos_ref, sin_ref, p_ref, f_ref, h_scr) = refs
    else:
        (tab_ref, x_ref, sh_ref, sc_ref, nw_ref, wlo_ref, whi_ref, gain_ref, hsum_ref,
         p_ref, f_ref, h_scr) = refs
        cos_ref = sin_ref = None
    j = pl.program_id(1)
    cls = tab_ref[2 * N_TILES + j]

    @pl.when(j == 0)
    def _():
        _modnorm_rows(x_ref, nw_ref, sh_ref, sc_ref, h_scr, tm)

    def half_tile(w_ref, half, mode):
        cols = slice(half * PROJ_HALF, (half + 1) * PROJ_HALF)
        w = w_ref[...].astype(BF16)
        gain = gain_ref[0, :, cols]
        for c in range(tm // ROW_CHUNK):
            rows = slice(c * ROW_CHUNK, (c + 1) * ROW_CHUNK)
            p = _dot(h_scr[rows, :], w)
            y = p
            if mode != CLS_PLAIN:
                ms = _dot((p * p).astype(BF16), hsum_ref[...])
                y = p * lax.rsqrt(ms + NORM_EPS) * gain
                if rope and mode in (CLS_ROPE, CLS_MIXED):
                    lane = lax.broadcasted_iota(jnp.int32, y.shape, 1)
                    up = pltpu.roll(y, PROJ_HALF - 16, 1)
                    down = pltpu.roll(y, 16, 1)
                    partner = jnp.where((lane & 31) < 16, up, down)
                    y = y * cos_ref[rows, :] + partner * sin_ref[rows, :]
                if mode == CLS_MIXED:
                    lane = lax.broadcasted_iota(jnp.int32, y.shape, 1)
                    y = jnp.where(lane < 2 * HEAD_DIM, y, p)
            p_ref[rows, cols] = y.astype(BF16)
            f_ref[rows, cols] = y

    def zero_half(half):
        cols = slice(half * PROJ_HALF, (half + 1) * PROJ_HALF)
        p_ref[:, cols] = jnp.zeros((tm, PROJ_HALF), BF16)
        f_ref[:, cols] = jnp.zeros((tm, PROJ_HALF), F32)

    modes = (CLS_PLAIN, CLS_NORM, CLS_ROPE) if rope else (CLS_PLAIN, CLS_NORM)
    for mode in modes:
        cond = (cls == mode)
        if not rope and mode == CLS_NORM:
            cond = (cls == CLS_NORM) | (cls == CLS_ROPE)

        @pl.when(cond)
        def _(mode=mode):
            half_tile(wlo_ref, 0, mode)
            half_tile(whi_ref, 1, mode)

    @pl.when(cls == CLS_MIXED)
    def _():
        half_tile(wlo_ref, 0, CLS_MIXED)
        zero_half(1)


def _in_projection(x, mods_l, mod_row, norm_w, w_in_l, tab, gain, hsum, rope_tabs, *, tm):
    n_tok = x.shape[0]
    rope = rope_tabs is not None
    n_f = F_TILES_LAT if rope else F_TILES_CTX

    in_specs = [
        pl.BlockSpec((tm, D_MODEL), lambda i, j, t: (i, 0)),
        pl.BlockSpec((1, 1, D_MODEL), lambda i, j, t: (mod_row(i) * MOD_CHUNKS + 0, 0, 0)),
        pl.BlockSpec((1, 1, D_MODEL), lambda i, j, t: (mod_row(i) * MOD_CHUNKS + 1, 0, 0)),
        pl.BlockSpec((1, D_MODEL), lambda i, j, t: (0, 0)),
        pl.BlockSpec((D_MODEL, PROJ_HALF), lambda i, j, t: (0, t[j])),
        pl.BlockSpec((D_MODEL, PROJ_HALF), lambda i, j, t: (0, t[N_TILES + j])),
        pl.BlockSpec((1, 1, PROJ_TILE), lambda i, j, t: (j, 0, 0)),
        pl.BlockSpec((PROJ_HALF, PROJ_HALF), lambda i, j, t: (0, 0)),
    ]
    args = [x, mods_l, mods_l, norm_w.reshape(1, D_MODEL), w_in_l, w_in_l, gain, hsum]
    if rope:
        cos_t, sin_t = rope_tabs
        in_specs += [pl.BlockSpec((tm, PROJ_HALF), lambda i, j, t: (0, 0)),
                     pl.BlockSpec((tm, PROJ_HALF), lambda i, j, t: (0, 0))]
        args += [cos_t, sin_t]
    out_specs = [
        pl.BlockSpec((tm, PROJ_TILE), lambda i, j, t: (i, jnp.maximum(j - P_FIRST, 0))),
        pl.BlockSpec((tm, PROJ_TILE), lambda i, j, t: (i, jnp.minimum(j, n_f))),
    ]
    out_shape = [jax.ShapeDtypeStruct((n_tok, P_TILES * PROJ_TILE), BF16),
                 jax.ShapeDtypeStruct((n_tok, (n_f + 1) * PROJ_TILE), F32)]
    return pl.pallas_call(
        functools.partial(_inproj_kernel, tm=tm, rope=rope),
        out_shape=out_shape,
        grid_spec=pltpu.PrefetchScalarGridSpec(
            num_scalar_prefetch=1,
            grid=(n_tok // tm, N_TILES),
            in_specs=in_specs,
            out_specs=out_specs,
            scratch_shapes=[pltpu.VMEM((tm, D_MODEL), BF16)],
        ),
        compiler_params=_params("arbitrary", "arbitrary"),
        name="in_projection_rope" if rope else "in_projection",
    )(tab, *args)


def _softmax_pv(segs, extra=None):
    m = functools.reduce(jnp.maximum, [jnp.max(s, axis=-1, keepdims=True) for s, _ in segs])
    if extra is not None:
        m = jnp.maximum(m, extra)
    den = None
    o = None
    for s, v in segs:
        e = jnp.exp(s - m)
        de = jnp.sum(e, axis=-1, keepdims=True)
        oe = _dot(e.astype(BF16), v)
        den = de if den is None else den + de
        o = oe if o is None else o + oe
    if extra is not None:
        den = den + jnp.exp(extra - m)
    return o / den


def _probs(scores):
    m = functools.reduce(jnp.maximum, [jnp.max(s, axis=-1, keepdims=True) for s in scores])
    es = [jnp.exp(s - m) for s in scores]
    den = functools.reduce(lambda a, b: a + b, [jnp.sum(e, axis=-1, keepdims=True) for e in es])
    inv = 1.0 / den
    return [e * inv for e in es]


def _diff_lambda_val(dl_ref, lam_init):
    lp = dl_ref[...]
    s1 = jnp.sum(lp[0:1] * lp[1:2], axis=-1, keepdims=True)
    s2 = jnp.sum(lp[2:3] * lp[3:4], axis=-1, keepdims=True)
    return jnp.exp(s1) - jnp.exp(s2) + lam_init


def _diff_finish(o, subln_ref, lam_init):
    ms = jnp.mean(o * o, axis=-1, keepdims=True)
    return o * lax.rsqrt(ms + NORM_EPS) * subln_ref[...] * (1.0 - lam_init)


def _head(ref, h, width=HEAD_DIM, rows=None):
    if rows is None:
        return ref[0, :, h * width:(h + 1) * width]
    return ref[0, rows, h * width:(h + 1) * width]


def _ctx_attn_kernel(sink_ref, sq_ref, skv_ref, nq_ref, nk_ref, nv_ref, dq_ref, dk_ref, dv_ref,
                     dl_ref, subln_ref, o_ref, *, t, lam_init):
    outs = []
    for hk in range(SWA_KV_HEADS):
        k = _head(skv_ref, hk)
        v = _head(skv_ref, SWA_KV_HEADS + hk)
        qs = jnp.concatenate([_head(sq_ref, hk * SWA_GROUP + g) for g in range(SWA_GROUP)], axis=0)
        sink = jnp.concatenate(
            [jnp.full((t, 1), sink_ref[hk * SWA_GROUP + g], F32) for g in range(SWA_GROUP)], axis=0)
        o = _softmax_pv([(_dot_nt(qs, k), v)], extra=sink)
        outs += [o[g * t:(g + 1) * t] for g in range(SWA_GROUP)]
    o_ref[0, :, 0:GROUP_W] = jnp.concatenate(outs, axis=1).astype(BF16)
    outs = []
    for h in range(NA_HEADS):
        outs.append(_softmax_pv([(_dot_nt(_head(nq_ref, h), _head(nk_ref, h)), _head(nv_ref, h))]))
    o_ref[0, :, GROUP_W:2 * GROUP_W] = jnp.concatenate(outs, axis=1).astype(BF16)
    lam = _diff_lambda_val(dl_ref, lam_init)
    for h in range(DIFF_HEADS):
        ps = [_probs([_dot_nt(_head(dq_ref, 2 * h + i), _head(dk_ref, 2 * h + i))])[0] for i in range(2)]
        pd = ps[0] - lam * ps[1]
        o = _dot(pd.astype(BF16), _head(dv_ref, h, DIFF_VDIM))
        o_ref[0, :, 1024 + h * 128: 1024 + (h + 1) * 128] = _diff_finish(o, subln_ref, lam_init).astype(BF16)


def _pcol(name, t):
    return pl.BlockSpec((1, t, PROJ_TILE), lambda i, c=P_COL[name]: (i, 0, c))


def _ctx_attention(p, sink, dlam, subln, lam_init, *, nb, t):
    p3 = p.reshape(nb, t, P_TILES * PROJ_TILE)
    names = ("sq", "skv", "nq", "nk", "nv", "dq", "dk", "dv")
    out = pl.pallas_call(
        functools.partial(_ctx_attn_kernel, t=t, lam_init=lam_init),
        out_shape=jax.ShapeDtypeStruct((nb, t, 3 * GROUP_W), BF16),
        grid=(nb,),
        in_specs=[pl.BlockSpec(memory_space=pltpu.SMEM)] + [_pcol(n, t) for n in names] + [
            pl.BlockSpec((4, HEAD_DIM), lambda i: (0, 0)),
            pl.BlockSpec((1, DIFF_VDIM), lambda i: (0, 0)),
        ],
        out_specs=pl.BlockSpec((1, t, 3 * GROUP_W), lambda i: (i, 0, 0)),
        compiler_params=_params("arbitrary"),
        name="ctx_attention",
    )(sink, *([p3] * len(names)), dlam, subln.reshape(1, DIFF_VDIM))
    return out.reshape(nb * t, 3 * GROUP_W)


def _swa_kernel(sink_ref, q_ref, kv_ref, ck_ref, cv_ref, o_ref, *, t, qb):
    n = pl.program_id(1)
    span = 3 * qb
    start = pl.multiple_of(jnp.clip((n - 1) * qb, 0, t - span), qb)
    row = (lax.broadcasted_iota(jnp.int32, (SWA_GROUP * qb, span), 0) & (qb - 1)) + n * qb
    col = lax.broadcasted_iota(jnp.int32, (SWA_GROUP * qb, span), 1) + start
    dist = row - col
    ok = (dist <= SWA_WINDOW) & (dist >= -SWA_WINDOW)
    win = pl.ds(start, span)
    outs = []
    for hk in range(SWA_KV_HEADS):
        kw = _head(kv_ref, hk, rows=win)
        vw = _head(kv_ref, SWA_KV_HEADS + hk, rows=win)
        ck = _head(ck_ref, hk)
        cv = _head(cv_ref, hk)
        qs = jnp.concatenate([_head(q_ref, hk * SWA_GROUP + g) for g in range(SWA_GROUP)], axis=0)
        sink = jnp.concatenate(
            [jnp.full((qb, 1), sink_ref[hk * SWA_GROUP + g], F32) for g in range(SWA_GROUP)], axis=0)
        s_loc = jnp.where(ok, _dot_nt(qs, kw), NEG_INF)
        s_ctx = _dot_nt(qs, ck)
        o = _softmax_pv([(s_loc, vw), (s_ctx, cv)], extra=sink)
        outs += [o[g * qb:(g + 1) * qb] for g in range(SWA_GROUP)]
    o_ref[0] = jnp.concatenate(outs, axis=1).astype(BF16)


def _swa_latent(p, ck, cv, sink, *, nb, t):
    qb = 128
    p3 = p.reshape(nb, t, P_TILES * PROJ_TILE)
    out = pl.pallas_call(
        functools.partial(_swa_kernel, t=t, qb=qb),
        out_shape=jax.ShapeDtypeStruct((nb, t, GROUP_W), BF16),
        grid=(nb, t // qb),
        in_specs=[
            pl.BlockSpec(memory_space=pltpu.SMEM),
            pl.BlockSpec((1, qb, PROJ_TILE), lambda b, n: (b, n, P_COL["sq"])),
            pl.BlockSpec((1, t, PROJ_TILE), lambda b, n: (b, 0, P_COL["skv"])),
            pl.BlockSpec((1,) + ck.shape[1:], lambda b, n: (b, 0, 0)),
            pl.BlockSpec((1,) + cv.shape[1:], lambda b, n: (b, 0, 0)),
        ],
        out_specs=pl.BlockSpec((1, qb, GROUP_W), lambda b, n: (b, n, 0)),
        compiler_params=_params("arbitrary", "arbitrary"),
        name="swa_latent",
    )(sink, p3, p3, ck, cv)
    return out.reshape(nb * t, GROUP_W)


def _na_row_start(r, rows):
    kh = min(NA_KH, rows)
    return min(max(r - kh // 2, 0), rows - kh)


def _na_win_start(qt, rows):
    return min(max(_na_row_start(2 * qt, rows), 0), rows - NA_WIN_ROWS)


def _na_bias_kernel(rpb_ref, o_ref, *, rows):
    h = pl.program_id(0)
    n_dr, n_dc = 2 * NA_KH - 1, 2 * NA_KW - 1
    qi = lax.broadcasted_iota(jnp.int32, (GRID_W, GRID_W), 0)
    ki = lax.broadcasted_iota(jnp.int32, (GRID_W, GRID_W), 1)
    dc = jnp.clip(ki - qi + (NA_KW - 1), 0, n_dc - 1)
    cs = jnp.clip(qi - NA_KW // 2, 0, GRID_W - NA_KW)
    col_ok = (ki >= cs) & (ki < cs + NA_KW)
    neg = jnp.full((GRID_W, GRID_W), NEG_INF, F32)
    tabs = []
    for dr in range(n_dr):
        acc = jnp.zeros((GRID_W, GRID_W), F32)
        for c in range(n_dc):
            acc = jnp.where(dc == c, rpb_ref[(h * n_dr + dr) * n_dc + c], acc)
        tabs.append(jnp.where(col_ok, acc, neg))
    kh = min(NA_KH, rows)
    for qt in range(rows // 2):
        ws = _na_win_start(qt, rows)
        bands = []
        for qq in range(2):
            qr = 2 * qt + qq
            rs = _na_row_start(qr, rows)
            blks = []
            for kk in range(NA_WIN_ROWS):
                kr = ws + kk
                blks.append(tabs[kr - qr + NA_KH - 1] if rs <= kr < rs + kh else neg)
            bands.append(jnp.concatenate(blks, axis=1))
        o_ref[0, qt] = jnp.concatenate(bands, axis=0)


def _na_bias(rpb, rows):
    n_qt = rows // 2
    return pl.pallas_call(
        functools.partial(_na_bias_kernel, rows=rows),
        out_shape=jax.ShapeDtypeStruct((NA_HEADS, n_qt, NA_QT, NA_WIN), F32),
        grid=(NA_HEADS,),
        in_specs=[pl.BlockSpec(memory_space=pltpu.SMEM)],
        out_specs=pl.BlockSpec((1, n_qt, NA_QT, NA_WIN), lambda h: (h, 0, 0, 0)),
        compiler_params=_params("arbitrary"),
        name="na_bias",
    )(rpb.reshape(-1))


def _na_kernel(q_ref, k_ref, v_ref, ck_ref, cv_ref, bias_ref, o_ref, *, rows):
    qt = pl.program_id(1)
    ws = jnp.clip(jnp.clip(2 * qt - NA_KH // 2, 0, rows - NA_KH), 0, rows - NA_WIN_ROWS)
    win = pl.ds(pl.multiple_of(ws * GRID_W, GRID_W), NA_WIN)
    outs = []
    for h in range(NA_HEADS):
        q = _head(q_ref, h)
        s_loc = _dot_nt(q, _head(k_ref, h, rows=win)) + bias_ref[h, 0]
        s_ctx = _dot_nt(q, _head(ck_ref, h))
        outs.append(_softmax_pv([(s_loc, _head(v_ref, h, rows=win)), (s_ctx, _head(cv_ref, h))]))
    o_ref[0] = jnp.concatenate(outs, axis=1).astype(BF16)


def _na_latent(p, ck, cv, bias, *, nb, t):
    rows = t // GRID_W
    p3 = p.reshape(nb, t, P_TILES * PROJ_TILE)
    out = pl.pallas_call(
        functools.partial(_na_kernel, rows=rows),
        out_shape=jax.ShapeDtypeStruct((nb, t, GROUP_W), BF16),
        grid=(nb, t // NA_QT),
        in_specs=[
            pl.BlockSpec((1, NA_QT, PROJ_TILE), lambda i, n: (i, n, P_COL["nq"])),
            pl.BlockSpec((1, t, PROJ_TILE), lambda i, n: (i, 0, P_COL["nk"])),
            pl.BlockSpec((1, t, PROJ_TILE), lambda i, n: (i, 0, P_COL["nv"])),
            pl.BlockSpec((1,) + ck.shape[1:], lambda i, n: (i, 0, 0)),
            pl.BlockSpec((1,) + cv.shape[1:], lambda i, n: (i, 0, 0)),
            pl.BlockSpec((NA_HEADS, 1, NA_QT, NA_WIN), lambda i, n: (0, n, 0, 0)),
        ],
        out_specs=pl.BlockSpec((1, NA_QT, GROUP_W), lambda i, n: (i, n, 0)),
        compiler_params=_params("arbitrary", "arbitrary"),
        name="na_latent",
    )(p3, p3, p3, ck, cv, bias)
    return out.reshape(nb * t, GROUP_W)


def _diff_kernel(q_ref, k_ref, v_ref, ck_ref, cv_ref, dl_ref, subln_ref, o_ref, *, lam_init):
    lam = _diff_lambda_val(dl_ref, lam_init)
    for h in range(DIFF_HEADS):
        pl_, pc_ = [], []
        for i in range(2):
            q = _head(q_ref, 2 * h + i)
            p_loc, p_ctx = _probs([_dot_nt(q, _head(k_ref, 2 * h + i)), _dot_nt(q, _head(ck_ref, 2 * h + i))])
            pl_.append(p_loc)
            pc_.append(p_ctx)
        pd_loc = pl_[0] - lam * pl_[1]
        pd_ctx = pc_[0] - lam * pc_[1]
        o = (_dot(pd_loc.astype(BF16), _head(v_ref, h, DIFF_VDIM))
             + _dot(pd_ctx.astype(BF16), _head(cv_ref, h, DIFF_VDIM)))
        o_ref[0, :, h * 128:(h + 1) * 128] = _diff_finish(o, subln_ref, lam_init).astype(BF16)


def _diff_latent(p, ck, cv, dlam, subln, lam_init, *, nb, t):
    tq = 256
    p3 = p.reshape(nb, t, P_TILES * PROJ_TILE)
    out = pl.pallas_call(
        functools.partial(_diff_kernel, lam_init=lam_init),
        out_shape=jax.ShapeDtypeStruct((nb, t, GROUP_W), BF16),
        grid=(nb, t // tq),
        in_specs=[
            pl.BlockSpec((1, tq, PROJ_TILE), lambda i, n: (i, n, P_COL["dq"])),
            pl.BlockSpec((1, t, PROJ_TILE), lambda i, n: (i, 0, P_COL["dk"])),
            pl.BlockSpec((1, t, PROJ_TILE), lambda i, n: (i, 0, P_COL["dv"])),
            pl.BlockSpec((1,) + ck.shape[1:], lambda i, n: (i, 0, 0)),
            pl.BlockSpec((1,) + cv.shape[1:], lambda i, n: (i, 0, 0)),
            pl.BlockSpec((4, HEAD_DIM), lambda i, n: (0, 0)),
            pl.BlockSpec((1, DIFF_VDIM), lambda i, n: (0, 0)),
        ],
        out_specs=pl.BlockSpec((1, tq, GROUP_W), lambda i, n: (i, n, 0)),
        compiler_params=_params("arbitrary", "arbitrary"),
        name="diff_latent",
    )(p3, p3, p3, ck, cv, dlam, subln.reshape(1, DIFF_VDIM))
    return out.reshape(nb * t, GROUP_W)


def _lru_kernel(x_ref, g_ref, cw_ref, cb_ref, w_ref, gb_ref, lam_ref, h0_ref, y_ref, st_ref, *, t):
    x = x_ref[0]
    row = lax.broadcasted_iota(jnp.int32, (t, LRU_CT), 0)
    u = cb_ref[...] + cw_ref[2:3, :] * x
    for tap, off in ((0, -2), (1, -1), (3, 1)):
        xs = pltpu.roll(x, (-off) % t, 0)
        ok = (row + off >= 0) & (row + off < t)
        u = u + cw_ref[tap:tap + 1, :] * jnp.where(ok, xs, 0.0)
    gates = _dot(u.astype(BF16), w_ref[0]) + gb_ref[0]
    total = None
    for d in range(2):
        r = _sigmoid(gates[:, (2 * d) * LRU_CT:(2 * d + 1) * LRU_CT])
        ig = _sigmoid(gates[:, (2 * d + 1) * LRU_CT:(2 * d + 2) * LRU_CT])
        nl = -lam_ref[d:d + 1, :]
        softplus = jnp.maximum(nl, 0.0) + jnp.log1p(jnp.exp(-jnp.abs(nl)))
        log_a = -LRU_C * r * softplus
        a = jnp.exp(log_a)
        bx = jnp.sqrt(jnp.tanh(-log_a) * (a * a + 1.0)) * (ig * u)
        edge = t - 1 if d == 1 else 0
        bx = jnp.where(row == edge, bx + a * h0_ref[0, d:d + 1, :], bx)
        s = 1
        while s < t:
            if d == 0:
                ok = row >= s
                a_sh = jnp.where(ok, pltpu.roll(a, s, 0), 1.0)
                b_sh = jnp.where(ok, pltpu.roll(bx, s, 0), 0.0)
            else:
                ok = row < t - s
                a_sh = jnp.where(ok, pltpu.roll(a, t - s, 0), 1.0)
                b_sh = jnp.where(ok, pltpu.roll(bx, t - s, 0), 0.0)
            bx = a * b_sh + bx
            a = a * a_sh
            s *= 2
        fin = t - 1 if d == 0 else 0
        st_ref[0, d:d + 1, :] = bx[fin:fin + 1, :]
        total = bx if total is None else total + bx
    g = g_ref[0]
    gelu = 0.5 * g * (1.0 + jnp.tanh(math.sqrt(2.0 / math.pi) * (g + 0.044715 * (g * g * g))))
    y_ref[0] = (total * gelu).astype(BF16)


def _lru_mixer(f, conv_w, conv_b, w_gates, b_gates, lam, h0, *, nb, t):
    f3 = f.reshape(nb, t, f.shape[1])
    nct = LRU_WIDTH // LRU_CT
    y, st = pl.pallas_call(
        functools.partial(_lru_kernel, t=t),
        out_shape=[jax.ShapeDtypeStruct((nb, t, LRU_WIDTH), BF16),
                   jax.ShapeDtypeStruct((nb, 2, LRU_WIDTH), F32)],
        grid=(nb, nct),
        in_specs=[
            pl.BlockSpec((1, t, LRU_CT), lambda i, c: (i, 0, c)),
            pl.BlockSpec((1, t, LRU_CT), lambda i, c: (i, 0, nct + c)),
            pl.BlockSpec((CONV_W, LRU_CT), lambda i, c: (0, c)),
            pl.BlockSpec((1, LRU_CT), lambda i, c: (0, c)),
            pl.BlockSpec((1, LRU_CT, 4 * LRU_CT), lambda i, c: (c, 0, 0)),
            pl.BlockSpec((1, 1, 4 * LRU_CT), lambda i, c: (c, 0, 0)),
            pl.BlockSpec((2, LRU_CT), lambda i, c: (0, c)),
            pl.BlockSpec((1, 2, LRU_CT), lambda i, c: (i, 0, c)),
        ],
        out_specs=[pl.BlockSpec((1, t, LRU_CT), lambda i, c: (i, 0, c)),
                   pl.BlockSpec((1, 2, LRU_CT), lambda i, c: (i, 0, c))],
        compiler_params=_params("arbitrary", "arbitrary"),
        name="lru_mixer",
    )(f3, f3, conv_w, conv_b.reshape(1, LRU_WIDTH), w_gates, b_gates, lam, h0)
    return y.reshape(nb * t, LRU_WIDTH), st


def _lru_gate_weights(wa, ba, wx, bx):
    nct = LRU_WIDTH // LRU_CT
    bpt = LRU_CT // LRU_BLOCK_W
    eye = jnp.eye(bpt, dtype=F32)

    def tile_w(w):
        w4 = w.reshape(nct, bpt, LRU_BLOCK_W, LRU_BLOCK_W)
        return jnp.einsum("cnij,nm->cnimj", w4, eye).reshape(nct, LRU_CT, LRU_CT)

    w = jnp.concatenate([tile_w(wa[0]), tile_w(wx[0]), tile_w(wa[1]), tile_w(wx[1])], axis=-1)
    b = jnp.concatenate([v.reshape(nct, 1, LRU_CT) for v in (ba[0], bx[0], ba[1], bx[1])], axis=-1)
    return w.astype(BF16), b


def _outproj_kernel(*refs, n_in):
    ins = refs[:n_in]
    w_ref, x_ref, g_ref, o_ref = refs[n_in:]
    acc = None
    off = 0
    for m_ref in ins:
        width = m_ref.shape[1]
        part = _dot(m_ref[...], w_ref[off:off + width, :].astype(BF16))
        acc = part if acc is None else acc + part
        off += width
    o_ref[...] = x_ref[...] + g_ref[0] * acc


def _out_projection(mixes, w_out_l, x, mods_l, mod_row, *, tm, tn):
    n_tok = x.shape[0]
    return pl.pallas_call(
        functools.partial(_outproj_kernel, n_in=len(mixes)),
        out_shape=jax.ShapeDtypeStruct((n_tok, D_MODEL), F32),
        grid=(n_tok // tm, D_MODEL // tn),
        in_specs=[pl.BlockSpec((tm, m.shape[1]), lambda i, j: (i, 0)) for m in mixes] + [
            pl.BlockSpec((D_MODEL, tn), lambda i, j: (0, j)),
            pl.BlockSpec((tm, tn), lambda i, j: (i, j)),
            pl.BlockSpec((1, 1, tn), lambda i, j: (mod_row(i) * MOD_CHUNKS + 2, 0, j)),
        ],
        out_specs=pl.BlockSpec((tm, tn), lambda i, j: (i, j)),
        compiler_params=_params("arbitrary", "arbitrary"),
        name="out_projection",
    )(*mixes, w_out_l, x, mods_l)


def _ffn_kernel(x_ref, sh_ref, sc_ref, g_ref, nw_ref, wg_ref, wu_ref, wd_ref, o_ref, h_scr, *, tm):
    k = pl.program_id(1)

    @pl.when(k == 0)
    def _():
        _modnorm_rows(x_ref, nw_ref, sh_ref, sc_ref, h_scr, tm)
        o_ref[...] = jnp.zeros_like(o_ref)

    wg = wg_ref[...].astype(BF16)
    wu = wu_ref[...].astype(BF16)
    wd = wd_ref[...].astype(BF16)
    for c in range(tm // ROW_CHUNK):
        rows = slice(c * ROW_CHUNK, (c + 1) * ROW_CHUNK)
        h = h_scr[rows, :]
        g = _dot(h, wg)
        u = _dot(h, wu)
        act = (g * _sigmoid(g)) * u
        o_ref[rows, :] += _dot(act.astype(BF16), wd)

    @pl.when(k == pl.num_programs(1) - 1)
    def _():
        def body(c, carry):
            r = pl.multiple_of(c * 128, 128)
            o_ref[pl.ds(r, 128), :] = x_ref[pl.ds(r, 128), :] + g_ref[0] * o_ref[pl.ds(r, 128), :]
            return carry
        lax.fori_loop(0, tm // 128, body, 0)


def _ffn(x, mods_l, mod_row, norm_w, wg, wu, wd, *, tm, th):
    n_tok = x.shape[0]
    return pl.pallas_call(
        functools.partial(_ffn_kernel, tm=tm),
        out_shape=jax.ShapeDtypeStruct((n_tok, D_MODEL), F32),
        grid=(n_tok // tm, FFN_HIDDEN // th),
        in_specs=[
            pl.BlockSpec((tm, D_MODEL), lambda i, k: (i, 0)),
            pl.BlockSpec((1, 1, D_MODEL), lambda i, k: (mod_row(i) * MOD_CHUNKS + 3, 0, 0)),
            pl.BlockSpec((1, 1, D_MODEL), lambda i, k: (mod_row(i) * MOD_CHUNKS + 4, 0, 0)),
            pl.BlockSpec((1, 1, D_MODEL), lambda i, k: (mod_row(i) * MOD_CHUNKS + 5, 0, 0)),
            pl.BlockSpec((1, D_MODEL), lambda i, k: (0, 0)),
            pl.BlockSpec((D_MODEL, th), lambda i, k: (0, k)),
            pl.BlockSpec((D_MODEL, th), lambda i, k: (0, k)),
            pl.BlockSpec((th, D_MODEL), lambda i, k: (k, 0)),
        ],
        out_specs=pl.BlockSpec((tm, D_MODEL), lambda i, k: (i, 0)),
        scratch_shapes=[pltpu.VMEM((tm, D_MODEL), BF16)],
        compiler_params=_params("arbitrary", "arbitrary"),
        name="ffn",
    )(x, mods_l, mods_l, mods_l, norm_w.reshape(1, D_MODEL), wg, wu, wd)


def _rope_tables(t_lat):
    nf = HEAD_DIM // 4
    tok = jnp.arange(t_lat)
    pos = jnp.stack([tok // GRID_W, tok % GRID_W], axis=-1).astype(F32)
    inv = ROPE_BASE ** (-jnp.arange(nf, dtype=F32) / nf)
    ang = pos[:, :, None] * inv
    cos, sin = jnp.cos(ang), jnp.sin(ang)
    cos_h = jnp.concatenate([cos[:, 0], cos[:, 0], cos[:, 1], cos[:, 1]], axis=-1)
    sin_h = jnp.concatenate([-sin[:, 0], sin[:, 0], -sin[:, 1], sin[:, 1]], axis=-1)
    reps = PROJ_HALF // HEAD_DIM
    return jnp.tile(cos_h, (1, reps)), jnp.tile(sin_h, (1, reps))


def _tile_table():
    lo = [_FIRST_BLOCK[n] for n in TILE_NAMES]
    hi = [b + 1 for b in lo]
    cls = [_TILE_CLASS[n] for n in TILE_NAMES]
    return jnp.asarray(np.array(lo + hi + cls, np.int32))


def _tile_gains(qk_gain_l):
    def tiled(g, n):
        return jnp.tile(g, n // HEAD_DIM)
    ones = jnp.ones((PROJ_TILE,), F32)
    per_tile = {
        "lx": ones, "lg": ones, "nv": ones, "dv": ones,
        "nk": tiled(qk_gain_l[1, 1], PROJ_TILE), "dk": tiled(qk_gain_l[2, 1], PROJ_TILE),
        "skv": jnp.concatenate([tiled(qk_gain_l[0, 1], 128), jnp.ones((PROJ_TILE - 128,), F32)]),
        "sq": tiled(qk_gain_l[0, 0], PROJ_TILE) * QK_SCALE,
        "nq": tiled(qk_gain_l[1, 0], PROJ_TILE) * QK_SCALE,
        "dq": tiled(qk_gain_l[2, 0], PROJ_TILE) * QK_SCALE,
    }
    return jnp.stack([per_tile[n] for n in TILE_NAMES]).reshape(N_TILES, 1, PROJ_TILE)


def _lambda_init(layer):
    return 0.8 - 0.6 * math.exp(-0.3 * layer)


def kernel(x_prompt, x_sample, cache_swa_k, cache_swa_v, cache_na_k, cache_na_v, cache_diff_k, cache_diff_v, state_lru, c, c_ctx, norm_mix, norm_ffn, w_mod, b_mod, w_in, w_out, qk_gain, swa_sink, na_rpb, diff_lambda, diff_subln, conv_w, conv_b, lru_wa, lru_ba, lru_wx, lru_bx, lru_L, w_ffn_gate, w_ffn_up, w_ffn_down):
    bc, s_ctx, _ = x_prompt.shape
    bd, t_lat, _ = x_sample.shape
    n_ctx, n_lat = bc * s_ctx, bd * t_lat
    rows = t_lat // GRID_W
    p_ctx = cache_swa_k.shape[2]

    cvecs = jnp.concatenate([c_ctx[None, :], c, jnp.zeros((8 - 1 - bd, D_MODEL), F32)], axis=0)
    mods = _modulation(cvecs, w_mod, b_mod).reshape(DEPTH, 8 * MOD_CHUNKS, 1, D_MODEL)

    hsum = jnp.asarray(np.kron(np.eye(PROJ_HALF // HEAD_DIM, dtype=np.float32),
                               np.full((HEAD_DIM, HEAD_DIM), 1.0 / HEAD_DIM, np.float32)), BF16)
    rope_tabs = _rope_tables(t_lat)
    tab = _tile_table()

    tm = 1024
    ctx_row = lambda i: 0
    lat_row = lambda i: 1 + (i * tm) // t_lat

    xc = x_prompt.reshape(n_ctx, D_MODEL)
    xs = x_sample.reshape(n_lat, D_MODEL)
    zero_state = jnp.zeros((bc, 2, LRU_WIDTH), F32)
    new_k = {name: [] for name in ("swa_k", "swa_v", "na_k", "na_v", "diff_k", "diff_v", "lru")}

    def cached(arr, l):
        return arr[:, l].reshape(bd, p_ctx, -1).astype(BF16)

    for l in range(DEPTH):
        lam_init = _lambda_init(l)
        mods_l = mods[l]
        gain = _tile_gains(qk_gain[l])
        w_gates, b_gates = _lru_gate_weights(lru_wa[l], lru_ba[l], lru_wx[l], lru_bx[l])
        ffn_w = (w_ffn_gate[l], w_ffn_up[l], w_ffn_down[l])

        p, f = _in_projection(xc, mods_l, ctx_row, norm_mix[l], w_in[l], tab, gain, hsum, None, tm=tm)
        mix3 = _ctx_attention(p, swa_sink[l], diff_lambda[l], diff_subln[l], lam_init, nb=bc, t=s_ctx)
        od, st = _lru_mixer(f, conv_w[l], conv_b[l], w_gates, b_gates, lru_L[l], zero_state, nb=bc, t=s_ctx)
        xc = _out_projection([mix3, od], w_out[l], xc, mods_l, ctx_row, tm=tm, tn=512)
        xc = _ffn(xc, mods_l, ctx_row, norm_ffn[l], *ffn_w, tm=tm, th=256)
        f3 = f.reshape(bc, s_ctx, -1)
        col = lambda name: TILE_NAMES.index(name) * PROJ_TILE
        new_k["na_k"].append(f3[:, :, col("nk"):col("nk") + 512].reshape(bc, s_ctx, NA_HEADS, HEAD_DIM))
        new_k["na_v"].append(f3[:, :, col("nv"):col("nv") + 512].reshape(bc, s_ctx, NA_HEADS, HEAD_DIM))
        new_k["diff_k"].append(f3[:, :, col("dk"):col("dk") + 512].reshape(bc, s_ctx, DIFF_HEADS, 2, HEAD_DIM))
        new_k["diff_v"].append(f3[:, :, col("dv"):col("dv") + 512].reshape(bc, s_ctx, DIFF_HEADS, DIFF_VDIM))
        new_k["swa_k"].append(f3[:, :, col("skv"):col("skv") + 128].reshape(bc, s_ctx, SWA_KV_HEADS, HEAD_DIM))
        new_k["swa_v"].append(f3[:, :, col("skv") + 128:col("skv") + 256].reshape(bc, s_ctx, SWA_KV_HEADS, HEAD_DIM))
        new_k["lru"].append(st)

        p, f = _in_projection(xs, mods_l, lat_row, norm_mix[l], w_in[l], tab, gain, hsum, rope_tabs, tm=tm)
        oa = _swa_latent(p, cached(cache_swa_k, l), cached(cache_swa_v, l), swa_sink[l], nb=bd, t=t_lat)
        bias = _na_bias(na_rpb[l], rows)
        ob = _na_latent(p, cached(cache_na_k, l), cached(cache_na_v, l), bias, nb=bd, t=t_lat)
        oc = _diff_latent(p, cached(cache_diff_k, l), cached(cache_diff_v, l),
                          diff_lambda[l], diff_subln[l], lam_init, nb=bd, t=t_lat)
        od, _ = _lru_mixer(f, conv_w[l], conv_b[l], w_gates, b_gates, lru_L[l], state_lru[:, l], nb=bd, t=t_lat)
        xs = _out_projection([oa, ob, oc, od], w_out[l], xs, mods_l, lat_row, tm=tm, tn=512)
        xs = _ffn(xs, mods_l, lat_row, norm_ffn[l], *ffn_w, tm=tm, th=256)

    stack = lambda name: jnp.stack(new_k[name], axis=1)
    return (xc.reshape(bc, s_ctx, D_MODEL), xs.reshape(bd, t_lat, D_MODEL),
            stack("swa_k"), stack("swa_v"), stack("na_k"), stack("na_v"),
            stack("diff_k"), stack("diff_v"), stack("lru"))
```

```python
import functools
import math

import jax
import jax.numpy as jnp
import numpy as np
from jax import lax
from jax.experimental import pallas as pl
from jax.experimental.pallas import tpu as pltpu

F32 = jnp.float32
BF16 = jnp.bfloat16

D_MODEL = 2048
DEPTH = 2
GRID_W = 64
HEAD_DIM = 64
GROUP_W = 512
SWA_HEADS = 8
SWA_KV_HEADS = 2
SWA_GROUP = 4
SWA_WINDOW = 128
NA_HEADS = 8
NA_KH = 8
NA_KW = 16
DIFF_HEADS = 4
DIFF_VDIM = 128
LRU_WIDTH = 512
LRU_BLOCKS = 8
LRU_BLOCK_W = 64
LRU_C = 8.0
CONV_W = 4
FFN_HIDDEN = 5632
ROPE_BASE = 10000.0
NORM_EPS = 1e-6
NEG_INF = -1e30
MOD_CHUNKS = 6
PROJ_W = 4864
QK_SCALE = HEAD_DIM ** -0.5

V7X_VMEM_LIMIT = 60 * 1024 * 1024

PROJ_HALF = 256
PROJ_TILE = 2 * PROJ_HALF
TILE_NAMES = ("lx", "lg", "nk", "dk", "nv", "dv", "skv", "sq", "nq", "dq")
N_TILES = len(TILE_NAMES)
_FIRST_BLOCK = {"sq": 0, "skv": 2, "nq": 3, "nk": 5, "nv": 7, "dq": 9, "dk": 11, "dv": 13, "lx": 15, "lg": 17}
CLS_PLAIN, CLS_NORM, CLS_ROPE, CLS_MIXED = 0, 1, 2, 3
_TILE_CLASS = {"lx": CLS_PLAIN, "lg": CLS_PLAIN, "nv": CLS_PLAIN, "dv": CLS_PLAIN,
               "nk": CLS_NORM, "nq": CLS_NORM, "dk": CLS_ROPE, "sq": CLS_ROPE, "dq": CLS_ROPE,
               "skv": CLS_MIXED}
P_FIRST = 2
P_TILES = N_TILES - P_FIRST
P_COL = {name: TILE_NAMES.index(name) - P_FIRST for name in TILE_NAMES[P_FIRST:]}
F_TILES = 2

NA_QT = 128
NA_WIN_ROWS = 10
NA_WIN = NA_WIN_ROWS * GRID_W
LRU_CT = 128
ROW_CHUNK = 256
FFN_ROW_CHUNK = 512


def _sigmoid(x):
    return 1.0 / (1.0 + jnp.exp(-x))


def _dot(a, b):
    return jnp.dot(a, b, preferred_element_type=F32)


def _dot_nt(a, b):
    return lax.dot_general(a, b, (((1,), (1,)), ((), ())), preferred_element_type=F32)


def _params(*sem):
    return pltpu.CompilerParams(dimension_semantics=sem, vmem_limit_bytes=V7X_VMEM_LIMIT)


def _mod_kernel(c_ref, w_ref, b_ref, o_ref):
    cv = c_ref[...]
    s = cv * _sigmoid(cv)
    o_ref[0] = _dot(s.astype(BF16), w_ref[0].astype(BF16)) + b_ref[0]


def _modulation(cvecs, w_mod, b_mod):
    tn = 1024
    n = MOD_CHUNKS * D_MODEL
    return pl.pallas_call(
        _mod_kernel,
        out_shape=jax.ShapeDtypeStruct((DEPTH, 8, n), F32),
        grid=(DEPTH, n // tn),
        in_specs=[
            pl.BlockSpec((8, D_MODEL), lambda l, j: (0, 0)),
            pl.BlockSpec((1, D_MODEL, tn), lambda l, j: (l, 0, j)),
            pl.BlockSpec((1, 1, tn), lambda l, j: (l, 0, j)),
        ],
        out_specs=pl.BlockSpec((1, 8, tn), lambda l, j: (l, 0, j)),
        compiler_params=_params("arbitrary", "arbitrary"),
        name="modulation",
    )(cvecs, w_mod, b_mod.reshape(DEPTH, 1, n))


def _modnorm_rows(x_ref, nw_ref, sh_ref, sc_ref, h_scr, rows, chunk=128):
    def body(c, carry):
        r = pl.multiple_of(c * chunk, chunk)
        x = x_ref[pl.ds(r, chunk), :]
        ms = jnp.mean(x * x, axis=-1, keepdims=True)
        y = x * lax.rsqrt(ms + NORM_EPS) * nw_ref[...]
        h_scr[pl.ds(r, chunk), :] = (y * (1.0 + sc_ref[0]) + sh_ref[0]).astype(BF16)
        return carry
    lax.fori_loop(0, rows // chunk, body, 0)


CACHE_NAMES = ("swa_k", "swa_v", "na_k", "na_v", "diff_k", "diff_v")
_CACHE_SRC = {"swa_k": ("skv", 0, 128), "swa_v": ("skv", 128, 128), "na_k": ("nk", 0, 512),
              "na_v": ("nv", 0, 512), "diff_k": ("dk", 0, 512), "diff_v": ("dv", 0, 512)}


def _inproj_kernel(*refs, tm, rope, n_cache, seq):
    n_in = 10 if rope else 8
    tab_ref = refs[0]
    (x_ref, sh_ref, sc_ref, nw_ref, wlo_ref, whi_ref, gain_ref, hsum_ref) = refs[1:9]
    cos_ref, sin_ref = (refs[9], refs[10]) if rope else (None, None)
    outs = refs[1 + n_in + n_cache:]
    p_ref, f_ref = outs[0], outs[1]
    cache_refs = outs[2:2 + n_cache]
    h_scr = outs[2 + n_cache]
    j = pl.program_id(1)
    cls = tab_ref[2 * N_TILES + j]

    @pl.when(j == 0)
    def _():
        _modnorm_rows(x_ref, nw_ref, sh_ref, sc_ref, h_scr, tm)

    def run_tile(mode, halves):
        w = {half: (wlo_ref if half == 0 else whi_ref)[0] for half in halves}
        units = [(half, c) for half in halves for c in range(tm // ROW_CHUNK)]

        def main(unit):
            half, c = unit
            return _dot(h_scr[c * ROW_CHUNK:(c + 1) * ROW_CHUNK, :], w[half])

        def finish(unit, p):
            half, c = unit
            rows = slice(c * ROW_CHUNK, (c + 1) * ROW_CHUNK)
            cols = slice(half * PROJ_HALF, (half + 1) * PROJ_HALF)
            y = p
            if mode != CLS_PLAIN:
                ms = _dot((p * p).astype(BF16), hsum_ref[...])
                y = p * lax.rsqrt(ms + NORM_EPS) * gain_ref[0, :, cols]
                if rope and mode in (CLS_ROPE, CLS_MIXED):
                    lane = lax.broadcasted_iota(jnp.int32, y.shape, 1)
                    up = pltpu.roll(y, PROJ_HALF - 16, 1)
                    down = pltpu.roll(y, 16, 1)
                    partner = jnp.where((lane & 31) < 16, up, down)
                    y = y * cos_ref[rows, :] + partner * sin_ref[rows, :]
                if mode == CLS_MIXED:
                    lane = lax.broadcasted_iota(jnp.int32, y.shape, 1)
                    y = jnp.where(lane < 2 * HEAD_DIM, y, p)
            p_ref[rows, cols] = y.astype(BF16)
            f_ref[rows, cols] = y

        prev, p_prev = units[0], main(units[0])
        for unit in units[1:]:
            p_next = main(unit)
            finish(prev, p_prev)
            prev, p_prev = unit, p_next
        finish(prev, p_prev)

    modes = (CLS_PLAIN, CLS_NORM, CLS_ROPE) if rope else (CLS_PLAIN, CLS_NORM)
    for mode in modes:
        cond = (cls == mode)
        if not rope and mode == CLS_NORM:
            cond = (cls == CLS_NORM) | (cls == CLS_ROPE)

        @pl.when(cond)
        def _(mode=mode):
            run_tile(mode, (0, 1))

    @pl.when(cls == CLS_MIXED)
    def _():
        run_tile(CLS_MIXED, (0,))
        hi = slice(PROJ_HALF, PROJ_TILE)
        p_ref[:, hi] = jnp.zeros((tm, PROJ_HALF), BF16)
        f_ref[:, hi] = jnp.zeros((tm, PROJ_HALF), F32)

    for name, c_ref in zip(CACHE_NAMES, cache_refs):
        tile, off, width = _CACHE_SRC[name]

        @pl.when(j == TILE_NAMES.index(tile))
        def _(c_ref=c_ref, off=off, width=width):
            for b in range(tm // seq):
                c_ref[b, 0] = f_ref[b * seq:(b + 1) * seq, off:off + width]


def _in_projection(x, mods_l, mod_row, norm_w, w_in_bf, layer, tab, gain, hsum, rope_tabs, caches, *, tm, seq):
    n_tok = x.shape[0]
    rope = rope_tabs is not None
    caches = [] if caches is None else list(caches)
    n_cache = len(caches)

    in_specs = [
        pl.BlockSpec((tm, D_MODEL), lambda i, j, t: (i, 0)),
        pl.BlockSpec((1, 1, D_MODEL), lambda i, j, t: (mod_row(i) * MOD_CHUNKS + 0, 0, 0)),
        pl.BlockSpec((1, 1, D_MODEL), lambda i, j, t: (mod_row(i) * MOD_CHUNKS + 1, 0, 0)),
        pl.BlockSpec((1, D_MODEL), lambda i, j, t: (0, 0)),
        pl.BlockSpec((1, D_MODEL, PROJ_HALF), lambda i, j, t: (layer, 0, t[j])),
        pl.BlockSpec((1, D_MODEL, PROJ_HALF), lambda i, j, t: (layer, 0, t[N_TILES + j])),
        pl.BlockSpec((1, 1, PROJ_TILE), lambda i, j, t: (j, 0, 0)),
        pl.BlockSpec((PROJ_HALF, PROJ_HALF), lambda i, j, t: (0, 0)),
    ]
    args = [x, mods_l, mods_l, norm_w.reshape(1, D_MODEL), w_in_bf, w_in_bf, gain, hsum]
    if rope:
        cos_t, sin_t = rope_tabs
        in_specs += [pl.BlockSpec((tm, PROJ_HALF), lambda i, j, t: (0, 0)),
                     pl.BlockSpec((tm, PROJ_HALF), lambda i, j, t: (0, 0))]
        args += [cos_t, sin_t]
    n_in = len(args)
    in_specs += [pl.BlockSpec(memory_space=pl.ANY)] * n_cache
    args += caches
    out_specs = [
        pl.BlockSpec((tm, PROJ_TILE), lambda i, j, t: (i, jnp.maximum(j - P_FIRST, 0))),
        pl.BlockSpec((tm, PROJ_TILE), lambda i, j, t: (i, jnp.minimum(j, F_TILES))),
    ]
    out_shape = [jax.ShapeDtypeStruct((n_tok, P_TILES * PROJ_TILE), BF16),
                 jax.ShapeDtypeStruct((n_tok, (F_TILES + 1) * PROJ_TILE), F32)]
    for arr in caches:
        width = arr.shape[-1]
        out_specs.append(pl.BlockSpec((tm // seq, 1, seq, width), lambda i, j, t: (i, layer, 0, 0)))
        out_shape.append(jax.ShapeDtypeStruct(arr.shape, arr.dtype))
    res = pl.pallas_call(
        functools.partial(_inproj_kernel, tm=tm, rope=rope, n_cache=n_cache, seq=seq),
        out_shape=out_shape,
        grid_spec=pltpu.PrefetchScalarGridSpec(
            num_scalar_prefetch=1,
            grid=(n_tok // tm, N_TILES),
            in_specs=in_specs,
            out_specs=out_specs,
            scratch_shapes=[pltpu.VMEM((tm, D_MODEL), BF16)],
        ),
        input_output_aliases={1 + n_in + k: 2 + k for k in range(n_cache)},
        compiler_params=_params("arbitrary", "arbitrary"),
        name="in_projection_rope" if rope else "in_projection",
    )(tab, *args)
    return res[0], res[1], list(res[2:])


def _softmax_pv(segs, extra=None):
    m = functools.reduce(jnp.maximum, [jnp.max(s, axis=-1, keepdims=True) for s, _ in segs])
    if extra is not None:
        m = jnp.maximum(m, extra)
    den = None
    o = None
    for s, v in segs:
        e = jnp.exp(s - m)
        de = jnp.sum(e, axis=-1, keepdims=True)
        oe = _dot(e.astype(BF16), v)
        den = de if den is None else den + de
        o = oe if o is None else o + oe
    if extra is not None:
        den = den + jnp.exp(extra - m)
    return o / den


def _probs(scores):
    m = functools.reduce(jnp.maximum, [jnp.max(s, axis=-1, keepdims=True) for s in scores])
    es = [jnp.exp(s - m) for s in scores]
    den = functools.reduce(lambda a, b: a + b, [jnp.sum(e, axis=-1, keepdims=True) for e in es])
    inv = 1.0 / den
    return [e * inv for e in es]


def _diff_lambda_val(dl_ref, lam_init):
    lp = dl_ref[...]
    s1 = jnp.sum(lp[0:1] * lp[1:2], axis=-1, keepdims=True)
    s2 = jnp.sum(lp[2:3] * lp[3:4], axis=-1, keepdims=True)
    return jnp.exp(s1) - jnp.exp(s2) + lam_init


def _diff_finish(o, subln_ref, lam_init):
    ms = jnp.mean(o * o, axis=-1, keepdims=True)
    return o * lax.rsqrt(ms + NORM_EPS) * subln_ref[...] * (1.0 - lam_init)


def _head(ref, h, width=HEAD_DIM, rows=None):
    if rows is None:
        return ref[0, :, h * width:(h + 1) * width]
    return ref[0, rows, h * width:(h + 1) * width]


def _ctx_attn_kernel(sink_ref, sq_ref, skv_ref, nq_ref, nk_ref, nv_ref, dq_ref, dk_ref, dv_ref,
                     dl_ref, subln_ref, o_ref, *, t, lam_init):
    outs = []
    for hk in range(SWA_KV_HEADS):
        k = _head(skv_ref, hk)
        v = _head(skv_ref, SWA_KV_HEADS + hk)
        qs = jnp.concatenate([_head(sq_ref, hk * SWA_GROUP + g) for g in range(SWA_GROUP)], axis=0)
        sink = jnp.concatenate(
            [jnp.full((t, 1), sink_ref[hk * SWA_GROUP + g], F32) for g in range(SWA_GROUP)], axis=0)
        o = _softmax_pv([(_dot_nt(qs, k), v)], extra=sink)
        outs += [o[g * t:(g + 1) * t] for g in range(SWA_GROUP)]
    o_ref[0, :, 0:GROUP_W] = jnp.concatenate(outs, axis=1).astype(BF16)
    outs = []
    for h in range(NA_HEADS):
        outs.append(_softmax_pv([(_dot_nt(_head(nq_ref, h), _head(nk_ref, h)), _head(nv_ref, h))]))
    o_ref[0, :, GROUP_W:2 * GROUP_W] = jnp.concatenate(outs, axis=1).astype(BF16)
    lam = _diff_lambda_val(dl_ref, lam_init)
    for h in range(DIFF_HEADS):
        ps = [_probs([_dot_nt(_head(dq_ref, 2 * h + i), _head(dk_ref, 2 * h + i))])[0] for i in range(2)]
        pd = ps[0] - lam * ps[1]
        o = _dot(pd.astype(BF16), _head(dv_ref, h, DIFF_VDIM))
        o_ref[0, :, 1024 + h * 128: 1024 + (h + 1) * 128] = _diff_finish(o, subln_ref, lam_init).astype(BF16)


def _pcol(name, t):
    return pl.BlockSpec((1, t, PROJ_TILE), lambda i, c=P_COL[name]: (i, 0, c))


def _ctx_attention(p, sink, dlam, subln, lam_init, *, nb, t):
    p3 = p.reshape(nb, t, P_TILES * PROJ_TILE)
    names = ("sq", "skv", "nq", "nk", "nv", "dq", "dk", "dv")
    out = pl.pallas_call(
        functools.partial(_ctx_attn_kernel, t=t, lam_init=lam_init),
        out_shape=jax.ShapeDtypeStruct((nb, t, 3 * GROUP_W), BF16),
        grid=(nb,),
        in_specs=[pl.BlockSpec(memory_space=pltpu.SMEM)] + [_pcol(n, t) for n in names] + [
            pl.BlockSpec((4, HEAD_DIM), lambda i: (0, 0)),
            pl.BlockSpec((1, DIFF_VDIM), lambda i: (0, 0)),
        ],
        out_specs=pl.BlockSpec((1, t, 3 * GROUP_W), lambda i: (i, 0, 0)),
        compiler_params=_params("arbitrary"),
        name="ctx_attention",
    )(sink, *([p3] * len(names)), dlam, subln.reshape(1, DIFF_VDIM))
    return out.reshape(nb * t, 3 * GROUP_W)


def _swa_kernel(sink_ref, q_ref, kv_ref, ck_ref, cv_ref, o_ref, *, t, qb):
    n = pl.program_id(1)
    span = 3 * qb
    start = pl.multiple_of(jnp.clip((n - 1) * qb, 0, t - span), qb)
    row = (lax.broadcasted_iota(jnp.int32, (SWA_GROUP * qb, span), 0) & (qb - 1)) + n * qb
    col = lax.broadcasted_iota(jnp.int32, (SWA_GROUP * qb, span), 1) + start
    dist = row - col
    ok = (dist <= SWA_WINDOW) & (dist >= -SWA_WINDOW)
    win = pl.ds(start, span)
    outs = []
    for hk in range(SWA_KV_HEADS):
        kw = _head(kv_ref, hk, rows=win)
        vw = _head(kv_ref, SWA_KV_HEADS + hk, rows=win)
        ck = _head(ck_ref, hk)
        cv = _head(cv_ref, hk)
        qs = jnp.concatenate([_head(q_ref, hk * SWA_GROUP + g) for g in range(SWA_GROUP)], axis=0)
        sink = jnp.concatenate(
            [jnp.full((qb, 1), sink_ref[hk * SWA_GROUP + g], F32) for g in range(SWA_GROUP)], axis=0)
        s_loc = jnp.where(ok, _dot_nt(qs, kw), NEG_INF)
        s_ctx = _dot_nt(qs, ck)
        o = _softmax_pv([(s_loc, vw), (s_ctx, cv)], extra=sink)
        outs += [o[g * qb:(g + 1) * qb] for g in range(SWA_GROUP)]
    o_ref[0] = jnp.concatenate(outs, axis=1).astype(BF16)


def _swa_latent(p, ck, cv, sink, *, nb, t):
    qb = 128
    p3 = p.reshape(nb, t, P_TILES * PROJ_TILE)
    out = pl.pallas_call(
        functools.partial(_swa_kernel, t=t, qb=qb),
        out_shape=jax.ShapeDtypeStruct((nb, t, GROUP_W), BF16),
        grid=(nb, t // qb),
        in_specs=[
            pl.BlockSpec(memory_space=pltpu.SMEM),
            pl.BlockSpec((1, qb, PROJ_TILE), lambda b, n: (b, n, P_COL["sq"])),
            pl.BlockSpec((1, t, PROJ_TILE), lambda b, n: (b, 0, P_COL["skv"])),
            pl.BlockSpec((1,) + ck.shape[1:], lambda b, n: (b, 0, 0)),
            pl.BlockSpec((1,) + cv.shape[1:], lambda b, n: (b, 0, 0)),
        ],
        out_specs=pl.BlockSpec((1, qb, GROUP_W), lambda b, n: (b, n, 0)),
        compiler_params=_params("arbitrary", "arbitrary"),
        name="swa_latent",
    )(sink, p3, p3, ck, cv)
    return out.reshape(nb * t, GROUP_W)


def _na_row_start(r, rows):
    kh = min(NA_KH, rows)
    return min(max(r - kh // 2, 0), rows - kh)


def _na_win_start(qt, rows):
    return min(max(_na_row_start(2 * qt, rows), 0), rows - NA_WIN_ROWS)


def _na_bias_kernel(rpb_ref, o_ref, *, rows):
    h = pl.program_id(0)
    n_dr, n_dc = 2 * NA_KH - 1, 2 * NA_KW - 1
    qi = lax.broadcasted_iota(jnp.int32, (GRID_W, GRID_W), 0)
    ki = lax.broadcasted_iota(jnp.int32, (GRID_W, GRID_W), 1)
    dc = jnp.clip(ki - qi + (NA_KW - 1), 0, n_dc - 1)
    cs = jnp.clip(qi - NA_KW // 2, 0, GRID_W - NA_KW)
    col_ok = (ki >= cs) & (ki < cs + NA_KW)
    neg = jnp.full((GRID_W, GRID_W), NEG_INF, F32)
    tabs = []
    for dr in range(n_dr):
        acc = jnp.zeros((GRID_W, GRID_W), F32)
        for c in range(n_dc):
            acc = jnp.where(dc == c, rpb_ref[(h * n_dr + dr) * n_dc + c], acc)
        tabs.append(jnp.where(col_ok, acc, neg))
    kh = min(NA_KH, rows)
    for qt in range(rows // 2):
        ws = _na_win_start(qt, rows)
        bands = []
        for qq in range(2):
            qr = 2 * qt + qq
            rs = _na_row_start(qr, rows)
            blks = []
            for kk in range(NA_WIN_ROWS):
                kr = ws + kk
                blks.append(tabs[kr - qr + NA_KH - 1] if rs <= kr < rs + kh else neg)
            bands.append(jnp.concatenate(blks, axis=1))
        o_ref[0, qt] = jnp.concatenate(bands, axis=0)


def _na_bias(rpb, rows):
    n_qt = rows // 2
    return pl.pallas_call(
        functools.partial(_na_bias_kernel, rows=rows),
        out_shape=jax.ShapeDtypeStruct((NA_HEADS, n_qt, NA_QT, NA_WIN), F32),
        grid=(NA_HEADS,),
        in_specs=[pl.BlockSpec(memory_space=pltpu.SMEM)],
        out_specs=pl.BlockSpec((1, n_qt, NA_QT, NA_WIN), lambda h: (h, 0, 0, 0)),
        compiler_params=_params("arbitrary"),
        name="na_bias",
    )(rpb.reshape(-1))


def _na_kernel(q_ref, k_ref, v_ref, ck_ref, cv_ref, bias_ref, o_ref, *, rows):
    qt = pl.program_id(1)
    ws = jnp.clip(jnp.clip(2 * qt - NA_KH // 2, 0, rows - NA_KH), 0, rows - NA_WIN_ROWS)
    win = pl.ds(pl.multiple_of(ws * GRID_W, GRID_W), NA_WIN)
    outs = []
    for h in range(NA_HEADS):
        q = _head(q_ref, h)
        s_loc = _dot_nt(q, _head(k_ref, h, rows=win)) + bias_ref[h, 0]
        s_ctx = _dot_nt(q, _head(ck_ref, h))
        outs.append(_softmax_pv([(s_loc, _head(v_ref, h, rows=win)), (s_ctx, _head(cv_ref, h))]))
    o_ref[0] = jnp.concatenate(outs, axis=1).astype(BF16)


def _na_latent(p, ck, cv, bias, *, nb, t):
    rows = t // GRID_W
    p3 = p.reshape(nb, t, P_TILES * PROJ_TILE)
    out = pl.pallas_call(
        functools.partial(_na_kernel, rows=rows),
        out_shape=jax.ShapeDtypeStruct((nb, t, GROUP_W), BF16),
        grid=(nb, t // NA_QT),
        in_specs=[
            pl.BlockSpec((1, NA_QT, PROJ_TILE), lambda i, n: (i, n, P_COL["nq"])),
            pl.BlockSpec((1, t, PROJ_TILE), lambda i, n: (i, 0, P_COL["nk"])),
            pl.BlockSpec((1, t, PROJ_TILE), lambda i, n: (i, 0, P_COL["nv"])),
            pl.BlockSpec((1,) + ck.shape[1:], lambda i, n: (i, 0, 0)),
            pl.BlockSpec((1,) + cv.shape[1:], lambda i, n: (i, 0, 0)),
            pl.BlockSpec((NA_HEADS, 1, NA_QT, NA_WIN), lambda i, n: (0, n, 0, 0)),
        ],
        out_specs=pl.BlockSpec((1, NA_QT, GROUP_W), lambda i, n: (i, n, 0)),
        compiler_params=_params("arbitrary", "arbitrary"),
        name="na_latent",
    )(p3, p3, p3, ck, cv, bias)
    return out.reshape(nb * t, GROUP_W)


def _diff_kernel(q_ref, k_ref, v_ref, ck_ref, cv_ref, dl_ref, subln_ref, o_ref, *, lam_init):
    lam = _diff_lambda_val(dl_ref, lam_init)
    for h in range(DIFF_HEADS):
        pl_, pc_ = [], []
        for i in range(2):
            q = _head(q_ref, 2 * h + i)
            p_loc, p_ctx = _probs([_dot_nt(q, _head(k_ref, 2 * h + i)), _dot_nt(q, _head(ck_ref, 2 * h + i))])
            pl_.append(p_loc)
            pc_.append(p_ctx)
        pd_loc = pl_[0] - lam * pl_[1]
        pd_ctx = pc_[0] - lam * pc_[1]
        o = (_dot(pd_loc.astype(BF16), _head(v_ref, h, DIFF_VDIM))
             + _dot(pd_ctx.astype(BF16), _head(cv_ref, h, DIFF_VDIM)))
        o_ref[0, :, h * 128:(h + 1) * 128] = _diff_finish(o, subln_ref, lam_init).astype(BF16)


def _diff_latent(p, ck, cv, dlam, subln, lam_init, *, nb, t):
    tq = 256
    p3 = p.reshape(nb, t, P_TILES * PROJ_TILE)
    out = pl.pallas_call(
        functools.partial(_diff_kernel, lam_init=lam_init),
        out_shape=jax.ShapeDtypeStruct((nb, t, GROUP_W), BF16),
        grid=(nb, t // tq),
        in_specs=[
            pl.BlockSpec((1, tq, PROJ_TILE), lambda i, n: (i, n, P_COL["dq"])),
            pl.BlockSpec((1, t, PROJ_TILE), lambda i, n: (i, 0, P_COL["dk"])),
            pl.BlockSpec((1, t, PROJ_TILE), lambda i, n: (i, 0, P_COL["dv"])),
            pl.BlockSpec((1,) + ck.shape[1:], lambda i, n: (i, 0, 0)),
            pl.BlockSpec((1,) + cv.shape[1:], lambda i, n: (i, 0, 0)),
            pl.BlockSpec((4, HEAD_DIM), lambda i, n: (0, 0)),
            pl.BlockSpec((1, DIFF_VDIM), lambda i, n: (0, 0)),
        ],
        out_specs=pl.BlockSpec((1, tq, GROUP_W), lambda i, n: (i, n, 0)),
        compiler_params=_params("arbitrary", "arbitrary"),
        name="diff_latent",
    )(p3, p3, p3, ck, cv, dlam, subln.reshape(1, DIFF_VDIM))
    return out.reshape(nb * t, GROUP_W)


def _lru_kernel(x_ref, g_ref, cw_ref, cb_ref, w_ref, gb_ref, lam_ref, h0_ref, y_ref, st_ref, *, t):
    x = x_ref[0]
    row = lax.broadcasted_iota(jnp.int32, (t, LRU_CT), 0)
    u = cb_ref[...] + cw_ref[2:3, :] * x
    for tap, off in ((0, -2), (1, -1), (3, 1)):
        xs = pltpu.roll(x, (-off) % t, 0)
        ok = (row + off >= 0) & (row + off < t)
        u = u + cw_ref[tap:tap + 1, :] * jnp.where(ok, xs, 0.0)
    gates = _dot(u.astype(BF16), w_ref[0]) + gb_ref[0]
    total = None
    for d in range(2):
        r = _sigmoid(gates[:, (2 * d) * LRU_CT:(2 * d + 1) * LRU_CT])
        ig = _sigmoid(gates[:, (2 * d + 1) * LRU_CT:(2 * d + 2) * LRU_CT])
        nl = -lam_ref[d:d + 1, :]
        softplus = jnp.maximum(nl, 0.0) + jnp.log1p(jnp.exp(-jnp.abs(nl)))
        log_a = -LRU_C * r * softplus
        a = jnp.exp(log_a)
        bx = jnp.sqrt(jnp.tanh(-log_a) * (a * a + 1.0)) * (ig * u)
        edge = t - 1 if d == 1 else 0
        bx = jnp.where(row == edge, bx + a * h0_ref[0, d:d + 1, :], bx)
        s = 1
        while s < t:
            if d == 0:
                ok = row >= s
                a_sh = jnp.where(ok, pltpu.roll(a, s, 0), 1.0)
                b_sh = jnp.where(ok, pltpu.roll(bx, s, 0), 0.0)
            else:
                ok = row < t - s
                a_sh = jnp.where(ok, pltpu.roll(a, t - s, 0), 1.0)
                b_sh = jnp.where(ok, pltpu.roll(bx, t - s, 0), 0.0)
            bx = a * b_sh + bx
            a = a * a_sh
            s *= 2
        fin = t - 1 if d == 0 else 0
        st_ref[0, d:d + 1, :] = bx[fin:fin + 1, :]
        total = bx if total is None else total + bx
    g = g_ref[0]
    gelu = 0.5 * g * (1.0 + jnp.tanh(math.sqrt(2.0 / math.pi) * (g + 0.044715 * (g * g * g))))
    y_ref[0] = (total * gelu).astype(BF16)


def _lru_mixer(f, conv_w, conv_b, w_gates, b_gates, lam, h0, *, nb, t):
    f3 = f.reshape(nb, t, f.shape[1])
    nct = LRU_WIDTH // LRU_CT
    y, st = pl.pallas_call(
        functools.partial(_lru_kernel, t=t),
        out_shape=[jax.ShapeDtypeStruct((nb, t, LRU_WIDTH), BF16),
                   jax.ShapeDtypeStruct((nb, 2, LRU_WIDTH), F32)],
        grid=(nb, nct),
        in_specs=[
            pl.BlockSpec((1, t, LRU_CT), lambda i, c: (i, 0, c)),
            pl.BlockSpec((1, t, LRU_CT), lambda i, c: (i, 0, nct + c)),
            pl.BlockSpec((CONV_W, LRU_CT), lambda i, c: (0, c)),
            pl.BlockSpec((1, LRU_CT), lambda i, c: (0, c)),
            pl.BlockSpec((1, LRU_CT, 4 * LRU_CT), lambda i, c: (c, 0, 0)),
            pl.BlockSpec((1, 1, 4 * LRU_CT), lambda i, c: (c, 0, 0)),
            pl.BlockSpec((2, LRU_CT), lambda i, c: (0, c)),
            pl.BlockSpec((1, 2, LRU_CT), lambda i, c: (i, 0, c)),
        ],
        out_specs=[pl.BlockSpec((1, t, LRU_CT), lambda i, c: (i, 0, c)),
                   pl.BlockSpec((1, 2, LRU_CT), lambda i, c: (i, 0, c))],
        compiler_params=_params("arbitrary", "arbitrary"),
        name="lru_mixer",
    )(f3, f3, conv_w, conv_b.reshape(1, LRU_WIDTH), w_gates, b_gates, lam, h0)
    return y.reshape(nb * t, LRU_WIDTH), st


def _lru_gate_weights(wa, ba, wx, bx):
    nct = LRU_WIDTH // LRU_CT
    bpt = LRU_CT // LRU_BLOCK_W
    eye = jnp.eye(bpt, dtype=F32)

    def tile_w(w):
        w4 = w.reshape(nct, bpt, LRU_BLOCK_W, LRU_BLOCK_W)
        return jnp.einsum("cnij,nm->cnimj", w4, eye).reshape(nct, LRU_CT, LRU_CT)

    w = jnp.concatenate([tile_w(wa[0]), tile_w(wx[0]), tile_w(wa[1]), tile_w(wx[1])], axis=-1)
    b = jnp.concatenate([v.reshape(nct, 1, LRU_CT) for v in (ba[0], bx[0], ba[1], bx[1])], axis=-1)
    return w.astype(BF16), b


def _outproj_kernel(*refs, n_in):
    ins = refs[:n_in]
    w_ref, x_ref, g_ref, o_ref = refs[n_in:]
    acc = None
    off = 0
    for m_ref in ins:
        width = m_ref.shape[1]
        part = _dot(m_ref[...], w_ref[0, off:off + width, :])
        acc = part if acc is None else acc + part
        off += width
    o_ref[...] = x_ref[...] + g_ref[0] * acc


def _out_projection(mixes, w_out_bf, layer, x, mods_l, mod_row, *, tm):
    n_tok = x.shape[0]
    return pl.pallas_call(
        functools.partial(_outproj_kernel, n_in=len(mixes)),
        out_shape=jax.ShapeDtypeStruct((n_tok, D_MODEL), F32),
        grid=(n_tok // tm,),
        in_specs=[pl.BlockSpec((tm, m.shape[1]), lambda i: (i, 0)) for m in mixes] + [
            pl.BlockSpec((1, D_MODEL, D_MODEL), lambda i: (layer, 0, 0)),
            pl.BlockSpec((tm, D_MODEL), lambda i: (i, 0)),
            pl.BlockSpec((1, 1, D_MODEL), lambda i: (mod_row(i) * MOD_CHUNKS + 2, 0, 0)),
        ],
        out_specs=pl.BlockSpec((tm, D_MODEL), lambda i: (i, 0)),
        compiler_params=_params("arbitrary"),
        name="out_projection",
    )(*mixes, w_out_bf, x, mods_l)


def _ffn_kernel(x_ref, sh_ref, sc_ref, g_ref, nw_ref, wg_ref, wu_ref, wd_ref, o_ref, h_scr, *, tm):
    k = pl.program_id(1)

    @pl.when(k == 0)
    def _():
        _modnorm_rows(x_ref, nw_ref, sh_ref, sc_ref, h_scr, tm)
        o_ref[...] = jnp.zeros_like(o_ref)

    wg = wg_ref[0].astype(BF16)
    wu = wu_ref[0].astype(BF16)
    wd = wd_ref[0].astype(BF16)
    for c in range(tm // FFN_ROW_CHUNK):
        rows = slice(c * FFN_ROW_CHUNK, (c + 1) * FFN_ROW_CHUNK)
        h = h_scr[rows, :]
        g = _dot(h, wg)
        u = _dot(h, wu)
        act = (g * _sigmoid(g)) * u
        o_ref[rows, :] += _dot(act.astype(BF16), wd)

    @pl.when(k == pl.num_programs(1) - 1)
    def _():
        def body(c, carry):
            r = pl.multiple_of(c * 128, 128)
            o_ref[pl.ds(r, 128), :] = x_ref[pl.ds(r, 128), :] + g_ref[0] * o_ref[pl.ds(r, 128), :]
            return carry
        lax.fori_loop(0, tm // 128, body, 0)


def _ffn(x, mods_l, mod_row, norm_w, wg, wu, wd, layer, *, tm, th):
    n_tok = x.shape[0]
    return pl.pallas_call(
        functools.partial(_ffn_kernel, tm=tm),
        out_shape=jax.ShapeDtypeStruct((n_tok, D_MODEL), F32),
        grid=(n_tok // tm, FFN_HIDDEN // th),
        in_specs=[
            pl.BlockSpec((tm, D_MODEL), lambda i, k: (i, 0), pipeline_mode=pl.Buffered(1)),
            pl.BlockSpec((1, 1, D_MODEL), lambda i, k: (mod_row(i) * MOD_CHUNKS + 3, 0, 0)),
            pl.BlockSpec((1, 1, D_MODEL), lambda i, k: (mod_row(i) * MOD_CHUNKS + 4, 0, 0)),
            pl.BlockSpec((1, 1, D_MODEL), lambda i, k: (mod_row(i) * MOD_CHUNKS + 5, 0, 0)),
            pl.BlockSpec((1, D_MODEL), lambda i, k: (0, 0)),
            pl.BlockSpec((1, D_MODEL, th), lambda i, k: (layer, 0, k)),
            pl.BlockSpec((1, D_MODEL, th), lambda i, k: (layer, 0, k)),
            pl.BlockSpec((1, th, D_MODEL), lambda i, k: (layer, k, 0)),
        ],
        out_specs=pl.BlockSpec((tm, D_MODEL), lambda i, k: (i, 0)),
        scratch_shapes=[pltpu.VMEM((tm, D_MODEL), BF16)],
        compiler_params=_params("arbitrary", "arbitrary"),
        name="ffn",
    )(x, mods_l, mods_l, mods_l, norm_w.reshape(1, D_MODEL), wg, wu, wd)


def _rope_tables(t_lat):
    nf = HEAD_DIM // 4
    tok = jnp.arange(t_lat)
    pos = jnp.stack([tok // GRID_W, tok % GRID_W], axis=-1).astype(F32)
    inv = ROPE_BASE ** (-jnp.arange(nf, dtype=F32) / nf)
    ang = pos[:, :, None] * inv
    cos, sin = jnp.cos(ang), jnp.sin(ang)
    cos_h = jnp.concatenate([cos[:, 0], cos[:, 0], cos[:, 1], cos[:, 1]], axis=-1)
    sin_h = jnp.concatenate([-sin[:, 0], sin[:, 0], -sin[:, 1], sin[:, 1]], axis=-1)
    reps = PROJ_HALF // HEAD_DIM
    return jnp.tile(cos_h, (1, reps)), jnp.tile(sin_h, (1, reps))


def _tile_table():
    lo = [_FIRST_BLOCK[n] for n in TILE_NAMES]
    hi = [b + 1 for b in lo]
    cls = [_TILE_CLASS[n] for n in TILE_NAMES]
    return jnp.asarray(np.array(lo + hi + cls, np.int32))


def _tile_gains(qk_gain_l):
    def tiled(g, n):
        return jnp.tile(g, n // HEAD_DIM)
    ones = jnp.ones((PROJ_TILE,), F32)
    per_tile = {
        "lx": ones, "lg": ones, "nv": ones, "dv": ones,
        "nk": tiled(qk_gain_l[1, 1], PROJ_TILE), "dk": tiled(qk_gain_l[2, 1], PROJ_TILE),
        "skv": jnp.concatenate([tiled(qk_gain_l[0, 1], 128), jnp.ones((PROJ_TILE - 128,), F32)]),
        "sq": tiled(qk_gain_l[0, 0], PROJ_TILE) * QK_SCALE,
        "nq": tiled(qk_gain_l[1, 0], PROJ_TILE) * QK_SCALE,
        "dq": tiled(qk_gain_l[2, 0], PROJ_TILE) * QK_SCALE,
    }
    return jnp.stack([per_tile[n] for n in TILE_NAMES]).reshape(N_TILES, 1, PROJ_TILE)


def _lambda_init(layer):
    return 0.8 - 0.6 * math.exp(-0.3 * layer)


def kernel(x_prompt, x_sample, cache_swa_k, cache_swa_v, cache_na_k, cache_na_v, cache_diff_k, cache_diff_v, state_lru, c, c_ctx, norm_mix, norm_ffn, w_mod, b_mod, w_in, w_out, qk_gain, swa_sink, na_rpb, diff_lambda, diff_subln, conv_w, conv_b, lru_wa, lru_ba, lru_wx, lru_bx, lru_L, w_ffn_gate, w_ffn_up, w_ffn_down):
    bc, s_ctx, _ = x_prompt.shape
    bd, t_lat, _ = x_sample.shape
    n_ctx, n_lat = bc * s_ctx, bd * t_lat
    rows = t_lat // GRID_W
    p_ctx = cache_swa_k.shape[2]

    cvecs = jnp.concatenate([c_ctx[None, :], c, jnp.zeros((8 - 1 - bd, D_MODEL), F32)], axis=0)
    mods = _modulation(cvecs, w_mod, b_mod).reshape(DEPTH, 8 * MOD_CHUNKS, 1, D_MODEL)

    hsum = jnp.asarray(np.kron(np.eye(PROJ_HALF // HEAD_DIM, dtype=np.float32),
                               np.full((HEAD_DIM, HEAD_DIM), 1.0 / HEAD_DIM, np.float32)), BF16)
    rope_tabs = _rope_tables(t_lat)
    tab = _tile_table()

    tm = 1024
    tm_out = 512
    ctx_row = lambda tile: (lambda i: 0)
    lat_row = lambda tile: (lambda i: 1 + (i * tile) // t_lat)

    xc = x_prompt.reshape(n_ctx, D_MODEL)
    xs = x_sample.reshape(n_lat, D_MODEL)
    zero_state = jnp.zeros((bc, 2, LRU_WIDTH), F32)
    caches = [jnp.zeros((bc, DEPTH, s_ctx, _CACHE_SRC[name][2]), F32) for name in CACHE_NAMES]
    lru_states = []
    w_in_bf = w_in.astype(BF16)
    w_out_bf = w_out.astype(BF16)

    def cached(arr, l):
        return arr[:, l].reshape(bd, p_ctx, -1).astype(BF16)

    for l in range(DEPTH):
        lam_init = _lambda_init(l)
        mods_l = mods[l]
        gain = _tile_gains(qk_gain[l])
        w_gates, b_gates = _lru_gate_weights(lru_wa[l], lru_ba[l], lru_wx[l], lru_bx[l])
        ffn_w = (w_ffn_gate, w_ffn_up, w_ffn_down, l)

        p, f, caches = _in_projection(xc, mods_l, ctx_row(tm), norm_mix[l], w_in_bf, l, tab, gain, hsum,
                                      None, caches, tm=tm, seq=s_ctx)
        mix3 = _ctx_attention(p, swa_sink[l], diff_lambda[l], diff_subln[l], lam_init, nb=bc, t=s_ctx)
        od, st = _lru_mixer(f, conv_w[l], conv_b[l], w_gates, b_gates, lru_L[l], zero_state, nb=bc, t=s_ctx)
        lru_states.append(st)
        xc = _out_projection([mix3, od], w_out_bf, l, xc, mods_l, ctx_row(tm_out), tm=tm_out)
        xc = _ffn(xc, mods_l, ctx_row(tm), norm_ffn[l], *ffn_w, tm=tm, th=512)

        p, f, _ = _in_projection(xs, mods_l, lat_row(tm), norm_mix[l], w_in_bf, l, tab, gain, hsum,
                                 rope_tabs, None, tm=tm, seq=t_lat)
        oa = _swa_latent(p, cached(cache_swa_k, l), cached(cache_swa_v, l), swa_sink[l], nb=bd, t=t_lat)
        bias = _na_bias(na_rpb[l], rows)
        ob = _na_latent(p, cached(cache_na_k, l), cached(cache_na_v, l), bias, nb=bd, t=t_lat)
        oc = _diff_latent(p, cached(cache_diff_k, l), cached(cache_diff_v, l),
                          diff_lambda[l], diff_subln[l], lam_init, nb=bd, t=t_lat)
        od, _ = _lru_mixer(f, conv_w[l], conv_b[l], w_gates, b_gates, lru_L[l], state_lru[:, l], nb=bd, t=t_lat)
        xs = _out_projection([oa, ob, oc, od], w_out_bf, l, xs, mods_l, lat_row(tm_out), tm=tm_out)
        xs = _ffn(xs, mods_l, lat_row(tm), norm_ffn[l], *ffn_w, tm=tm, th=512)

    swa_k, swa_v, na_k, na_v, diff_k, diff_v = caches
    return (xc.reshape(bc, s_ctx, D_MODEL), xs.reshape(bd, t_lat, D_MODEL),
            swa_k.reshape(bc, DEPTH, s_ctx, SWA_KV_HEADS, HEAD_DIM),
            swa_v.reshape(bc, DEPTH, s_ctx, SWA_KV_HEADS, HEAD_DIM),
            na_k.reshape(bc, DEPTH, s_ctx, NA_HEADS, HEAD_DIM),
            na_v.reshape(bc, DEPTH, s_ctx, NA_HEADS, HEAD_DIM),
            diff_k.reshape(bc, DEPTH, s_ctx, DIFF_HEADS, 2, HEAD_DIM),
            diff_v.reshape(bc, DEPTH, s_ctx, DIFF_HEADS, DIFF_VDIM),
            jnp.stack(lru_states, axis=1))
```

```python
import functools
import math

import jax
import jax.numpy as jnp
import numpy as np
from jax import lax
from jax.experimental import pallas as pl
from jax.experimental.pallas import tpu as pltpu

F32 = jnp.float32
BF16 = jnp.bfloat16

D_MODEL = 2048
DEPTH = 2
GRID_W = 64
HEAD_DIM = 64
GROUP_W = 512
SWA_HEADS = 8
SWA_KV_HEADS = 2
SWA_GROUP = 4
SWA_WINDOW = 128
NA_HEADS = 8
NA_KH = 8
NA_KW = 16
DIFF_HEADS = 4
DIFF_VDIM = 128
LRU_WIDTH = 512
LRU_BLOCKS = 8
LRU_BLOCK_W = 64
LRU_C = 8.0
CONV_W = 4
FFN_HIDDEN = 5632
ROPE_BASE = 10000.0
NORM_EPS = 1e-6
NEG_INF = -1e30
MOD_CHUNKS = 6
PROJ_W = 4864
QK_SCALE = HEAD_DIM ** -0.5

V7X_VMEM_LIMIT = 60 * 1024 * 1024

PROJ_HALF = 256
PROJ_TILE = 2 * PROJ_HALF
TILE_NAMES = ("lx", "lg", "nk", "dk", "nv", "dv", "skv", "sq", "nq", "dq")
N_TILES = len(TILE_NAMES)
_FIRST_BLOCK = {"sq": 0, "skv": 2, "nq": 3, "nk": 5, "nv": 7, "dq": 9, "dk": 11, "dv": 13, "lx": 15, "lg": 17}
CLS_PLAIN, CLS_NORM, CLS_ROPE, CLS_MIXED = 0, 1, 2, 3
_TILE_CLASS = {"lx": CLS_PLAIN, "lg": CLS_PLAIN, "nv": CLS_PLAIN, "dv": CLS_PLAIN,
               "nk": CLS_NORM, "nq": CLS_NORM, "dk": CLS_ROPE, "sq": CLS_ROPE, "dq": CLS_ROPE,
               "skv": CLS_MIXED}
P_FIRST = 2
P_TILES = N_TILES - P_FIRST
P_COL = {name: TILE_NAMES.index(name) - P_FIRST for name in TILE_NAMES[P_FIRST:]}
F_TILES = 2

NA_QT = 128
NA_WIN_ROWS = 10
NA_WIN = NA_WIN_ROWS * GRID_W
LRU_CT = 128
ROW_CHUNK = 256
FFN_ROW_CHUNK = 512


def _sigmoid(x):
    return 1.0 / (1.0 + jnp.exp(-x))


def _dot(a, b):
    return jnp.dot(a, b, preferred_element_type=F32)


def _dot_nt(a, b):
    return lax.dot_general(a, b, (((1,), (1,)), ((), ())), preferred_element_type=F32)


def _params(*sem):
    return pltpu.CompilerParams(dimension_semantics=sem, vmem_limit_bytes=V7X_VMEM_LIMIT)


def _mod_kernel(c_ref, w_ref, b_ref, o_ref):
    cv = c_ref[...]
    s = cv * _sigmoid(cv)
    o_ref[0] = _dot(s.astype(BF16), w_ref[0].astype(BF16)) + b_ref[0]


def _modulation(cvecs, w_mod, b_mod):
    tn = 1024
    n = MOD_CHUNKS * D_MODEL
    return pl.pallas_call(
        _mod_kernel,
        out_shape=jax.ShapeDtypeStruct((DEPTH, 8, n), F32),
        grid=(DEPTH, n // tn),
        in_specs=[
            pl.BlockSpec((8, D_MODEL), lambda l, j: (0, 0)),
            pl.BlockSpec((1, D_MODEL, tn), lambda l, j: (l, 0, j)),
            pl.BlockSpec((1, 1, tn), lambda l, j: (l, 0, j)),
        ],
        out_specs=pl.BlockSpec((1, 8, tn), lambda l, j: (l, 0, j)),
        compiler_params=_params("arbitrary", "arbitrary"),
        name="modulation",
    )(cvecs, w_mod, b_mod.reshape(DEPTH, 1, n))


def _modnorm_rows(x_ref, nw_ref, sh_ref, sc_ref, h_scr, rows, chunk=128):
    def body(c, carry):
        r = pl.multiple_of(c * chunk, chunk)
        x = x_ref[pl.ds(r, chunk), :]
        ms = jnp.mean(x * x, axis=-1, keepdims=True)
        y = x * lax.rsqrt(ms + NORM_EPS) * nw_ref[...]
        h_scr[pl.ds(r, chunk), :] = (y * (1.0 + sc_ref[0]) + sh_ref[0]).astype(BF16)
        return carry
    lax.fori_loop(0, rows // chunk, body, 0)


CACHE_NAMES = ("swa_k", "swa_v", "na_k", "na_v", "diff_k", "diff_v")
_CACHE_SRC = {"swa_k": ("skv", 0, 128), "swa_v": ("skv", 128, 128), "na_k": ("nk", 0, 512),
              "na_v": ("nv", 0, 512), "diff_k": ("dk", 0, 512), "diff_v": ("dv", 0, 512)}


def _inproj_kernel(*refs, tm, rope, n_cache, seq):
    n_in = 10 if rope else 8
    tab_ref = refs[0]
    (x_ref, sh_ref, sc_ref, nw_ref, wlo_ref, whi_ref, gain_ref, hsum_ref) = refs[1:9]
    cos_ref, sin_ref = (refs[9], refs[10]) if rope else (None, None)
    outs = refs[1 + n_in + n_cache:]
    p_ref, f_ref = outs[0], outs[1]
    cache_refs = outs[2:2 + n_cache]
    h_scr = outs[2 + n_cache]
    j = pl.program_id(1)
    cls = tab_ref[2 * N_TILES + j]

    @pl.when(j == 0)
    def _():
        _modnorm_rows(x_ref, nw_ref, sh_ref, sc_ref, h_scr, tm)

    def run_tile(mode, halves):
        w = {half: (wlo_ref if half == 0 else whi_ref)[0] for half in halves}
        units = [(half, c) for half in halves for c in range(tm // ROW_CHUNK)]

        def main(unit):
            half, c = unit
            return _dot(h_scr[c * ROW_CHUNK:(c + 1) * ROW_CHUNK, :], w[half])

        def finish(unit, p):
            half, c = unit
            rows = slice(c * ROW_CHUNK, (c + 1) * ROW_CHUNK)
            cols = slice(half * PROJ_HALF, (half + 1) * PROJ_HALF)
            y = p
            if mode != CLS_PLAIN:
                ms = _dot((p * p).astype(BF16), hsum_ref[...])
                y = p * lax.rsqrt(ms + NORM_EPS) * gain_ref[0, :, cols]
                if rope and mode in (CLS_ROPE, CLS_MIXED):
                    lane = lax.broadcasted_iota(jnp.int32, y.shape, 1)
                    up = pltpu.roll(y, PROJ_HALF - 16, 1)
                    down = pltpu.roll(y, 16, 1)
                    partner = jnp.where((lane & 31) < 16, up, down)
                    y = y * cos_ref[rows, :] + partner * sin_ref[rows, :]
                if mode == CLS_MIXED:
                    lane = lax.broadcasted_iota(jnp.int32, y.shape, 1)
                    y = jnp.where(lane < 2 * HEAD_DIM, y, p)
            p_ref[rows, cols] = y.astype(BF16)
            f_ref[rows, cols] = y

        prev, p_prev = units[0], main(units[0])
        for unit in units[1:]:
            p_next = main(unit)
            finish(prev, p_prev)
            prev, p_prev = unit, p_next
        finish(prev, p_prev)

    modes = (CLS_PLAIN, CLS_NORM, CLS_ROPE) if rope else (CLS_PLAIN, CLS_NORM)
    for mode in modes:
        cond = (cls == mode)
        if not rope and mode == CLS_NORM:
            cond = (cls == CLS_NORM) | (cls == CLS_ROPE)

        @pl.when(cond)
        def _(mode=mode):
            run_tile(mode, (0, 1))

    @pl.when(cls == CLS_MIXED)
    def _():
        run_tile(CLS_MIXED, (0,))
        hi = slice(PROJ_HALF, PROJ_TILE)
        p_ref[:, hi] = jnp.zeros((tm, PROJ_HALF), BF16)
        f_ref[:, hi] = jnp.zeros((tm, PROJ_HALF), F32)

    for name, c_ref in zip(CACHE_NAMES, cache_refs):
        tile, off, width = _CACHE_SRC[name]

        @pl.when(j == TILE_NAMES.index(tile))
        def _(c_ref=c_ref, off=off, width=width):
            for b in range(tm // seq):
                c_ref[b, 0] = f_ref[b * seq:(b + 1) * seq, off:off + width]


def _in_projection(x, mods_l, mod_row, norm_w, w_in_bf, layer, tab, gain, hsum, rope_tabs, caches, *, tm, seq):
    n_tok = x.shape[0]
    rope = rope_tabs is not None
    caches = [] if caches is None else list(caches)
    n_cache = len(caches)

    in_specs = [
        pl.BlockSpec((tm, D_MODEL), lambda i, j, t: (i, 0)),
        pl.BlockSpec((1, 1, D_MODEL), lambda i, j, t: (mod_row(i) * MOD_CHUNKS + 0, 0, 0)),
        pl.BlockSpec((1, 1, D_MODEL), lambda i, j, t: (mod_row(i) * MOD_CHUNKS + 1, 0, 0)),
        pl.BlockSpec((1, D_MODEL), lambda i, j, t: (0, 0)),
        pl.BlockSpec((1, D_MODEL, PROJ_HALF), lambda i, j, t: (layer, 0, t[j])),
        pl.BlockSpec((1, D_MODEL, PROJ_HALF), lambda i, j, t: (layer, 0, t[N_TILES + j])),
        pl.BlockSpec((1, 1, PROJ_TILE), lambda i, j, t: (j, 0, 0)),
        pl.BlockSpec((PROJ_HALF, PROJ_HALF), lambda i, j, t: (0, 0)),
    ]
    args = [x, mods_l, mods_l, norm_w.reshape(1, D_MODEL), w_in_bf, w_in_bf, gain, hsum]
    if rope:
        cos_t, sin_t = rope_tabs
        in_specs += [pl.BlockSpec((tm, PROJ_HALF), lambda i, j, t: (0, 0)),
                     pl.BlockSpec((tm, PROJ_HALF), lambda i, j, t: (0, 0))]
        args += [cos_t, sin_t]
    n_in = len(args)
    in_specs += [pl.BlockSpec(memory_space=pl.ANY)] * n_cache
    args += caches
    out_specs = [
        pl.BlockSpec((tm, PROJ_TILE), lambda i, j, t: (i, jnp.maximum(j - P_FIRST, 0))),
        pl.BlockSpec((tm, PROJ_TILE), lambda i, j, t: (i, jnp.minimum(j, F_TILES))),
    ]
    out_shape = [jax.ShapeDtypeStruct((n_tok, P_TILES * PROJ_TILE), BF16),
                 jax.ShapeDtypeStruct((n_tok, (F_TILES + 1) * PROJ_TILE), F32)]
    for arr in caches:
        width = arr.shape[-1]
        out_specs.append(pl.BlockSpec((tm // seq, 1, seq, width), lambda i, j, t: (i, layer, 0, 0)))
        out_shape.append(jax.ShapeDtypeStruct(arr.shape, arr.dtype))
    res = pl.pallas_call(
        functools.partial(_inproj_kernel, tm=tm, rope=rope, n_cache=n_cache, seq=seq),
        out_shape=out_shape,
        grid_spec=pltpu.PrefetchScalarGridSpec(
            num_scalar_prefetch=1,
            grid=(n_tok // tm, N_TILES),
            in_specs=in_specs,
            out_specs=out_specs,
            scratch_shapes=[pltpu.VMEM((tm, D_MODEL), BF16)],
        ),
        input_output_aliases={1 + n_in + k: 2 + k for k in range(n_cache)},
        compiler_params=_params("arbitrary", "arbitrary"),
        name="in_projection_rope" if rope else "in_projection",
    )(tab, *args)
    return res[0], res[1], list(res[2:])


def _joint(scores):
    return scores[0] if len(scores) == 1 else jnp.concatenate(scores, axis=1)


def _softmax_pv(segs, extra=None):
    s = _joint([sc for sc, _ in segs])
    m = jnp.max(s, axis=-1, keepdims=True)
    if extra is not None:
        m = jnp.maximum(m, extra)
    e = jnp.exp(s - m)
    den = jnp.sum(e, axis=-1, keepdims=True)
    if extra is not None:
        den = den + jnp.exp(extra - m)
    e = e.astype(BF16)
    o, off = None, 0
    for sc, v in segs:
        n = sc.shape[1]
        part = _dot(e[:, off:off + n], v)
        o = part if o is None else o + part
        off += n
    return o / den


def _probs(scores):
    s = _joint(scores)
    e = jnp.exp(s - jnp.max(s, axis=-1, keepdims=True))
    return e * (1.0 / jnp.sum(e, axis=-1, keepdims=True))


def _half_vecs():
    lane = lax.broadcasted_iota(jnp.int32, (1, 2 * HEAD_DIM), 1)
    lo = (lane < HEAD_DIM).astype(BF16)
    return lo, (1.0 - lo).astype(BF16)


def _pair_block(ref, m, rows=None):
    if rows is None:
        return ref[0, :, m * 128:(m + 1) * 128]
    return ref[0, rows, m * 128:(m + 1) * 128]


def _merge_pair(o_even, o_odd):
    lane = lax.broadcasted_iota(jnp.int32, o_even.shape, 1)
    return jnp.where(lane < HEAD_DIM, o_even, o_odd)


def _diff_lambda_val(dl_ref, lam_init):
    lp = dl_ref[...]
    s1 = jnp.sum(lp[0:1] * lp[1:2], axis=-1, keepdims=True)
    s2 = jnp.sum(lp[2:3] * lp[3:4], axis=-1, keepdims=True)
    return jnp.exp(s1) - jnp.exp(s2) + lam_init


def _diff_finish(o, subln_ref, lam_init):
    ms = jnp.mean(o * o, axis=-1, keepdims=True)
    return o * lax.rsqrt(ms + NORM_EPS) * subln_ref[...] * (1.0 - lam_init)


def _swap_halves_bf16(x):
    return pltpu.roll(x.astype(F32), HEAD_DIM, 1).astype(BF16)


def _gqa_queries(q_ref, hk, vecs):
    parts = []
    for g in range(SWA_GROUP):
        h = hk * SWA_GROUP + g
        q = _pair_block(q_ref, h // 2)
        if h % 2 != hk:
            q = _swap_halves_bf16(q)
        parts.append(q * vecs[hk])
    return jnp.concatenate(parts, axis=0)


def _gqa_outputs(o, hk, rows):
    blocks = []
    for j in range(SWA_GROUP // 2):
        halves = []
        for g in (2 * j, 2 * j + 1):
            og = o[g * rows:(g + 1) * rows]
            halves.append(og if g % 2 == hk else pltpu.roll(og, HEAD_DIM, 1))
        blocks.append(_merge_pair(halves[0], halves[1]))
    return blocks


def _pipelined(items, depth=2):
    pending, outs = [], []
    for score_fn, finish_fn in items:
        pending.append((finish_fn, score_fn()))
        if len(pending) > depth:
            fn, s = pending.pop(0)
            outs.append(fn(s))
    for fn, s in pending:
        outs.append(fn(s))
    return outs


def _ctx_attn_kernel(sink_ref, sq_ref, skv_ref, nq_ref, nk_ref, nv_ref, dq_ref, dk_ref, dv_ref,
                     dl_ref, subln_ref, o_ref, *, t, lam_init):
    lam = _diff_lambda_val(dl_ref, lam_init)
    vecs = _half_vecs()

    def swa_item(hk):
        def scores():
            return _dot_nt(_gqa_queries(sq_ref, hk, vecs), _pair_block(skv_ref, 0))

        def finish(s):
            sink = jnp.concatenate(
                [jnp.full((t, 1), sink_ref[hk * SWA_GROUP + g], F32) for g in range(SWA_GROUP)], axis=0)
            return _gqa_outputs(_softmax_pv([(s, _pair_block(skv_ref, 1))], extra=sink), hk, t)
        return scores, finish

    def na_item(m):
        def scores():
            q, k = _pair_block(nq_ref, m), _pair_block(nk_ref, m)
            return [_dot_nt(q * vecs[i], k) for i in range(2)]

        def finish(s):
            v = _pair_block(nv_ref, m)
            return [_merge_pair(_softmax_pv([(s[0], v)]), _softmax_pv([(s[1], v)]))]
        return scores, finish

    def diff_item(h):
        def scores():
            q, k = _pair_block(dq_ref, h), _pair_block(dk_ref, h)
            return [_dot_nt(q * vecs[i], k) for i in range(2)]

        def finish(s):
            pd = _probs([s[0]]) - lam * _probs([s[1]])
            o = _dot(pd.astype(BF16), _pair_block(dv_ref, h))
            return [_diff_finish(o, subln_ref, lam_init)]
        return scores, finish

    items = ([swa_item(hk) for hk in range(SWA_KV_HEADS)] + [na_item(m) for m in range(NA_HEADS // 2)]
             + [diff_item(h) for h in range(DIFF_HEADS)])
    outs = [piece for res in _pipelined(items) for piece in res]
    o_ref[0] = jnp.concatenate(outs, axis=1).astype(BF16)


def _pcol(name, t):
    return pl.BlockSpec((1, t, PROJ_TILE), lambda i, c=P_COL[name]: (i, 0, c))


def _ctx_attention(p, sink, dlam, subln, lam_init, *, nb, t):
    p3 = p.reshape(nb, t, P_TILES * PROJ_TILE)
    names = ("sq", "skv", "nq", "nk", "nv", "dq", "dk", "dv")
    out = pl.pallas_call(
        functools.partial(_ctx_attn_kernel, t=t, lam_init=lam_init),
        out_shape=jax.ShapeDtypeStruct((nb, t, 3 * GROUP_W), BF16),
        grid=(nb,),
        in_specs=[pl.BlockSpec(memory_space=pltpu.SMEM)] + [_pcol(n, t) for n in names] + [
            pl.BlockSpec((4, HEAD_DIM), lambda i: (0, 0)),
            pl.BlockSpec((1, DIFF_VDIM), lambda i: (0, 0)),
        ],
        out_specs=pl.BlockSpec((1, t, 3 * GROUP_W), lambda i: (i, 0, 0)),
        compiler_params=_params("arbitrary"),
        name="ctx_attention",
    )(sink, *([p3] * len(names)), dlam, subln.reshape(1, DIFF_VDIM))
    return out.reshape(nb * t, 3 * GROUP_W)


def _swa_kernel(sink_ref, q_ref, kv_ref, ck_ref, cv_ref, o_ref, *, t, qb):
    n = pl.program_id(1)
    span = 3 * qb
    start = pl.multiple_of(jnp.clip((n - 1) * qb, 0, t - span), qb)
    row = (lax.broadcasted_iota(jnp.int32, (SWA_GROUP * qb, span), 0) & (qb - 1)) + n * qb
    col = lax.broadcasted_iota(jnp.int32, (SWA_GROUP * qb, span), 1) + start
    dist = row - col
    ok = (dist <= SWA_WINDOW) & (dist >= -SWA_WINDOW)
    win = pl.ds(start, span)

    vecs = _half_vecs()

    def item(hk):
        def scores():
            qs = _gqa_queries(q_ref, hk, vecs)
            return _dot_nt(qs, _pair_block(kv_ref, 0, rows=win)), _dot_nt(qs, ck_ref[0])

        def finish(s):
            sink = jnp.concatenate(
                [jnp.full((qb, 1), sink_ref[hk * SWA_GROUP + g], F32) for g in range(SWA_GROUP)], axis=0)
            s_loc = jnp.where(ok, s[0], NEG_INF)
            o = _softmax_pv([(s_loc, _pair_block(kv_ref, 1, rows=win)), (s[1], cv_ref[0])], extra=sink)
            return _gqa_outputs(o, hk, qb)
        return scores, finish

    outs = [piece for res in _pipelined([item(hk) for hk in range(SWA_KV_HEADS)]) for piece in res]
    o_ref[0] = jnp.concatenate(outs, axis=1).astype(BF16)


def _swa_latent(p, ck, cv, sink, *, nb, t):
    qb = 128
    p3 = p.reshape(nb, t, P_TILES * PROJ_TILE)
    out = pl.pallas_call(
        functools.partial(_swa_kernel, t=t, qb=qb),
        out_shape=jax.ShapeDtypeStruct((nb, t, GROUP_W), BF16),
        grid=(nb, t // qb),
        in_specs=[
            pl.BlockSpec(memory_space=pltpu.SMEM),
            pl.BlockSpec((1, qb, PROJ_TILE), lambda b, n: (b, n, P_COL["sq"])),
            pl.BlockSpec((1, t, PROJ_TILE), lambda b, n: (b, 0, P_COL["skv"])),
            pl.BlockSpec((1,) + ck.shape[1:], lambda b, n: (b, 0, 0)),
            pl.BlockSpec((1,) + cv.shape[1:], lambda b, n: (b, 0, 0)),
        ],
        out_specs=pl.BlockSpec((1, qb, GROUP_W), lambda b, n: (b, n, 0)),
        compiler_params=_params("arbitrary", "arbitrary"),
        name="swa_latent",
    )(sink, p3, p3, ck, cv)
    return out.reshape(nb * t, GROUP_W)


def _na_row_start(r, rows):
    kh = min(NA_KH, rows)
    return min(max(r - kh // 2, 0), rows - kh)


def _na_win_start(qt, rows):
    return min(max(_na_row_start(2 * qt, rows), 0), rows - NA_WIN_ROWS)


def _na_bias_kernel(rpb_ref, o_ref, *, rows):
    h = pl.program_id(0)
    n_dr, n_dc = 2 * NA_KH - 1, 2 * NA_KW - 1
    qi = lax.broadcasted_iota(jnp.int32, (GRID_W, GRID_W), 0)
    ki = lax.broadcasted_iota(jnp.int32, (GRID_W, GRID_W), 1)
    dc = jnp.clip(ki - qi + (NA_KW - 1), 0, n_dc - 1)
    cs = jnp.clip(qi - NA_KW // 2, 0, GRID_W - NA_KW)
    col_ok = (ki >= cs) & (ki < cs + NA_KW)
    neg = jnp.full((GRID_W, GRID_W), NEG_INF, F32)
    tabs = []
    for dr in range(n_dr):
        acc = jnp.zeros((GRID_W, GRID_W), F32)
        for c in range(n_dc):
            acc = jnp.where(dc == c, rpb_ref[(h * n_dr + dr) * n_dc + c], acc)
        tabs.append(jnp.where(col_ok, acc, neg))
    kh = min(NA_KH, rows)
    for qt in range(rows // 2):
        ws = _na_win_start(qt, rows)
        bands = []
        for qq in range(2):
            qr = 2 * qt + qq
            rs = _na_row_start(qr, rows)
            blks = []
            for kk in range(NA_WIN_ROWS):
                kr = ws + kk
                blks.append(tabs[kr - qr + NA_KH - 1] if rs <= kr < rs + kh else neg)
            bands.append(jnp.concatenate(blks, axis=1))
        o_ref[0, qt] = jnp.concatenate(bands, axis=0)


def _na_bias(rpb, rows):
    n_qt = rows // 2
    return pl.pallas_call(
        functools.partial(_na_bias_kernel, rows=rows),
        out_shape=jax.ShapeDtypeStruct((NA_HEADS, n_qt, NA_QT, NA_WIN), F32),
        grid=(NA_HEADS,),
        in_specs=[pl.BlockSpec(memory_space=pltpu.SMEM)],
        out_specs=pl.BlockSpec((1, n_qt, NA_QT, NA_WIN), lambda h: (h, 0, 0, 0)),
        compiler_params=_params("arbitrary"),
        name="na_bias",
    )(rpb.reshape(-1))


def _na_kernel(q_ref, k_ref, v_ref, ck_ref, cv_ref, bias_ref, o_ref, *, rows):
    qt = pl.program_id(1)
    ws = jnp.clip(jnp.clip(2 * qt - NA_KH // 2, 0, rows - NA_KH), 0, rows - NA_WIN_ROWS)
    win = pl.ds(pl.multiple_of(ws * GRID_W, GRID_W), NA_WIN)

    vecs = _half_vecs()

    def item(m):
        def scores():
            q = _pair_block(q_ref, m)
            k, ck = _pair_block(k_ref, m, rows=win), _pair_block(ck_ref, m)
            return [(_dot_nt(q * vecs[i], k), _dot_nt(q * vecs[i], ck)) for i in range(2)]

        def finish(s):
            v, cv = _pair_block(v_ref, m, rows=win), _pair_block(cv_ref, m)
            o = [_softmax_pv([(s[i][0] + bias_ref[2 * m + i, 0], v), (s[i][1], cv)]) for i in range(2)]
            return _merge_pair(o[0], o[1])
        return scores, finish

    outs = _pipelined([item(m) for m in range(NA_HEADS // 2)], depth=1)
    o_ref[0] = jnp.concatenate(outs, axis=1).astype(BF16)


def _na_latent(p, ck, cv, bias, *, nb, t):
    rows = t // GRID_W
    p3 = p.reshape(nb, t, P_TILES * PROJ_TILE)
    out = pl.pallas_call(
        functools.partial(_na_kernel, rows=rows),
        out_shape=jax.ShapeDtypeStruct((nb, t, GROUP_W), BF16),
        grid=(nb, t // NA_QT),
        in_specs=[
            pl.BlockSpec((1, NA_QT, PROJ_TILE), lambda i, n: (i, n, P_COL["nq"])),
            pl.BlockSpec((1, t, PROJ_TILE), lambda i, n: (i, 0, P_COL["nk"])),
            pl.BlockSpec((1, t, PROJ_TILE), lambda i, n: (i, 0, P_COL["nv"])),
            pl.BlockSpec((1,) + ck.shape[1:], lambda i, n: (i, 0, 0)),
            pl.BlockSpec((1,) + cv.shape[1:], lambda i, n: (i, 0, 0)),
            pl.BlockSpec((NA_HEADS, 1, NA_QT, NA_WIN), lambda i, n: (0, n, 0, 0)),
        ],
        out_specs=pl.BlockSpec((1, NA_QT, GROUP_W), lambda i, n: (i, n, 0)),
        compiler_params=_params("arbitrary", "arbitrary"),
        name="na_latent",
    )(p3, p3, p3, ck, cv, bias)
    return out.reshape(nb * t, GROUP_W)


def _diff_kernel(q_ref, k_ref, v_ref, ck_ref, cv_ref, dl_ref, subln_ref, o_ref, *, lam_init):
    lam = _diff_lambda_val(dl_ref, lam_init)

    vecs = _half_vecs()

    def item(h):
        def scores():
            q = _pair_block(q_ref, h)
            k, ck = _pair_block(k_ref, h), _pair_block(ck_ref, h)
            return [(_dot_nt(q * vecs[i], k), _dot_nt(q * vecs[i], ck)) for i in range(2)]

        def finish(s):
            pd = (_probs(list(s[0])) - lam * _probs(list(s[1]))).astype(BF16)
            n_loc = s[0][0].shape[1]
            o = _dot(pd[:, :n_loc], _pair_block(v_ref, h)) + _dot(pd[:, n_loc:], _pair_block(cv_ref, h))
            return _diff_finish(o, subln_ref, lam_init)
        return scores, finish

    outs = _pipelined([item(h) for h in range(DIFF_HEADS)], depth=1)
    o_ref[0] = jnp.concatenate(outs, axis=1).astype(BF16)


def _diff_latent(p, ck, cv, dlam, subln, lam_init, *, nb, t):
    tq = 256
    p3 = p.reshape(nb, t, P_TILES * PROJ_TILE)
    out = pl.pallas_call(
        functools.partial(_diff_kernel, lam_init=lam_init),
        out_shape=jax.ShapeDtypeStruct((nb, t, GROUP_W), BF16),
        grid=(nb, t // tq),
        in_specs=[
            pl.BlockSpec((1, tq, PROJ_TILE), lambda i, n: (i, n, P_COL["dq"])),
            pl.BlockSpec((1, t, PROJ_TILE), lambda i, n: (i, 0, P_COL["dk"])),
            pl.BlockSpec((1, t, PROJ_TILE), lambda i, n: (i, 0, P_COL["dv"])),
            pl.BlockSpec((1,) + ck.shape[1:], lambda i, n: (i, 0, 0)),
            pl.BlockSpec((1,) + cv.shape[1:], lambda i, n: (i, 0, 0)),
            pl.BlockSpec((4, HEAD_DIM), lambda i, n: (0, 0)),
            pl.BlockSpec((1, DIFF_VDIM), lambda i, n: (0, 0)),
        ],
        out_specs=pl.BlockSpec((1, tq, GROUP_W), lambda i, n: (i, n, 0)),
        compiler_params=_params("arbitrary", "arbitrary"),
        name="diff_latent",
    )(p3, p3, p3, ck, cv, dlam, subln.reshape(1, DIFF_VDIM))
    return out.reshape(nb * t, GROUP_W)


def _lru_kernel(x_ref, g_ref, cw_ref, cb_ref, w_ref, gb_ref, lam_ref, h0_ref, y_ref, st_ref, *, t):
    x = x_ref[0]
    row = lax.broadcasted_iota(jnp.int32, (t, LRU_CT), 0)
    u = cb_ref[...] + cw_ref[2:3, :] * x
    for tap, off in ((0, -2), (1, -1), (3, 1)):
        xs = pltpu.roll(x, (-off) % t, 0)
        ok = (row + off >= 0) & (row + off < t)
        u = u + cw_ref[tap:tap + 1, :] * jnp.where(ok, xs, 0.0)
    gates = _dot(u.astype(BF16), w_ref[0]) + gb_ref[0]
    total = None
    for d in range(2):
        r = _sigmoid(gates[:, (2 * d) * LRU_CT:(2 * d + 1) * LRU_CT])
        ig = _sigmoid(gates[:, (2 * d + 1) * LRU_CT:(2 * d + 2) * LRU_CT])
        nl = -lam_ref[d:d + 1, :]
        softplus = jnp.maximum(nl, 0.0) + jnp.log1p(jnp.exp(-jnp.abs(nl)))
        log_a = -LRU_C * r * softplus
        a = jnp.exp(log_a)
        bx = jnp.sqrt(jnp.tanh(-log_a) * (a * a + 1.0)) * (ig * u)
        edge = t - 1 if d == 1 else 0
        bx = jnp.where(row == edge, bx + a * h0_ref[0, d:d + 1, :], bx)
        s = 1
        while s < t:
            if d == 0:
                ok = row >= s
                a_sh = jnp.where(ok, pltpu.roll(a, s, 0), 1.0)
                b_sh = jnp.where(ok, pltpu.roll(bx, s, 0), 0.0)
            else:
                ok = row < t - s
                a_sh = jnp.where(ok, pltpu.roll(a, t - s, 0), 1.0)
                b_sh = jnp.where(ok, pltpu.roll(bx, t - s, 0), 0.0)
            bx = a * b_sh + bx
            a = a * a_sh
            s *= 2
        fin = t - 1 if d == 0 else 0
        st_ref[0, d:d + 1, :] = bx[fin:fin + 1, :]
        total = bx if total is None else total + bx
    g = g_ref[0]
    gelu = 0.5 * g * (1.0 + jnp.tanh(math.sqrt(2.0 / math.pi) * (g + 0.044715 * (g * g * g))))
    y_ref[0] = (total * gelu).astype(BF16)


def _lru_mixer(f, conv_w, conv_b, w_gates, b_gates, lam, h0, *, nb, t):
    f3 = f.reshape(nb, t, f.shape[1])
    nct = LRU_WIDTH // LRU_CT
    y, st = pl.pallas_call(
        functools.partial(_lru_kernel, t=t),
        out_shape=[jax.ShapeDtypeStruct((nb, t, LRU_WIDTH), BF16),
                   jax.ShapeDtypeStruct((nb, 2, LRU_WIDTH), F32)],
        grid=(nb, nct),
        in_specs=[
            pl.BlockSpec((1, t, LRU_CT), lambda i, c: (i, 0, c)),
            pl.BlockSpec((1, t, LRU_CT), lambda i, c: (i, 0, nct + c)),
            pl.BlockSpec((CONV_W, LRU_CT), lambda i, c: (0, c)),
            pl.BlockSpec((1, LRU_CT), lambda i, c: (0, c)),
            pl.BlockSpec((1, LRU_CT, 4 * LRU_CT), lambda i, c: (c, 0, 0)),
            pl.BlockSpec((1, 1, 4 * LRU_CT), lambda i, c: (c, 0, 0)),
            pl.BlockSpec((2, LRU_CT), lambda i, c: (0, c)),
            pl.BlockSpec((1, 2, LRU_CT), lambda i, c: (i, 0, c)),
        ],
        out_specs=[pl.BlockSpec((1, t, LRU_CT), lambda i, c: (i, 0, c)),
                   pl.BlockSpec((1, 2, LRU_CT), lambda i, c: (i, 0, c))],
        compiler_params=_params("arbitrary", "arbitrary"),
        name="lru_mixer",
    )(f3, f3, conv_w, conv_b.reshape(1, LRU_WIDTH), w_gates, b_gates, lam, h0)
    return y.reshape(nb * t, LRU_WIDTH), st


def _lru_gate_weights(wa, ba, wx, bx):
    nct = LRU_WIDTH // LRU_CT
    bpt = LRU_CT // LRU_BLOCK_W
    eye = jnp.eye(bpt, dtype=F32)

    def tile_w(w):
        w4 = w.reshape(nct, bpt, LRU_BLOCK_W, LRU_BLOCK_W)
        return jnp.einsum("cnij,nm->cnimj", w4, eye).reshape(nct, LRU_CT, LRU_CT)

    w = jnp.concatenate([tile_w(wa[0]), tile_w(wx[0]), tile_w(wa[1]), tile_w(wx[1])], axis=-1)
    b = jnp.concatenate([v.reshape(nct, 1, LRU_CT) for v in (ba[0], bx[0], ba[1], bx[1])], axis=-1)
    return w.astype(BF16), b


def _outproj_kernel(*refs, n_in):
    ins = refs[:n_in]
    w_ref, x_ref, g_ref, o_ref = refs[n_in:]
    acc = None
    off = 0
    for m_ref in ins:
        width = m_ref.shape[1]
        part = _dot(m_ref[...], w_ref[0, off:off + width, :])
        acc = part if acc is None else acc + part
        off += width
    o_ref[...] = x_ref[...] + g_ref[0] * acc


def _out_projection(mixes, w_out_bf, layer, x, mods_l, mod_row, *, tm):
    n_tok = x.shape[0]
    return pl.pallas_call(
        functools.partial(_outproj_kernel, n_in=len(mixes)),
        out_shape=jax.ShapeDtypeStruct((n_tok, D_MODEL), F32),
        grid=(n_tok // tm,),
        in_specs=[pl.BlockSpec((tm, m.shape[1]), lambda i: (i, 0)) for m in mixes] + [
            pl.BlockSpec((1, D_MODEL, D_MODEL), lambda i: (layer, 0, 0)),
            pl.BlockSpec((tm, D_MODEL), lambda i: (i, 0)),
            pl.BlockSpec((1, 1, D_MODEL), lambda i: (mod_row(i) * MOD_CHUNKS + 2, 0, 0)),
        ],
        out_specs=pl.BlockSpec((tm, D_MODEL), lambda i: (i, 0)),
        compiler_params=_params("arbitrary"),
        name="out_projection",
    )(*mixes, w_out_bf, x, mods_l)


def _ffn_kernel(x_ref, sh_ref, sc_ref, g_ref, nw_ref, wg_ref, wu_ref, wd_ref, o_ref, h_scr, *, tm):
    k = pl.program_id(1)

    @pl.when(k == 0)
    def _():
        _modnorm_rows(x_ref, nw_ref, sh_ref, sc_ref, h_scr, tm)
        o_ref[...] = jnp.zeros_like(o_ref)

    wg = wg_ref[0].astype(BF16)
    wu = wu_ref[0].astype(BF16)
    wd = wd_ref[0].astype(BF16)
    for c in range(tm // FFN_ROW_CHUNK):
        rows = slice(c * FFN_ROW_CHUNK, (c + 1) * FFN_ROW_CHUNK)
        h = h_scr[rows, :]
        g = _dot(h, wg)
        u = _dot(h, wu)
        act = (g * _sigmoid(g)) * u
        o_ref[rows, :] += _dot(act.astype(BF16), wd)

    @pl.when(k == pl.num_programs(1) - 1)
    def _():
        def body(c, carry):
            r = pl.multiple_of(c * 128, 128)
            o_ref[pl.ds(r, 128), :] = x_ref[pl.ds(r, 128), :] + g_ref[0] * o_ref[pl.ds(r, 128), :]
            return carry
        lax.fori_loop(0, tm // 128, body, 0)


def _ffn(x, mods_l, mod_row, norm_w, wg, wu, wd, layer, *, tm, th):
    n_tok = x.shape[0]
    return pl.pallas_call(
        functools.partial(_ffn_kernel, tm=tm),
        out_shape=jax.ShapeDtypeStruct((n_tok, D_MODEL), F32),
        grid=(n_tok // tm, FFN_HIDDEN // th),
        in_specs=[
            pl.BlockSpec((tm, D_MODEL), lambda i, k: (i, 0), pipeline_mode=pl.Buffered(1)),
            pl.BlockSpec((1, 1, D_MODEL), lambda i, k: (mod_row(i) * MOD_CHUNKS + 3, 0, 0)),
            pl.BlockSpec((1, 1, D_MODEL), lambda i, k: (mod_row(i) * MOD_CHUNKS + 4, 0, 0)),
            pl.BlockSpec((1, 1, D_MODEL), lambda i, k: (mod_row(i) * MOD_CHUNKS + 5, 0, 0)),
            pl.BlockSpec((1, D_MODEL), lambda i, k: (0, 0)),
            pl.BlockSpec((1, D_MODEL, th), lambda i, k: (layer, 0, k)),
            pl.BlockSpec((1, D_MODEL, th), lambda i, k: (layer, 0, k)),
            pl.BlockSpec((1, th, D_MODEL), lambda i, k: (layer, k, 0)),
        ],
        out_specs=pl.BlockSpec((tm, D_MODEL), lambda i, k: (i, 0)),
        scratch_shapes=[pltpu.VMEM((tm, D_MODEL), BF16)],
        compiler_params=_params("arbitrary", "arbitrary"),
        name="ffn",
    )(x, mods_l, mods_l, mods_l, norm_w.reshape(1, D_MODEL), wg, wu, wd)


def _rope_tables(t_lat):
    nf = HEAD_DIM // 4
    tok = jnp.arange(t_lat)
    pos = jnp.stack([tok // GRID_W, tok % GRID_W], axis=-1).astype(F32)
    inv = ROPE_BASE ** (-jnp.arange(nf, dtype=F32) / nf)
    ang = pos[:, :, None] * inv
    cos, sin = jnp.cos(ang), jnp.sin(ang)
    cos_h = jnp.concatenate([cos[:, 0], cos[:, 0], cos[:, 1], cos[:, 1]], axis=-1)
    sin_h = jnp.concatenate([-sin[:, 0], sin[:, 0], -sin[:, 1], sin[:, 1]], axis=-1)
    reps = PROJ_HALF // HEAD_DIM
    return jnp.tile(cos_h, (1, reps)), jnp.tile(sin_h, (1, reps))


def _tile_table():
    lo = [_FIRST_BLOCK[n] for n in TILE_NAMES]
    hi = [b + 1 for b in lo]
    cls = [_TILE_CLASS[n] for n in TILE_NAMES]
    return jnp.asarray(np.array(lo + hi + cls, np.int32))


def _tile_gains(qk_gain_l):
    def tiled(g, n):
        return jnp.tile(g, n // HEAD_DIM)
    ones = jnp.ones((PROJ_TILE,), F32)
    per_tile = {
        "lx": ones, "lg": ones, "nv": ones, "dv": ones,
        "nk": tiled(qk_gain_l[1, 1], PROJ_TILE), "dk": tiled(qk_gain_l[2, 1], PROJ_TILE),
        "skv": jnp.concatenate([tiled(qk_gain_l[0, 1], 128), jnp.ones((PROJ_TILE - 128,), F32)]),
        "sq": tiled(qk_gain_l[0, 0], PROJ_TILE) * QK_SCALE,
        "nq": tiled(qk_gain_l[1, 0], PROJ_TILE) * QK_SCALE,
        "dq": tiled(qk_gain_l[2, 0], PROJ_TILE) * QK_SCALE,
    }
    return jnp.stack([per_tile[n] for n in TILE_NAMES]).reshape(N_TILES, 1, PROJ_TILE)


def _lambda_init(layer):
    return 0.8 - 0.6 * math.exp(-0.3 * layer)


def kernel(x_prompt, x_sample, cache_swa_k, cache_swa_v, cache_na_k, cache_na_v, cache_diff_k, cache_diff_v, state_lru, c, c_ctx, norm_mix, norm_ffn, w_mod, b_mod, w_in, w_out, qk_gain, swa_sink, na_rpb, diff_lambda, diff_subln, conv_w, conv_b, lru_wa, lru_ba, lru_wx, lru_bx, lru_L, w_ffn_gate, w_ffn_up, w_ffn_down):
    bc, s_ctx, _ = x_prompt.shape
    bd, t_lat, _ = x_sample.shape
    n_ctx, n_lat = bc * s_ctx, bd * t_lat
    rows = t_lat // GRID_W
    p_ctx = cache_swa_k.shape[2]

    cvecs = jnp.concatenate([c_ctx[None, :], c, jnp.zeros((8 - 1 - bd, D_MODEL), F32)], axis=0)
    mods = _modulation(cvecs, w_mod, b_mod).reshape(DEPTH, 8 * MOD_CHUNKS, 1, D_MODEL)

    hsum = jnp.asarray(np.kron(np.eye(PROJ_HALF // HEAD_DIM, dtype=np.float32),
                               np.full((HEAD_DIM, HEAD_DIM), 1.0 / HEAD_DIM, np.float32)), BF16)
    rope_tabs = _rope_tables(t_lat)
    tab = _tile_table()

    tm = 1024
    tm_out = 512
    ctx_row = lambda tile: (lambda i: 0)
    lat_row = lambda tile: (lambda i: 1 + (i * tile) // t_lat)

    xc = x_prompt.reshape(n_ctx, D_MODEL)
    xs = x_sample.reshape(n_lat, D_MODEL)
    zero_state = jnp.zeros((bc, 2, LRU_WIDTH), F32)
    caches = [jnp.zeros((bc, DEPTH, s_ctx, _CACHE_SRC[name][2]), F32) for name in CACHE_NAMES]
    lru_states = []
    w_in_bf = w_in.astype(BF16)
    w_out_bf = w_out.astype(BF16)

    def cached(arr, l):
        return arr[:, l].reshape(bd, p_ctx, -1).astype(BF16)

    for l in range(DEPTH):
        lam_init = _lambda_init(l)
        mods_l = mods[l]
        gain = _tile_gains(qk_gain[l])
        w_gates, b_gates = _lru_gate_weights(lru_wa[l], lru_ba[l], lru_wx[l], lru_bx[l])
        ffn_w = (w_ffn_gate, w_ffn_up, w_ffn_down, l)

        p, f, caches = _in_projection(xc, mods_l, ctx_row(tm), norm_mix[l], w_in_bf, l, tab, gain, hsum,
                                      None, caches, tm=tm, seq=s_ctx)
        mix3 = _ctx_attention(p, swa_sink[l], diff_lambda[l], diff_subln[l], lam_init, nb=bc, t=s_ctx)
        od, st = _lru_mixer(f, conv_w[l], conv_b[l], w_gates, b_gates, lru_L[l], zero_state, nb=bc, t=s_ctx)
        lru_states.append(st)
        xc = _out_projection([mix3, od], w_out_bf, l, xc, mods_l, ctx_row(tm_out), tm=tm_out)
        xc = _ffn(xc, mods_l, ctx_row(tm), norm_ffn[l], *ffn_w, tm=tm, th=512)

        p, f, _ = _in_projection(xs, mods_l, lat_row(tm), norm_mix[l], w_in_bf, l, tab, gain, hsum,
                                 rope_tabs, None, tm=tm, seq=t_lat)
        oa = _swa_latent(p, cached(cache_swa_k, l), cached(cache_swa_v, l), swa_sink[l], nb=bd, t=t_lat)
        bias = _na_bias(na_rpb[l], rows)
        ob = _na_latent(p, cached(cache_na_k, l), cached(cache_na_v, l), bias, nb=bd, t=t_lat)
        oc = _diff_latent(p, cached(cache_diff_k, l), cached(cache_diff_v, l),
                          diff_lambda[l], diff_subln[l], lam_init, nb=bd, t=t_lat)
        od, _ = _lru_mixer(f, conv_w[l], conv_b[l], w_gates, b_gates, lru_L[l], state_lru[:, l], nb=bd, t=t_lat)
        xs = _out_projection([oa, ob, oc, od], w_out_bf, l, xs, mods_l, lat_row(tm_out), tm=tm_out)
        xs = _ffn(xs, mods_l, lat_row(tm), norm_ffn[l], *ffn_w, tm=tm, th=512)

    swa_k, swa_v, na_k, na_v, diff_k, diff_v = caches
    return (xc.reshape(bc, s_ctx, D_MODEL), xs.reshape(bd, t_lat, D_MODEL),
            swa_k.reshape(bc, DEPTH, s_ctx, SWA_KV_HEADS, HEAD_DIM),
            swa_v.reshape(bc, DEPTH, s_ctx, SWA_KV_HEADS, HEAD_DIM),
            na_k.reshape(bc, DEPTH, s_ctx, NA_HEADS, HEAD_DIM),
            na_v.reshape(bc, DEPTH, s_ctx, NA_HEADS, HEAD_DIM),
            diff_k.reshape(bc, DEPTH, s_ctx, DIFF_HEADS, 2, HEAD_DIM),
            diff_v.reshape(bc, DEPTH, s_ctx, DIFF_HEADS, DIFF_VDIM),
            jnp.stack(lru_states, axis=1))
```

```python
import functools
import math

import jax
import jax.numpy as jnp
import numpy as np
from jax import lax
from jax.experimental import pallas as pl
from jax.experimental.pallas import tpu as pltpu

F32 = jnp.float32
BF16 = jnp.bfloat16

D_MODEL = 2048
DEPTH = 2
GRID_W = 64
HEAD_DIM = 64
GROUP_W = 512
SWA_HEADS = 8
SWA_KV_HEADS = 2
SWA_GROUP = 4
SWA_WINDOW = 128
NA_HEADS = 8
NA_KH = 8
NA_KW = 16
DIFF_HEADS = 4
DIFF_VDIM = 128
LRU_WIDTH = 512
LRU_BLOCKS = 8
LRU_BLOCK_W = 64
LRU_C = 8.0
CONV_W = 4
FFN_HIDDEN = 5632
ROPE_BASE = 10000.0
NORM_EPS = 1e-6
NEG_INF = -1e30
MOD_CHUNKS = 6
PROJ_W = 4864
QK_SCALE = HEAD_DIM ** -0.5

V7X_VMEM_LIMIT = 60 * 1024 * 1024

PROJ_HALF = 256
PROJ_TILE = 2 * PROJ_HALF
TILE_NAMES = ("lx", "lg", "nk", "dk", "nv", "dv", "skv", "sq", "nq", "dq")
N_TILES = len(TILE_NAMES)
_FIRST_BLOCK = {"sq": 0, "skv": 2, "nq": 3, "nk": 5, "nv": 7, "dq": 9, "dk": 11, "dv": 13, "lx": 15, "lg": 17}
CLS_PLAIN, CLS_NORM, CLS_ROPE, CLS_MIXED = 0, 1, 2, 3
_TILE_CLASS = {"lx": CLS_PLAIN, "lg": CLS_PLAIN, "nv": CLS_PLAIN, "dv": CLS_PLAIN,
               "nk": CLS_NORM, "nq": CLS_NORM, "dk": CLS_ROPE, "sq": CLS_ROPE, "dq": CLS_ROPE,
               "skv": CLS_MIXED}
P_FIRST = 2
P_TILES = N_TILES - P_FIRST
P_COL = {name: TILE_NAMES.index(name) - P_FIRST for name in TILE_NAMES[P_FIRST:]}
F_TILES = 2

NA_QT = 128
NA_WIN_ROWS = 10
NA_WIN = NA_WIN_ROWS * GRID_W
LRU_CT = 128
ROW_CHUNK = 256
FFN_ROW_CHUNK = 512


def _sigmoid(x):
    return 1.0 / (1.0 + jnp.exp(-x))


def _dot(a, b):
    return jnp.dot(a, b, preferred_element_type=F32)


def _dot_nt(a, b):
    return lax.dot_general(a, b, (((1,), (1,)), ((), ())), preferred_element_type=F32)


def _params(*sem):
    return pltpu.CompilerParams(dimension_semantics=sem, vmem_limit_bytes=V7X_VMEM_LIMIT)


def _mod_kernel(c_ref, w_ref, b_ref, o_ref):
    cv = c_ref[...]
    s = cv * _sigmoid(cv)
    o_ref[0] = _dot(s.astype(BF16), w_ref[0].astype(BF16)) + b_ref[0]


def _modulation(cvecs, w_mod, b_mod):
    tn = 1024
    n = MOD_CHUNKS * D_MODEL
    return pl.pallas_call(
        _mod_kernel,
        out_shape=jax.ShapeDtypeStruct((DEPTH, 8, n), F32),
        grid=(DEPTH, n // tn),
        in_specs=[
            pl.BlockSpec((8, D_MODEL), lambda l, j: (0, 0)),
            pl.BlockSpec((1, D_MODEL, tn), lambda l, j: (l, 0, j)),
            pl.BlockSpec((1, 1, tn), lambda l, j: (l, 0, j)),
        ],
        out_specs=pl.BlockSpec((1, 8, tn), lambda l, j: (l, 0, j)),
        compiler_params=_params("arbitrary", "arbitrary"),
        name="modulation",
    )(cvecs, w_mod, b_mod.reshape(DEPTH, 1, n))


def _modnorm_static(x_ref, nw_ref, sh_ref, sc_ref, h_scr, r0, n_rows, chunk=128):
    for c in range(n_rows // chunk):
        rows = slice(r0 + c * chunk, r0 + (c + 1) * chunk)
        x = x_ref[rows, :]
        ms = jnp.mean(x * x, axis=-1, keepdims=True)
        y = x * lax.rsqrt(ms + NORM_EPS) * nw_ref[...]
        h_scr[rows, :] = (y * (1.0 + sc_ref[0]) + sh_ref[0]).astype(BF16)


CACHE_NAMES = ("swa_k", "swa_v", "na_k", "na_v", "diff_k", "diff_v")
_CACHE_SRC = {"swa_k": ("skv", 0, 128), "swa_v": ("skv", 128, 128), "na_k": ("nk", 0, 512),
              "na_v": ("nv", 0, 512), "diff_k": ("dk", 0, 512), "diff_v": ("dv", 0, 512)}


def _inproj_kernel(*refs, tm, rope, n_cache, seq):
    n_in = 10 if rope else 8
    tab_ref = refs[0]
    (x_ref, sh_ref, sc_ref, nw_ref, wlo_ref, whi_ref, gain_ref, hsum_ref) = refs[1:9]
    cos_ref, sin_ref = (refs[9], refs[10]) if rope else (None, None)
    outs = refs[1 + n_in + n_cache:]
    p_ref, f_ref = outs[0], outs[1]
    cache_refs = outs[2:2 + n_cache]
    h_scr = outs[2 + n_cache]
    j = pl.program_id(1)
    cls = tab_ref[2 * N_TILES + j]

    def run_tile(mode, halves, first=False):
        w = {half: (wlo_ref if half == 0 else whi_ref)[0] for half in halves}
        n_chunks = tm // ROW_CHUNK
        units = [(half, c) for half in halves for c in range(n_chunks)]

        def norm_chunk(c):
            _modnorm_static(x_ref, nw_ref, sh_ref, sc_ref, h_scr, c * ROW_CHUNK, ROW_CHUNK)

        if first:
            norm_chunk(0)

        def main(unit):
            half, c = unit
            if first and half == halves[0] and c + 1 < n_chunks:
                norm_chunk(c + 1)
            return _dot(h_scr[c * ROW_CHUNK:(c + 1) * ROW_CHUNK, :], w[half])

        def finish(unit, p):
            half, c = unit
            rows = slice(c * ROW_CHUNK, (c + 1) * ROW_CHUNK)
            cols = slice(half * PROJ_HALF, (half + 1) * PROJ_HALF)
            y = p
            if mode != CLS_PLAIN:
                ms = _dot((p * p).astype(BF16), hsum_ref[...])
                y = p * lax.rsqrt(ms + NORM_EPS) * gain_ref[0, :, cols]
                if rope and mode in (CLS_ROPE, CLS_MIXED):
                    lane = lax.broadcasted_iota(jnp.int32, y.shape, 1)
                    up = pltpu.roll(y, PROJ_HALF - 16, 1)
                    down = pltpu.roll(y, 16, 1)
                    partner = jnp.where((lane & 31) < 16, up, down)
                    y = y * cos_ref[rows, :] + partner * sin_ref[rows, :]
                if mode == CLS_MIXED:
                    lane = lax.broadcasted_iota(jnp.int32, y.shape, 1)
                    y = jnp.where(lane < 2 * HEAD_DIM, y, p)
            p_ref[rows, cols] = y.astype(BF16)
            f_ref[rows, cols] = y

        prev, p_prev = units[0], main(units[0])
        for unit in units[1:]:
            p_next = main(unit)
            finish(prev, p_prev)
            prev, p_prev = unit, p_next
        finish(prev, p_prev)

    assert _TILE_CLASS[TILE_NAMES[0]] == CLS_PLAIN

    @pl.when(j == 0)
    def _():
        run_tile(CLS_PLAIN, (0, 1), first=True)

    modes = (CLS_PLAIN, CLS_NORM, CLS_ROPE) if rope else (CLS_PLAIN, CLS_NORM)
    for mode in modes:
        cond = (cls == mode)
        if mode == CLS_PLAIN:
            cond = cond & (j > 0)
        if not rope and mode == CLS_NORM:
            cond = (cls == CLS_NORM) | (cls == CLS_ROPE)

        @pl.when(cond)
        def _(mode=mode):
            run_tile(mode, (0, 1))

    @pl.when(cls == CLS_MIXED)
    def _():
        run_tile(CLS_MIXED, (0,))
        hi = slice(PROJ_HALF, PROJ_TILE)
        p_ref[:, hi] = jnp.zeros((tm, PROJ_HALF), BF16)
        f_ref[:, hi] = jnp.zeros((tm, PROJ_HALF), F32)

    for name, c_ref in zip(CACHE_NAMES, cache_refs):
        tile, off, width = _CACHE_SRC[name]

        @pl.when(j == TILE_NAMES.index(tile))
        def _(c_ref=c_ref, off=off, width=width):
            for b in range(tm // seq):
                c_ref[b, 0] = f_ref[b * seq:(b + 1) * seq, off:off + width]


def _in_projection(x, mods_l, mod_row, norm_w, w_in_bf, layer, tab, gain, hsum, rope_tabs, caches, *, tm, seq):
    n_tok = x.shape[0]
    rope = rope_tabs is not None
    caches = [] if caches is None else list(caches)
    n_cache = len(caches)

    in_specs = [
        pl.BlockSpec((tm, D_MODEL), lambda i, j, t: (i, 0)),
        pl.BlockSpec((1, 1, D_MODEL), lambda i, j, t: (mod_row(i) * MOD_CHUNKS + 0, 0, 0)),
        pl.BlockSpec((1, 1, D_MODEL), lambda i, j, t: (mod_row(i) * MOD_CHUNKS + 1, 0, 0)),
        pl.BlockSpec((1, D_MODEL), lambda i, j, t: (0, 0)),
        pl.BlockSpec((1, D_MODEL, PROJ_HALF), lambda i, j, t: (layer, 0, t[j])),
        pl.BlockSpec((1, D_MODEL, PROJ_HALF), lambda i, j, t: (layer, 0, t[N_TILES + j])),
        pl.BlockSpec((1, 1, PROJ_TILE), lambda i, j, t: (j, 0, 0)),
        pl.BlockSpec((PROJ_HALF, PROJ_HALF), lambda i, j, t: (0, 0)),
    ]
    args = [x, mods_l, mods_l, norm_w.reshape(1, D_MODEL), w_in_bf, w_in_bf, gain, hsum]
    if rope:
        cos_t, sin_t = rope_tabs
        in_specs += [pl.BlockSpec((tm, PROJ_HALF), lambda i, j, t: (0, 0)),
                     pl.BlockSpec((tm, PROJ_HALF), lambda i, j, t: (0, 0))]
        args += [cos_t, sin_t]
    n_in = len(args)
    in_specs += [pl.BlockSpec(memory_space=pl.ANY)] * n_cache
    args += caches
    out_specs = [
        pl.BlockSpec((tm, PROJ_TILE), lambda i, j, t: (i, jnp.maximum(j - P_FIRST, 0))),
        pl.BlockSpec((tm, PROJ_TILE), lambda i, j, t: (i, jnp.minimum(j, F_TILES))),
    ]
    out_shape = [jax.ShapeDtypeStruct((n_tok, P_TILES * PROJ_TILE), BF16),
                 jax.ShapeDtypeStruct((n_tok, (F_TILES + 1) * PROJ_TILE), F32)]
    for arr in caches:
        width = arr.shape[-1]
        out_specs.append(pl.BlockSpec((tm // seq, 1, seq, width), lambda i, j, t: (i, layer, 0, 0)))
        out_shape.append(jax.ShapeDtypeStruct(arr.shape, arr.dtype))
    res = pl.pallas_call(
        functools.partial(_inproj_kernel, tm=tm, rope=rope, n_cache=n_cache, seq=seq),
        out_shape=out_shape,
        grid_spec=pltpu.PrefetchScalarGridSpec(
            num_scalar_prefetch=1,
            grid=(n_tok // tm, N_TILES),
            in_specs=in_specs,
            out_specs=out_specs,
            scratch_shapes=[pltpu.VMEM((tm, D_MODEL), BF16)],
        ),
        input_output_aliases={1 + n_in + k: 2 + k for k in range(n_cache)},
        compiler_params=_params("arbitrary", "arbitrary"),
        name="in_projection_rope" if rope else "in_projection",
    )(tab, *args)
    return res[0], res[1], list(res[2:])


def _joint(scores):
    return scores[0] if len(scores) == 1 else jnp.concatenate(scores, axis=1)


def _softmax_pv(segs, extra=None):
    s = _joint([sc for sc, _ in segs])
    m = jnp.max(s, axis=-1, keepdims=True)
    if extra is not None:
        m = jnp.maximum(m, extra)
    e = jnp.exp(s - m)
    den = jnp.sum(e, axis=-1, keepdims=True)
    if extra is not None:
        den = den + jnp.exp(extra - m)
    e = e.astype(BF16)
    o, off = None, 0
    for sc, v in segs:
        n = sc.shape[1]
        part = _dot(e[:, off:off + n], v)
        o = part if o is None else o + part
        off += n
    return o / den


def _probs(scores):
    s = _joint(scores)
    e = jnp.exp(s - jnp.max(s, axis=-1, keepdims=True))
    return e * (1.0 / jnp.sum(e, axis=-1, keepdims=True))


def _half_vecs():
    lane = lax.broadcasted_iota(jnp.int32, (1, 2 * HEAD_DIM), 1)
    lo = (lane < HEAD_DIM).astype(BF16)
    return lo, (1.0 - lo).astype(BF16)


def _pair_block(ref, m, rows=None):
    if rows is None:
        return ref[0, :, m * 128:(m + 1) * 128]
    return ref[0, rows, m * 128:(m + 1) * 128]


def _merge_pair(o_even, o_odd):
    lane = lax.broadcasted_iota(jnp.int32, o_even.shape, 1)
    return jnp.where(lane < HEAD_DIM, o_even, o_odd)


def _diff_lambda_val(dl_ref, lam_init):
    lp = dl_ref[...]
    s1 = jnp.sum(lp[0:1] * lp[1:2], axis=-1, keepdims=True)
    s2 = jnp.sum(lp[2:3] * lp[3:4], axis=-1, keepdims=True)
    return jnp.exp(s1) - jnp.exp(s2) + lam_init


def _diff_finish(o, subln_ref, lam_init):
    ms = jnp.mean(o * o, axis=-1, keepdims=True)
    return o * lax.rsqrt(ms + NORM_EPS) * subln_ref[...] * (1.0 - lam_init)


def _swap_halves_bf16(x):
    return pltpu.roll(x.astype(F32), HEAD_DIM, 1).astype(BF16)


def _gqa_queries(q_ref, hk, vecs):
    parts = []
    for g in range(SWA_GROUP):
        h = hk * SWA_GROUP + g
        q = _pair_block(q_ref, h // 2)
        if h % 2 != hk:
            q = _swap_halves_bf16(q)
        parts.append(q * vecs[hk])
    return jnp.concatenate(parts, axis=0)


def _gqa_outputs(o, hk, rows):
    blocks = []
    for j in range(SWA_GROUP // 2):
        halves = []
        for g in (2 * j, 2 * j + 1):
            og = o[g * rows:(g + 1) * rows]
            halves.append(og if g % 2 == hk else pltpu.roll(og, HEAD_DIM, 1))
        blocks.append(_merge_pair(halves[0], halves[1]))
    return blocks


def _pipelined(items, depth=2):
    pending, outs = [], []
    for score_fn, finish_fn in items:
        pending.append((finish_fn, score_fn()))
        if len(pending) > depth:
            fn, s = pending.pop(0)
            outs.append(fn(s))
    for fn, s in pending:
        outs.append(fn(s))
    return outs


def _ctx_attn_kernel(sink_ref, sq_ref, skv_ref, nq_ref, nk_ref, nv_ref, dq_ref, dk_ref, dv_ref,
                     dl_ref, subln_ref, o_ref, *, t, lam_init):
    lam = _diff_lambda_val(dl_ref, lam_init)
    vecs = _half_vecs()

    def swa_item(hk):
        def scores():
            return _dot_nt(_gqa_queries(sq_ref, hk, vecs), _pair_block(skv_ref, 0))

        def finish(s):
            sink = jnp.concatenate(
                [jnp.full((t, 1), sink_ref[hk * SWA_GROUP + g], F32) for g in range(SWA_GROUP)], axis=0)
            return _gqa_outputs(_softmax_pv([(s, _pair_block(skv_ref, 1))], extra=sink), hk, t)
        return scores, finish

    def na_item(m):
        def scores():
            q, k = _pair_block(nq_ref, m), _pair_block(nk_ref, m)
            return [_dot_nt(q * vecs[i], k) for i in range(2)]

        def finish(s):
            v = _pair_block(nv_ref, m)
            return [_merge_pair(_softmax_pv([(s[0], v)]), _softmax_pv([(s[1], v)]))]
        return scores, finish

    def diff_item(h):
        def scores():
            q, k = _pair_block(dq_ref, h), _pair_block(dk_ref, h)
            return [_dot_nt(q * vecs[i], k) for i in range(2)]

        def finish(s):
            pd = _probs([s[0]]) - lam * _probs([s[1]])
            o = _dot(pd.astype(BF16), _pair_block(dv_ref, h))
            return [_diff_finish(o, subln_ref, lam_init)]
        return scores, finish

    items = ([swa_item(hk) for hk in range(SWA_KV_HEADS)] + [na_item(m) for m in range(NA_HEADS // 2)]
             + [diff_item(h) for h in range(DIFF_HEADS)])
    outs = [piece for res in _pipelined(items) for piece in res]
    o_ref[0] = jnp.concatenate(outs, axis=1).astype(BF16)


def _pcol(name, t):
    return pl.BlockSpec((1, t, PROJ_TILE), lambda i, c=P_COL[name]: (i, 0, c))


def _ctx_attention(p, sink, dlam, subln, lam_init, *, nb, t):
    p3 = p.reshape(nb, t, P_TILES * PROJ_TILE)
    names = ("sq", "skv", "nq", "nk", "nv", "dq", "dk", "dv")
    out = pl.pallas_call(
        functools.partial(_ctx_attn_kernel, t=t, lam_init=lam_init),
        out_shape=jax.ShapeDtypeStruct((nb, t, 3 * GROUP_W), BF16),
        grid=(nb,),
        in_specs=[pl.BlockSpec(memory_space=pltpu.SMEM)] + [_pcol(n, t) for n in names] + [
            pl.BlockSpec((4, HEAD_DIM), lambda i: (0, 0)),
            pl.BlockSpec((1, DIFF_VDIM), lambda i: (0, 0)),
        ],
        out_specs=pl.BlockSpec((1, t, 3 * GROUP_W), lambda i: (i, 0, 0)),
        compiler_params=_params("arbitrary"),
        name="ctx_attention",
    )(sink, *([p3] * len(names)), dlam, subln.reshape(1, DIFF_VDIM))
    return out.reshape(nb * t, 3 * GROUP_W)


def _swa_kernel(sink_ref, q_ref, kv_ref, ck_ref, cv_ref, o_ref, *, t, qb):
    n = pl.program_id(1)
    span = 3 * qb
    start = pl.multiple_of(jnp.clip((n - 1) * qb, 0, t - span), qb)
    row = (lax.broadcasted_iota(jnp.int32, (SWA_GROUP * qb, span), 0) & (qb - 1)) + n * qb
    col = lax.broadcasted_iota(jnp.int32, (SWA_GROUP * qb, span), 1) + start
    dist = row - col
    ok = (dist <= SWA_WINDOW) & (dist >= -SWA_WINDOW)
    win = pl.ds(start, span)

    vecs = _half_vecs()

    def item(hk):
        def scores():
            qs = _gqa_queries(q_ref, hk, vecs)
            return _dot_nt(qs, _pair_block(kv_ref, 0, rows=win)), _dot_nt(qs, ck_ref[0])

        def finish(s):
            sink = jnp.concatenate(
                [jnp.full((qb, 1), sink_ref[hk * SWA_GROUP + g], F32) for g in range(SWA_GROUP)], axis=0)
            s_loc = jnp.where(ok, s[0], NEG_INF)
            o = _softmax_pv([(s_loc, _pair_block(kv_ref, 1, rows=win)), (s[1], cv_ref[0])], extra=sink)
            return _gqa_outputs(o, hk, qb)
        return scores, finish

    outs = [piece for res in _pipelined([item(hk) for hk in range(SWA_KV_HEADS)]) for piece in res]
    o_ref[0] = jnp.concatenate(outs, axis=1).astype(BF16)


def _swa_latent(p, ck, cv, layer, sink, *, nb, t):
    qb = 128
    p3 = p.reshape(nb, t, P_TILES * PROJ_TILE)
    out = pl.pallas_call(
        functools.partial(_swa_kernel, t=t, qb=qb),
        out_shape=jax.ShapeDtypeStruct((nb, t, GROUP_W), BF16),
        grid=(nb, t // qb),
        in_specs=[
            pl.BlockSpec(memory_space=pltpu.SMEM),
            pl.BlockSpec((1, qb, PROJ_TILE), lambda b, n: (b, n, P_COL["sq"])),
            pl.BlockSpec((1, t, PROJ_TILE), lambda b, n: (b, 0, P_COL["skv"])),
            pl.BlockSpec((1,) + ck.shape[1:], lambda b, n: (b * DEPTH + layer, 0, 0)),
            pl.BlockSpec((1,) + cv.shape[1:], lambda b, n: (b * DEPTH + layer, 0, 0)),
        ],
        out_specs=pl.BlockSpec((1, qb, GROUP_W), lambda b, n: (b, n, 0)),
        compiler_params=_params("arbitrary", "arbitrary"),
        name="swa_latent",
    )(sink, p3, p3, ck, cv)
    return out.reshape(nb * t, GROUP_W)


def _na_row_start(r, rows):
    kh = min(NA_KH, rows)
    return min(max(r - kh // 2, 0), rows - kh)


def _na_win_start(qt, rows):
    return min(max(_na_row_start(2 * qt, rows), 0), rows - NA_WIN_ROWS)


def _na_bias_kernel(rpb_ref, o_ref, *, rows):
    h = pl.program_id(0)
    n_dr, n_dc = 2 * NA_KH - 1, 2 * NA_KW - 1
    qi = lax.broadcasted_iota(jnp.int32, (GRID_W, GRID_W), 0)
    ki = lax.broadcasted_iota(jnp.int32, (GRID_W, GRID_W), 1)
    dc = jnp.clip(ki - qi + (NA_KW - 1), 0, n_dc - 1)
    cs = jnp.clip(qi - NA_KW // 2, 0, GRID_W - NA_KW)
    col_ok = (ki >= cs) & (ki < cs + NA_KW)
    neg = jnp.full((GRID_W, GRID_W), NEG_INF, F32)
    tabs = []
    for dr in range(n_dr):
        acc = jnp.zeros((GRID_W, GRID_W), F32)
        for c in range(n_dc):
            acc = jnp.where(dc == c, rpb_ref[(h * n_dr + dr) * n_dc + c], acc)
        tabs.append(jnp.where(col_ok, acc, neg))
    kh = min(NA_KH, rows)
    for qt in range(rows // 2):
        ws = _na_win_start(qt, rows)
        bands = []
        for qq in range(2):
            qr = 2 * qt + qq
            rs = _na_row_start(qr, rows)
            blks = []
            for kk in range(NA_WIN_ROWS):
                kr = ws + kk
                blks.append(tabs[kr - qr + NA_KH - 1] if rs <= kr < rs + kh else neg)
            bands.append(jnp.concatenate(blks, axis=1))
        o_ref[0, qt] = jnp.concatenate(bands, axis=0)


def _na_bias(rpb, rows):
    n_qt = rows // 2
    return pl.pallas_call(
        functools.partial(_na_bias_kernel, rows=rows),
        out_shape=jax.ShapeDtypeStruct((NA_HEADS, n_qt, NA_QT, NA_WIN), F32),
        grid=(NA_HEADS,),
        in_specs=[pl.BlockSpec(memory_space=pltpu.SMEM)],
        out_specs=pl.BlockSpec((1, n_qt, NA_QT, NA_WIN), lambda h: (h, 0, 0, 0)),
        compiler_params=_params("arbitrary"),
        name="na_bias",
    )(rpb.reshape(-1))


def _na_kernel(q_ref, k_ref, v_ref, ck_ref, cv_ref, bias_ref, o_ref, *, rows):
    qt = pl.program_id(1)
    ws = jnp.clip(jnp.clip(2 * qt - NA_KH // 2, 0, rows - NA_KH), 0, rows - NA_WIN_ROWS)
    win = pl.ds(pl.multiple_of(ws * GRID_W, GRID_W), NA_WIN)

    vecs = _half_vecs()

    def item(m):
        def scores():
            q = _pair_block(q_ref, m)
            k, ck = _pair_block(k_ref, m, rows=win), _pair_block(ck_ref, m)
            return [(_dot_nt(q * vecs[i], k), _dot_nt(q * vecs[i], ck)) for i in range(2)]

        def finish(s):
            v, cv = _pair_block(v_ref, m, rows=win), _pair_block(cv_ref, m)
            o = [_softmax_pv([(s[i][0] + bias_ref[2 * m + i, 0], v), (s[i][1], cv)]) for i in range(2)]
            return _merge_pair(o[0], o[1])
        return scores, finish

    outs = _pipelined([item(m) for m in range(NA_HEADS // 2)], depth=1)
    o_ref[0] = jnp.concatenate(outs, axis=1).astype(BF16)


def _na_latent(p, ck, cv, layer, bias, *, nb, t):
    rows = t // GRID_W
    p3 = p.reshape(nb, t, P_TILES * PROJ_TILE)
    out = pl.pallas_call(
        functools.partial(_na_kernel, rows=rows),
        out_shape=jax.ShapeDtypeStruct((nb, t, GROUP_W), BF16),
        grid=(nb, t // NA_QT),
        in_specs=[
            pl.BlockSpec((1, NA_QT, PROJ_TILE), lambda i, n: (i, n, P_COL["nq"])),
            pl.BlockSpec((1, t, PROJ_TILE), lambda i, n: (i, 0, P_COL["nk"])),
            pl.BlockSpec((1, t, PROJ_TILE), lambda i, n: (i, 0, P_COL["nv"])),
            pl.BlockSpec((1,) + ck.shape[1:], lambda i, n: (i * DEPTH + layer, 0, 0)),
            pl.BlockSpec((1,) + cv.shape[1:], lambda i, n: (i * DEPTH + layer, 0, 0)),
            pl.BlockSpec((NA_HEADS, 1, NA_QT, NA_WIN), lambda i, n: (0, n, 0, 0)),
        ],
        out_specs=pl.BlockSpec((1, NA_QT, GROUP_W), lambda i, n: (i, n, 0)),
        compiler_params=_params("arbitrary", "arbitrary"),
        name="na_latent",
    )(p3, p3, p3, ck, cv, bias)
    return out.reshape(nb * t, GROUP_W)


def _diff_kernel(q_ref, k_ref, v_ref, ck_ref, cv_ref, dl_ref, subln_ref, o_ref, *, lam_init):
    lam = _diff_lambda_val(dl_ref, lam_init)

    vecs = _half_vecs()

    def item(h):
        def scores():
            q = _pair_block(q_ref, h)
            k, ck = _pair_block(k_ref, h), _pair_block(ck_ref, h)
            return [(_dot_nt(q * vecs[i], k), _dot_nt(q * vecs[i], ck)) for i in range(2)]

        def finish(s):
            pd = (_probs(list(s[0])) - lam * _probs(list(s[1]))).astype(BF16)
            n_loc = s[0][0].shape[1]
            o = _dot(pd[:, :n_loc], _pair_block(v_ref, h)) + _dot(pd[:, n_loc:], _pair_block(cv_ref, h))
            return _diff_finish(o, subln_ref, lam_init)
        return scores, finish

    outs = _pipelined([item(h) for h in range(DIFF_HEADS)], depth=1)
    o_ref[0] = jnp.concatenate(outs, axis=1).astype(BF16)


def _diff_latent(p, ck, cv, layer, dlam, subln, lam_init, *, nb, t):
    tq = 256
    p3 = p.reshape(nb, t, P_TILES * PROJ_TILE)
    out = pl.pallas_call(
        functools.partial(_diff_kernel, lam_init=lam_init),
        out_shape=jax.ShapeDtypeStruct((nb, t, GROUP_W), BF16),
        grid=(nb, t // tq),
        in_specs=[
            pl.BlockSpec((1, tq, PROJ_TILE), lambda i, n: (i, n, P_COL["dq"])),
            pl.BlockSpec((1, t, PROJ_TILE), lambda i, n: (i, 0, P_COL["dk"])),
            pl.BlockSpec((1, t, PROJ_TILE), lambda i, n: (i, 0, P_COL["dv"])),
            pl.BlockSpec((1,) + ck.shape[1:], lambda i, n: (i * DEPTH + layer, 0, 0)),
            pl.BlockSpec((1,) + cv.shape[1:], lambda i, n: (i * DEPTH + layer, 0, 0)),
            pl.BlockSpec((4, HEAD_DIM), lambda i, n: (0, 0)),
            pl.BlockSpec((1, DIFF_VDIM), lambda i, n: (0, 0)),
        ],
        out_specs=pl.BlockSpec((1, tq, GROUP_W), lambda i, n: (i, n, 0)),
        compiler_params=_params("arbitrary", "arbitrary"),
        name="diff_latent",
    )(p3, p3, p3, ck, cv, dlam, subln.reshape(1, DIFF_VDIM))
    return out.reshape(nb * t, GROUP_W)


SUBLANES = 8


def _linear_scan(a, b, row, t, reverse, sa_ref, sb_ref, sc_ref):
    nt = t // SUBLANES
    sub = row & (SUBLANES - 1)

    def doubling(a, b, idx, n, steps):
        for s in steps:
            if reverse:
                ok = idx < n - s
                a_sh = jnp.where(ok, pltpu.roll(a, a.shape[0] - s, 0), 1.0)
                b_sh = jnp.where(ok, pltpu.roll(b, b.shape[0] - s, 0), 0.0)
            else:
                ok = idx >= s
                a_sh = jnp.where(ok, pltpu.roll(a, s, 0), 1.0)
                b_sh = jnp.where(ok, pltpu.roll(b, s, 0), 0.0)
            b = a * b_sh + b
            a = a * a_sh
        return a, b

    tiles = (nt, SUBLANES, LRU_CT)
    sub3 = lax.broadcasted_iota(jnp.int32, tiles, 1)

    def tile_doubling(a, b):
        for s in (1, 2, 4):
            ok = (sub3 < SUBLANES - s) if reverse else (sub3 >= s)
            sh = SUBLANES - s if reverse else s
            a_sh = jnp.where(ok, pltpu.roll(a, sh, 1), 1.0)
            b_sh = jnp.where(ok, pltpu.roll(b, sh, 1), 0.0)
            b = a * b_sh + b
            a = a * a_sh
        return a, b

    a, b = tile_doubling(a.reshape(tiles), b.reshape(tiles))
    a, b = a.reshape(t, LRU_CT), b.reshape(t, LRU_CT)
    sa_ref[...] = a
    sb_ref[...] = b
    last = 0 if reverse else SUBLANES - 1
    at = sa_ref[pl.ds(last, nt, stride=SUBLANES), :]
    bt = sb_ref[pl.ds(last, nt, stride=SUBLANES), :]
    rowt = lax.broadcasted_iota(jnp.int32, (nt, LRU_CT), 0)
    steps, s = [], 1
    while s < nt:
        steps.append(s)
        s *= 2
    _, bt = doubling(at, bt, rowt, nt, steps)
    if reverse:
        carry = jnp.where(rowt < nt - 1, pltpu.roll(bt, nt - 1, 0), 0.0)
    else:
        carry = jnp.where(rowt >= 1, pltpu.roll(bt, 1, 0), 0.0)
    for r in range(SUBLANES):
        sc_ref[pl.ds(r, nt, stride=SUBLANES), :] = carry
    return b + a * sc_ref[...]


def _lru_kernel(x_ref, g_ref, cw_ref, cb_ref, w_ref, gb_ref, lam_ref, h0_ref, y_ref, st_ref,
                sa_ref, sb_ref, sc_ref, *, t):
    x = x_ref[0]
    row = lax.broadcasted_iota(jnp.int32, (t, LRU_CT), 0)
    u = cb_ref[...] + cw_ref[2:3, :] * x
    for tap, off in ((0, -2), (1, -1), (3, 1)):
        xs = pltpu.roll(x, (-off) % t, 0)
        ok = (row + off >= 0) & (row + off < t)
        u = u + cw_ref[tap:tap + 1, :] * jnp.where(ok, xs, 0.0)
    gates = _dot(u.astype(BF16), w_ref[0]) + gb_ref[0]
    total = None
    for d in range(2):
        r = _sigmoid(gates[:, (2 * d) * LRU_CT:(2 * d + 1) * LRU_CT])
        ig = _sigmoid(gates[:, (2 * d + 1) * LRU_CT:(2 * d + 2) * LRU_CT])
        nl = -lam_ref[d:d + 1, :]
        softplus = jnp.maximum(nl, 0.0) + jnp.log1p(jnp.exp(-jnp.abs(nl)))
        log_a = -LRU_C * r * softplus
        a = jnp.exp(log_a)
        bx = jnp.sqrt(jnp.tanh(-log_a) * (a * a + 1.0)) * (ig * u)
        edge = t - 1 if d == 1 else 0
        bx = jnp.where(row == edge, bx + a * h0_ref[0, d:d + 1, :], bx)
        bx = _linear_scan(a, bx, row, t, d == 1, sa_ref, sb_ref, sc_ref)
        fin = t - 1 if d == 0 else 0
        st_ref[0, d:d + 1, :] = bx[fin:fin + 1, :]
        total = bx if total is None else total + bx
    g = g_ref[0]
    gelu = 0.5 * g * (1.0 + jnp.tanh(math.sqrt(2.0 / math.pi) * (g + 0.044715 * (g * g * g))))
    y_ref[0] = (total * gelu).astype(BF16)


def _lru_mixer(f, conv_w, conv_b, w_gates, b_gates, lam, h0, *, nb, t):
    f3 = f.reshape(nb, t, f.shape[1])
    nct = LRU_WIDTH // LRU_CT
    y, st = pl.pallas_call(
        functools.partial(_lru_kernel, t=t),
        out_shape=[jax.ShapeDtypeStruct((nb, t, LRU_WIDTH), BF16),
                   jax.ShapeDtypeStruct((nb, 2, LRU_WIDTH), F32)],
        grid=(nb, nct),
        in_specs=[
            pl.BlockSpec((1, t, LRU_CT), lambda i, c: (i, 0, c)),
            pl.BlockSpec((1, t, LRU_CT), lambda i, c: (i, 0, nct + c)),
            pl.BlockSpec((CONV_W, LRU_CT), lambda i, c: (0, c)),
            pl.BlockSpec((1, LRU_CT), lambda i, c: (0, c)),
            pl.BlockSpec((1, LRU_CT, 4 * LRU_CT), lambda i, c: (c, 0, 0)),
            pl.BlockSpec((1, 1, 4 * LRU_CT), lambda i, c: (c, 0, 0)),
            pl.BlockSpec((2, LRU_CT), lambda i, c: (0, c)),
            pl.BlockSpec((1, 2, LRU_CT), lambda i, c: (i, 0, c)),
        ],
        out_specs=[pl.BlockSpec((1, t, LRU_CT), lambda i, c: (i, 0, c)),
                   pl.BlockSpec((1, 2, LRU_CT), lambda i, c: (i, 0, c))],
        scratch_shapes=[pltpu.VMEM((t, LRU_CT), F32)] * 3,
        compiler_params=_params("arbitrary", "arbitrary"),
        name="lru_mixer",
    )(f3, f3, conv_w, conv_b.reshape(1, LRU_WIDTH), w_gates, b_gates, lam, h0)
    return y.reshape(nb * t, LRU_WIDTH), st


def _lru_gate_weights(wa, ba, wx, bx):
    nct = LRU_WIDTH // LRU_CT
    bpt = LRU_CT // LRU_BLOCK_W
    eye = jnp.eye(bpt, dtype=F32)

    def tile_w(w):
        w4 = w.reshape(nct, bpt, LRU_BLOCK_W, LRU_BLOCK_W)
        return jnp.einsum("cnij,nm->cnimj", w4, eye).reshape(nct, LRU_CT, LRU_CT)

    w = jnp.concatenate([tile_w(wa[0]), tile_w(wx[0]), tile_w(wa[1]), tile_w(wx[1])], axis=-1)
    b = jnp.concatenate([v.reshape(nct, 1, LRU_CT) for v in (ba[0], bx[0], ba[1], bx[1])], axis=-1)
    return w.astype(BF16), b


def _outproj_kernel(*refs, n_in):
    ins = refs[:n_in]
    w_ref, x_ref, g_ref, o_ref = refs[n_in:]
    acc = None
    off = 0
    for m_ref in ins:
        width = m_ref.shape[1]
        part = _dot(m_ref[...], w_ref[0, off:off + width, :])
        acc = part if acc is None else acc + part
        off += width
    o_ref[...] = x_ref[...] + g_ref[0] * acc


def _out_projection(mixes, w_out_bf, layer, x, mods_l, mod_row, *, tm):
    n_tok = x.shape[0]
    return pl.pallas_call(
        functools.partial(_outproj_kernel, n_in=len(mixes)),
        out_shape=jax.ShapeDtypeStruct((n_tok, D_MODEL), F32),
        grid=(n_tok // tm,),
        in_specs=[pl.BlockSpec((tm, m.shape[1]), lambda i: (i, 0)) for m in mixes] + [
            pl.BlockSpec((1, D_MODEL, D_MODEL), lambda i: (layer, 0, 0)),
            pl.BlockSpec((tm, D_MODEL), lambda i: (i, 0)),
            pl.BlockSpec((1, 1, D_MODEL), lambda i: (mod_row(i) * MOD_CHUNKS + 2, 0, 0)),
        ],
        out_specs=pl.BlockSpec((tm, D_MODEL), lambda i: (i, 0)),
        compiler_params=_params("arbitrary"),
        name="out_projection",
    )(*mixes, w_out_bf, x, mods_l)


def _ffn_kernel(x_ref, sh_ref, sc_ref, g_ref, nw_ref, wg_ref, wu_ref, wd_ref, o_ref, h_scr, *, tm):
    k = pl.program_id(1)

    def step(first):
        wg = wg_ref[0].astype(BF16)
        wu = wu_ref[0].astype(BF16)
        wd = wd_ref[0].astype(BF16)
        chunk = ROW_CHUNK if first else FFN_ROW_CHUNK

        def norm_chunk(c):
            _modnorm_static(x_ref, nw_ref, sh_ref, sc_ref, h_scr, c * chunk, chunk)

        if first:
            norm_chunk(0)
        for c in range(tm // chunk):
            if first and c + 1 < tm // chunk:
                norm_chunk(c + 1)
            rows = slice(c * chunk, (c + 1) * chunk)
            h = h_scr[rows, :]
            g = _dot(h, wg)
            u = _dot(h, wu)
            act = (g * _sigmoid(g)) * u
            down = _dot(act.astype(BF16), wd)
            if first:
                o_ref[rows, :] = down
            else:
                o_ref[rows, :] += down

    @pl.when(k == 0)
    def _():
        step(True)

    @pl.when(k > 0)
    def _():
        step(False)

    @pl.when(k == pl.num_programs(1) - 1)
    def _():
        def body(c, carry):
            r = pl.multiple_of(c * 128, 128)
            o_ref[pl.ds(r, 128), :] = x_ref[pl.ds(r, 128), :] + g_ref[0] * o_ref[pl.ds(r, 128), :]
            return carry
        lax.fori_loop(0, tm // 128, body, 0)


def _ffn(x, mods_l, mod_row, norm_w, wg, wu, wd, layer, *, tm, th):
    n_tok = x.shape[0]
    return pl.pallas_call(
        functools.partial(_ffn_kernel, tm=tm),
        out_shape=jax.ShapeDtypeStruct((n_tok, D_MODEL), F32),
        grid=(n_tok // tm, FFN_HIDDEN // th),
        in_specs=[
            pl.BlockSpec((tm, D_MODEL), lambda i, k: (i, 0), pipeline_mode=pl.Buffered(1)),
            pl.BlockSpec((1, 1, D_MODEL), lambda i, k: (mod_row(i) * MOD_CHUNKS + 3, 0, 0)),
            pl.BlockSpec((1, 1, D_MODEL), lambda i, k: (mod_row(i) * MOD_CHUNKS + 4, 0, 0)),
            pl.BlockSpec((1, 1, D_MODEL), lambda i, k: (mod_row(i) * MOD_CHUNKS + 5, 0, 0)),
            pl.BlockSpec((1, D_MODEL), lambda i, k: (0, 0)),
            pl.BlockSpec((1, D_MODEL, th), lambda i, k: (layer, 0, k)),
            pl.BlockSpec((1, D_MODEL, th), lambda i, k: (layer, 0, k)),
            pl.BlockSpec((1, th, D_MODEL), lambda i, k: (layer, k, 0)),
        ],
        out_specs=pl.BlockSpec((tm, D_MODEL), lambda i, k: (i, 0)),
        scratch_shapes=[pltpu.VMEM((tm, D_MODEL), BF16)],
        compiler_params=_params("arbitrary", "arbitrary"),
        name="ffn",
    )(x, mods_l, mods_l, mods_l, norm_w.reshape(1, D_MODEL), wg, wu, wd)


def _rope_tables(t_lat):
    nf = HEAD_DIM // 4
    tok = jnp.arange(t_lat)
    pos = jnp.stack([tok // GRID_W, tok % GRID_W], axis=-1).astype(F32)
    inv = ROPE_BASE ** (-jnp.arange(nf, dtype=F32) / nf)
    ang = pos[:, :, None] * inv
    cos, sin = jnp.cos(ang), jnp.sin(ang)
    cos_h = jnp.concatenate([cos[:, 0], cos[:, 0], cos[:, 1], cos[:, 1]], axis=-1)
    sin_h = jnp.concatenate([-sin[:, 0], sin[:, 0], -sin[:, 1], sin[:, 1]], axis=-1)
    reps = PROJ_HALF // HEAD_DIM
    return jnp.tile(cos_h, (1, reps)), jnp.tile(sin_h, (1, reps))


def _tile_table():
    lo = [_FIRST_BLOCK[n] for n in TILE_NAMES]
    hi = [b + 1 for b in lo]
    cls = [_TILE_CLASS[n] for n in TILE_NAMES]
    return jnp.asarray(np.array(lo + hi + cls, np.int32))


def _tile_gains(qk_gain_l):
    def tiled(g, n):
        return jnp.tile(g, n // HEAD_DIM)
    ones = jnp.ones((PROJ_TILE,), F32)
    per_tile = {
        "lx": ones, "lg": ones, "nv": ones, "dv": ones,
        "nk": tiled(qk_gain_l[1, 1], PROJ_TILE), "dk": tiled(qk_gain_l[2, 1], PROJ_TILE),
        "skv": jnp.concatenate([tiled(qk_gain_l[0, 1], 128), jnp.ones((PROJ_TILE - 128,), F32)]),
        "sq": tiled(qk_gain_l[0, 0], PROJ_TILE) * QK_SCALE,
        "nq": tiled(qk_gain_l[1, 0], PROJ_TILE) * QK_SCALE,
        "dq": tiled(qk_gain_l[2, 0], PROJ_TILE) * QK_SCALE,
    }
    return jnp.stack([per_tile[n] for n in TILE_NAMES]).reshape(N_TILES, 1, PROJ_TILE)


def _lambda_init(layer):
    return 0.8 - 0.6 * math.exp(-0.3 * layer)


def kernel(x_prompt, x_sample, cache_swa_k, cache_swa_v, cache_na_k, cache_na_v, cache_diff_k, cache_diff_v, state_lru, c, c_ctx, norm_mix, norm_ffn, w_mod, b_mod, w_in, w_out, qk_gain, swa_sink, na_rpb, diff_lambda, diff_subln, conv_w, conv_b, lru_wa, lru_ba, lru_wx, lru_bx, lru_L, w_ffn_gate, w_ffn_up, w_ffn_down):
    bc, s_ctx, _ = x_prompt.shape
    bd, t_lat, _ = x_sample.shape
    n_ctx, n_lat = bc * s_ctx, bd * t_lat
    rows = t_lat // GRID_W
    p_ctx = cache_swa_k.shape[2]

    cvecs = jnp.concatenate([c_ctx[None, :], c, jnp.zeros((8 - 1 - bd, D_MODEL), F32)], axis=0)
    mods = _modulation(cvecs, w_mod, b_mod).reshape(DEPTH, 8 * MOD_CHUNKS, 1, D_MODEL)

    hsum = jnp.asarray(np.kron(np.eye(PROJ_HALF // HEAD_DIM, dtype=np.float32),
                               np.full((HEAD_DIM, HEAD_DIM), 1.0 / HEAD_DIM, np.float32)), BF16)
    rope_tabs = _rope_tables(t_lat)
    tab = _tile_table()

    tm = 1024
    tm_out = 512
    ctx_row = lambda tile: (lambda i: 0)
    lat_row = lambda tile: (lambda i: 1 + (i * tile) // t_lat)

    xc = x_prompt.reshape(n_ctx, D_MODEL)
    xs = x_sample.reshape(n_lat, D_MODEL)
    zero_state = jnp.zeros((bc, 2, LRU_WIDTH), F32)
    caches = [jnp.zeros((bc, DEPTH, s_ctx, _CACHE_SRC[name][2]), F32) for name in CACHE_NAMES]
    lru_states = []
    w_in_bf = w_in.astype(BF16)
    w_out_bf = w_out.astype(BF16)

    def cached(arr):
        return arr.astype(BF16).reshape(bd * DEPTH, p_ctx, -1)

    c_swa_k, c_swa_v = cached(cache_swa_k), cached(cache_swa_v)
    c_na_k, c_na_v = cached(cache_na_k), cached(cache_na_v)
    c_diff_k, c_diff_v = cached(cache_diff_k), cached(cache_diff_v)

    for l in range(DEPTH):
        lam_init = _lambda_init(l)
        mods_l = mods[l]
        gain = _tile_gains(qk_gain[l])
        w_gates, b_gates = _lru_gate_weights(lru_wa[l], lru_ba[l], lru_wx[l], lru_bx[l])
        ffn_w = (w_ffn_gate, w_ffn_up, w_ffn_down, l)

        p, f, caches = _in_projection(xc, mods_l, ctx_row(tm), norm_mix[l], w_in_bf, l, tab, gain, hsum,
                                      None, caches, tm=tm, seq=s_ctx)
        mix3 = _ctx_attention(p, swa_sink[l], diff_lambda[l], diff_subln[l], lam_init, nb=bc, t=s_ctx)
        od, st = _lru_mixer(f, conv_w[l], conv_b[l], w_gates, b_gates, lru_L[l], zero_state, nb=bc, t=s_ctx)
        lru_states.append(st)
        xc = _out_projection([mix3, od], w_out_bf, l, xc, mods_l, ctx_row(tm_out), tm=tm_out)
        xc = _ffn(xc, mods_l, ctx_row(tm), norm_ffn[l], *ffn_w, tm=tm, th=512)

        p, f, _ = _in_projection(xs, mods_l, lat_row(tm), norm_mix[l], w_in_bf, l, tab, gain, hsum,
                                 rope_tabs, None, tm=tm, seq=t_lat)
        oa = _swa_latent(p, c_swa_k, c_swa_v, l, swa_sink[l], nb=bd, t=t_lat)
        bias = _na_bias(na_rpb[l], rows)
        ob = _na_latent(p, c_na_k, c_na_v, l, bias, nb=bd, t=t_lat)
        oc = _diff_latent(p, c_diff_k, c_diff_v, l, diff_lambda[l], diff_subln[l], lam_init, nb=bd, t=t_lat)
        od, _ = _lru_mixer(f, conv_w[l], conv_b[l], w_gates, b_gates, lru_L[l], state_lru[:, l], nb=bd, t=t_lat)
        xs = _out_projection([oa, ob, oc, od], w_out_bf, l, xs, mods_l, lat_row(tm_out), tm=tm_out)
        xs = _ffn(xs, mods_l, lat_row(tm), norm_ffn[l], *ffn_w, tm=tm, th=512)

    swa_k, swa_v, na_k, na_v, diff_k, diff_v = caches
    return (xc.reshape(bc, s_ctx, D_MODEL), xs.reshape(bd, t_lat, D_MODEL),
            swa_k.reshape(bc, DEPTH, s_ctx, SWA_KV_HEADS, HEAD_DIM),
            swa_v.reshape(bc, DEPTH, s_ctx, SWA_KV_HEADS, HEAD_DIM),
            na_k.reshape(bc, DEPTH, s_ctx, NA_HEADS, HEAD_DIM),
            na_v.reshape(bc, DEPTH, s_ctx, NA_HEADS, HEAD_DIM),
            diff_k.reshape(bc, DEPTH, s_ctx, DIFF_HEADS, 2, HEAD_DIM),
            diff_v.reshape(bc, DEPTH, s_ctx, DIFF_HEADS, DIFF_VDIM),
            jnp.stack(lru_states, axis=1))
```

```python
import functools
import math

import jax
import jax.numpy as jnp
import numpy as np
from jax import lax
from jax.experimental import pallas as pl
from jax.experimental.pallas import tpu as pltpu

F32 = jnp.float32
BF16 = jnp.bfloat16

D_MODEL = 2048
DEPTH = 2
GRID_W = 64
HEAD_DIM = 64
GROUP_W = 512
SWA_HEADS = 8
SWA_KV_HEADS = 2
SWA_GROUP = 4
SWA_WINDOW = 128
NA_HEADS = 8
NA_KH = 8
NA_KW = 16
DIFF_HEADS = 4
DIFF_VDIM = 128
LRU_WIDTH = 512
LRU_BLOCKS = 8
LRU_BLOCK_W = 64
LRU_C = 8.0
CONV_W = 4
FFN_HIDDEN = 5632
ROPE_BASE = 10000.0
NORM_EPS = 1e-6
NEG_INF = -1e30
MOD_CHUNKS = 6
PROJ_W = 4864
LOG2E = math.log2(math.e)
QK_SCALE = HEAD_DIM ** -0.5 * LOG2E

V7X_VMEM_LIMIT = 60 * 1024 * 1024

PROJ_HALF = 256
PROJ_TILE = 2 * PROJ_HALF
TILE_NAMES = ("lx", "lg", "nk", "dk", "nv", "dv", "skv", "sq", "nq", "dq")
N_TILES = len(TILE_NAMES)
_FIRST_BLOCK = {"sq": 0, "skv": 2, "nq": 3, "nk": 5, "nv": 7, "dq": 9, "dk": 11, "dv": 13, "lx": 15, "lg": 17}
CLS_PLAIN, CLS_NORM, CLS_ROPE, CLS_MIXED = 0, 1, 2, 3
_TILE_CLASS = {"lx": CLS_PLAIN, "lg": CLS_PLAIN, "nv": CLS_PLAIN, "dv": CLS_PLAIN,
               "nk": CLS_NORM, "nq": CLS_NORM, "dk": CLS_ROPE, "sq": CLS_ROPE, "dq": CLS_ROPE,
               "skv": CLS_MIXED}
P_FIRST = 2
P_TILES = N_TILES - P_FIRST
P_COL = {name: TILE_NAMES.index(name) - P_FIRST for name in TILE_NAMES[P_FIRST:]}
F_TILES = 2

NA_QT = 128
NA_WIN_ROWS = 10
NA_WIN = NA_WIN_ROWS * GRID_W
LRU_CT = 128
ROW_CHUNK = 256
FFN_ROW_CHUNK = 512


def _sigmoid(x):
    return 1.0 / (1.0 + jnp.exp(-x))


def _dot(a, b):
    return jnp.dot(a, b, preferred_element_type=F32)


def _dot_nt(a, b):
    return lax.dot_general(a, b, (((1,), (1,)), ((), ())), preferred_element_type=F32)


def _params(*sem):
    return pltpu.CompilerParams(dimension_semantics=sem, vmem_limit_bytes=V7X_VMEM_LIMIT)


def _mod_kernel(c_ref, w_ref, b_ref, o_ref):
    cv = c_ref[...]
    s = cv * _sigmoid(cv)
    o_ref[0] = _dot(s.astype(BF16), w_ref[0].astype(BF16)) + b_ref[0]


def _modulation(cvecs, w_mod, b_mod):
    tn = 1024
    n = MOD_CHUNKS * D_MODEL
    return pl.pallas_call(
        _mod_kernel,
        out_shape=jax.ShapeDtypeStruct((DEPTH, 8, n), F32),
        grid=(DEPTH, n // tn),
        in_specs=[
            pl.BlockSpec((8, D_MODEL), lambda l, j: (0, 0)),
            pl.BlockSpec((1, D_MODEL, tn), lambda l, j: (l, 0, j)),
            pl.BlockSpec((1, 1, tn), lambda l, j: (l, 0, j)),
        ],
        out_specs=pl.BlockSpec((1, 8, tn), lambda l, j: (l, 0, j)),
        compiler_params=_params("arbitrary", "arbitrary"),
        name="modulation",
    )(cvecs, w_mod, b_mod.reshape(DEPTH, 1, n))


def _modnorm_static(x_ref, nw_ref, sh_ref, sc_ref, h_scr, r0, n_rows, chunk=128):
    for c in range(n_rows // chunk):
        rows = slice(r0 + c * chunk, r0 + (c + 1) * chunk)
        x = x_ref[rows, :]
        ms = jnp.mean(x * x, axis=-1, keepdims=True)
        y = x * lax.rsqrt(ms + NORM_EPS) * nw_ref[...]
        h_scr[rows, :] = (y * (1.0 + sc_ref[0]) + sh_ref[0]).astype(BF16)


CACHE_NAMES = ("swa_k", "swa_v", "na_k", "na_v", "diff_k", "diff_v")
_CACHE_SRC = {"swa_k": ("skv", 0, 128), "swa_v": ("skv", 128, 128), "na_k": ("nk", 0, 512),
              "na_v": ("nv", 0, 512), "diff_k": ("dk", 0, 512), "diff_v": ("dv", 0, 512)}


def _inproj_kernel(*refs, tm, rope, n_cache, seq):
    n_in = 10 if rope else 8
    tab_ref = refs[0]
    (x_ref, sh_ref, sc_ref, nw_ref, wlo_ref, whi_ref, gain_ref, hsum_ref) = refs[1:9]
    cos_ref, sin_ref = (refs[9], refs[10]) if rope else (None, None)
    outs = refs[1 + n_in + n_cache:]
    p_ref, f_ref = outs[0], outs[1]
    cache_refs = outs[2:2 + n_cache]
    h_scr = outs[2 + n_cache]
    j = pl.program_id(1)
    cls = tab_ref[2 * N_TILES + j]

    def run_tile(mode, halves, first=False):
        w = {half: (wlo_ref if half == 0 else whi_ref)[0] for half in halves}
        n_chunks = tm // ROW_CHUNK
        units = [(half, c) for half in halves for c in range(n_chunks)]

        def norm_chunk(c):
            _modnorm_static(x_ref, nw_ref, sh_ref, sc_ref, h_scr, c * ROW_CHUNK, ROW_CHUNK)

        if first:
            norm_chunk(0)

        def main(unit):
            half, c = unit
            if first and half == halves[0] and c + 1 < n_chunks:
                norm_chunk(c + 1)
            return _dot(h_scr[c * ROW_CHUNK:(c + 1) * ROW_CHUNK, :], w[half])

        def finish(unit, p):
            half, c = unit
            rows = slice(c * ROW_CHUNK, (c + 1) * ROW_CHUNK)
            cols = slice(half * PROJ_HALF, (half + 1) * PROJ_HALF)
            y = p
            if mode != CLS_PLAIN:
                ms = _dot((p * p).astype(BF16), hsum_ref[...])
                y = p * lax.rsqrt(ms + NORM_EPS) * gain_ref[0, :, cols]
                if rope and mode in (CLS_ROPE, CLS_MIXED):
                    lane = lax.broadcasted_iota(jnp.int32, y.shape, 1)
                    up = pltpu.roll(y, PROJ_HALF - 16, 1)
                    down = pltpu.roll(y, 16, 1)
                    partner = jnp.where((lane & 31) < 16, up, down)
                    y = y * cos_ref[rows, :] + partner * sin_ref[rows, :]
                if mode == CLS_MIXED:
                    lane = lax.broadcasted_iota(jnp.int32, y.shape, 1)
                    y = jnp.where(lane < 2 * HEAD_DIM, y, p)
            p_ref[rows, cols] = y.astype(BF16)
            f_ref[rows, cols] = y

        prev, p_prev = units[0], main(units[0])
        for unit in units[1:]:
            p_next = main(unit)
            finish(prev, p_prev)
            prev, p_prev = unit, p_next
        finish(prev, p_prev)

    assert _TILE_CLASS[TILE_NAMES[0]] == CLS_PLAIN

    @pl.when(j == 0)
    def _():
        run_tile(CLS_PLAIN, (0, 1), first=True)

    modes = (CLS_PLAIN, CLS_NORM, CLS_ROPE) if rope else (CLS_PLAIN, CLS_NORM)
    for mode in modes:
        cond = (cls == mode)
        if mode == CLS_PLAIN:
            cond = cond & (j > 0)
        if not rope and mode == CLS_NORM:
            cond = (cls == CLS_NORM) | (cls == CLS_ROPE)

        @pl.when(cond)
        def _(mode=mode):
            run_tile(mode, (0, 1))

    @pl.when(cls == CLS_MIXED)
    def _():
        run_tile(CLS_MIXED, (0,))
        hi = slice(PROJ_HALF, PROJ_TILE)
        p_ref[:, hi] = jnp.zeros((tm, PROJ_HALF), BF16)
        f_ref[:, hi] = jnp.zeros((tm, PROJ_HALF), F32)

    for name, c_ref in zip(CACHE_NAMES, cache_refs):
        tile, off, width = _CACHE_SRC[name]

        @pl.when(j == TILE_NAMES.index(tile))
        def _(c_ref=c_ref, off=off, width=width):
            for b in range(tm // seq):
                c_ref[b, 0] = f_ref[b * seq:(b + 1) * seq, off:off + width]


def _in_projection(x, mods_l, mod_row, norm_w, w_in_bf, layer, tab, gain, hsum, rope_tabs, caches, *, tm, seq):
    n_tok = x.shape[0]
    rope = rope_tabs is not None
    caches = [] if caches is None else list(caches)
    n_cache = len(caches)

    in_specs = [
        pl.BlockSpec((tm, D_MODEL), lambda i, j, t: (i, 0)),
        pl.BlockSpec((1, 1, D_MODEL), lambda i, j, t: (mod_row(i) * MOD_CHUNKS + 0, 0, 0)),
        pl.BlockSpec((1, 1, D_MODEL), lambda i, j, t: (mod_row(i) * MOD_CHUNKS + 1, 0, 0)),
        pl.BlockSpec((1, D_MODEL), lambda i, j, t: (0, 0)),
        pl.BlockSpec((1, D_MODEL, PROJ_HALF), lambda i, j, t: (layer, 0, t[j])),
        pl.BlockSpec((1, D_MODEL, PROJ_HALF), lambda i, j, t: (layer, 0, t[N_TILES + j])),
        pl.BlockSpec((1, 1, PROJ_TILE), lambda i, j, t: (j, 0, 0)),
        pl.BlockSpec((PROJ_HALF, PROJ_HALF), lambda i, j, t: (0, 0)),
    ]
    args = [x, mods_l, mods_l, norm_w.reshape(1, D_MODEL), w_in_bf, w_in_bf, gain, hsum]
    if rope:
        cos_t, sin_t = rope_tabs
        in_specs += [pl.BlockSpec((tm, PROJ_HALF), lambda i, j, t: (0, 0)),
                     pl.BlockSpec((tm, PROJ_HALF), lambda i, j, t: (0, 0))]
        args += [cos_t, sin_t]
    n_in = len(args)
    in_specs += [pl.BlockSpec(memory_space=pl.ANY)] * n_cache
    args += caches
    out_specs = [
        pl.BlockSpec((tm, PROJ_TILE), lambda i, j, t: (i, jnp.maximum(j - P_FIRST, 0))),
        pl.BlockSpec((tm, PROJ_TILE), lambda i, j, t: (i, jnp.minimum(j, F_TILES))),
    ]
    out_shape = [jax.ShapeDtypeStruct((n_tok, P_TILES * PROJ_TILE), BF16),
                 jax.ShapeDtypeStruct((n_tok, (F_TILES + 1) * PROJ_TILE), F32)]
    for arr in caches:
        width = arr.shape[-1]
        out_specs.append(pl.BlockSpec((tm // seq, 1, seq, width), lambda i, j, t: (i, layer, 0, 0)))
        out_shape.append(jax.ShapeDtypeStruct(arr.shape, arr.dtype))
    res = pl.pallas_call(
        functools.partial(_inproj_kernel, tm=tm, rope=rope, n_cache=n_cache, seq=seq),
        out_shape=out_shape,
        grid_spec=pltpu.PrefetchScalarGridSpec(
            num_scalar_prefetch=1,
            grid=(n_tok // tm, N_TILES),
            in_specs=in_specs,
            out_specs=out_specs,
            scratch_shapes=[pltpu.VMEM((tm, D_MODEL), BF16)],
        ),
        input_output_aliases={1 + n_in + k: 2 + k for k in range(n_cache)},
        compiler_params=_params("arbitrary", "arbitrary"),
        name="in_projection_rope" if rope else "in_projection",
    )(tab, *args)
    return res[0], res[1], list(res[2:])


def _joint(scores):
    return scores[0] if len(scores) == 1 else jnp.concatenate(scores, axis=1)


def _softmax_pv(segs, extra=None):
    s = _joint([sc for sc, _ in segs])
    m = jnp.max(s, axis=-1, keepdims=True)
    if extra is not None:
        m = jnp.maximum(m, extra)
    e = jnp.exp2(s - m)
    den = jnp.sum(e, axis=-1, keepdims=True)
    if extra is not None:
        den = den + jnp.exp2(extra - m)
    e = e.astype(BF16)
    o, off = None, 0
    for sc, v in segs:
        n = sc.shape[1]
        part = _dot(e[:, off:off + n], v)
        o = part if o is None else o + part
        off += n
    return o / den


def _probs(scores):
    s = _joint(scores)
    e = jnp.exp2(s - jnp.max(s, axis=-1, keepdims=True))
    return e * (1.0 / jnp.sum(e, axis=-1, keepdims=True))


def _half_vecs():
    lane = lax.broadcasted_iota(jnp.int32, (1, 2 * HEAD_DIM), 1)
    lo = (lane < HEAD_DIM).astype(BF16)
    return lo, (1.0 - lo).astype(BF16)


def _pair_block(ref, m, rows=None, b=0):
    if rows is None:
        return ref[b, :, m * 128:(m + 1) * 128]
    return ref[b, rows, m * 128:(m + 1) * 128]


def _merge_pair(o_even, o_odd):
    lane = lax.broadcasted_iota(jnp.int32, o_even.shape, 1)
    return jnp.where(lane < HEAD_DIM, o_even, o_odd)


def _diff_lambda_val(dl_ref, lam_init):
    lp = dl_ref[...]
    s1 = jnp.sum(lp[0:1] * lp[1:2], axis=-1, keepdims=True)
    s2 = jnp.sum(lp[2:3] * lp[3:4], axis=-1, keepdims=True)
    return jnp.exp(s1) - jnp.exp(s2) + lam_init


def _diff_finish(o, subln_ref, lam_init):
    ms = jnp.mean(o * o, axis=-1, keepdims=True)
    return o * lax.rsqrt(ms + NORM_EPS) * subln_ref[...] * (1.0 - lam_init)


def _swap_halves_bf16(x):
    return pltpu.roll(x.astype(F32), HEAD_DIM, 1).astype(BF16)


def _gqa_queries(q_ref, hk, vecs, b=0, rows=None):
    parts = []
    for g in range(SWA_GROUP):
        h = hk * SWA_GROUP + g
        q = _pair_block(q_ref, h // 2, rows=rows, b=b)
        if h % 2 != hk:
            q = _swap_halves_bf16(q)
        parts.append(q * vecs[hk])
    return jnp.concatenate(parts, axis=0)


def _gqa_outputs(o, hk, rows):
    blocks = []
    for j in range(SWA_GROUP // 2):
        halves = []
        for g in (2 * j, 2 * j + 1):
            og = o[g * rows:(g + 1) * rows]
            halves.append(og if g % 2 == hk else pltpu.roll(og, HEAD_DIM, 1))
        blocks.append(_merge_pair(halves[0], halves[1]))
    return blocks


def _pipelined(items, depth=2):
    pending, outs = [], []
    for score_fn, finish_fn in items:
        pending.append((finish_fn, score_fn()))
        if len(pending) > depth:
            fn, s = pending.pop(0)
            outs.append(fn(s))
    for fn, s in pending:
        outs.append(fn(s))
    return outs


def _ctx_attn_kernel(sink_ref, sq_ref, skv_ref, nq_ref, nk_ref, nv_ref, dq_ref, dk_ref, dv_ref,
                     dl_ref, subln_ref, o_ref, *, t, lam_init, bb):
    lam = _diff_lambda_val(dl_ref, lam_init)
    vecs = _half_vecs()

    def swa_item(b, hk):
        def scores():
            return _dot_nt(_gqa_queries(sq_ref, hk, vecs, b=b), _pair_block(skv_ref, 0, b=b))

        def finish(s):
            sink = jnp.concatenate(
                [jnp.full((t, 1), sink_ref[hk * SWA_GROUP + g] * LOG2E, F32) for g in range(SWA_GROUP)], axis=0)
            return _gqa_outputs(_softmax_pv([(s, _pair_block(skv_ref, 1, b=b))], extra=sink), hk, t)
        return scores, finish

    def na_item(b, m):
        def scores():
            q, k = _pair_block(nq_ref, m, b=b), _pair_block(nk_ref, m, b=b)
            return [_dot_nt(q * vecs[i], k) for i in range(2)]

        def finish(s):
            v = _pair_block(nv_ref, m, b=b)
            return [_merge_pair(_softmax_pv([(s[0], v)]), _softmax_pv([(s[1], v)]))]
        return scores, finish

    def diff_item(b, h):
        def scores():
            q, k = _pair_block(dq_ref, h, b=b), _pair_block(dk_ref, h, b=b)
            return [_dot_nt(q * vecs[i], k) for i in range(2)]

        def finish(s):
            pd = _probs([s[0]]) - lam * _probs([s[1]])
            o = _dot(pd.astype(BF16), _pair_block(dv_ref, h, b=b))
            return [_diff_finish(o, subln_ref, lam_init)]
        return scores, finish

    items = []
    for make, n in ((swa_item, SWA_KV_HEADS), (na_item, NA_HEADS // 2), (diff_item, DIFF_HEADS)):
        items += [make(b, i) for i in range(n) for b in range(bb)]
    res = _pipelined(items, depth=2 * bb)
    for b in range(bb):
        pieces = [piece for r in res[b::bb] for piece in r]
        o_ref[b] = jnp.concatenate(pieces, axis=1).astype(BF16)


CTX_ROWS_PER_STEP = 2


def _pcol(name, t, bb=1):
    return pl.BlockSpec((bb, t, PROJ_TILE), lambda i, c=P_COL[name]: (i, 0, c))


def _ctx_attention(p, sink, dlam, subln, lam_init, *, nb, t):
    bb = CTX_ROWS_PER_STEP
    p3 = p.reshape(nb, t, P_TILES * PROJ_TILE)
    names = ("sq", "skv", "nq", "nk", "nv", "dq", "dk", "dv")
    out = pl.pallas_call(
        functools.partial(_ctx_attn_kernel, t=t, lam_init=lam_init, bb=bb),
        out_shape=jax.ShapeDtypeStruct((nb, t, 3 * GROUP_W), BF16),
        grid=(nb // bb,),
        in_specs=[pl.BlockSpec(memory_space=pltpu.SMEM)] + [_pcol(n, t, bb) for n in names] + [
            pl.BlockSpec((4, HEAD_DIM), lambda i: (0, 0)),
            pl.BlockSpec((1, DIFF_VDIM), lambda i: (0, 0)),
        ],
        out_specs=pl.BlockSpec((bb, t, 3 * GROUP_W), lambda i: (i, 0, 0)),
        compiler_params=_params("arbitrary"),
        name="ctx_attention",
    )(sink, *([p3] * len(names)), dlam, subln.reshape(1, DIFF_VDIM))
    return out.reshape(nb * t, 3 * GROUP_W)


Q_BLOCKS_PER_STEP = 2


def _swa_kernel(sink_ref, q_ref, kv_ref, ck_ref, cv_ref, o_ref, *, t, qb):
    span = 3 * qb
    vecs = _half_vecs()

    def item(j, hk):
        n = pl.program_id(1) * Q_BLOCKS_PER_STEP + j
        start = pl.multiple_of(jnp.clip((n - 1) * qb, 0, t - span), qb)
        win = pl.ds(start, span)
        qrows = slice(j * qb, (j + 1) * qb)

        def scores():
            qs = _gqa_queries(q_ref, hk, vecs, rows=qrows)
            return _dot_nt(qs, _pair_block(kv_ref, 0, rows=win)), _dot_nt(qs, ck_ref[0])

        def finish(s):
            row = (lax.broadcasted_iota(jnp.int32, (SWA_GROUP * qb, span), 0) & (qb - 1)) + n * qb
            col = lax.broadcasted_iota(jnp.int32, (SWA_GROUP * qb, span), 1) + start
            dist = row - col
            ok = (dist <= SWA_WINDOW) & (dist >= -SWA_WINDOW)
            sink = jnp.concatenate(
                [jnp.full((qb, 1), sink_ref[hk * SWA_GROUP + g] * LOG2E, F32) for g in range(SWA_GROUP)], axis=0)
            s_loc = jnp.where(ok, s[0], NEG_INF)
            o = _softmax_pv([(s_loc, _pair_block(kv_ref, 1, rows=win)), (s[1], cv_ref[0])], extra=sink)
            return _gqa_outputs(o, hk, qb)
        return scores, finish

    items = [item(j, hk) for hk in range(SWA_KV_HEADS) for j in range(Q_BLOCKS_PER_STEP)]
    res = _pipelined(items, depth=Q_BLOCKS_PER_STEP)
    for j in range(Q_BLOCKS_PER_STEP):
        pieces = [piece for r in res[j::Q_BLOCKS_PER_STEP] for piece in r]
        o_ref[0, j * qb:(j + 1) * qb, :] = jnp.concatenate(pieces, axis=1).astype(BF16)


def _swa_latent(p, ck, cv, layer, sink, *, nb, t):
    qb = 128
    step_rows = qb * Q_BLOCKS_PER_STEP
    p3 = p.reshape(nb, t, P_TILES * PROJ_TILE)
    out = pl.pallas_call(
        functools.partial(_swa_kernel, t=t, qb=qb),
        out_shape=jax.ShapeDtypeStruct((nb, t, GROUP_W), BF16),
        grid=(nb, t // step_rows),
        in_specs=[
            pl.BlockSpec(memory_space=pltpu.SMEM),
            pl.BlockSpec((1, step_rows, PROJ_TILE), lambda b, n: (b, n, P_COL["sq"])),
            pl.BlockSpec((1, t, PROJ_TILE), lambda b, n: (b, 0, P_COL["skv"])),
            pl.BlockSpec((1,) + ck.shape[1:], lambda b, n: (b * DEPTH + layer, 0, 0)),
            pl.BlockSpec((1,) + cv.shape[1:], lambda b, n: (b * DEPTH + layer, 0, 0)),
        ],
        out_specs=pl.BlockSpec((1, step_rows, GROUP_W), lambda b, n: (b, n, 0)),
        compiler_params=_params("arbitrary", "arbitrary"),
        name="swa_latent",
    )(sink, p3, p3, ck, cv)
    return out.reshape(nb * t, GROUP_W)


def _na_row_start(r, rows):
    kh = min(NA_KH, rows)
    return min(max(r - kh // 2, 0), rows - kh)


def _na_win_start(qt, rows):
    return min(max(_na_row_start(2 * qt, rows), 0), rows - NA_WIN_ROWS)


def _na_bias_kernel(rpb_ref, o_ref, *, rows):
    h = pl.program_id(0)
    n_dr, n_dc = 2 * NA_KH - 1, 2 * NA_KW - 1
    qi = lax.broadcasted_iota(jnp.int32, (GRID_W, GRID_W), 0)
    ki = lax.broadcasted_iota(jnp.int32, (GRID_W, GRID_W), 1)
    dc = jnp.clip(ki - qi + (NA_KW - 1), 0, n_dc - 1)
    cs = jnp.clip(qi - NA_KW // 2, 0, GRID_W - NA_KW)
    col_ok = (ki >= cs) & (ki < cs + NA_KW)
    neg = jnp.full((GRID_W, GRID_W), NEG_INF, F32)
    tabs = []
    for dr in range(n_dr):
        acc = jnp.zeros((GRID_W, GRID_W), F32)
        for c in range(n_dc):
            acc = jnp.where(dc == c, rpb_ref[(h * n_dr + dr) * n_dc + c], acc)
        tabs.append(jnp.where(col_ok, acc * LOG2E, neg))
    kh = min(NA_KH, rows)
    for qt in range(rows // 2):
        ws = _na_win_start(qt, rows)
        bands = []
        for qq in range(2):
            qr = 2 * qt + qq
            rs = _na_row_start(qr, rows)
            blks = []
            for kk in range(NA_WIN_ROWS):
                kr = ws + kk
                blks.append(tabs[kr - qr + NA_KH - 1] if rs <= kr < rs + kh else neg)
            bands.append(jnp.concatenate(blks, axis=1))
        o_ref[0, qt] = jnp.concatenate(bands, axis=0)


def _na_bias(rpb, rows):
    n_qt = rows // 2
    return pl.pallas_call(
        functools.partial(_na_bias_kernel, rows=rows),
        out_shape=jax.ShapeDtypeStruct((NA_HEADS, n_qt, NA_QT, NA_WIN), F32),
        grid=(NA_HEADS,),
        in_specs=[pl.BlockSpec(memory_space=pltpu.SMEM)],
        out_specs=pl.BlockSpec((1, n_qt, NA_QT, NA_WIN), lambda h: (h, 0, 0, 0)),
        compiler_params=_params("arbitrary"),
        name="na_bias",
    )(rpb.reshape(-1))


def _na_kernel(q_ref, k_ref, v_ref, ck_ref, cv_ref, bias_ref, o_ref, *, rows):
    vecs = _half_vecs()

    def item(j, m):
        qt = pl.program_id(1) * Q_BLOCKS_PER_STEP + j
        ws = jnp.clip(jnp.clip(2 * qt - NA_KH // 2, 0, rows - NA_KH), 0, rows - NA_WIN_ROWS)
        win = pl.ds(pl.multiple_of(ws * GRID_W, GRID_W), NA_WIN)
        qrows = slice(j * NA_QT, (j + 1) * NA_QT)

        def scores():
            q = _pair_block(q_ref, m, rows=qrows)
            k, ck = _pair_block(k_ref, m, rows=win), _pair_block(ck_ref, m)
            return [(_dot_nt(q * vecs[i], k), _dot_nt(q * vecs[i], ck)) for i in range(2)]

        def finish(s):
            v, cv = _pair_block(v_ref, m, rows=win), _pair_block(cv_ref, m)
            o = [_softmax_pv([(s[i][0] + bias_ref[2 * m + i, j], v), (s[i][1], cv)]) for i in range(2)]
            return _merge_pair(o[0], o[1])
        return scores, finish

    items = [item(j, m) for m in range(NA_HEADS // 2) for j in range(Q_BLOCKS_PER_STEP)]
    res = _pipelined(items, depth=2 * Q_BLOCKS_PER_STEP)
    for j in range(Q_BLOCKS_PER_STEP):
        o_ref[0, j * NA_QT:(j + 1) * NA_QT, :] = jnp.concatenate(res[j::Q_BLOCKS_PER_STEP], axis=1).astype(BF16)


def _na_latent(p, ck, cv, layer, bias, *, nb, t):
    rows = t // GRID_W
    p3 = p.reshape(nb, t, P_TILES * PROJ_TILE)
    out = pl.pallas_call(
        functools.partial(_na_kernel, rows=rows),
        out_shape=jax.ShapeDtypeStruct((nb, t, GROUP_W), BF16),
        grid=(nb, t // (NA_QT * Q_BLOCKS_PER_STEP)),
        in_specs=[
            pl.BlockSpec((1, NA_QT * Q_BLOCKS_PER_STEP, PROJ_TILE), lambda i, n: (i, n, P_COL["nq"])),
            pl.BlockSpec((1, t, PROJ_TILE), lambda i, n: (i, 0, P_COL["nk"])),
            pl.BlockSpec((1, t, PROJ_TILE), lambda i, n: (i, 0, P_COL["nv"])),
            pl.BlockSpec((1,) + ck.shape[1:], lambda i, n: (i * DEPTH + layer, 0, 0)),
            pl.BlockSpec((1,) + cv.shape[1:], lambda i, n: (i * DEPTH + layer, 0, 0)),
            pl.BlockSpec((NA_HEADS, Q_BLOCKS_PER_STEP, NA_QT, NA_WIN), lambda i, n: (0, n, 0, 0)),
        ],
        out_specs=pl.BlockSpec((1, NA_QT * Q_BLOCKS_PER_STEP, GROUP_W), lambda i, n: (i, n, 0)),
        compiler_params=_params("arbitrary", "arbitrary"),
        name="na_latent",
    )(p3, p3, p3, ck, cv, bias)
    return out.reshape(nb * t, GROUP_W)


def _diff_kernel(q_ref, k_ref, v_ref, ck_ref, cv_ref, dl_ref, subln_ref, o_ref, *, lam_init, tq):
    lam = _diff_lambda_val(dl_ref, lam_init)

    vecs = _half_vecs()

    def item(j, h):
        qrows = slice(j * tq, (j + 1) * tq)

        def scores():
            q = _pair_block(q_ref, h, rows=qrows)
            k, ck = _pair_block(k_ref, h), _pair_block(ck_ref, h)
            return [(_dot_nt(q * vecs[i], k), _dot_nt(q * vecs[i], ck)) for i in range(2)]

        def finish(s):
            pd = (_probs(list(s[0])) - lam * _probs(list(s[1]))).astype(BF16)
            n_loc = s[0][0].shape[1]
            o = _dot(pd[:, :n_loc], _pair_block(v_ref, h)) + _dot(pd[:, n_loc:], _pair_block(cv_ref, h))
            return _diff_finish(o, subln_ref, lam_init)
        return scores, finish

    items = [item(j, h) for h in range(DIFF_HEADS) for j in range(Q_BLOCKS_PER_STEP)]
    res = _pipelined(items, depth=Q_BLOCKS_PER_STEP)
    for j in range(Q_BLOCKS_PER_STEP):
        o_ref[0, j * tq:(j + 1) * tq, :] = jnp.concatenate(res[j::Q_BLOCKS_PER_STEP], axis=1).astype(BF16)


def _diff_latent(p, ck, cv, layer, dlam, subln, lam_init, *, nb, t):
    tq_sub = 256
    tq = tq_sub * Q_BLOCKS_PER_STEP
    p3 = p.reshape(nb, t, P_TILES * PROJ_TILE)
    out = pl.pallas_call(
        functools.partial(_diff_kernel, lam_init=lam_init, tq=tq_sub),
        out_shape=jax.ShapeDtypeStruct((nb, t, GROUP_W), BF16),
        grid=(nb, t // tq),
        in_specs=[
            pl.BlockSpec((1, tq, PROJ_TILE), lambda i, n: (i, n, P_COL["dq"])),
            pl.BlockSpec((1, t, PROJ_TILE), lambda i, n: (i, 0, P_COL["dk"])),
            pl.BlockSpec((1, t, PROJ_TILE), lambda i, n: (i, 0, P_COL["dv"])),
            pl.BlockSpec((1,) + ck.shape[1:], lambda i, n: (i * DEPTH + layer, 0, 0)),
            pl.BlockSpec((1,) + cv.shape[1:], lambda i, n: (i * DEPTH + layer, 0, 0)),
            pl.BlockSpec((4, HEAD_DIM), lambda i, n: (0, 0)),
            pl.BlockSpec((1, DIFF_VDIM), lambda i, n: (0, 0)),
        ],
        out_specs=pl.BlockSpec((1, tq, GROUP_W), lambda i, n: (i, n, 0)),
        compiler_params=_params("arbitrary", "arbitrary"),
        name="diff_latent",
    )(p3, p3, p3, ck, cv, dlam, subln.reshape(1, DIFF_VDIM))
    return out.reshape(nb * t, GROUP_W)


SUBLANES = 8


def _linear_scan(a, b, row, t, reverse, sa_ref, sb_ref, sc_ref):
    nt = t // SUBLANES
    sub = row & (SUBLANES - 1)

    def doubling(a, b, idx, n, steps):
        for s in steps:
            if reverse:
                ok = idx < n - s
                a_sh = jnp.where(ok, pltpu.roll(a, a.shape[0] - s, 0), 1.0)
                b_sh = jnp.where(ok, pltpu.roll(b, b.shape[0] - s, 0), 0.0)
            else:
                ok = idx >= s
                a_sh = jnp.where(ok, pltpu.roll(a, s, 0), 1.0)
                b_sh = jnp.where(ok, pltpu.roll(b, s, 0), 0.0)
            b = a * b_sh + b
            a = a * a_sh
        return a, b

    tiles = (nt, SUBLANES, LRU_CT)
    sub3 = lax.broadcasted_iota(jnp.int32, tiles, 1)

    def tile_doubling(a, b):
        for s in (1, 2, 4):
            ok = (sub3 < SUBLANES - s) if reverse else (sub3 >= s)
            sh = SUBLANES - s if reverse else s
            a_sh = jnp.where(ok, pltpu.roll(a, sh, 1), 1.0)
            b_sh = jnp.where(ok, pltpu.roll(b, sh, 1), 0.0)
            b = a * b_sh + b
            a = a * a_sh
        return a, b

    a, b = tile_doubling(a.reshape(tiles), b.reshape(tiles))
    a, b = a.reshape(t, LRU_CT), b.reshape(t, LRU_CT)
    sa_ref[...] = a
    sb_ref[...] = b
    last = 0 if reverse else SUBLANES - 1
    at = sa_ref[pl.ds(last, nt, stride=SUBLANES), :]
    bt = sb_ref[pl.ds(last, nt, stride=SUBLANES), :]
    rowt = lax.broadcasted_iota(jnp.int32, (nt, LRU_CT), 0)
    steps, s = [], 1
    while s < nt:
        steps.append(s)
        s *= 2
    _, bt = doubling(at, bt, rowt, nt, steps)
    if reverse:
        carry = jnp.where(rowt < nt - 1, pltpu.roll(bt, nt - 1, 0), 0.0)
    else:
        carry = jnp.where(rowt >= 1, pltpu.roll(bt, 1, 0), 0.0)
    for r in range(SUBLANES):
        sc_ref[pl.ds(r, nt, stride=SUBLANES), :] = carry
    return b + a * sc_ref[...]


def _lru_kernel(x_ref, g_ref, cw_ref, cb_ref, w_ref, gb_ref, lam_ref, h0_ref, y_ref, st_ref, *scratch, t, cpt):
    row = lax.broadcasted_iota(jnp.int32, (t, LRU_CT), 0)
    for ci in range(cpt):
        lanes = slice(ci * LRU_CT, (ci + 1) * LRU_CT)
        x = x_ref[0, :, lanes]
        u = cb_ref[:, lanes] + cw_ref[2:3, lanes] * x
        for tap, off in ((0, -2), (1, -1), (3, 1)):
            xs = pltpu.roll(x, (-off) % t, 0)
            ok = (row + off >= 0) & (row + off < t)
            u = u + cw_ref[tap:tap + 1, lanes] * jnp.where(ok, xs, 0.0)
        gates = _dot(u.astype(BF16), w_ref[ci]) + gb_ref[ci]
        total = None
        for d in range(2):
            r = _sigmoid(gates[:, (2 * d) * LRU_CT:(2 * d + 1) * LRU_CT])
            ig = _sigmoid(gates[:, (2 * d + 1) * LRU_CT:(2 * d + 2) * LRU_CT])
            nl = -lam_ref[d:d + 1, lanes]
            softplus = jnp.maximum(nl, 0.0) + jnp.log1p(jnp.exp(-jnp.abs(nl)))
            log_a = -LRU_C * r * softplus
            a = jnp.exp(log_a)
            bx = jnp.sqrt(jnp.tanh(-log_a) * (a * a + 1.0)) * (ig * u)
            edge = t - 1 if d == 1 else 0
            bx = jnp.where(row == edge, bx + a * h0_ref[0, d:d + 1, lanes], bx)
            sa_ref, sb_ref, sc_ref = scratch[3 * (2 * ci + d):3 * (2 * ci + d) + 3]
            bx = _linear_scan(a, bx, row, t, d == 1, sa_ref, sb_ref, sc_ref)
            fin = t - 1 if d == 0 else 0
            st_ref[0, d:d + 1, lanes] = bx[fin:fin + 1, :]
            total = bx if total is None else total + bx
        g = g_ref[0, :, lanes]
        gelu = 0.5 * g * (1.0 + jnp.tanh(math.sqrt(2.0 / math.pi) * (g + 0.044715 * (g * g * g))))
        y_ref[0, :, lanes] = (total * gelu).astype(BF16)


def _lru_mixer(f, conv_w, conv_b, w_gates, b_gates, lam, h0, *, nb, t, cpt):
    f3 = f.reshape(nb, t, f.shape[1])
    nct = LRU_WIDTH // LRU_CT
    wide = cpt * LRU_CT
    y, st = pl.pallas_call(
        functools.partial(_lru_kernel, t=t, cpt=cpt),
        out_shape=[jax.ShapeDtypeStruct((nb, t, LRU_WIDTH), BF16),
                   jax.ShapeDtypeStruct((nb, 2, LRU_WIDTH), F32)],
        grid=(nb, nct // cpt),
        in_specs=[
            pl.BlockSpec((1, t, wide), lambda i, c: (i, 0, c)),
            pl.BlockSpec((1, t, wide), lambda i, c: (i, 0, nct // cpt + c)),
            pl.BlockSpec((CONV_W, wide), lambda i, c: (0, c)),
            pl.BlockSpec((1, wide), lambda i, c: (0, c)),
            pl.BlockSpec((cpt, LRU_CT, 4 * LRU_CT), lambda i, c: (c, 0, 0)),
            pl.BlockSpec((cpt, 1, 4 * LRU_CT), lambda i, c: (c, 0, 0)),
            pl.BlockSpec((2, wide), lambda i, c: (0, c)),
            pl.BlockSpec((1, 2, wide), lambda i, c: (i, 0, c)),
        ],
        out_specs=[pl.BlockSpec((1, t, wide), lambda i, c: (i, 0, c)),
                   pl.BlockSpec((1, 2, wide), lambda i, c: (i, 0, c))],
        scratch_shapes=[pltpu.VMEM((t, LRU_CT), F32)] * (6 * cpt),
        compiler_params=_params("arbitrary", "arbitrary"),
        name="lru_mixer",
    )(f3, f3, conv_w, conv_b.reshape(1, LRU_WIDTH), w_gates, b_gates, lam, h0)
    return y.reshape(nb * t, LRU_WIDTH), st


def _lru_gate_weights(wa, ba, wx, bx):
    nct = LRU_WIDTH // LRU_CT
    bpt = LRU_CT // LRU_BLOCK_W
    eye = jnp.eye(bpt, dtype=F32)

    def tile_w(w):
        w4 = w.reshape(nct, bpt, LRU_BLOCK_W, LRU_BLOCK_W)
        return jnp.einsum("cnij,nm->cnimj", w4, eye).reshape(nct, LRU_CT, LRU_CT)

    w = jnp.concatenate([tile_w(wa[0]), tile_w(wx[0]), tile_w(wa[1]), tile_w(wx[1])], axis=-1)
    b = jnp.concatenate([v.reshape(nct, 1, LRU_CT) for v in (ba[0], bx[0], ba[1], bx[1])], axis=-1)
    return w.astype(BF16), b


def _outproj_kernel(*refs, n_in):
    ins = refs[:n_in]
    w_ref, x_ref, g_ref, o_ref = refs[n_in:]
    acc = None
    off = 0
    for m_ref in ins:
        width = m_ref.shape[1]
        part = _dot(m_ref[...], w_ref[0, off:off + width, :])
        acc = part if acc is None else acc + part
        off += width
    o_ref[...] = x_ref[...] + g_ref[0] * acc


def _out_projection(mixes, w_out_bf, layer, x, mods_l, mod_row, *, tm):
    n_tok = x.shape[0]
    return pl.pallas_call(
        functools.partial(_outproj_kernel, n_in=len(mixes)),
        out_shape=jax.ShapeDtypeStruct((n_tok, D_MODEL), F32),
        grid=(n_tok // tm,),
        in_specs=[pl.BlockSpec((tm, m.shape[1]), lambda i: (i, 0)) for m in mixes] + [
            pl.BlockSpec((1, D_MODEL, D_MODEL), lambda i: (layer, 0, 0)),
            pl.BlockSpec((tm, D_MODEL), lambda i: (i, 0)),
            pl.BlockSpec((1, 1, D_MODEL), lambda i: (mod_row(i) * MOD_CHUNKS + 2, 0, 0)),
        ],
        out_specs=pl.BlockSpec((tm, D_MODEL), lambda i: (i, 0)),
        compiler_params=_params("arbitrary"),
        name="out_projection",
    )(*mixes, w_out_bf, x, mods_l)


def _ffn_kernel(x_ref, sh_ref, sc_ref, g_ref, nw_ref, wg_ref, wu_ref, wd_ref, o_ref, h_scr, *, tm):
    k = pl.program_id(1)

    def step(first):
        wg = wg_ref[0].astype(BF16)
        wu = wu_ref[0].astype(BF16)
        wd = wd_ref[0].astype(BF16)
        chunk = ROW_CHUNK if first else FFN_ROW_CHUNK

        def norm_chunk(c):
            _modnorm_static(x_ref, nw_ref, sh_ref, sc_ref, h_scr, c * chunk, chunk)

        if first:
            norm_chunk(0)
        for c in range(tm // chunk):
            if first and c + 1 < tm // chunk:
                norm_chunk(c + 1)
            rows = slice(c * chunk, (c + 1) * chunk)
            h = h_scr[rows, :]
            g = _dot(h, wg)
            u = _dot(h, wu)
            act = (g * _sigmoid(g)) * u
            down = _dot(act.astype(BF16), wd)
            if first:
                o_ref[rows, :] = down
            else:
                o_ref[rows, :] += down

    @pl.when(k == 0)
    def _():
        step(True)

    @pl.when(k > 0)
    def _():
        step(False)

    @pl.when(k == pl.num_programs(1) - 1)
    def _():
        def body(c, carry):
            r = pl.multiple_of(c * 128, 128)
            o_ref[pl.ds(r, 128), :] = x_ref[pl.ds(r, 128), :] + g_ref[0] * o_ref[pl.ds(r, 128), :]
            return carry
        lax.fori_loop(0, tm // 128, body, 0)


def _ffn(x, mods_l, mod_row, norm_w, wg, wu, wd, layer, *, tm, th):
    n_tok = x.shape[0]
    return pl.pallas_call(
        functools.partial(_ffn_kernel, tm=tm),
        out_shape=jax.ShapeDtypeStruct((n_tok, D_MODEL), F32),
        grid=(n_tok // tm, FFN_HIDDEN // th),
        in_specs=[
            pl.BlockSpec((tm, D_MODEL), lambda i, k: (i, 0), pipeline_mode=pl.Buffered(1)),
            pl.BlockSpec((1, 1, D_MODEL), lambda i, k: (mod_row(i) * MOD_CHUNKS + 3, 0, 0)),
            pl.BlockSpec((1, 1, D_MODEL), lambda i, k: (mod_row(i) * MOD_CHUNKS + 4, 0, 0)),
            pl.BlockSpec((1, 1, D_MODEL), lambda i, k: (mod_row(i) * MOD_CHUNKS + 5, 0, 0)),
            pl.BlockSpec((1, D_MODEL), lambda i, k: (0, 0)),
            pl.BlockSpec((1, D_MODEL, th), lambda i, k: (layer, 0, k)),
            pl.BlockSpec((1, D_MODEL, th), lambda i, k: (layer, 0, k)),
            pl.BlockSpec((1, th, D_MODEL), lambda i, k: (layer, k, 0)),
        ],
        out_specs=pl.BlockSpec((tm, D_MODEL), lambda i, k: (i, 0)),
        scratch_shapes=[pltpu.VMEM((tm, D_MODEL), BF16)],
        compiler_params=_params("arbitrary", "arbitrary"),
        name="ffn",
    )(x, mods_l, mods_l, mods_l, norm_w.reshape(1, D_MODEL), wg, wu, wd)


def _rope_tables(t_lat):
    nf = HEAD_DIM // 4
    tok = jnp.arange(t_lat)
    pos = jnp.stack([tok // GRID_W, tok % GRID_W], axis=-1).astype(F32)
    inv = ROPE_BASE ** (-jnp.arange(nf, dtype=F32) / nf)
    ang = pos[:, :, None] * inv
    cos, sin = jnp.cos(ang), jnp.sin(ang)
    cos_h = jnp.concatenate([cos[:, 0], cos[:, 0], cos[:, 1], cos[:, 1]], axis=-1)
    sin_h = jnp.concatenate([-sin[:, 0], sin[:, 0], -sin[:, 1], sin[:, 1]], axis=-1)
    reps = PROJ_HALF // HEAD_DIM
    return jnp.tile(cos_h, (1, reps)), jnp.tile(sin_h, (1, reps))


def _tile_table():
    lo = [_FIRST_BLOCK[n] for n in TILE_NAMES]
    hi = [b + 1 for b in lo]
    cls = [_TILE_CLASS[n] for n in TILE_NAMES]
    return jnp.asarray(np.array(lo + hi + cls, np.int32))


def _tile_gains(qk_gain_l):
    def tiled(g, n):
        return jnp.tile(g, n // HEAD_DIM)
    ones = jnp.ones((PROJ_TILE,), F32)
    per_tile = {
        "lx": ones, "lg": ones, "nv": ones, "dv": ones,
        "nk": tiled(qk_gain_l[1, 1], PROJ_TILE), "dk": tiled(qk_gain_l[2, 1], PROJ_TILE),
        "skv": jnp.concatenate([tiled(qk_gain_l[0, 1], 128), jnp.ones((PROJ_TILE - 128,), F32)]),
        "sq": tiled(qk_gain_l[0, 0], PROJ_TILE) * QK_SCALE,
        "nq": tiled(qk_gain_l[1, 0], PROJ_TILE) * QK_SCALE,
        "dq": tiled(qk_gain_l[2, 0], PROJ_TILE) * QK_SCALE,
    }
    return jnp.stack([per_tile[n] for n in TILE_NAMES]).reshape(N_TILES, 1, PROJ_TILE)


def _lambda_init(layer):
    return 0.8 - 0.6 * math.exp(-0.3 * layer)


def kernel(x_prompt, x_sample, cache_swa_k, cache_swa_v, cache_na_k, cache_na_v, cache_diff_k, cache_diff_v, state_lru, c, c_ctx, norm_mix, norm_ffn, w_mod, b_mod, w_in, w_out, qk_gain, swa_sink, na_rpb, diff_lambda, diff_subln, conv_w, conv_b, lru_wa, lru_ba, lru_wx, lru_bx, lru_L, w_ffn_gate, w_ffn_up, w_ffn_down):
    bc, s_ctx, _ = x_prompt.shape
    bd, t_lat, _ = x_sample.shape
    n_ctx, n_lat = bc * s_ctx, bd * t_lat
    rows = t_lat // GRID_W
    p_ctx = cache_swa_k.shape[2]

    cvecs = jnp.concatenate([c_ctx[None, :], c, jnp.zeros((8 - 1 - bd, D_MODEL), F32)], axis=0)
    mods = _modulation(cvecs, w_mod, b_mod).reshape(DEPTH, 8 * MOD_CHUNKS, 1, D_MODEL)

    hsum = jnp.asarray(np.kron(np.eye(PROJ_HALF // HEAD_DIM, dtype=np.float32),
                               np.full((HEAD_DIM, HEAD_DIM), 1.0 / HEAD_DIM, np.float32)), BF16)
    rope_tabs = _rope_tables(t_lat)
    tab = _tile_table()

    tm = 1024
    tm_out = 512
    ctx_row = lambda tile: (lambda i: 0)
    lat_row = lambda tile: (lambda i: 1 + (i * tile) // t_lat)

    xc = x_prompt.reshape(n_ctx, D_MODEL)
    xs = x_sample.reshape(n_lat, D_MODEL)
    zero_state = jnp.zeros((bc, 2, LRU_WIDTH), F32)
    caches = [jnp.zeros((bc, DEPTH, s_ctx, _CACHE_SRC[name][2]), F32) for name in CACHE_NAMES]
    lru_states = []
    w_in_bf = w_in.astype(BF16)
    w_out_bf = w_out.astype(BF16)

    def cached(arr):
        return arr.astype(BF16).reshape(bd * DEPTH, p_ctx, -1)

    c_swa_k, c_swa_v = cached(cache_swa_k), cached(cache_swa_v)
    c_na_k, c_na_v = cached(cache_na_k), cached(cache_na_v)
    c_diff_k, c_diff_v = cached(cache_diff_k), cached(cache_diff_v)

    for l in range(DEPTH):
        lam_init = _lambda_init(l)
        mods_l = mods[l]
        gain = _tile_gains(qk_gain[l])
        w_gates, b_gates = _lru_gate_weights(lru_wa[l], lru_ba[l], lru_wx[l], lru_bx[l])
        ffn_w = (w_ffn_gate, w_ffn_up, w_ffn_down, l)

        p, f, caches = _in_projection(xc, mods_l, ctx_row(tm), norm_mix[l], w_in_bf, l, tab, gain, hsum,
                                      None, caches, tm=tm, seq=s_ctx)
        mix3 = _ctx_attention(p, swa_sink[l], diff_lambda[l], diff_subln[l], lam_init, nb=bc, t=s_ctx)
        od, st = _lru_mixer(f, conv_w[l], conv_b[l], w_gates, b_gates, lru_L[l], zero_state,
                            nb=bc, t=s_ctx, cpt=4)
        lru_states.append(st)
        xc = _out_projection([mix3, od], w_out_bf, l, xc, mods_l, ctx_row(tm_out), tm=tm_out)
        xc = _ffn(xc, mods_l, ctx_row(tm), norm_ffn[l], *ffn_w, tm=tm, th=512)

        p, f, _ = _in_projection(xs, mods_l, lat_row(tm), norm_mix[l], w_in_bf, l, tab, gain, hsum,
                                 rope_tabs, None, tm=tm, seq=t_lat)
        oa = _swa_latent(p, c_swa_k, c_swa_v, l, swa_sink[l], nb=bd, t=t_lat)
        bias = _na_bias(na_rpb[l], rows)
        ob = _na_latent(p, c_na_k, c_na_v, l, bias, nb=bd, t=t_lat)
        oc = _diff_latent(p, c_diff_k, c_diff_v, l, diff_lambda[l], diff_subln[l], lam_init, nb=bd, t=t_lat)
        od, _ = _lru_mixer(f, conv_w[l], conv_b[l], w_gates, b_gates, lru_L[l], state_lru[:, l],
                           nb=bd, t=t_lat, cpt=2)
        xs = _out_projection([oa, ob, oc, od], w_out_bf, l, xs, mods_l, lat_row(tm_out), tm=tm_out)
        xs = _ffn(xs, mods_l, lat_row(tm), norm_ffn[l], *ffn_w, tm=tm, th=512)

    swa_k, swa_v, na_k, na_v, diff_k, diff_v = caches
    return (xc.reshape(bc, s_ctx, D_MODEL), xs.reshape(bd, t_lat, D_MODEL),
            swa_k.reshape(bc, DEPTH, s_ctx, SWA_KV_HEADS, HEAD_DIM),
            swa_v.reshape(bc, DEPTH, s_ctx, SWA_KV_HEADS, HEAD_DIM),
            na_k.reshape(bc, DEPTH, s_ctx, NA_HEADS, HEAD_DIM),
            na_v.reshape(bc, DEPTH, s_ctx, NA_HEADS, HEAD_DIM),
            diff_k.reshape(bc, DEPTH, s_ctx, DIFF_HEADS, 2, HEAD_DIM),
            diff_v.reshape(bc, DEPTH, s_ctx, DIFF_HEADS, DIFF_VDIM),
            jnp.stack(lru_states, axis=1))
```

```python
import functools
import math

import jax
import jax.numpy as jnp
import numpy as np
from jax import lax
from jax.experimental import pallas as pl
from jax.experimental.pallas import tpu as pltpu

F32 = jnp.float32
BF16 = jnp.bfloat16

D_MODEL = 2048
DEPTH = 2
GRID_W = 64
HEAD_DIM = 64
GROUP_W = 512
SWA_HEADS = 8
SWA_KV_HEADS = 2
SWA_GROUP = 4
SWA_WINDOW = 128
NA_HEADS = 8
NA_KH = 8
NA_KW = 16
DIFF_HEADS = 4
DIFF_VDIM = 128
LRU_WIDTH = 512
LRU_BLOCKS = 8
LRU_BLOCK_W = 64
LRU_C = 8.0
CONV_W = 4
FFN_HIDDEN = 5632
ROPE_BASE = 10000.0
NORM_EPS = 1e-6
NEG_INF = -1e30
MOD_CHUNKS = 6
PROJ_W = 4864
LOG2E = math.log2(math.e)
QK_SCALE = HEAD_DIM ** -0.5 * LOG2E

V7X_VMEM_LIMIT = 60 * 1024 * 1024

PROJ_HALF = 256
PROJ_TILE = 2 * PROJ_HALF
TILE_NAMES = ("lx", "lg", "nk", "dk", "nv", "dv", "skv", "sq", "nq", "dq")
N_TILES = len(TILE_NAMES)
_FIRST_BLOCK = {"sq": 0, "skv": 2, "nq": 3, "nk": 5, "nv": 7, "dq": 9, "dk": 11, "dv": 13, "lx": 15, "lg": 17}
CLS_PLAIN, CLS_NORM, CLS_ROPE, CLS_MIXED = 0, 1, 2, 3
_TILE_CLASS = {"lx": CLS_PLAIN, "lg": CLS_PLAIN, "nv": CLS_PLAIN, "dv": CLS_PLAIN,
               "nk": CLS_NORM, "nq": CLS_NORM, "dk": CLS_ROPE, "sq": CLS_ROPE, "dq": CLS_ROPE,
               "skv": CLS_MIXED}
P_FIRST = 2
P_TILES = N_TILES - P_FIRST
P_COL = {name: TILE_NAMES.index(name) - P_FIRST for name in TILE_NAMES[P_FIRST:]}
F_TILES = 2

NA_QT = 128
NA_WIN_ROWS = 10
NA_WIN = NA_WIN_ROWS * GRID_W
LRU_CT = 128
ROW_CHUNK = 256
FFN_ROW_CHUNK = 512


def _sigmoid(x):
    return 1.0 / (1.0 + jnp.exp(-x))


def _dot(a, b):
    return jnp.dot(a, b, preferred_element_type=F32)


def _dot_nt(a, b):
    return lax.dot_general(a, b, (((1,), (1,)), ((), ())), preferred_element_type=F32)


def _params(*sem):
    return pltpu.CompilerParams(dimension_semantics=sem, vmem_limit_bytes=V7X_VMEM_LIMIT)


def _mod_kernel(c_ref, w_ref, b_ref, o_ref):
    cv = c_ref[...]
    s = cv * _sigmoid(cv)
    o_ref[0] = _dot(s.astype(BF16), w_ref[0].astype(BF16)) + b_ref[0]


def _modulation(cvecs, w_mod, b_mod):
    tn = 1024
    n = MOD_CHUNKS * D_MODEL
    return pl.pallas_call(
        _mod_kernel,
        out_shape=jax.ShapeDtypeStruct((DEPTH, 8, n), F32),
        grid=(DEPTH, n // tn),
        in_specs=[
            pl.BlockSpec((8, D_MODEL), lambda l, j: (0, 0)),
            pl.BlockSpec((1, D_MODEL, tn), lambda l, j: (l, 0, j)),
            pl.BlockSpec((1, 1, tn), lambda l, j: (l, 0, j)),
        ],
        out_specs=pl.BlockSpec((1, 8, tn), lambda l, j: (l, 0, j)),
        compiler_params=_params("arbitrary", "arbitrary"),
        name="modulation",
    )(cvecs, w_mod, b_mod.reshape(DEPTH, 1, n))


def _modnorm_static(x_ref, nw_ref, sh_ref, sc_ref, h_scr, r0, n_rows, chunk=128):
    for c in range(n_rows // chunk):
        rows = slice(r0 + c * chunk, r0 + (c + 1) * chunk)
        x = x_ref[rows, :]
        ms = jnp.mean(x * x, axis=-1, keepdims=True)
        y = x * lax.rsqrt(ms + NORM_EPS) * nw_ref[...]
        h_scr[rows, :] = (y * (1.0 + sc_ref[0]) + sh_ref[0]).astype(BF16)


CACHE_NAMES = ("swa_k", "swa_v", "na_k", "na_v", "diff_k", "diff_v")
_CACHE_SRC = {"swa_k": ("skv", 0, 128), "swa_v": ("skv", 128, 128), "na_k": ("nk", 0, 512),
              "na_v": ("nv", 0, 512), "diff_k": ("dk", 0, 512), "diff_v": ("dv", 0, 512)}


def _inproj_kernel(*refs, tm, rope, n_cache, seq):
    n_in = 10 if rope else 8
    tab_ref = refs[0]
    (x_ref, sh_ref, sc_ref, nw_ref, wlo_ref, whi_ref, gain_ref, hsum_ref) = refs[1:9]
    cos_ref, sin_ref = (refs[9], refs[10]) if rope else (None, None)
    outs = refs[1 + n_in + n_cache:]
    p_ref, f_ref = outs[0], outs[1]
    cache_refs = outs[2:2 + n_cache]
    h_scr = outs[2 + n_cache]
    j = pl.program_id(1)
    cls = tab_ref[2 * N_TILES + j]

    def run_tile(mode, halves, first=False):
        w = {half: (wlo_ref if half == 0 else whi_ref)[0] for half in halves}
        n_chunks = tm // ROW_CHUNK
        units = [(half, c) for half in halves for c in range(n_chunks)]

        def norm_chunk(c):
            _modnorm_static(x_ref, nw_ref, sh_ref, sc_ref, h_scr, c * ROW_CHUNK, ROW_CHUNK)

        if first:
            norm_chunk(0)

        def main(unit):
            half, c = unit
            if first and half == halves[0] and c + 1 < n_chunks:
                norm_chunk(c + 1)
            return _dot(h_scr[c * ROW_CHUNK:(c + 1) * ROW_CHUNK, :], w[half])

        def finish(unit, p):
            half, c = unit
            rows = slice(c * ROW_CHUNK, (c + 1) * ROW_CHUNK)
            cols = slice(half * PROJ_HALF, (half + 1) * PROJ_HALF)
            y = p
            if mode != CLS_PLAIN:
                ms = _dot((p * p).astype(BF16), hsum_ref[...])
                y = p * lax.rsqrt(ms + NORM_EPS) * gain_ref[0, :, cols]
                if rope and mode in (CLS_ROPE, CLS_MIXED):
                    lane = lax.broadcasted_iota(jnp.int32, y.shape, 1)
                    up = pltpu.roll(y, PROJ_HALF - 16, 1)
                    down = pltpu.roll(y, 16, 1)
                    partner = jnp.where((lane & 31) < 16, up, down)
                    y = y * cos_ref[rows, :] + partner * sin_ref[rows, :]
                if mode == CLS_MIXED:
                    lane = lax.broadcasted_iota(jnp.int32, y.shape, 1)
                    y = jnp.where(lane < 2 * HEAD_DIM, y, p)
            p_ref[rows, cols] = y.astype(BF16)
            f_ref[rows, cols] = y

        prev, p_prev = units[0], main(units[0])
        for unit in units[1:]:
            p_next = main(unit)
            finish(prev, p_prev)
            prev, p_prev = unit, p_next
        finish(prev, p_prev)

    assert _TILE_CLASS[TILE_NAMES[0]] == CLS_PLAIN

    @pl.when(j == 0)
    def _():
        run_tile(CLS_PLAIN, (0, 1), first=True)

    modes = (CLS_PLAIN, CLS_NORM, CLS_ROPE) if rope else (CLS_PLAIN, CLS_NORM)
    for mode in modes:
        cond = (cls == mode)
        if mode == CLS_PLAIN:
            cond = cond & (j > 0)
        if not rope and mode == CLS_NORM:
            cond = (cls == CLS_NORM) | (cls == CLS_ROPE)

        @pl.when(cond)
        def _(mode=mode):
            run_tile(mode, (0, 1))

    @pl.when(cls == CLS_MIXED)
    def _():
        run_tile(CLS_MIXED, (0,))
        hi = slice(PROJ_HALF, PROJ_TILE)
        p_ref[:, hi] = jnp.zeros((tm, PROJ_HALF), BF16)
        f_ref[:, hi] = jnp.zeros((tm, PROJ_HALF), F32)

    for name, c_ref in zip(CACHE_NAMES, cache_refs):
        tile, off, width = _CACHE_SRC[name]

        @pl.when(j == TILE_NAMES.index(tile))
        def _(c_ref=c_ref, off=off, width=width):
            for b in range(tm // seq):
                c_ref[b, 0] = f_ref[b * seq:(b + 1) * seq, off:off + width]


def _in_projection(x, mods_l, mod_row, norm_w, w_in_bf, layer, tab, gain, hsum, rope_tabs, caches, *, tm, seq):
    n_tok = x.shape[0]
    rope = rope_tabs is not None
    caches = [] if caches is None else list(caches)
    n_cache = len(caches)

    in_specs = [
        pl.BlockSpec((tm, D_MODEL), lambda i, j, t: (i, 0)),
        pl.BlockSpec((1, 1, D_MODEL), lambda i, j, t: (mod_row(i) * MOD_CHUNKS + 0, 0, 0)),
        pl.BlockSpec((1, 1, D_MODEL), lambda i, j, t: (mod_row(i) * MOD_CHUNKS + 1, 0, 0)),
        pl.BlockSpec((1, D_MODEL), lambda i, j, t: (0, 0)),
        pl.BlockSpec((1, D_MODEL, PROJ_HALF), lambda i, j, t: (layer, 0, t[j])),
        pl.BlockSpec((1, D_MODEL, PROJ_HALF), lambda i, j, t: (layer, 0, t[N_TILES + j])),
        pl.BlockSpec((1, 1, PROJ_TILE), lambda i, j, t: (j, 0, 0)),
        pl.BlockSpec((PROJ_HALF, PROJ_HALF), lambda i, j, t: (0, 0)),
    ]
    args = [x, mods_l, mods_l, norm_w.reshape(1, D_MODEL), w_in_bf, w_in_bf, gain, hsum]
    if rope:
        cos_t, sin_t = rope_tabs
        in_specs += [pl.BlockSpec((tm, PROJ_HALF), lambda i, j, t: (0, 0)),
                     pl.BlockSpec((tm, PROJ_HALF), lambda i, j, t: (0, 0))]
        args += [cos_t, sin_t]
    n_in = len(args)
    in_specs += [pl.BlockSpec(memory_space=pl.ANY)] * n_cache
    args += caches
    out_specs = [
        pl.BlockSpec((tm, PROJ_TILE), lambda i, j, t: (i, jnp.maximum(j - P_FIRST, 0))),
        pl.BlockSpec((tm, PROJ_TILE), lambda i, j, t: (i, jnp.minimum(j, F_TILES))),
    ]
    out_shape = [jax.ShapeDtypeStruct((n_tok, P_TILES * PROJ_TILE), BF16),
                 jax.ShapeDtypeStruct((n_tok, (F_TILES + 1) * PROJ_TILE), F32)]
    for arr in caches:
        width = arr.shape[-1]
        out_specs.append(pl.BlockSpec((tm // seq, 1, seq, width), lambda i, j, t: (i, layer, 0, 0)))
        out_shape.append(jax.ShapeDtypeStruct(arr.shape, arr.dtype))
    res = pl.pallas_call(
        functools.partial(_inproj_kernel, tm=tm, rope=rope, n_cache=n_cache, seq=seq),
        out_shape=out_shape,
        grid_spec=pltpu.PrefetchScalarGridSpec(
            num_scalar_prefetch=1,
            grid=(n_tok // tm, N_TILES),
            in_specs=in_specs,
            out_specs=out_specs,
            scratch_shapes=[pltpu.VMEM((tm, D_MODEL), BF16)],
        ),
        input_output_aliases={1 + n_in + k: 2 + k for k in range(n_cache)},
        compiler_params=_params("arbitrary", "arbitrary"),
        name="in_projection_rope" if rope else "in_projection",
    )(tab, *args)
    return res[0], res[1], list(res[2:])


def _joint(scores):
    return scores[0] if len(scores) == 1 else jnp.concatenate(scores, axis=1)


def _softmax_pv(segs, extra=None):
    s = _joint([sc for sc, _ in segs])
    m = jnp.max(s, axis=-1, keepdims=True)
    if extra is not None:
        m = jnp.maximum(m, extra)
    e = jnp.exp2(s - m)
    den = jnp.sum(e, axis=-1, keepdims=True)
    if extra is not None:
        den = den + jnp.exp2(extra - m)
    e = e.astype(BF16)
    o, off = None, 0
    for sc, v in segs:
        n = sc.shape[1]
        part = _dot(e[:, off:off + n], v)
        o = part if o is None else o + part
        off += n
    return o / den


def _probs(scores):
    s = _joint(scores)
    e = jnp.exp2(s - jnp.max(s, axis=-1, keepdims=True))
    return e * (1.0 / jnp.sum(e, axis=-1, keepdims=True))


def _half_vecs():
    lane = lax.broadcasted_iota(jnp.int32, (1, 2 * HEAD_DIM), 1)
    lo = (lane < HEAD_DIM).astype(BF16)
    return lo, (1.0 - lo).astype(BF16)


def _pair_block(ref, m, rows=None, b=0):
    if rows is None:
        return ref[b, :, m * 128:(m + 1) * 128]
    return ref[b, rows, m * 128:(m + 1) * 128]


def _both_halves(q, vecs):
    return jnp.concatenate([q * vecs[0], q * vecs[1]], axis=0)


def _merge_pair(o_even, o_odd):
    lane = lax.broadcasted_iota(jnp.int32, o_even.shape, 1)
    return jnp.where(lane < HEAD_DIM, o_even, o_odd)


def _diff_lambda_val(dl_ref, lam_init):
    lp = dl_ref[...]
    s1 = jnp.sum(lp[0:1] * lp[1:2], axis=-1, keepdims=True)
    s2 = jnp.sum(lp[2:3] * lp[3:4], axis=-1, keepdims=True)
    return jnp.exp(s1) - jnp.exp(s2) + lam_init


def _diff_finish(o, subln_ref, lam_init):
    ms = jnp.mean(o * o, axis=-1, keepdims=True)
    return o * lax.rsqrt(ms + NORM_EPS) * subln_ref[...] * (1.0 - lam_init)


def _swap_halves_bf16(x):
    return pltpu.roll(x.astype(F32), HEAD_DIM, 1).astype(BF16)


def _gqa_queries(q_ref, hk, vecs, b=0, rows=None):
    parts = []
    for g in range(SWA_GROUP):
        h = hk * SWA_GROUP + g
        q = _pair_block(q_ref, h // 2, rows=rows, b=b)
        if h % 2 != hk:
            q = _swap_halves_bf16(q)
        parts.append(q * vecs[hk])
    return jnp.concatenate(parts, axis=0)


def _gqa_outputs(o, hk, rows):
    blocks = []
    for j in range(SWA_GROUP // 2):
        halves = []
        for g in (2 * j, 2 * j + 1):
            og = o[g * rows:(g + 1) * rows]
            halves.append(og if g % 2 == hk else pltpu.roll(og, HEAD_DIM, 1))
        blocks.append(_merge_pair(halves[0], halves[1]))
    return blocks


def _pipelined(items, depth=2):
    pending, outs = [], []
    for score_fn, finish_fn in items:
        pending.append((finish_fn, score_fn()))
        if len(pending) > depth:
            fn, s = pending.pop(0)
            outs.append(fn(s))
    for fn, s in pending:
        outs.append(fn(s))
    return outs


def _ctx_attn_kernel(sink_ref, sq_ref, skv_ref, nq_ref, nk_ref, nv_ref, dq_ref, dk_ref, dv_ref,
                     dl_ref, subln_ref, o_ref, *, t, lam_init, bb):
    lam = _diff_lambda_val(dl_ref, lam_init)
    vecs = _half_vecs()

    def swa_item(b, hk):
        def scores():
            return _dot_nt(_gqa_queries(sq_ref, hk, vecs, b=b), _pair_block(skv_ref, 0, b=b))

        def finish(s):
            sink = jnp.concatenate(
                [jnp.full((t, 1), sink_ref[hk * SWA_GROUP + g] * LOG2E, F32) for g in range(SWA_GROUP)], axis=0)
            return _gqa_outputs(_softmax_pv([(s, _pair_block(skv_ref, 1, b=b))], extra=sink), hk, t)
        return scores, finish

    def na_item(b, m):
        def scores():
            return _dot_nt(_both_halves(_pair_block(nq_ref, m, b=b), vecs), _pair_block(nk_ref, m, b=b))

        def finish(s):
            o = _softmax_pv([(s, _pair_block(nv_ref, m, b=b))])
            return [_merge_pair(o[:t], o[t:])]
        return scores, finish

    def diff_item(b, h):
        def scores():
            return _dot_nt(_both_halves(_pair_block(dq_ref, h, b=b), vecs), _pair_block(dk_ref, h, b=b))

        def finish(s):
            p = _probs([s])
            pd = p[:t] - lam * p[t:]
            o = _dot(pd.astype(BF16), _pair_block(dv_ref, h, b=b))
            return [_diff_finish(o, subln_ref, lam_init)]
        return scores, finish

    items = []
    for make, n in ((swa_item, SWA_KV_HEADS), (na_item, NA_HEADS // 2), (diff_item, DIFF_HEADS)):
        items += [make(b, i) for i in range(n) for b in range(bb)]
    res = _pipelined(items, depth=2 * bb)
    for b in range(bb):
        pieces = [piece for r in res[b::bb] for piece in r]
        o_ref[b] = jnp.concatenate(pieces, axis=1).astype(BF16)


CTX_ROWS_PER_STEP = 4


def _pcol(name, t, bb=1):
    return pl.BlockSpec((bb, t, PROJ_TILE), lambda i, c=P_COL[name]: (i, 0, c))


def _ctx_attention(p, sink, dlam, subln, lam_init, *, nb, t):
    bb = CTX_ROWS_PER_STEP
    p3 = p.reshape(nb, t, P_TILES * PROJ_TILE)
    names = ("sq", "skv", "nq", "nk", "nv", "dq", "dk", "dv")
    out = pl.pallas_call(
        functools.partial(_ctx_attn_kernel, t=t, lam_init=lam_init, bb=bb),
        out_shape=jax.ShapeDtypeStruct((nb, t, 3 * GROUP_W), BF16),
        grid=(nb // bb,),
        in_specs=[pl.BlockSpec(memory_space=pltpu.SMEM)] + [_pcol(n, t, bb) for n in names] + [
            pl.BlockSpec((4, HEAD_DIM), lambda i: (0, 0)),
            pl.BlockSpec((1, DIFF_VDIM), lambda i: (0, 0)),
        ],
        out_specs=pl.BlockSpec((bb, t, 3 * GROUP_W), lambda i: (i, 0, 0)),
        compiler_params=_params("arbitrary"),
        name="ctx_attention",
    )(sink, *([p3] * len(names)), dlam, subln.reshape(1, DIFF_VDIM))
    return out.reshape(nb * t, 3 * GROUP_W)


Q_BLOCKS_PER_STEP = 2


def _swa_kernel(sink_ref, q_ref, kv_ref, ck_ref, cv_ref, o_ref, *, t, qb):
    span = 3 * qb
    vecs = _half_vecs()

    def item(j, hk):
        n = pl.program_id(1) * Q_BLOCKS_PER_STEP + j
        start = pl.multiple_of(jnp.clip((n - 1) * qb, 0, t - span), qb)
        win = pl.ds(start, span)
        qrows = slice(j * qb, (j + 1) * qb)

        def scores():
            qs = _gqa_queries(q_ref, hk, vecs, rows=qrows)
            return _dot_nt(qs, _pair_block(kv_ref, 0, rows=win)), _dot_nt(qs, ck_ref[0])

        def finish(s):
            row = (lax.broadcasted_iota(jnp.int32, (SWA_GROUP * qb, span), 0) & (qb - 1)) + n * qb
            col = lax.broadcasted_iota(jnp.int32, (SWA_GROUP * qb, span), 1) + start
            dist = row - col
            ok = (dist <= SWA_WINDOW) & (dist >= -SWA_WINDOW)
            sink = jnp.concatenate(
                [jnp.full((qb, 1), sink_ref[hk * SWA_GROUP + g] * LOG2E, F32) for g in range(SWA_GROUP)], axis=0)
            s_loc = jnp.where(ok, s[0], NEG_INF)
            o = _softmax_pv([(s_loc, _pair_block(kv_ref, 1, rows=win)), (s[1], cv_ref[0])], extra=sink)
            return _gqa_outputs(o, hk, qb)
        return scores, finish

    items = [item(j, hk) for hk in range(SWA_KV_HEADS) for j in range(Q_BLOCKS_PER_STEP)]
    res = _pipelined(items, depth=Q_BLOCKS_PER_STEP)
    for j in range(Q_BLOCKS_PER_STEP):
        pieces = [piece for r in res[j::Q_BLOCKS_PER_STEP] for piece in r]
        o_ref[0, j * qb:(j + 1) * qb, :] = jnp.concatenate(pieces, axis=1).astype(BF16)


def _swa_latent(p, ck, cv, layer, sink, *, nb, t):
    qb = 128
    step_rows = qb * Q_BLOCKS_PER_STEP
    p3 = p.reshape(nb, t, P_TILES * PROJ_TILE)
    out = pl.pallas_call(
        functools.partial(_swa_kernel, t=t, qb=qb),
        out_shape=jax.ShapeDtypeStruct((nb, t, GROUP_W), BF16),
        grid=(nb, t // step_rows),
        in_specs=[
            pl.BlockSpec(memory_space=pltpu.SMEM),
            pl.BlockSpec((1, step_rows, PROJ_TILE), lambda b, n: (b, n, P_COL["sq"])),
            pl.BlockSpec((1, t, PROJ_TILE), lambda b, n: (b, 0, P_COL["skv"])),
            pl.BlockSpec((1,) + ck.shape[1:], lambda b, n: (b * DEPTH + layer, 0, 0)),
            pl.BlockSpec((1,) + cv.shape[1:], lambda b, n: (b * DEPTH + layer, 0, 0)),
        ],
        out_specs=pl.BlockSpec((1, step_rows, GROUP_W), lambda b, n: (b, n, 0)),
        compiler_params=_params("arbitrary", "arbitrary"),
        name="swa_latent",
    )(sink, p3, p3, ck, cv)
    return out.reshape(nb * t, GROUP_W)


def _na_row_start(r, rows):
    kh = min(NA_KH, rows)
    return min(max(r - kh // 2, 0), rows - kh)


def _na_win_start(qt, rows):
    return min(max(_na_row_start(2 * qt, rows), 0), rows - NA_WIN_ROWS)


def _na_bias_kernel(rpb_ref, o_ref, *, rows):
    h = pl.program_id(0)
    n_dr, n_dc = 2 * NA_KH - 1, 2 * NA_KW - 1
    qi = lax.broadcasted_iota(jnp.int32, (GRID_W, GRID_W), 0)
    ki = lax.broadcasted_iota(jnp.int32, (GRID_W, GRID_W), 1)
    dc = jnp.clip(ki - qi + (NA_KW - 1), 0, n_dc - 1)
    cs = jnp.clip(qi - NA_KW // 2, 0, GRID_W - NA_KW)
    col_ok = (ki >= cs) & (ki < cs + NA_KW)
    neg = jnp.full((GRID_W, GRID_W), NEG_INF, F32)
    tabs = []
    for dr in range(n_dr):
        acc = jnp.zeros((GRID_W, GRID_W), F32)
        for c in range(n_dc):
            acc = jnp.where(dc == c, rpb_ref[(h * n_dr + dr) * n_dc + c], acc)
        tabs.append(jnp.where(col_ok, acc * LOG2E, neg))
    kh = min(NA_KH, rows)
    for qt in range(rows // 2):
        ws = _na_win_start(qt, rows)
        bands = []
        for qq in range(2):
            qr = 2 * qt + qq
            rs = _na_row_start(qr, rows)
            blks = []
            for kk in range(NA_WIN_ROWS):
                kr = ws + kk
                blks.append(tabs[kr - qr + NA_KH - 1] if rs <= kr < rs + kh else neg)
            bands.append(jnp.concatenate(blks, axis=1))
        o_ref[0, qt] = jnp.concatenate(bands, axis=0)


def _na_bias(rpb, rows):
    n_qt = rows // 2
    return pl.pallas_call(
        functools.partial(_na_bias_kernel, rows=rows),
        out_shape=jax.ShapeDtypeStruct((NA_HEADS, n_qt, NA_QT, NA_WIN), F32),
        grid=(NA_HEADS,),
        in_specs=[pl.BlockSpec(memory_space=pltpu.SMEM)],
        out_specs=pl.BlockSpec((1, n_qt, NA_QT, NA_WIN), lambda h: (h, 0, 0, 0)),
        compiler_params=_params("arbitrary"),
        name="na_bias",
    )(rpb.reshape(-1))


def _na_kernel(q_ref, k_ref, v_ref, ck_ref, cv_ref, bias_ref, o_ref, *, rows):
    vecs = _half_vecs()

    def item(j, m):
        qt = pl.program_id(1) * Q_BLOCKS_PER_STEP + j
        ws = jnp.clip(jnp.clip(2 * qt - NA_KH // 2, 0, rows - NA_KH), 0, rows - NA_WIN_ROWS)
        win = pl.ds(pl.multiple_of(ws * GRID_W, GRID_W), NA_WIN)
        qrows = slice(j * NA_QT, (j + 1) * NA_QT)

        def scores():
            q2 = _both_halves(_pair_block(q_ref, m, rows=qrows), vecs)
            return _dot_nt(q2, _pair_block(k_ref, m, rows=win)), _dot_nt(q2, _pair_block(ck_ref, m))

        def finish(s):
            v, cv = _pair_block(v_ref, m, rows=win), _pair_block(cv_ref, m)
            bias = jnp.concatenate([bias_ref[2 * m, j], bias_ref[2 * m + 1, j]], axis=0)
            o = _softmax_pv([(s[0] + bias, v), (s[1], cv)])
            return _merge_pair(o[:NA_QT], o[NA_QT:])
        return scores, finish

    items = [item(j, m) for m in range(NA_HEADS // 2) for j in range(Q_BLOCKS_PER_STEP)]
    res = _pipelined(items, depth=2 * Q_BLOCKS_PER_STEP)
    for j in range(Q_BLOCKS_PER_STEP):
        o_ref[0, j * NA_QT:(j + 1) * NA_QT, :] = jnp.concatenate(res[j::Q_BLOCKS_PER_STEP], axis=1).astype(BF16)


def _na_latent(p, ck, cv, layer, bias, *, nb, t):
    rows = t // GRID_W
    p3 = p.reshape(nb, t, P_TILES * PROJ_TILE)
    out = pl.pallas_call(
        functools.partial(_na_kernel, rows=rows),
        out_shape=jax.ShapeDtypeStruct((nb, t, GROUP_W), BF16),
        grid=(nb, t // (NA_QT * Q_BLOCKS_PER_STEP)),
        in_specs=[
            pl.BlockSpec((1, NA_QT * Q_BLOCKS_PER_STEP, PROJ_TILE), lambda i, n: (i, n, P_COL["nq"])),
            pl.BlockSpec((1, t, PROJ_TILE), lambda i, n: (i, 0, P_COL["nk"])),
            pl.BlockSpec((1, t, PROJ_TILE), lambda i, n: (i, 0, P_COL["nv"])),
            pl.BlockSpec((1,) + ck.shape[1:], lambda i, n: (i * DEPTH + layer, 0, 0)),
            pl.BlockSpec((1,) + cv.shape[1:], lambda i, n: (i * DEPTH + layer, 0, 0)),
            pl.BlockSpec((NA_HEADS, Q_BLOCKS_PER_STEP, NA_QT, NA_WIN), lambda i, n: (0, n, 0, 0)),
        ],
        out_specs=pl.BlockSpec((1, NA_QT * Q_BLOCKS_PER_STEP, GROUP_W), lambda i, n: (i, n, 0)),
        compiler_params=_params("arbitrary", "arbitrary"),
        name="na_latent",
    )(p3, p3, p3, ck, cv, bias)
    return out.reshape(nb * t, GROUP_W)


def _diff_kernel(q_ref, k_ref, v_ref, ck_ref, cv_ref, dl_ref, subln_ref, o_ref, *, lam_init, tq):
    lam = _diff_lambda_val(dl_ref, lam_init)

    vecs = _half_vecs()

    def item(j, h):
        qrows = slice(j * tq, (j + 1) * tq)

        def scores():
            q2 = _both_halves(_pair_block(q_ref, h, rows=qrows), vecs)
            return _dot_nt(q2, _pair_block(k_ref, h)), _dot_nt(q2, _pair_block(ck_ref, h))

        def finish(s):
            p = _probs(list(s))
            pd = (p[:tq] - lam * p[tq:]).astype(BF16)
            n_loc = s[0].shape[1]
            o = _dot(pd[:, :n_loc], _pair_block(v_ref, h)) + _dot(pd[:, n_loc:], _pair_block(cv_ref, h))
            return _diff_finish(o, subln_ref, lam_init)
        return scores, finish

    items = [item(j, h) for h in range(DIFF_HEADS) for j in range(Q_BLOCKS_PER_STEP)]
    res = _pipelined(items, depth=Q_BLOCKS_PER_STEP)
    for j in range(Q_BLOCKS_PER_STEP):
        o_ref[0, j * tq:(j + 1) * tq, :] = jnp.concatenate(res[j::Q_BLOCKS_PER_STEP], axis=1).astype(BF16)


def _diff_latent(p, ck, cv, layer, dlam, subln, lam_init, *, nb, t):
    tq_sub = 256
    tq = tq_sub * Q_BLOCKS_PER_STEP
    p3 = p.reshape(nb, t, P_TILES * PROJ_TILE)
    out = pl.pallas_call(
        functools.partial(_diff_kernel, lam_init=lam_init, tq=tq_sub),
        out_shape=jax.ShapeDtypeStruct((nb, t, GROUP_W), BF16),
        grid=(nb, t // tq),
        in_specs=[
            pl.BlockSpec((1, tq, PROJ_TILE), lambda i, n: (i, n, P_COL["dq"])),
            pl.BlockSpec((1, t, PROJ_TILE), lambda i, n: (i, 0, P_COL["dk"])),
            pl.BlockSpec((1, t, PROJ_TILE), lambda i, n: (i, 0, P_COL["dv"])),
            pl.BlockSpec((1,) + ck.shape[1:], lambda i, n: (i * DEPTH + layer, 0, 0)),
            pl.BlockSpec((1,) + cv.shape[1:], lambda i, n: (i * DEPTH + layer, 0, 0)),
            pl.BlockSpec((4, HEAD_DIM), lambda i, n: (0, 0)),
            pl.BlockSpec((1, DIFF_VDIM), lambda i, n: (0, 0)),
        ],
        out_specs=pl.BlockSpec((1, tq, GROUP_W), lambda i, n: (i, n, 0)),
        compiler_params=_params("arbitrary", "arbitrary"),
        name="diff_latent",
    )(p3, p3, p3, ck, cv, dlam, subln.reshape(1, DIFF_VDIM))
    return out.reshape(nb * t, GROUP_W)


SUBLANES = 8


def _linear_scan(a, b, row, t, reverse, sa_ref, sb_ref, sc_ref):
    nt = t // SUBLANES
    sub = row & (SUBLANES - 1)

    def doubling(a, b, idx, n, steps):
        for s in steps:
            if reverse:
                ok = idx < n - s
                a_sh = jnp.where(ok, pltpu.roll(a, a.shape[0] - s, 0), 1.0)
                b_sh = jnp.where(ok, pltpu.roll(b, b.shape[0] - s, 0), 0.0)
            else:
                ok = idx >= s
                a_sh = jnp.where(ok, pltpu.roll(a, s, 0), 1.0)
                b_sh = jnp.where(ok, pltpu.roll(b, s, 0), 0.0)
            b = a * b_sh + b
            a = a * a_sh
        return a, b

    tiles = (nt, SUBLANES, LRU_CT)
    sub3 = lax.broadcasted_iota(jnp.int32, tiles, 1)

    def tile_doubling(a, b):
        for s in (1, 2, 4):
            ok = (sub3 < SUBLANES - s) if reverse else (sub3 >= s)
            sh = SUBLANES - s if reverse else s
            a_sh = jnp.where(ok, pltpu.roll(a, sh, 1), 1.0)
            b_sh = jnp.where(ok, pltpu.roll(b, sh, 1), 0.0)
            b = a * b_sh + b
            a = a * a_sh
        return a, b

    a, b = tile_doubling(a.reshape(tiles), b.reshape(tiles))
    a, b = a.reshape(t, LRU_CT), b.reshape(t, LRU_CT)
    sa_ref[...] = a
    sb_ref[...] = b
    last = 0 if reverse else SUBLANES - 1
    at = sa_ref[pl.ds(last, nt, stride=SUBLANES), :]
    bt = sb_ref[pl.ds(last, nt, stride=SUBLANES), :]
    rowt = lax.broadcasted_iota(jnp.int32, (nt, LRU_CT), 0)
    steps, s = [], 1
    while s < nt:
        steps.append(s)
        s *= 2
    _, bt = doubling(at, bt, rowt, nt, steps)
    if reverse:
        carry = jnp.where(rowt < nt - 1, pltpu.roll(bt, nt - 1, 0), 0.0)
    else:
        carry = jnp.where(rowt >= 1, pltpu.roll(bt, 1, 0), 0.0)
    for r in range(SUBLANES):
        sc_ref[pl.ds(r, nt, stride=SUBLANES), :] = carry
    return b + a * sc_ref[...]


def _lru_kernel(x_ref, g_ref, cw_ref, cb_ref, w_ref, gb_ref, lam_ref, h0_ref, y_ref, st_ref, *scratch, t, cpt):
    row = lax.broadcasted_iota(jnp.int32, (t, LRU_CT), 0)
    for ci in range(cpt):
        lanes = slice(ci * LRU_CT, (ci + 1) * LRU_CT)
        x = x_ref[0, :, lanes]
        u = cb_ref[:, lanes] + cw_ref[2:3, lanes] * x
        for tap, off in ((0, -2), (1, -1), (3, 1)):
            xs = pltpu.roll(x, (-off) % t, 0)
            ok = (row + off >= 0) & (row + off < t)
            u = u + cw_ref[tap:tap + 1, lanes] * jnp.where(ok, xs, 0.0)
        gates = _dot(u.astype(BF16), w_ref[ci]) + gb_ref[ci]
        total = None
        for d in range(2):
            r = _sigmoid(gates[:, (2 * d) * LRU_CT:(2 * d + 1) * LRU_CT])
            ig = _sigmoid(gates[:, (2 * d + 1) * LRU_CT:(2 * d + 2) * LRU_CT])
            nl = -lam_ref[d:d + 1, lanes]
            softplus = jnp.maximum(nl, 0.0) + jnp.log1p(jnp.exp(-jnp.abs(nl)))
            log_a = -LRU_C * r * softplus
            a = jnp.exp(log_a)
            bx = jnp.sqrt(jnp.tanh(-log_a) * (a * a + 1.0)) * (ig * u)
            edge = t - 1 if d == 1 else 0
            bx = jnp.where(row == edge, bx + a * h0_ref[0, d:d + 1, lanes], bx)
            sa_ref, sb_ref, sc_ref = scratch[3 * (2 * ci + d):3 * (2 * ci + d) + 3]
            bx = _linear_scan(a, bx, row, t, d == 1, sa_ref, sb_ref, sc_ref)
            fin = t - 1 if d == 0 else 0
            st_ref[0, d:d + 1, lanes] = bx[fin:fin + 1, :]
            total = bx if total is None else total + bx
        g = g_ref[0, :, lanes]
        gelu = 0.5 * g * (1.0 + jnp.tanh(math.sqrt(2.0 / math.pi) * (g + 0.044715 * (g * g * g))))
        y_ref[0, :, lanes] = (total * gelu).astype(BF16)


def _lru_mixer(f, conv_w, conv_b, w_gates, b_gates, lam, h0, *, nb, t, cpt):
    f3 = f.reshape(nb, t, f.shape[1])
    nct = LRU_WIDTH // LRU_CT
    wide = cpt * LRU_CT
    y, st = pl.pallas_call(
        functools.partial(_lru_kernel, t=t, cpt=cpt),
        out_shape=[jax.ShapeDtypeStruct((nb, t, LRU_WIDTH), BF16),
                   jax.ShapeDtypeStruct((nb, 2, LRU_WIDTH), F32)],
        grid=(nb, nct // cpt),
        in_specs=[
            pl.BlockSpec((1, t, wide), lambda i, c: (i, 0, c)),
            pl.BlockSpec((1, t, wide), lambda i, c: (i, 0, nct // cpt + c)),
            pl.BlockSpec((CONV_W, wide), lambda i, c: (0, c)),
            pl.BlockSpec((1, wide), lambda i, c: (0, c)),
            pl.BlockSpec((cpt, LRU_CT, 4 * LRU_CT), lambda i, c: (c, 0, 0)),
            pl.BlockSpec((cpt, 1, 4 * LRU_CT), lambda i, c: (c, 0, 0)),
            pl.BlockSpec((2, wide), lambda i, c: (0, c)),
            pl.BlockSpec((1, 2, wide), lambda i, c: (i, 0, c)),
        ],
        out_specs=[pl.BlockSpec((1, t, wide), lambda i, c: (i, 0, c)),
                   pl.BlockSpec((1, 2, wide), lambda i, c: (i, 0, c))],
        scratch_shapes=[pltpu.VMEM((t, LRU_CT), F32)] * (6 * cpt),
        compiler_params=_params("arbitrary", "arbitrary"),
        name="lru_mixer",
    )(f3, f3, conv_w, conv_b.reshape(1, LRU_WIDTH), w_gates, b_gates, lam, h0)
    return y.reshape(nb * t, LRU_WIDTH), st


def _lru_gate_weights(wa, ba, wx, bx):
    nct = LRU_WIDTH // LRU_CT
    bpt = LRU_CT // LRU_BLOCK_W
    eye = jnp.eye(bpt, dtype=F32)

    def tile_w(w):
        w4 = w.reshape(nct, bpt, LRU_BLOCK_W, LRU_BLOCK_W)
        return jnp.einsum("cnij,nm->cnimj", w4, eye).reshape(nct, LRU_CT, LRU_CT)

    w = jnp.concatenate([tile_w(wa[0]), tile_w(wx[0]), tile_w(wa[1]), tile_w(wx[1])], axis=-1)
    b = jnp.concatenate([v.reshape(nct, 1, LRU_CT) for v in (ba[0], bx[0], ba[1], bx[1])], axis=-1)
    return w.astype(BF16), b


def _outproj_kernel(*refs, n_in):
    ins = refs[:n_in]
    w_ref, x_ref, g_ref, o_ref = refs[n_in:]
    acc = None
    off = 0
    for m_ref in ins:
        width = m_ref.shape[1]
        part = _dot(m_ref[...], w_ref[0, off:off + width, :])
        acc = part if acc is None else acc + part
        off += width
    o_ref[...] = x_ref[...] + g_ref[0] * acc


def _out_projection(mixes, w_out_bf, layer, x, mods_l, mod_row, *, tm):
    n_tok = x.shape[0]
    return pl.pallas_call(
        functools.partial(_outproj_kernel, n_in=len(mixes)),
        out_shape=jax.ShapeDtypeStruct((n_tok, D_MODEL), F32),
        grid=(n_tok // tm,),
        in_specs=[pl.BlockSpec((tm, m.shape[1]), lambda i: (i, 0)) for m in mixes] + [
            pl.BlockSpec((1, D_MODEL, D_MODEL), lambda i: (layer, 0, 0)),
            pl.BlockSpec((tm, D_MODEL), lambda i: (i, 0)),
            pl.BlockSpec((1, 1, D_MODEL), lambda i: (mod_row(i) * MOD_CHUNKS + 2, 0, 0)),
        ],
        out_specs=pl.BlockSpec((tm, D_MODEL), lambda i: (i, 0)),
        compiler_params=_params("arbitrary"),
        name="out_projection",
    )(*mixes, w_out_bf, x, mods_l)


def _ffn_kernel(x_ref, sh_ref, sc_ref, g_ref, nw_ref, wg_ref, wu_ref, wd_ref, o_ref, h_scr, *, tm):
    k = pl.program_id(1)

    def step(first):
        wg = wg_ref[0].astype(BF16)
        wu = wu_ref[0].astype(BF16)
        wd = wd_ref[0].astype(BF16)
        chunk = ROW_CHUNK if first else FFN_ROW_CHUNK

        def norm_chunk(c):
            _modnorm_static(x_ref, nw_ref, sh_ref, sc_ref, h_scr, c * chunk, chunk)

        if first:
            norm_chunk(0)
        for c in range(tm // chunk):
            if first and c + 1 < tm // chunk:
                norm_chunk(c + 1)
            rows = slice(c * chunk, (c + 1) * chunk)
            h = h_scr[rows, :]
            g = _dot(h, wg)
            u = _dot(h, wu)
            act = (g * _sigmoid(g)) * u
            down = _dot(act.astype(BF16), wd)
            if first:
                o_ref[rows, :] = down
            else:
                o_ref[rows, :] += down

    @pl.when(k == 0)
    def _():
        step(True)

    @pl.when(k > 0)
    def _():
        step(False)

    @pl.when(k == pl.num_programs(1) - 1)
    def _():
        def body(c, carry):
            r = pl.multiple_of(c * 128, 128)
            o_ref[pl.ds(r, 128), :] = x_ref[pl.ds(r, 128), :] + g_ref[0] * o_ref[pl.ds(r, 128), :]
            return carry
        lax.fori_loop(0, tm // 128, body, 0)


def _ffn(x, mods_l, mod_row, norm_w, wg, wu, wd, layer, *, tm, th):
    n_tok = x.shape[0]
    return pl.pallas_call(
        functools.partial(_ffn_kernel, tm=tm),
        out_shape=jax.ShapeDtypeStruct((n_tok, D_MODEL), F32),
        grid=(n_tok // tm, FFN_HIDDEN // th),
        in_specs=[
            pl.BlockSpec((tm, D_MODEL), lambda i, k: (i, 0), pipeline_mode=pl.Buffered(1)),
            pl.BlockSpec((1, 1, D_MODEL), lambda i, k: (mod_row(i) * MOD_CHUNKS + 3, 0, 0)),
            pl.BlockSpec((1, 1, D_MODEL), lambda i, k: (mod_row(i) * MOD_CHUNKS + 4, 0, 0)),
            pl.BlockSpec((1, 1, D_MODEL), lambda i, k: (mod_row(i) * MOD_CHUNKS + 5, 0, 0)),
            pl.BlockSpec((1, D_MODEL), lambda i, k: (0, 0)),
            pl.BlockSpec((1, D_MODEL, th), lambda i, k: (layer, 0, k)),
            pl.BlockSpec((1, D_MODEL, th), lambda i, k: (layer, 0, k)),
            pl.BlockSpec((1, th, D_MODEL), lambda i, k: (layer, k, 0)),
        ],
        out_specs=pl.BlockSpec((tm, D_MODEL), lambda i, k: (i, 0)),
        scratch_shapes=[pltpu.VMEM((tm, D_MODEL), BF16)],
        compiler_params=_params("arbitrary", "arbitrary"),
        name="ffn",
    )(x, mods_l, mods_l, mods_l, norm_w.reshape(1, D_MODEL), wg, wu, wd)


def _rope_tables(t_lat):
    nf = HEAD_DIM // 4
    tok = jnp.arange(t_lat)
    pos = jnp.stack([tok // GRID_W, tok % GRID_W], axis=-1).astype(F32)
    inv = ROPE_BASE ** (-jnp.arange(nf, dtype=F32) / nf)
    ang = pos[:, :, None] * inv
    cos, sin = jnp.cos(ang), jnp.sin(ang)
    cos_h = jnp.concatenate([cos[:, 0], cos[:, 0], cos[:, 1], cos[:, 1]], axis=-1)
    sin_h = jnp.concatenate([-sin[:, 0], sin[:, 0], -sin[:, 1], sin[:, 1]], axis=-1)
    reps = PROJ_HALF // HEAD_DIM
    return jnp.tile(cos_h, (1, reps)), jnp.tile(sin_h, (1, reps))


def _tile_table():
    lo = [_FIRST_BLOCK[n] for n in TILE_NAMES]
    hi = [b + 1 for b in lo]
    cls = [_TILE_CLASS[n] for n in TILE_NAMES]
    return jnp.asarray(np.array(lo + hi + cls, np.int32))


def _tile_gains(qk_gain_l):
    def tiled(g, n):
        return jnp.tile(g, n // HEAD_DIM)
    ones = jnp.ones((PROJ_TILE,), F32)
    per_tile = {
        "lx": ones, "lg": ones, "nv": ones, "dv": ones,
        "nk": tiled(qk_gain_l[1, 1], PROJ_TILE), "dk": tiled(qk_gain_l[2, 1], PROJ_TILE),
        "skv": jnp.concatenate([tiled(qk_gain_l[0, 1], 128), jnp.ones((PROJ_TILE - 128,), F32)]),
        "sq": tiled(qk_gain_l[0, 0], PROJ_TILE) * QK_SCALE,
        "nq": tiled(qk_gain_l[1, 0], PROJ_TILE) * QK_SCALE,
        "dq": tiled(qk_gain_l[2, 0], PROJ_TILE) * QK_SCALE,
    }
    return jnp.stack([per_tile[n] for n in TILE_NAMES]).reshape(N_TILES, 1, PROJ_TILE)


def _lambda_init(layer):
    return 0.8 - 0.6 * math.exp(-0.3 * layer)


def kernel(x_prompt, x_sample, cache_swa_k, cache_swa_v, cache_na_k, cache_na_v, cache_diff_k, cache_diff_v, state_lru, c, c_ctx, norm_mix, norm_ffn, w_mod, b_mod, w_in, w_out, qk_gain, swa_sink, na_rpb, diff_lambda, diff_subln, conv_w, conv_b, lru_wa, lru_ba, lru_wx, lru_bx, lru_L, w_ffn_gate, w_ffn_up, w_ffn_down):
    bc, s_ctx, _ = x_prompt.shape
    bd, t_lat, _ = x_sample.shape
    n_ctx, n_lat = bc * s_ctx, bd * t_lat
    rows = t_lat // GRID_W
    p_ctx = cache_swa_k.shape[2]

    cvecs = jnp.concatenate([c_ctx[None, :], c, jnp.zeros((8 - 1 - bd, D_MODEL), F32)], axis=0)
    mods = _modulation(cvecs, w_mod, b_mod).reshape(DEPTH, 8 * MOD_CHUNKS, 1, D_MODEL)

    hsum = jnp.asarray(np.kron(np.eye(PROJ_HALF // HEAD_DIM, dtype=np.float32),
                               np.full((HEAD_DIM, HEAD_DIM), 1.0 / HEAD_DIM, np.float32)), BF16)
    rope_tabs = _rope_tables(t_lat)
    tab = _tile_table()

    tm = 1024
    tm_out = 512
    ctx_row = lambda tile: (lambda i: 0)
    lat_row = lambda tile: (lambda i: 1 + (i * tile) // t_lat)

    xc = x_prompt.reshape(n_ctx, D_MODEL)
    xs = x_sample.reshape(n_lat, D_MODEL)
    zero_state = jnp.zeros((bc, 2, LRU_WIDTH), F32)
    caches = [jnp.zeros((bc, DEPTH, s_ctx, _CACHE_SRC[name][2]), F32) for name in CACHE_NAMES]
    lru_states = []
    w_in_bf = w_in.astype(BF16)
    w_out_bf = w_out.astype(BF16)

    def cached(arr):
        return arr.astype(BF16).reshape(bd * DEPTH, p_ctx, -1)

    c_swa_k, c_swa_v = cached(cache_swa_k), cached(cache_swa_v)
    c_na_k, c_na_v = cached(cache_na_k), cached(cache_na_v)
    c_diff_k, c_diff_v = cached(cache_diff_k), cached(cache_diff_v)

    for l in range(DEPTH):
        lam_init = _lambda_init(l)
        mods_l = mods[l]
        gain = _tile_gains(qk_gain[l])
        w_gates, b_gates = _lru_gate_weights(lru_wa[l], lru_ba[l], lru_wx[l], lru_bx[l])
        ffn_w = (w_ffn_gate, w_ffn_up, w_ffn_down, l)

        p, f, caches = _in_projection(xc, mods_l, ctx_row(tm), norm_mix[l], w_in_bf, l, tab, gain, hsum,
                                      None, caches, tm=tm, seq=s_ctx)
        mix3 = _ctx_attention(p, swa_sink[l], diff_lambda[l], diff_subln[l], lam_init, nb=bc, t=s_ctx)
        od, st = _lru_mixer(f, conv_w[l], conv_b[l], w_gates, b_gates, lru_L[l], zero_state,
                            nb=bc, t=s_ctx, cpt=4)
        lru_states.append(st)
        xc = _out_projection([mix3, od], w_out_bf, l, xc, mods_l, ctx_row(tm_out), tm=tm_out)
        xc = _ffn(xc, mods_l, ctx_row(tm), norm_ffn[l], *ffn_w, tm=tm, th=512)

        p, f, _ = _in_projection(xs, mods_l, lat_row(tm), norm_mix[l], w_in_bf, l, tab, gain, hsum,
                                 rope_tabs, None, tm=tm, seq=t_lat)
        oa = _swa_latent(p, c_swa_k, c_swa_v, l, swa_sink[l], nb=bd, t=t_lat)
        bias = _na_bias(na_rpb[l], rows)
        ob = _na_latent(p, c_na_k, c_na_v, l, bias, nb=bd, t=t_lat)
        oc = _diff_latent(p, c_diff_k, c_diff_v, l, diff_lambda[l], diff_subln[l], lam_init, nb=bd, t=t_lat)
        od, _ = _lru_mixer(f, conv_w[l], conv_b[l], w_gates, b_gates, lru_L[l], state_lru[:, l],
                           nb=bd, t=t_lat, cpt=2)
        xs = _out_projection([oa, ob, oc, od], w_out_bf, l, xs, mods_l, lat_row(tm_out), tm=tm_out)
        xs = _ffn(xs, mods_l, lat_row(tm), norm_ffn[l], *ffn_w, tm=tm, th=512)

    swa_k, swa_v, na_k, na_v, diff_k, diff_v = caches
    return (xc.reshape(bc, s_ctx, D_MODEL), xs.reshape(bd, t_lat, D_MODEL),
            swa_k.reshape(bc, DEPTH, s_ctx, SWA_KV_HEADS, HEAD_DIM),
            swa_v.reshape(bc, DEPTH, s_ctx, SWA_KV_HEADS, HEAD_DIM),
            na_k.reshape(bc, DEPTH, s_ctx, NA_HEADS, HEAD_DIM),
            na_v.reshape(bc, DEPTH, s_ctx, NA_HEADS, HEAD_DIM),
            diff_k.reshape(bc, DEPTH, s_ctx, DIFF_HEADS, 2, HEAD_DIM),
            diff_v.reshape(bc, DEPTH, s_ctx, DIFF_HEADS, DIFF_VDIM),
            jnp.stack(lru_states, axis=1))
```

```python
import functools
import math

import jax
import jax.numpy as jnp
import numpy as np
from jax import lax
from jax.experimental import pallas as pl
from jax.experimental.pallas import tpu as pltpu

F32 = jnp.float32
BF16 = jnp.bfloat16

D_MODEL = 2048
DEPTH = 2
GRID_W = 64
HEAD_DIM = 64
GROUP_W = 512
SWA_HEADS = 8
SWA_KV_HEADS = 2
SWA_GROUP = 4
SWA_WINDOW = 128
NA_HEADS = 8
NA_KH = 8
NA_KW = 16
DIFF_HEADS = 4
DIFF_VDIM = 128
LRU_WIDTH = 512
LRU_BLOCKS = 8
LRU_BLOCK_W = 64
LRU_C = 8.0
CONV_W = 4
FFN_HIDDEN = 5632
ROPE_BASE = 10000.0
NORM_EPS = 1e-6
NEG_INF = -1e30
MOD_CHUNKS = 6
PROJ_W = 4864
LOG2E = math.log2(math.e)
QK_SCALE = HEAD_DIM ** -0.5 * LOG2E

V7X_VMEM_LIMIT = 60 * 1024 * 1024

PROJ_HALF = 256
PROJ_TILE = 2 * PROJ_HALF
TILE_NAMES = ("lx", "lg", "nk", "dk", "nv", "dv", "skv", "sq", "nq", "dq")
N_TILES = len(TILE_NAMES)
_FIRST_BLOCK = {"sq": 0, "skv": 2, "nq": 3, "nk": 5, "nv": 7, "dq": 9, "dk": 11, "dv": 13, "lx": 15, "lg": 17}
CLS_PLAIN, CLS_NORM, CLS_ROPE, CLS_MIXED = 0, 1, 2, 3
_TILE_CLASS = {"lx": CLS_PLAIN, "lg": CLS_PLAIN, "nv": CLS_PLAIN, "dv": CLS_PLAIN,
               "nk": CLS_NORM, "nq": CLS_NORM, "dk": CLS_ROPE, "sq": CLS_ROPE, "dq": CLS_ROPE,
               "skv": CLS_MIXED}
P_FIRST = 2
P_TILES = N_TILES - P_FIRST
P_COL = {name: TILE_NAMES.index(name) - P_FIRST for name in TILE_NAMES[P_FIRST:]}
F_TILES = 2

NA_QT = 128
NA_WIN_ROWS = 10
NA_WIN = NA_WIN_ROWS * GRID_W
LRU_CT = 128
ROW_CHUNK = 256
FFN_ROW_CHUNK = 512


def _sigmoid(x):
    return 1.0 / (1.0 + jnp.exp(-x))


def _dot(a, b):
    return jnp.dot(a, b, preferred_element_type=F32)


def _dot_nt(a, b):
    return lax.dot_general(a, b, (((1,), (1,)), ((), ())), preferred_element_type=F32)


def _params(*sem):
    return pltpu.CompilerParams(dimension_semantics=sem, vmem_limit_bytes=V7X_VMEM_LIMIT)


def _mod_kernel(c_ref, w_ref, b_ref, o_ref):
    cv = c_ref[...]
    s = cv * _sigmoid(cv)
    o_ref[0] = _dot(s.astype(BF16), w_ref[0].astype(BF16)) + b_ref[0]


def _modulation(cvecs, w_mod, b_mod):
    tn = 1024
    n = MOD_CHUNKS * D_MODEL
    return pl.pallas_call(
        _mod_kernel,
        out_shape=jax.ShapeDtypeStruct((DEPTH, 8, n), F32),
        grid=(DEPTH, n // tn),
        in_specs=[
            pl.BlockSpec((8, D_MODEL), lambda l, j: (0, 0)),
            pl.BlockSpec((1, D_MODEL, tn), lambda l, j: (l, 0, j)),
            pl.BlockSpec((1, 1, tn), lambda l, j: (l, 0, j)),
        ],
        out_specs=pl.BlockSpec((1, 8, tn), lambda l, j: (l, 0, j)),
        compiler_params=_params("arbitrary", "arbitrary"),
        name="modulation",
    )(cvecs, w_mod, b_mod.reshape(DEPTH, 1, n))


def _modnorm_static(x_ref, nw_ref, sh_ref, sc_ref, h_scr, r0, n_rows, chunk=128):
    for c in range(n_rows // chunk):
        rows = slice(r0 + c * chunk, r0 + (c + 1) * chunk)
        x = x_ref[rows, :]
        ms = jnp.mean(x * x, axis=-1, keepdims=True)
        y = x * lax.rsqrt(ms + NORM_EPS) * nw_ref[...]
        h_scr[rows, :] = (y * (1.0 + sc_ref[0]) + sh_ref[0]).astype(BF16)


CACHE_NAMES = ("swa_k", "swa_v", "na_k", "na_v", "diff_k", "diff_v")
_CACHE_SRC = {"swa_k": ("skv", 0, 128), "swa_v": ("skv", 128, 128), "na_k": ("nk", 0, 512),
              "na_v": ("nv", 0, 512), "diff_k": ("dk", 0, 512), "diff_v": ("dv", 0, 512)}


def _inproj_kernel(*refs, tm, rope, n_cache, seq):
    n_in = 10 if rope else 8
    tab_ref = refs[0]
    (x_ref, sh_ref, sc_ref, nw_ref, wlo_ref, whi_ref, gain_ref, hsum_ref) = refs[1:9]
    cos_ref, sin_ref = (refs[9], refs[10]) if rope else (None, None)
    outs = refs[1 + n_in + n_cache:]
    p_ref, f_ref = outs[0], outs[1]
    cache_refs = outs[2:2 + n_cache]
    h_scr = outs[2 + n_cache]
    j = pl.program_id(1)
    cls = tab_ref[2 * N_TILES + j]

    def run_tile(mode, halves, first=False):
        w = {half: (wlo_ref if half == 0 else whi_ref)[0].astype(BF16) for half in halves}
        n_chunks = tm // ROW_CHUNK
        units = [(half, c) for half in halves for c in range(n_chunks)]

        def norm_chunk(c):
            _modnorm_static(x_ref, nw_ref, sh_ref, sc_ref, h_scr, c * ROW_CHUNK, ROW_CHUNK)

        if first:
            norm_chunk(0)

        def main(unit):
            half, c = unit
            if first and half == halves[0] and c + 1 < n_chunks:
                norm_chunk(c + 1)
            return _dot(h_scr[c * ROW_CHUNK:(c + 1) * ROW_CHUNK, :], w[half])

        def finish(unit, p):
            half, c = unit
            rows = slice(c * ROW_CHUNK, (c + 1) * ROW_CHUNK)
            cols = slice(half * PROJ_HALF, (half + 1) * PROJ_HALF)
            y = p
            if mode != CLS_PLAIN:
                ms = _dot((p * p).astype(BF16), hsum_ref[...])
                y = p * lax.rsqrt(ms + NORM_EPS) * gain_ref[0, :, cols]
                if rope and mode in (CLS_ROPE, CLS_MIXED):
                    lane = lax.broadcasted_iota(jnp.int32, y.shape, 1)
                    up = pltpu.roll(y, PROJ_HALF - 16, 1)
                    down = pltpu.roll(y, 16, 1)
                    partner = jnp.where((lane & 31) < 16, up, down)
                    y = y * cos_ref[rows, :] + partner * sin_ref[rows, :]
                if mode == CLS_MIXED:
                    lane = lax.broadcasted_iota(jnp.int32, y.shape, 1)
                    y = jnp.where(lane < 2 * HEAD_DIM, y, p)
            p_ref[rows, cols] = y.astype(BF16)
            f_ref[rows, cols] = y

        prev, p_prev = units[0], main(units[0])
        for unit in units[1:]:
            p_next = main(unit)
            finish(prev, p_prev)
            prev, p_prev = unit, p_next
        finish(prev, p_prev)

    assert _TILE_CLASS[TILE_NAMES[0]] == CLS_PLAIN

    @pl.when(j == 0)
    def _():
        run_tile(CLS_PLAIN, (0, 1), first=True)

    modes = (CLS_PLAIN, CLS_NORM, CLS_ROPE) if rope else (CLS_PLAIN, CLS_NORM)
    for mode in modes:
        cond = (cls == mode)
        if mode == CLS_PLAIN:
            cond = cond & (j > 0)
        if not rope and mode == CLS_NORM:
            cond = (cls == CLS_NORM) | (cls == CLS_ROPE)

        @pl.when(cond)
        def _(mode=mode):
            run_tile(mode, (0, 1))

    @pl.when(cls == CLS_MIXED)
    def _():
        run_tile(CLS_MIXED, (0,))
        hi = slice(PROJ_HALF, PROJ_TILE)
        p_ref[:, hi] = jnp.zeros((tm, PROJ_HALF), BF16)
        f_ref[:, hi] = jnp.zeros((tm, PROJ_HALF), F32)

    for name, c_ref in zip(CACHE_NAMES, cache_refs):
        tile, off, width = _CACHE_SRC[name]

        @pl.when(j == TILE_NAMES.index(tile))
        def _(c_ref=c_ref, off=off, width=width):
            for b in range(tm // seq):
                c_ref[b, 0] = f_ref[b * seq:(b + 1) * seq, off:off + width]


def _in_projection(x, mods_l, mod_row, norm_w, w_in_bf, layer, tab, gain, hsum, rope_tabs, caches, *, tm, seq):
    n_tok = x.shape[0]
    rope = rope_tabs is not None
    caches = [] if caches is None else list(caches)
    n_cache = len(caches)

    in_specs = [
        pl.BlockSpec((tm, D_MODEL), lambda i, j, t: (i, 0)),
        pl.BlockSpec((1, 1, D_MODEL), lambda i, j, t: (mod_row(i) * MOD_CHUNKS + 0, 0, 0)),
        pl.BlockSpec((1, 1, D_MODEL), lambda i, j, t: (mod_row(i) * MOD_CHUNKS + 1, 0, 0)),
        pl.BlockSpec((1, D_MODEL), lambda i, j, t: (0, 0)),
        pl.BlockSpec((1, D_MODEL, PROJ_HALF), lambda i, j, t: (layer, 0, t[j])),
        pl.BlockSpec((1, D_MODEL, PROJ_HALF), lambda i, j, t: (layer, 0, t[N_TILES + j])),
        pl.BlockSpec((1, 1, PROJ_TILE), lambda i, j, t: (j, 0, 0)),
        pl.BlockSpec((PROJ_HALF, PROJ_HALF), lambda i, j, t: (0, 0)),
    ]
    args = [x, mods_l, mods_l, norm_w.reshape(1, D_MODEL), w_in_bf, w_in_bf, gain, hsum]
    if rope:
        cos_t, sin_t = rope_tabs
        in_specs += [pl.BlockSpec((tm, PROJ_HALF), lambda i, j, t: (0, 0)),
                     pl.BlockSpec((tm, PROJ_HALF), lambda i, j, t: (0, 0))]
        args += [cos_t, sin_t]
    n_in = len(args)
    in_specs += [pl.BlockSpec(memory_space=pl.ANY)] * n_cache
    args += caches
    out_specs = [
        pl.BlockSpec((tm, PROJ_TILE), lambda i, j, t: (i, jnp.maximum(j - P_FIRST, 0))),
        pl.BlockSpec((tm, PROJ_TILE), lambda i, j, t: (i, jnp.minimum(j, F_TILES))),
    ]
    out_shape = [jax.ShapeDtypeStruct((n_tok, P_TILES * PROJ_TILE), BF16),
                 jax.ShapeDtypeStruct((n_tok, (F_TILES + 1) * PROJ_TILE), F32)]
    for arr in caches:
        width = arr.shape[-1]
        out_specs.append(pl.BlockSpec((tm // seq, 1, seq, width), lambda i, j, t: (i, layer, 0, 0)))
        out_shape.append(jax.ShapeDtypeStruct(arr.shape, arr.dtype))
    res = pl.pallas_call(
        functools.partial(_inproj_kernel, tm=tm, rope=rope, n_cache=n_cache, seq=seq),
        out_shape=out_shape,
        grid_spec=pltpu.PrefetchScalarGridSpec(
            num_scalar_prefetch=1,
            grid=(n_tok // tm, N_TILES),
            in_specs=in_specs,
            out_specs=out_specs,
            scratch_shapes=[pltpu.VMEM((tm, D_MODEL), BF16)],
        ),
        input_output_aliases={1 + n_in + k: 2 + k for k in range(n_cache)},
        compiler_params=_params("arbitrary", "arbitrary"),
        name="in_projection_rope" if rope else "in_projection",
    )(tab, *args)
    return res[0], res[1], list(res[2:])


def _joint(scores):
    return scores[0] if len(scores) == 1 else jnp.concatenate(scores, axis=1)


PV_W = 2 * HEAD_DIM


def _weights_pv(segs, extra=None):
    s = _joint([sc for sc, _ in segs])
    m = jnp.max(s, axis=-1, keepdims=True)
    if extra is not None:
        m = jnp.maximum(m, extra)
    e = jnp.exp2(s - m).astype(BF16)
    acc, off = None, 0
    for sc, v in segs:
        n = sc.shape[1]
        v1 = jnp.concatenate([v, jnp.ones((n, PV_W), BF16)], axis=1)
        part = _dot(e[:, off:off + n], v1)
        acc = part if acc is None else acc + part
        off += n
    num, den = acc[:, :PV_W], acc[:, PV_W:]
    if extra is not None:
        den = den + jnp.exp2(extra - m)
    return num, den


def _softmax_pv(segs, extra=None):
    num, den = _weights_pv(segs, extra)
    return num / den


def _half_vecs():
    lane = lax.broadcasted_iota(jnp.int32, (1, 2 * HEAD_DIM), 1)
    lo = (lane < HEAD_DIM).astype(BF16)
    return lo, (1.0 - lo).astype(BF16)


def _pair_block(ref, m, rows=None, b=0):
    if rows is None:
        return ref[b, :, m * 128:(m + 1) * 128]
    return ref[b, rows, m * 128:(m + 1) * 128]


def _both_halves(q, vecs):
    return jnp.concatenate([q * vecs[0], q * vecs[1]], axis=0)


def _merge_pair(o_even, o_odd):
    lane = lax.broadcasted_iota(jnp.int32, o_even.shape, 1)
    return jnp.where(lane < HEAD_DIM, o_even, o_odd)


def _diff_lambda_val(dl_ref, lam_init):
    lp = dl_ref[...]
    s1 = jnp.sum(lp[0:1] * lp[1:2], axis=-1, keepdims=True)
    s2 = jnp.sum(lp[2:3] * lp[3:4], axis=-1, keepdims=True)
    return jnp.exp(s1) - jnp.exp(s2) + lam_init


def _diff_finish(o, subln_ref, lam_init):
    ms = jnp.mean(o * o, axis=-1, keepdims=True)
    return o * lax.rsqrt(ms + NORM_EPS) * subln_ref[...] * (1.0 - lam_init)


def _swap_halves_bf16(x):
    return pltpu.roll(x.astype(F32), HEAD_DIM, 1).astype(BF16)


def _gqa_queries(q_ref, hk, vecs, b=0, rows=None):
    parts = []
    for g in range(SWA_GROUP):
        h = hk * SWA_GROUP + g
        q = _pair_block(q_ref, h // 2, rows=rows, b=b)
        if h % 2 != hk:
            q = _swap_halves_bf16(q)
        parts.append(q * vecs[hk])
    return jnp.concatenate(parts, axis=0)


def _gqa_outputs(o, hk, rows):
    blocks = []
    for j in range(SWA_GROUP // 2):
        halves = []
        for g in (2 * j, 2 * j + 1):
            og = o[g * rows:(g + 1) * rows]
            halves.append(og if g % 2 == hk else pltpu.roll(og, HEAD_DIM, 1))
        blocks.append(_merge_pair(halves[0], halves[1]))
    return blocks


def _pipelined(items, depth=2):
    pending, outs = [], []
    for score_fn, finish_fn in items:
        pending.append((finish_fn, score_fn()))
        if len(pending) > depth:
            fn, s = pending.pop(0)
            outs.append(fn(s))
    for fn, s in pending:
        outs.append(fn(s))
    return outs


def _ctx_attn_kernel(sink_ref, sq_ref, skv_ref, nq_ref, nk_ref, nv_ref, dq_ref, dk_ref, dv_ref,
                     dl_ref, subln_ref, o_ref, *, t, lam_init, bb):
    lam = _diff_lambda_val(dl_ref, lam_init)
    vecs = _half_vecs()

    def swa_item(b, hk):
        def scores():
            return _dot_nt(_gqa_queries(sq_ref, hk, vecs, b=b), _pair_block(skv_ref, 0, b=b))

        def finish(s):
            sink = jnp.concatenate(
                [jnp.full((t, 1), sink_ref[hk * SWA_GROUP + g] * LOG2E, F32) for g in range(SWA_GROUP)], axis=0)
            return _gqa_outputs(_softmax_pv([(s, _pair_block(skv_ref, 1, b=b))], extra=sink), hk, t)
        return scores, finish

    def na_item(b, m):
        def scores():
            return _dot_nt(_both_halves(_pair_block(nq_ref, m, b=b), vecs), _pair_block(nk_ref, m, b=b))

        def finish(s):
            o = _softmax_pv([(s, _pair_block(nv_ref, m, b=b))])
            return [_merge_pair(o[:t], o[t:])]
        return scores, finish

    def diff_item(b, h):
        def scores():
            return _dot_nt(_both_halves(_pair_block(dq_ref, h, b=b), vecs), _pair_block(dk_ref, h, b=b))

        def finish(s):
            o = _softmax_pv([(s, _pair_block(dv_ref, h, b=b))])
            return [_diff_finish(o[:t] - lam * o[t:], subln_ref, lam_init)]
        return scores, finish

    items = []
    for make, n in ((swa_item, SWA_KV_HEADS), (na_item, NA_HEADS // 2), (diff_item, DIFF_HEADS)):
        items += [make(b, i) for i in range(n) for b in range(bb)]
    res = _pipelined(items, depth=2 * bb)
    for b in range(bb):
        pieces = [piece for r in res[b::bb] for piece in r]
        o_ref[b] = jnp.concatenate(pieces, axis=1).astype(BF16)


CTX_ROWS_PER_STEP = 4


def _pcol(name, t, bb=1):
    return pl.BlockSpec((bb, t, PROJ_TILE), lambda i, c=P_COL[name]: (i, 0, c))


def _ctx_attention(p, sink, dlam, subln, lam_init, *, nb, t):
    bb = CTX_ROWS_PER_STEP
    p3 = p.reshape(nb, t, P_TILES * PROJ_TILE)
    names = ("sq", "skv", "nq", "nk", "nv", "dq", "dk", "dv")
    out = pl.pallas_call(
        functools.partial(_ctx_attn_kernel, t=t, lam_init=lam_init, bb=bb),
        out_shape=jax.ShapeDtypeStruct((nb, t, 3 * GROUP_W), BF16),
        grid=(nb // bb,),
        in_specs=[pl.BlockSpec(memory_space=pltpu.SMEM)] + [_pcol(n, t, bb) for n in names] + [
            pl.BlockSpec((4, HEAD_DIM), lambda i: (0, 0)),
            pl.BlockSpec((1, DIFF_VDIM), lambda i: (0, 0)),
        ],
        out_specs=pl.BlockSpec((bb, t, 3 * GROUP_W), lambda i: (i, 0, 0)),
        compiler_params=_params("arbitrary"),
        name="ctx_attention",
    )(sink, *([p3] * len(names)), dlam, subln.reshape(1, DIFF_VDIM))
    return out.reshape(nb * t, 3 * GROUP_W)


Q_BLOCKS_PER_STEP = 2


def _swa_kernel(sink_ref, q_ref, kv_ref, ck_ref, cv_ref, o_ref, *, t, qb):
    span = 3 * qb
    vecs = _half_vecs()

    def item(j, hk):
        n = pl.program_id(1) * Q_BLOCKS_PER_STEP + j
        start = pl.multiple_of(jnp.clip((n - 1) * qb, 0, t - span), qb)
        win = pl.ds(start, span)
        qrows = slice(j * qb, (j + 1) * qb)

        def scores():
            qs = _gqa_queries(q_ref, hk, vecs, rows=qrows)
            return _dot_nt(qs, _pair_block(kv_ref, 0, rows=win)), _dot_nt(qs, ck_ref[0])

        def finish(s):
            row = (lax.broadcasted_iota(jnp.int32, (SWA_GROUP * qb, span), 0) & (qb - 1)) + n * qb
            col = lax.broadcasted_iota(jnp.int32, (SWA_GROUP * qb, span), 1) + start
            dist = row - col
            ok = (dist <= SWA_WINDOW) & (dist >= -SWA_WINDOW)
            sink = jnp.concatenate(
                [jnp.full((qb, 1), sink_ref[hk * SWA_GROUP + g] * LOG2E, F32) for g in range(SWA_GROUP)], axis=0)
            s_loc = jnp.where(ok, s[0], NEG_INF)
            o = _softmax_pv([(s_loc, _pair_block(kv_ref, 1, rows=win)), (s[1], cv_ref[0])], extra=sink)
            return _gqa_outputs(o, hk, qb)
        return scores, finish

    items = [item(j, hk) for hk in range(SWA_KV_HEADS) for j in range(Q_BLOCKS_PER_STEP)]
    res = _pipelined(items, depth=Q_BLOCKS_PER_STEP)
    for j in range(Q_BLOCKS_PER_STEP):
        pieces = [piece for r in res[j::Q_BLOCKS_PER_STEP] for piece in r]
        o_ref[0, j * qb:(j + 1) * qb, :] = jnp.concatenate(pieces, axis=1).astype(BF16)


def _swa_latent(p, ck, cv, layer, sink, *, nb, t):
    qb = 128
    step_rows = qb * Q_BLOCKS_PER_STEP
    p3 = p.reshape(nb, t, P_TILES * PROJ_TILE)
    out = pl.pallas_call(
        functools.partial(_swa_kernel, t=t, qb=qb),
        out_shape=jax.ShapeDtypeStruct((nb, t, GROUP_W), BF16),
        grid=(nb, t // step_rows),
        in_specs=[
            pl.BlockSpec(memory_space=pltpu.SMEM),
            pl.BlockSpec((1, step_rows, PROJ_TILE), lambda b, n: (b, n, P_COL["sq"])),
            pl.BlockSpec((1, t, PROJ_TILE), lambda b, n: (b, 0, P_COL["skv"])),
            pl.BlockSpec((1,) + ck.shape[1:], lambda b, n: (b * DEPTH + layer, 0, 0)),
            pl.BlockSpec((1,) + cv.shape[1:], lambda b, n: (b * DEPTH + layer, 0, 0)),
        ],
        out_specs=pl.BlockSpec((1, step_rows, GROUP_W), lambda b, n: (b, n, 0)),
        compiler_params=_params("arbitrary", "arbitrary"),
        name="swa_latent",
    )(sink, p3, p3, ck, cv)
    return out.reshape(nb * t, GROUP_W)


def _na_row_start(r, rows):
    kh = min(NA_KH, rows)
    return min(max(r - kh // 2, 0), rows - kh)


def _na_win_start(qt, rows):
    return min(max(_na_row_start(2 * qt, rows), 0), rows - NA_WIN_ROWS)


def _na_bias_kernel(rpb_ref, o_ref, *, rows):
    h = pl.program_id(0)
    n_dr, n_dc = 2 * NA_KH - 1, 2 * NA_KW - 1
    qi = lax.broadcasted_iota(jnp.int32, (GRID_W, GRID_W), 0)
    ki = lax.broadcasted_iota(jnp.int32, (GRID_W, GRID_W), 1)
    dc = jnp.clip(ki - qi + (NA_KW - 1), 0, n_dc - 1)
    cs = jnp.clip(qi - NA_KW // 2, 0, GRID_W - NA_KW)
    col_ok = (ki >= cs) & (ki < cs + NA_KW)
    neg = jnp.full((GRID_W, GRID_W), NEG_INF, F32)
    tabs = []
    for dr in range(n_dr):
        acc = jnp.zeros((GRID_W, GRID_W), F32)
        for c in range(n_dc):
            acc = jnp.where(dc == c, rpb_ref[(h * n_dr + dr) * n_dc + c], acc)
        tabs.append(jnp.where(col_ok, acc * LOG2E, neg))
    kh = min(NA_KH, rows)
    for qt in range(rows // 2):
        ws = _na_win_start(qt, rows)
        bands = []
        for qq in range(2):
            qr = 2 * qt + qq
            rs = _na_row_start(qr, rows)
            blks = []
            for kk in range(NA_WIN_ROWS):
                kr = ws + kk
                blks.append(tabs[kr - qr + NA_KH - 1] if rs <= kr < rs + kh else neg)
            bands.append(jnp.concatenate(blks, axis=1))
        o_ref[0, qt] = jnp.concatenate(bands, axis=0)


def _na_bias(rpb, rows):
    n_qt = rows // 2
    return pl.pallas_call(
        functools.partial(_na_bias_kernel, rows=rows),
        out_shape=jax.ShapeDtypeStruct((NA_HEADS, n_qt, NA_QT, NA_WIN), F32),
        grid=(NA_HEADS,),
        in_specs=[pl.BlockSpec(memory_space=pltpu.SMEM)],
        out_specs=pl.BlockSpec((1, n_qt, NA_QT, NA_WIN), lambda h: (h, 0, 0, 0)),
        compiler_params=_params("arbitrary"),
        name="na_bias",
    )(rpb.reshape(-1))


def _na_kernel(q_ref, k_ref, v_ref, ck_ref, cv_ref, bias_ref, o_ref, *, rows):
    vecs = _half_vecs()

    def item(j, m):
        qt = pl.program_id(1) * Q_BLOCKS_PER_STEP + j
        ws = jnp.clip(jnp.clip(2 * qt - NA_KH // 2, 0, rows - NA_KH), 0, rows - NA_WIN_ROWS)
        win = pl.ds(pl.multiple_of(ws * GRID_W, GRID_W), NA_WIN)
        qrows = slice(j * NA_QT, (j + 1) * NA_QT)

        def scores():
            q2 = _both_halves(_pair_block(q_ref, m, rows=qrows), vecs)
            return _dot_nt(q2, _pair_block(k_ref, m, rows=win)), _dot_nt(q2, _pair_block(ck_ref, m))

        def finish(s):
            v, cv = _pair_block(v_ref, m, rows=win), _pair_block(cv_ref, m)
            bias = jnp.concatenate([bias_ref[2 * m, j], bias_ref[2 * m + 1, j]], axis=0)
            o = _softmax_pv([(s[0] + bias, v), (s[1], cv)])
            return _merge_pair(o[:NA_QT], o[NA_QT:])
        return scores, finish

    items = [item(j, m) for m in range(NA_HEADS // 2) for j in range(Q_BLOCKS_PER_STEP)]
    res = _pipelined(items, depth=2 * Q_BLOCKS_PER_STEP)
    for j in range(Q_BLOCKS_PER_STEP):
        o_ref[0, j * NA_QT:(j + 1) * NA_QT, :] = jnp.concatenate(res[j::Q_BLOCKS_PER_STEP], axis=1).astype(BF16)


def _na_latent(p, ck, cv, layer, bias, *, nb, t):
    rows = t // GRID_W
    p3 = p.reshape(nb, t, P_TILES * PROJ_TILE)
    out = pl.pallas_call(
        functools.partial(_na_kernel, rows=rows),
        out_shape=jax.ShapeDtypeStruct((nb, t, GROUP_W), BF16),
        grid=(nb, t // (NA_QT * Q_BLOCKS_PER_STEP)),
        in_specs=[
            pl.BlockSpec((1, NA_QT * Q_BLOCKS_PER_STEP, PROJ_TILE), lambda i, n: (i, n, P_COL["nq"])),
            pl.BlockSpec((1, t, PROJ_TILE), lambda i, n: (i, 0, P_COL["nk"])),
            pl.BlockSpec((1, t, PROJ_TILE), lambda i, n: (i, 0, P_COL["nv"])),
            pl.BlockSpec((1,) + ck.shape[1:], lambda i, n: (i * DEPTH + layer, 0, 0)),
            pl.BlockSpec((1,) + cv.shape[1:], lambda i, n: (i * DEPTH + layer, 0, 0)),
            pl.BlockSpec((NA_HEADS, Q_BLOCKS_PER_STEP, NA_QT, NA_WIN), lambda i, n: (0, n, 0, 0)),
        ],
        out_specs=pl.BlockSpec((1, NA_QT * Q_BLOCKS_PER_STEP, GROUP_W), lambda i, n: (i, n, 0)),
        compiler_params=_params("arbitrary", "arbitrary"),
        name="na_latent",
    )(p3, p3, p3, ck, cv, bias)
    return out.reshape(nb * t, GROUP_W)


def _diff_kernel(q_ref, k_ref, v_ref, ck_ref, cv_ref, dl_ref, subln_ref, o_ref, *, lam_init, tq):
    lam = _diff_lambda_val(dl_ref, lam_init)

    vecs = _half_vecs()

    def item(j, h):
        qrows = slice(j * tq, (j + 1) * tq)

        def scores():
            q2 = _both_halves(_pair_block(q_ref, h, rows=qrows), vecs)
            return _dot_nt(q2, _pair_block(k_ref, h)), _dot_nt(q2, _pair_block(ck_ref, h))

        def finish(s):
            o = _softmax_pv([(s[0], _pair_block(v_ref, h)), (s[1], _pair_block(cv_ref, h))])
            return _diff_finish(o[:tq] - lam * o[tq:], subln_ref, lam_init)
        return scores, finish

    items = [item(j, h) for h in range(DIFF_HEADS) for j in range(Q_BLOCKS_PER_STEP)]
    res = _pipelined(items, depth=Q_BLOCKS_PER_STEP)
    for j in range(Q_BLOCKS_PER_STEP):
        o_ref[0, j * tq:(j + 1) * tq, :] = jnp.concatenate(res[j::Q_BLOCKS_PER_STEP], axis=1).astype(BF16)


def _diff_latent(p, ck, cv, layer, dlam, subln, lam_init, *, nb, t):
    tq_sub = 256
    tq = tq_sub * Q_BLOCKS_PER_STEP
    p3 = p.reshape(nb, t, P_TILES * PROJ_TILE)
    out = pl.pallas_call(
        functools.partial(_diff_kernel, lam_init=lam_init, tq=tq_sub),
        out_shape=jax.ShapeDtypeStruct((nb, t, GROUP_W), BF16),
        grid=(nb, t // tq),
        in_specs=[
            pl.BlockSpec((1, tq, PROJ_TILE), lambda i, n: (i, n, P_COL["dq"])),
            pl.BlockSpec((1, t, PROJ_TILE), lambda i, n: (i, 0, P_COL["dk"])),
            pl.BlockSpec((1, t, PROJ_TILE), lambda i, n: (i, 0, P_COL["dv"])),
            pl.BlockSpec((1,) + ck.shape[1:], lambda i, n: (i * DEPTH + layer, 0, 0)),
            pl.BlockSpec((1,) + cv.shape[1:], lambda i, n: (i * DEPTH + layer, 0, 0)),
            pl.BlockSpec((4, HEAD_DIM), lambda i, n: (0, 0)),
            pl.BlockSpec((1, DIFF_VDIM), lambda i, n: (0, 0)),
        ],
        out_specs=pl.BlockSpec((1, tq, GROUP_W), lambda i, n: (i, n, 0)),
        compiler_params=_params("arbitrary", "arbitrary"),
        name="diff_latent",
    )(p3, p3, p3, ck, cv, dlam, subln.reshape(1, DIFF_VDIM))
    return out.reshape(nb * t, GROUP_W)


SUBLANES = 8


def _linear_scan(a, b, row, t, reverse, sa_ref, sb_ref, sc_ref):
    nt = t // SUBLANES
    sub = row & (SUBLANES - 1)

    def doubling(a, b, idx, n, steps):
        for s in steps:
            if reverse:
                ok = idx < n - s
                a_sh = jnp.where(ok, pltpu.roll(a, a.shape[0] - s, 0), 1.0)
                b_sh = jnp.where(ok, pltpu.roll(b, b.shape[0] - s, 0), 0.0)
            else:
                ok = idx >= s
                a_sh = jnp.where(ok, pltpu.roll(a, s, 0), 1.0)
                b_sh = jnp.where(ok, pltpu.roll(b, s, 0), 0.0)
            b = a * b_sh + b
            a = a * a_sh
        return a, b

    tiles = (nt, SUBLANES, LRU_CT)
    sub3 = lax.broadcasted_iota(jnp.int32, tiles, 1)

    def tile_doubling(a, b):
        for s in (1, 2, 4):
            ok = (sub3 < SUBLANES - s) if reverse else (sub3 >= s)
            sh = SUBLANES - s if reverse else s
            a_sh = jnp.where(ok, pltpu.roll(a, sh, 1), 1.0)
            b_sh = jnp.where(ok, pltpu.roll(b, sh, 1), 0.0)
            b = a * b_sh + b
            a = a * a_sh
        return a, b

    a, b = tile_doubling(a.reshape(tiles), b.reshape(tiles))
    a, b = a.reshape(t, LRU_CT), b.reshape(t, LRU_CT)
    sa_ref[...] = a
    sb_ref[...] = b
    last = 0 if reverse else SUBLANES - 1
    at = sa_ref[pl.ds(last, nt, stride=SUBLANES), :]
    bt = sb_ref[pl.ds(last, nt, stride=SUBLANES), :]
    rowt = lax.broadcasted_iota(jnp.int32, (nt, LRU_CT), 0)
    steps, s = [], 1
    while s < nt:
        steps.append(s)
        s *= 2
    _, bt = doubling(at, bt, rowt, nt, steps)
    if reverse:
        carry = jnp.where(rowt < nt - 1, pltpu.roll(bt, nt - 1, 0), 0.0)
    else:
        carry = jnp.where(rowt >= 1, pltpu.roll(bt, 1, 0), 0.0)
    for r in range(SUBLANES):
        sc_ref[pl.ds(r, nt, stride=SUBLANES), :] = carry
    return b + a * sc_ref[...]


def _lru_kernel(x_ref, g_ref, cw_ref, cb_ref, w_ref, gb_ref, lam_ref, h0_ref, y_ref, st_ref, *scratch, t, cpt):
    row = lax.broadcasted_iota(jnp.int32, (t, LRU_CT), 0)
    for ci in range(cpt):
        lanes = slice(ci * LRU_CT, (ci + 1) * LRU_CT)
        x = x_ref[0, :, lanes]
        u = cb_ref[:, lanes] + cw_ref[2:3, lanes] * x
        for tap, off in ((0, -2), (1, -1), (3, 1)):
            xs = pltpu.roll(x, (-off) % t, 0)
            ok = (row + off >= 0) & (row + off < t)
            u = u + cw_ref[tap:tap + 1, lanes] * jnp.where(ok, xs, 0.0)
        gates = _dot(u.astype(BF16), w_ref[ci]) + gb_ref[ci]
        total = None
        for d in range(2):
            r = _sigmoid(gates[:, (2 * d) * LRU_CT:(2 * d + 1) * LRU_CT])
            ig = _sigmoid(gates[:, (2 * d + 1) * LRU_CT:(2 * d + 2) * LRU_CT])
            nl = -lam_ref[d:d + 1, lanes]
            softplus = jnp.maximum(nl, 0.0) + jnp.log1p(jnp.exp(-jnp.abs(nl)))
            log_a = -LRU_C * r * softplus
            a = jnp.exp(log_a)
            bx = jnp.sqrt(jnp.tanh(-log_a) * (a * a + 1.0)) * (ig * u)
            edge = t - 1 if d == 1 else 0
            bx = jnp.where(row == edge, bx + a * h0_ref[0, d:d + 1, lanes], bx)
            sa_ref, sb_ref, sc_ref = scratch[3 * (2 * ci + d):3 * (2 * ci + d) + 3]
            bx = _linear_scan(a, bx, row, t, d == 1, sa_ref, sb_ref, sc_ref)
            fin = t - 1 if d == 0 else 0
            st_ref[0, d:d + 1, lanes] = bx[fin:fin + 1, :]
            total = bx if total is None else total + bx
        g = g_ref[0, :, lanes]
        gelu = 0.5 * g * (1.0 + jnp.tanh(math.sqrt(2.0 / math.pi) * (g + 0.044715 * (g * g * g))))
        y_ref[0, :, lanes] = (total * gelu).astype(BF16)


def _lru_mixer(f, conv_w, conv_b, w_gates, b_gates, lam, h0, *, nb, t, cpt):
    f3 = f.reshape(nb, t, f.shape[1])
    nct = LRU_WIDTH // LRU_CT
    wide = cpt * LRU_CT
    y, st = pl.pallas_call(
        functools.partial(_lru_kernel, t=t, cpt=cpt),
        out_shape=[jax.ShapeDtypeStruct((nb, t, LRU_WIDTH), BF16),
                   jax.ShapeDtypeStruct((nb, 2, LRU_WIDTH), F32)],
        grid=(nb, nct // cpt),
        in_specs=[
            pl.BlockSpec((1, t, wide), lambda i, c: (i, 0, c)),
            pl.BlockSpec((1, t, wide), lambda i, c: (i, 0, nct // cpt + c)),
            pl.BlockSpec((CONV_W, wide), lambda i, c: (0, c)),
            pl.BlockSpec((1, wide), lambda i, c: (0, c)),
            pl.BlockSpec((cpt, LRU_CT, 4 * LRU_CT), lambda i, c: (c, 0, 0)),
            pl.BlockSpec((cpt, 1, 4 * LRU_CT), lambda i, c: (c, 0, 0)),
            pl.BlockSpec((2, wide), lambda i, c: (0, c)),
            pl.BlockSpec((1, 2, wide), lambda i, c: (i, 0, c)),
        ],
        out_specs=[pl.BlockSpec((1, t, wide), lambda i, c: (i, 0, c)),
                   pl.BlockSpec((1, 2, wide), lambda i, c: (i, 0, c))],
        scratch_shapes=[pltpu.VMEM((t, LRU_CT), F32)] * (6 * cpt),
        compiler_params=_params("arbitrary", "arbitrary"),
        name="lru_mixer",
    )(f3, f3, conv_w, conv_b.reshape(1, LRU_WIDTH), w_gates, b_gates, lam, h0)
    return y.reshape(nb * t, LRU_WIDTH), st


def _lru_gate_weights(wa, ba, wx, bx):
    nct = LRU_WIDTH // LRU_CT
    bpt = LRU_CT // LRU_BLOCK_W
    eye = jnp.eye(bpt, dtype=F32)

    def tile_w(w):
        w4 = w.reshape(nct, bpt, LRU_BLOCK_W, LRU_BLOCK_W)
        return jnp.einsum("cnij,nm->cnimj", w4, eye).reshape(nct, LRU_CT, LRU_CT)

    w = jnp.concatenate([tile_w(wa[0]), tile_w(wx[0]), tile_w(wa[1]), tile_w(wx[1])], axis=-1)
    b = jnp.concatenate([v.reshape(nct, 1, LRU_CT) for v in (ba[0], bx[0], ba[1], bx[1])], axis=-1)
    return w.astype(BF16), b


def _outproj_kernel(*refs, n_in):
    ins = refs[:n_in]
    w_ref, x_ref, g_ref, o_ref = refs[n_in:]
    acc = None
    off = 0
    for m_ref in ins:
        width = m_ref.shape[1]
        part = _dot(m_ref[...], w_ref[0, off:off + width, :])
        acc = part if acc is None else acc + part
        off += width
    o_ref[...] = x_ref[...] + g_ref[0] * acc


def _out_projection(mixes, w_out_bf, layer, x, mods_l, mod_row, *, tm):
    n_tok = x.shape[0]
    return pl.pallas_call(
        functools.partial(_outproj_kernel, n_in=len(mixes)),
        out_shape=jax.ShapeDtypeStruct((n_tok, D_MODEL), F32),
        grid=(n_tok // tm,),
        in_specs=[pl.BlockSpec((tm, m.shape[1]), lambda i: (i, 0)) for m in mixes] + [
            pl.BlockSpec((1, D_MODEL, D_MODEL), lambda i: (layer, 0, 0)),
            pl.BlockSpec((tm, D_MODEL), lambda i: (i, 0)),
            pl.BlockSpec((1, 1, D_MODEL), lambda i: (mod_row(i) * MOD_CHUNKS + 2, 0, 0)),
        ],
        out_specs=pl.BlockSpec((tm, D_MODEL), lambda i: (i, 0)),
        compiler_params=_params("arbitrary"),
        name="out_projection",
    )(*mixes, w_out_bf, x, mods_l)


def _ffn_kernel(x_ref, sh_ref, sc_ref, g_ref, nw_ref, wg_ref, wu_ref, wd_ref, o_ref, h_scr, *, tm):
    k = pl.program_id(1)

    def step(first):
        wg = wg_ref[0].astype(BF16)
        wu = wu_ref[0].astype(BF16)
        wd = wd_ref[0].astype(BF16)
        chunk = ROW_CHUNK if first else FFN_ROW_CHUNK

        def norm_chunk(c):
            _modnorm_static(x_ref, nw_ref, sh_ref, sc_ref, h_scr, c * chunk, chunk)

        if first:
            norm_chunk(0)
        for c in range(tm // chunk):
            if first and c + 1 < tm // chunk:
                norm_chunk(c + 1)
            rows = slice(c * chunk, (c + 1) * chunk)
            h = h_scr[rows, :]
            g = _dot(h, wg)
            u = _dot(h, wu)
            act = (g * _sigmoid(g)) * u
            down = _dot(act.astype(BF16), wd)
            if first:
                o_ref[rows, :] = down
            else:
                o_ref[rows, :] += down

    @pl.when(k == 0)
    def _():
        step(True)

    @pl.when(k > 0)
    def _():
        step(False)

    @pl.when(k == pl.num_programs(1) - 1)
    def _():
        def body(c, carry):
            r = pl.multiple_of(c * 128, 128)
            o_ref[pl.ds(r, 128), :] = x_ref[pl.ds(r, 128), :] + g_ref[0] * o_ref[pl.ds(r, 128), :]
            return carry
        lax.fori_loop(0, tm // 128, body, 0)


def _ffn(x, mods_l, mod_row, norm_w, wg, wu, wd, layer, *, tm, th):
    n_tok = x.shape[0]
    return pl.pallas_call(
        functools.partial(_ffn_kernel, tm=tm),
        out_shape=jax.ShapeDtypeStruct((n_tok, D_MODEL), F32),
        grid=(n_tok // tm, FFN_HIDDEN // th),
        in_specs=[
            pl.BlockSpec((tm, D_MODEL), lambda i, k: (i, 0), pipeline_mode=pl.Buffered(1)),
            pl.BlockSpec((1, 1, D_MODEL), lambda i, k: (mod_row(i) * MOD_CHUNKS + 3, 0, 0)),
            pl.BlockSpec((1, 1, D_MODEL), lambda i, k: (mod_row(i) * MOD_CHUNKS + 4, 0, 0)),
            pl.BlockSpec((1, 1, D_MODEL), lambda i, k: (mod_row(i) * MOD_CHUNKS + 5, 0, 0)),
            pl.BlockSpec((1, D_MODEL), lambda i, k: (0, 0)),
            pl.BlockSpec((1, D_MODEL, th), lambda i, k: (layer, 0, k)),
            pl.BlockSpec((1, D_MODEL, th), lambda i, k: (layer, 0, k)),
            pl.BlockSpec((1, th, D_MODEL), lambda i, k: (layer, k, 0)),
        ],
        out_specs=pl.BlockSpec((tm, D_MODEL), lambda i, k: (i, 0)),
        scratch_shapes=[pltpu.VMEM((tm, D_MODEL), BF16)],
        compiler_params=_params("arbitrary", "arbitrary"),
        name="ffn",
    )(x, mods_l, mods_l, mods_l, norm_w.reshape(1, D_MODEL), wg, wu, wd)


def _rope_tables(t_lat):
    nf = HEAD_DIM // 4
    tok = jnp.arange(t_lat)
    pos = jnp.stack([tok // GRID_W, tok % GRID_W], axis=-1).astype(F32)
    inv = ROPE_BASE ** (-jnp.arange(nf, dtype=F32) / nf)
    ang = pos[:, :, None] * inv
    cos, sin = jnp.cos(ang), jnp.sin(ang)
    cos_h = jnp.concatenate([cos[:, 0], cos[:, 0], cos[:, 1], cos[:, 1]], axis=-1)
    sin_h = jnp.concatenate([-sin[:, 0], sin[:, 0], -sin[:, 1], sin[:, 1]], axis=-1)
    reps = PROJ_HALF // HEAD_DIM
    return jnp.tile(cos_h, (1, reps)), jnp.tile(sin_h, (1, reps))


def _tile_table():
    lo = [_FIRST_BLOCK[n] for n in TILE_NAMES]
    hi = [b + 1 for b in lo]
    cls = [_TILE_CLASS[n] for n in TILE_NAMES]
    return jnp.asarray(np.array(lo + hi + cls, np.int32))


def _tile_gains(qk_gain_l):
    def tiled(g, n):
        return jnp.tile(g, n // HEAD_DIM)
    ones = jnp.ones((PROJ_TILE,), F32)
    per_tile = {
        "lx": ones, "lg": ones, "nv": ones, "dv": ones,
        "nk": tiled(qk_gain_l[1, 1], PROJ_TILE), "dk": tiled(qk_gain_l[2, 1], PROJ_TILE),
        "skv": jnp.concatenate([tiled(qk_gain_l[0, 1], 128), jnp.ones((PROJ_TILE - 128,), F32)]),
        "sq": tiled(qk_gain_l[0, 0], PROJ_TILE) * QK_SCALE,
        "nq": tiled(qk_gain_l[1, 0], PROJ_TILE) * QK_SCALE,
        "dq": tiled(qk_gain_l[2, 0], PROJ_TILE) * QK_SCALE,
    }
    return jnp.stack([per_tile[n] for n in TILE_NAMES]).reshape(N_TILES, 1, PROJ_TILE)


def _lambda_init(layer):
    return 0.8 - 0.6 * math.exp(-0.3 * layer)


def kernel(x_prompt, x_sample, cache_swa_k, cache_swa_v, cache_na_k, cache_na_v, cache_diff_k, cache_diff_v, state_lru, c, c_ctx, norm_mix, norm_ffn, w_mod, b_mod, w_in, w_out, qk_gain, swa_sink, na_rpb, diff_lambda, diff_subln, conv_w, conv_b, lru_wa, lru_ba, lru_wx, lru_bx, lru_L, w_ffn_gate, w_ffn_up, w_ffn_down):
    bc, s_ctx, _ = x_prompt.shape
    bd, t_lat, _ = x_sample.shape
    n_ctx, n_lat = bc * s_ctx, bd * t_lat
    rows = t_lat // GRID_W
    p_ctx = cache_swa_k.shape[2]

    cvecs = jnp.concatenate([c_ctx[None, :], c, jnp.zeros((8 - 1 - bd, D_MODEL), F32)], axis=0)
    mods = _modulation(cvecs, w_mod, b_mod).reshape(DEPTH, 8 * MOD_CHUNKS, 1, D_MODEL)

    hsum = jnp.asarray(np.kron(np.eye(PROJ_HALF // HEAD_DIM, dtype=np.float32),
                               np.full((HEAD_DIM, HEAD_DIM), 1.0 / HEAD_DIM, np.float32)), BF16)
    rope_tabs = _rope_tables(t_lat)
    tab = _tile_table()

    tm = 1024
    tm_out = 512
    ctx_row = lambda tile: (lambda i: 0)
    lat_row = lambda tile: (lambda i: 1 + (i * tile) // t_lat)

    xc = x_prompt.reshape(n_ctx, D_MODEL)
    xs = x_sample.reshape(n_lat, D_MODEL)
    zero_state = jnp.zeros((bc, 2, LRU_WIDTH), F32)
    caches = [jnp.zeros((bc, DEPTH, s_ctx, _CACHE_SRC[name][2]), F32) for name in CACHE_NAMES]
    lru_states = []
    w_in_bf = w_in
    w_out_bf = w_out.astype(BF16)

    def cached(arr):
        return arr.astype(BF16).reshape(bd * DEPTH, p_ctx, -1)

    c_swa_k, c_swa_v = cached(cache_swa_k), cached(cache_swa_v)
    c_na_k, c_na_v = cached(cache_na_k), cached(cache_na_v)
    c_diff_k, c_diff_v = cached(cache_diff_k), cached(cache_diff_v)

    for l in range(DEPTH):
        lam_init = _lambda_init(l)
        mods_l = mods[l]
        gain = _tile_gains(qk_gain[l])
        w_gates, b_gates = _lru_gate_weights(lru_wa[l], lru_ba[l], lru_wx[l], lru_bx[l])
        ffn_w = (w_ffn_gate, w_ffn_up, w_ffn_down, l)

        p, f, caches = _in_projection(xc, mods_l, ctx_row(tm), norm_mix[l], w_in_bf, l, tab, gain, hsum,
                                      None, caches, tm=tm, seq=s_ctx)
        mix3 = _ctx_attention(p, swa_sink[l], diff_lambda[l], diff_subln[l], lam_init, nb=bc, t=s_ctx)
        od, st = _lru_mixer(f, conv_w[l], conv_b[l], w_gates, b_gates, lru_L[l], zero_state,
                            nb=bc, t=s_ctx, cpt=4)
        lru_states.append(st)
        xc = _out_projection([mix3, od], w_out_bf, l, xc, mods_l, ctx_row(tm_out), tm=tm_out)
        xc = _ffn(xc, mods_l, ctx_row(tm), norm_ffn[l], *ffn_w, tm=tm, th=512)

        p, f, _ = _in_projection(xs, mods_l, lat_row(tm), norm_mix[l], w_in_bf, l, tab, gain, hsum,
                                 rope_tabs, None, tm=tm, seq=t_lat)
        oa = _swa_latent(p, c_swa_k, c_swa_v, l, swa_sink[l], nb=bd, t=t_lat)
        bias = _na_bias(na_rpb[l], rows)
        ob = _na_latent(p, c_na_k, c_na_v, l, bias, nb=bd, t=t_lat)
        oc = _diff_latent(p, c_diff_k, c_diff_v, l, diff_lambda[l], diff_subln[l], lam_init, nb=bd, t=t_lat)
        od, _ = _lru_mixer(f, conv_w[l], conv_b[l], w_gates, b_gates, lru_L[l], state_lru[:, l],
                           nb=bd, t=t_lat, cpt=2)
        xs = _out_projection([oa, ob, oc, od], w_out_bf, l, xs, mods_l, lat_row(tm_out), tm=tm_out)
        xs = _ffn(xs, mods_l, lat_row(tm), norm_ffn[l], *ffn_w, tm=tm, th=512)

    swa_k, swa_v, na_k, na_v, diff_k, diff_v = caches
    return (xc.reshape(bc, s_ctx, D_MODEL), xs.reshape(bd, t_lat, D_MODEL),
            swa_k.reshape(bc, DEPTH, s_ctx, SWA_KV_HEADS, HEAD_DIM),
            swa_v.reshape(bc, DEPTH, s_ctx, SWA_KV_HEADS, HEAD_DIM),
            na_k.reshape(bc, DEPTH, s_ctx, NA_HEADS, HEAD_DIM),
            na_v.reshape(bc, DEPTH, s_ctx, NA_HEADS, HEAD_DIM),
            diff_k.reshape(bc, DEPTH, s_ctx, DIFF_HEADS, 2, HEAD_DIM),
            diff_v.reshape(bc, DEPTH, s_ctx, DIFF_HEADS, DIFF_VDIM),
            jnp.stack(lru_states, axis=1))
```

```python
import functools
import math

import jax
import jax.numpy as jnp
import numpy as np
from jax import lax
from jax.experimental import pallas as pl
from jax.experimental.pallas import tpu as pltpu

F32 = jnp.float32
BF16 = jnp.bfloat16

D_MODEL = 2048
DEPTH = 2
GRID_W = 64
HEAD_DIM = 64
GROUP_W = 512
SWA_HEADS = 8
SWA_KV_HEADS = 2
SWA_GROUP = 4
SWA_WINDOW = 128
NA_HEADS = 8
NA_KH = 8
NA_KW = 16
DIFF_HEADS = 4
DIFF_VDIM = 128
LRU_WIDTH = 512
LRU_BLOCKS = 8
LRU_BLOCK_W = 64
LRU_C = 8.0
CONV_W = 4
FFN_HIDDEN = 5632
ROPE_BASE = 10000.0
NORM_EPS = 1e-6
NEG_INF = -1e30
MOD_CHUNKS = 6
PROJ_W = 4864
LOG2E = math.log2(math.e)
QK_SCALE = HEAD_DIM ** -0.5 * LOG2E

V7X_VMEM_LIMIT = 60 * 1024 * 1024

PROJ_HALF = 256
PROJ_TILE = 2 * PROJ_HALF
TILE_NAMES = ("lx", "lg", "nk", "dk", "nv", "dv", "skv", "sq", "nq", "dq")
N_TILES = len(TILE_NAMES)
_FIRST_BLOCK = {"sq": 0, "skv": 2, "nq": 3, "nk": 5, "nv": 7, "dq": 9, "dk": 11, "dv": 13, "lx": 15, "lg": 17}
CLS_PLAIN, CLS_NORM, CLS_ROPE, CLS_MIXED = 0, 1, 2, 3
_TILE_CLASS = {"lx": CLS_PLAIN, "lg": CLS_PLAIN, "nv": CLS_PLAIN, "dv": CLS_PLAIN,
               "nk": CLS_NORM, "nq": CLS_NORM, "dk": CLS_ROPE, "sq": CLS_ROPE, "dq": CLS_ROPE,
               "skv": CLS_MIXED}
P_FIRST = 2
P_TILES = N_TILES - P_FIRST
P_COL = {name: TILE_NAMES.index(name) - P_FIRST for name in TILE_NAMES[P_FIRST:]}
F_TILES = 2

NA_QT = 128
NA_WIN_ROWS = 10
NA_WIN = NA_WIN_ROWS * GRID_W
LRU_CT = 128
ROW_CHUNK = 256
FFN_ROW_CHUNK = 512


def _sigmoid(x):
    return 1.0 / (1.0 + jnp.exp(-x))


def _dot(a, b):
    return jnp.dot(a, b, preferred_element_type=F32)


def _dot_nt(a, b):
    return lax.dot_general(a, b, (((1,), (1,)), ((), ())), preferred_element_type=F32)


def _params(*sem):
    return pltpu.CompilerParams(dimension_semantics=sem, vmem_limit_bytes=V7X_VMEM_LIMIT)


def _mod_kernel(c_ref, w_ref, b_ref, o_ref):
    cv = c_ref[...]
    s = cv * _sigmoid(cv)
    o_ref[0] = _dot(s.astype(BF16), w_ref[0].astype(BF16)) + b_ref[0]


def _modulation(cvecs, w_mod, b_mod):
    tn = 1024
    n = MOD_CHUNKS * D_MODEL
    return pl.pallas_call(
        _mod_kernel,
        out_shape=jax.ShapeDtypeStruct((DEPTH, 8, n), F32),
        grid=(DEPTH, n // tn),
        in_specs=[
            pl.BlockSpec((8, D_MODEL), lambda l, j: (0, 0)),
            pl.BlockSpec((1, D_MODEL, tn), lambda l, j: (l, 0, j)),
            pl.BlockSpec((1, 1, tn), lambda l, j: (l, 0, j)),
        ],
        out_specs=pl.BlockSpec((1, 8, tn), lambda l, j: (l, 0, j)),
        compiler_params=_params("arbitrary", "arbitrary"),
        name="modulation",
    )(cvecs, w_mod, b_mod.reshape(DEPTH, 1, n))


def _modnorm_static(x_ref, nw_ref, sh_ref, sc_ref, h_scr, r0, n_rows, chunk=128):
    for c in range(n_rows // chunk):
        rows = slice(r0 + c * chunk, r0 + (c + 1) * chunk)
        x = x_ref[rows, :]
        ms = jnp.mean(x * x, axis=-1, keepdims=True)
        y = x * lax.rsqrt(ms + NORM_EPS) * nw_ref[...]
        h_scr[rows, :] = (y * (1.0 + sc_ref[0]) + sh_ref[0]).astype(BF16)


CACHE_NAMES = ("swa_k", "swa_v", "na_k", "na_v", "diff_k", "diff_v")
_CACHE_SRC = {"swa_k": ("skv", 0, 128), "swa_v": ("skv", 128, 128), "na_k": ("nk", 0, 512),
              "na_v": ("nv", 0, 512), "diff_k": ("dk", 0, 512), "diff_v": ("dv", 0, 512)}


def _inproj_kernel(*refs, tm, rope, n_cache, seq):
    n_in = 10 if rope else 8
    tab_ref = refs[0]
    (x_ref, sh_ref, sc_ref, nw_ref, wlo_ref, whi_ref, gain_ref, hsum_ref) = refs[1:9]
    cos_ref, sin_ref = (refs[9], refs[10]) if rope else (None, None)
    outs = refs[1 + n_in + n_cache:]
    p_ref, f_ref = outs[0], outs[1]
    cache_refs = outs[2:2 + n_cache]
    h_scr = outs[2 + n_cache]
    j = pl.program_id(1)
    cls = tab_ref[2 * N_TILES + j]

    def run_tile(mode, halves, first=False):
        w = {half: (wlo_ref if half == 0 else whi_ref)[0].astype(BF16) for half in halves}
        n_chunks = tm // ROW_CHUNK
        units = [(half, c) for half in halves for c in range(n_chunks)]

        def norm_chunk(c):
            _modnorm_static(x_ref, nw_ref, sh_ref, sc_ref, h_scr, c * ROW_CHUNK, ROW_CHUNK)

        if first:
            norm_chunk(0)

        def main(unit):
            half, c = unit
            if first and half == halves[0] and c + 1 < n_chunks:
                norm_chunk(c + 1)
            return _dot(h_scr[c * ROW_CHUNK:(c + 1) * ROW_CHUNK, :], w[half])

        def finish(unit, p):
            half, c = unit
            rows = slice(c * ROW_CHUNK, (c + 1) * ROW_CHUNK)
            cols = slice(half * PROJ_HALF, (half + 1) * PROJ_HALF)
            y = p
            if mode != CLS_PLAIN:
                ms = _dot((p * p).astype(BF16), hsum_ref[...])
                y = p * lax.rsqrt(ms + NORM_EPS) * gain_ref[0, :, cols]
                if rope and mode in (CLS_ROPE, CLS_MIXED):
                    lane = lax.broadcasted_iota(jnp.int32, y.shape, 1)
                    up = pltpu.roll(y, PROJ_HALF - 16, 1)
                    down = pltpu.roll(y, 16, 1)
                    partner = jnp.where((lane & 31) < 16, up, down)
                    y = y * cos_ref[rows, :] + partner * sin_ref[rows, :]
                if mode == CLS_MIXED:
                    lane = lax.broadcasted_iota(jnp.int32, y.shape, 1)
                    y = jnp.where(lane < 2 * HEAD_DIM, y, p)
            p_ref[rows, cols] = y.astype(BF16)
            f_ref[rows, cols] = y

        prev, p_prev = units[0], main(units[0])
        for unit in units[1:]:
            p_next = main(unit)
            finish(prev, p_prev)
            prev, p_prev = unit, p_next
        finish(prev, p_prev)

    assert _TILE_CLASS[TILE_NAMES[0]] == CLS_PLAIN

    @pl.when(j == 0)
    def _():
        run_tile(CLS_PLAIN, (0, 1), first=True)

    modes = (CLS_PLAIN, CLS_NORM, CLS_ROPE) if rope else (CLS_PLAIN, CLS_NORM)
    for mode in modes:
        cond = (cls == mode)
        if mode == CLS_PLAIN:
            cond = cond & (j > 0)
        if not rope and mode == CLS_NORM:
            cond = (cls == CLS_NORM) | (cls == CLS_ROPE)

        @pl.when(cond)
        def _(mode=mode):
            run_tile(mode, (0, 1))

    @pl.when(cls == CLS_MIXED)
    def _():
        run_tile(CLS_MIXED, (0,))
        hi = slice(PROJ_HALF, PROJ_TILE)
        p_ref[:, hi] = jnp.zeros((tm, PROJ_HALF), BF16)
        f_ref[:, hi] = jnp.zeros((tm, PROJ_HALF), F32)

    for name, c_ref in zip(CACHE_NAMES, cache_refs):
        tile, off, width = _CACHE_SRC[name]

        @pl.when(j == TILE_NAMES.index(tile))
        def _(c_ref=c_ref, off=off, width=width):
            for b in range(tm // seq):
                c_ref[b, 0] = f_ref[b * seq:(b + 1) * seq, off:off + width]


def _in_projection(x, mods_l, mod_row, norm_w, w_in_bf, layer, tab, gain, hsum, rope_tabs, caches, *, tm, seq):
    n_tok = x.shape[0]
    rope = rope_tabs is not None
    caches = [] if caches is None else list(caches)
    n_cache = len(caches)

    in_specs = [
        pl.BlockSpec((tm, D_MODEL), lambda i, j, t: (i, 0)),
        pl.BlockSpec((1, 1, D_MODEL), lambda i, j, t: (mod_row(i) * MOD_CHUNKS + 0, 0, 0)),
        pl.BlockSpec((1, 1, D_MODEL), lambda i, j, t: (mod_row(i) * MOD_CHUNKS + 1, 0, 0)),
        pl.BlockSpec((1, D_MODEL), lambda i, j, t: (0, 0)),
        pl.BlockSpec((1, D_MODEL, PROJ_HALF), lambda i, j, t: (layer, 0, t[j])),
        pl.BlockSpec((1, D_MODEL, PROJ_HALF), lambda i, j, t: (layer, 0, t[N_TILES + j])),
        pl.BlockSpec((1, 1, PROJ_TILE), lambda i, j, t: (j, 0, 0)),
        pl.BlockSpec((PROJ_HALF, PROJ_HALF), lambda i, j, t: (0, 0)),
    ]
    args = [x, mods_l, mods_l, norm_w.reshape(1, D_MODEL), w_in_bf, w_in_bf, gain, hsum]
    if rope:
        cos_t, sin_t = rope_tabs
        in_specs += [pl.BlockSpec((tm, PROJ_HALF), lambda i, j, t: (0, 0)),
                     pl.BlockSpec((tm, PROJ_HALF), lambda i, j, t: (0, 0))]
        args += [cos_t, sin_t]
    n_in = len(args)
    in_specs += [pl.BlockSpec(memory_space=pl.ANY)] * n_cache
    args += caches
    out_specs = [
        pl.BlockSpec((tm, PROJ_TILE), lambda i, j, t: (i, jnp.maximum(j - P_FIRST, 0))),
        pl.BlockSpec((tm, PROJ_TILE), lambda i, j, t: (i, jnp.minimum(j, F_TILES))),
    ]
    out_shape = [jax.ShapeDtypeStruct((n_tok, P_TILES * PROJ_TILE), BF16),
                 jax.ShapeDtypeStruct((n_tok, (F_TILES + 1) * PROJ_TILE), F32)]
    for arr in caches:
        width = arr.shape[-1]
        out_specs.append(pl.BlockSpec((tm // seq, 1, seq, width), lambda i, j, t: (i, layer, 0, 0)))
        out_shape.append(jax.ShapeDtypeStruct(arr.shape, arr.dtype))
    res = pl.pallas_call(
        functools.partial(_inproj_kernel, tm=tm, rope=rope, n_cache=n_cache, seq=seq),
        out_shape=out_shape,
        grid_spec=pltpu.PrefetchScalarGridSpec(
            num_scalar_prefetch=1,
            grid=(n_tok // tm, N_TILES),
            in_specs=in_specs,
            out_specs=out_specs,
            scratch_shapes=[pltpu.VMEM((tm, D_MODEL), BF16)],
        ),
        input_output_aliases={1 + n_in + k: 2 + k for k in range(n_cache)},
        compiler_params=_params("arbitrary", "arbitrary"),
        name="in_projection_rope" if rope else "in_projection",
    )(tab, *args)
    return res[0], res[1], list(res[2:])


def _joint(scores):
    return scores[0] if len(scores) == 1 else jnp.concatenate(scores, axis=1)


PV_W = 2 * HEAD_DIM


def _weights_pv(segs, extra=None):
    s = _joint([sc for sc, _ in segs])
    m = jnp.max(s, axis=-1, keepdims=True)
    if extra is not None:
        m = jnp.maximum(m, extra)
    e = jnp.exp2(s - m).astype(BF16)
    acc, off = None, 0
    for sc, v in segs:
        n = sc.shape[1]
        v1 = jnp.concatenate([v, jnp.ones((n, PV_W), BF16)], axis=1)
        part = _dot(e[:, off:off + n], v1)
        acc = part if acc is None else acc + part
        off += n
    num, den = acc[:, :PV_W], acc[:, PV_W:]
    if extra is not None:
        den = den + jnp.exp2(extra - m)
    return num, den


def _softmax_pv(segs, extra=None):
    num, den = _weights_pv(segs, extra)
    return num / den


def _half_vecs():
    lane = lax.broadcasted_iota(jnp.int32, (1, 2 * HEAD_DIM), 1)
    lo = (lane < HEAD_DIM).astype(BF16)
    return lo, (1.0 - lo).astype(BF16)


def _pair_block(ref, m, rows=None, b=0):
    if rows is None:
        return ref[b, :, m * 128:(m + 1) * 128]
    return ref[b, rows, m * 128:(m + 1) * 128]


def _both_halves(q, vecs):
    return jnp.concatenate([q * vecs[0], q * vecs[1]], axis=0)


def _merge_pair(o_even, o_odd):
    lane = lax.broadcasted_iota(jnp.int32, o_even.shape, 1)
    return jnp.where(lane < HEAD_DIM, o_even, o_odd)


def _diff_lambda_val(dl_ref, lam_init):
    lp = dl_ref[...]
    s1 = jnp.sum(lp[0:1] * lp[1:2], axis=-1, keepdims=True)
    s2 = jnp.sum(lp[2:3] * lp[3:4], axis=-1, keepdims=True)
    return jnp.exp(s1) - jnp.exp(s2) + lam_init


def _diff_finish(o, subln_ref, lam_init):
    ms = jnp.mean(o * o, axis=-1, keepdims=True)
    return o * lax.rsqrt(ms + NORM_EPS) * subln_ref[...] * (1.0 - lam_init)


def _swap_halves_bf16(x):
    return pltpu.roll(x.astype(F32), HEAD_DIM, 1).astype(BF16)


def _gqa_queries(q_ref, hk, vecs, b=0, rows=None):
    parts = []
    for g in range(SWA_GROUP):
        h = hk * SWA_GROUP + g
        q = _pair_block(q_ref, h // 2, rows=rows, b=b)
        if h % 2 != hk:
            q = _swap_halves_bf16(q)
        parts.append(q * vecs[hk])
    return jnp.concatenate(parts, axis=0)


def _gqa_outputs(o, hk, rows):
    blocks = []
    for j in range(SWA_GROUP // 2):
        halves = []
        for g in (2 * j, 2 * j + 1):
            og = o[g * rows:(g + 1) * rows]
            halves.append(og if g % 2 == hk else pltpu.roll(og, HEAD_DIM, 1))
        blocks.append(_merge_pair(halves[0], halves[1]))
    return blocks


def _pipelined(items, depth=2):
    pending, outs = [], []
    for score_fn, finish_fn in items:
        pending.append((finish_fn, score_fn()))
        if len(pending) > depth:
            fn, s = pending.pop(0)
            outs.append(fn(s))
    for fn, s in pending:
        outs.append(fn(s))
    return outs


def _ctx_attn_kernel(sink_ref, sq_ref, skv_ref, nq_ref, nk_ref, nv_ref, dq_ref, dk_ref, dv_ref,
                     dl_ref, subln_ref, o_ref, *, t, lam_init, bb):
    lam = _diff_lambda_val(dl_ref, lam_init)
    vecs = _half_vecs()

    def swa_item(b, hk):
        def scores():
            return _dot_nt(_gqa_queries(sq_ref, hk, vecs, b=b), _pair_block(skv_ref, 0, b=b))

        def finish(s):
            sink = jnp.concatenate(
                [jnp.full((t, 1), sink_ref[hk * SWA_GROUP + g] * LOG2E, F32) for g in range(SWA_GROUP)], axis=0)
            return _gqa_outputs(_softmax_pv([(s, _pair_block(skv_ref, 1, b=b))], extra=sink), hk, t)
        return scores, finish

    def na_item(b, m):
        def scores():
            return _dot_nt(_both_halves(_pair_block(nq_ref, m, b=b), vecs), _pair_block(nk_ref, m, b=b))

        def finish(s):
            o = _softmax_pv([(s, _pair_block(nv_ref, m, b=b))])
            return [_merge_pair(o[:t], o[t:])]
        return scores, finish

    def diff_item(b, h):
        def scores():
            return _dot_nt(_both_halves(_pair_block(dq_ref, h, b=b), vecs), _pair_block(dk_ref, h, b=b))

        def finish(s):
            o = _softmax_pv([(s, _pair_block(dv_ref, h, b=b))])
            return [_diff_finish(o[:t] - lam * o[t:], subln_ref, lam_init)]
        return scores, finish

    items = []
    for make, n in ((swa_item, SWA_KV_HEADS), (na_item, NA_HEADS // 2), (diff_item, DIFF_HEADS)):
        items += [make(b, i) for i in range(n) for b in range(bb)]
    res = _pipelined(items, depth=2 * bb)
    for b in range(bb):
        pieces = [piece for r in res[b::bb] for piece in r]
        o_ref[b] = jnp.concatenate(pieces, axis=1).astype(BF16)


CTX_ROWS_PER_STEP = 4


def _pcol(name, t, bb=1):
    return pl.BlockSpec((bb, t, PROJ_TILE), lambda i, c=P_COL[name]: (i, 0, c))


def _ctx_attention(p, sink, dlam, subln, lam_init, *, nb, t):
    bb = CTX_ROWS_PER_STEP
    p3 = p.reshape(nb, t, P_TILES * PROJ_TILE)
    names = ("sq", "skv", "nq", "nk", "nv", "dq", "dk", "dv")
    out = pl.pallas_call(
        functools.partial(_ctx_attn_kernel, t=t, lam_init=lam_init, bb=bb),
        out_shape=jax.ShapeDtypeStruct((nb, t, 3 * GROUP_W), BF16),
        grid=(nb // bb,),
        in_specs=[pl.BlockSpec(memory_space=pltpu.SMEM)] + [_pcol(n, t, bb) for n in names] + [
            pl.BlockSpec((4, HEAD_DIM), lambda i: (0, 0)),
            pl.BlockSpec((1, DIFF_VDIM), lambda i: (0, 0)),
        ],
        out_specs=pl.BlockSpec((bb, t, 3 * GROUP_W), lambda i: (i, 0, 0)),
        compiler_params=_params("arbitrary"),
        name="ctx_attention",
    )(sink, *([p3] * len(names)), dlam, subln.reshape(1, DIFF_VDIM))
    return out.reshape(nb * t, 3 * GROUP_W)


Q_BLOCKS_PER_STEP = 4


def _swa_kernel(sink_ref, q_ref, kv_ref, ck_ref, cv_ref, o_ref, *, t, qb):
    span = 3 * qb
    vecs = _half_vecs()

    def item(j, hk):
        n = pl.program_id(1) * Q_BLOCKS_PER_STEP + j
        start = pl.multiple_of(jnp.clip((n - 1) * qb, 0, t - span), qb)
        win = pl.ds(start, span)
        qrows = slice(j * qb, (j + 1) * qb)

        def scores():
            qs = _gqa_queries(q_ref, hk, vecs, rows=qrows)
            return _dot_nt(qs, _pair_block(kv_ref, 0, rows=win)), _dot_nt(qs, ck_ref[0])

        def finish(s):
            row = (lax.broadcasted_iota(jnp.int32, (SWA_GROUP * qb, span), 0) & (qb - 1)) + n * qb
            col = lax.broadcasted_iota(jnp.int32, (SWA_GROUP * qb, span), 1) + start
            dist = row - col
            ok = (dist <= SWA_WINDOW) & (dist >= -SWA_WINDOW)
            sink = jnp.concatenate(
                [jnp.full((qb, 1), sink_ref[hk * SWA_GROUP + g] * LOG2E, F32) for g in range(SWA_GROUP)], axis=0)
            s_loc = jnp.where(ok, s[0], NEG_INF)
            o = _softmax_pv([(s_loc, _pair_block(kv_ref, 1, rows=win)), (s[1], cv_ref[0])], extra=sink)
            return _gqa_outputs(o, hk, qb)
        return scores, finish

    items = [item(j, hk) for hk in range(SWA_KV_HEADS) for j in range(Q_BLOCKS_PER_STEP)]
    res = _pipelined(items, depth=Q_BLOCKS_PER_STEP)
    for j in range(Q_BLOCKS_PER_STEP):
        pieces = [piece for r in res[j::Q_BLOCKS_PER_STEP] for piece in r]
        o_ref[0, j * qb:(j + 1) * qb, :] = jnp.concatenate(pieces, axis=1).astype(BF16)


def _swa_latent(p, ck, cv, layer, sink, *, nb, t):
    qb = 128
    step_rows = qb * Q_BLOCKS_PER_STEP
    p3 = p.reshape(nb, t, P_TILES * PROJ_TILE)
    out = pl.pallas_call(
        functools.partial(_swa_kernel, t=t, qb=qb),
        out_shape=jax.ShapeDtypeStruct((nb, t, GROUP_W), BF16),
        grid=(nb, t // step_rows),
        in_specs=[
            pl.BlockSpec(memory_space=pltpu.SMEM),
            pl.BlockSpec((1, step_rows, PROJ_TILE), lambda b, n: (b, n, P_COL["sq"])),
            pl.BlockSpec((1, t, PROJ_TILE), lambda b, n: (b, 0, P_COL["skv"])),
            pl.BlockSpec((1,) + ck.shape[1:], lambda b, n: (b * DEPTH + layer, 0, 0)),
            pl.BlockSpec((1,) + cv.shape[1:], lambda b, n: (b * DEPTH + layer, 0, 0)),
        ],
        out_specs=pl.BlockSpec((1, step_rows, GROUP_W), lambda b, n: (b, n, 0)),
        compiler_params=_params("arbitrary", "arbitrary"),
        name="swa_latent",
    )(sink, p3, p3, ck, cv)
    return out.reshape(nb * t, GROUP_W)


def _na_row_start(r, rows):
    kh = min(NA_KH, rows)
    return min(max(r - kh // 2, 0), rows - kh)


def _na_win_start(qt, rows):
    return min(max(_na_row_start(2 * qt, rows), 0), rows - NA_WIN_ROWS)


def _na_bias_kernel(rpb_ref, o_ref, *, rows):
    h = pl.program_id(0)
    n_dr, n_dc = 2 * NA_KH - 1, 2 * NA_KW - 1
    qi = lax.broadcasted_iota(jnp.int32, (GRID_W, GRID_W), 0)
    ki = lax.broadcasted_iota(jnp.int32, (GRID_W, GRID_W), 1)
    dc = jnp.clip(ki - qi + (NA_KW - 1), 0, n_dc - 1)
    cs = jnp.clip(qi - NA_KW // 2, 0, GRID_W - NA_KW)
    col_ok = (ki >= cs) & (ki < cs + NA_KW)
    neg = jnp.full((GRID_W, GRID_W), NEG_INF, F32)
    tabs = []
    for dr in range(n_dr):
        acc = jnp.zeros((GRID_W, GRID_W), F32)
        for c in range(n_dc):
            acc = jnp.where(dc == c, rpb_ref[(h * n_dr + dr) * n_dc + c], acc)
        tabs.append(jnp.where(col_ok, acc * LOG2E, neg))
    kh = min(NA_KH, rows)
    for qt in range(rows // 2):
        ws = _na_win_start(qt, rows)
        bands = []
        for qq in range(2):
            qr = 2 * qt + qq
            rs = _na_row_start(qr, rows)
            blks = []
            for kk in range(NA_WIN_ROWS):
                kr = ws + kk
                blks.append(tabs[kr - qr + NA_KH - 1] if rs <= kr < rs + kh else neg)
            bands.append(jnp.concatenate(blks, axis=1))
        o_ref[0, qt] = jnp.concatenate(bands, axis=0)


def _na_bias(rpb, rows):
    n_qt = rows // 2
    return pl.pallas_call(
        functools.partial(_na_bias_kernel, rows=rows),
        out_shape=jax.ShapeDtypeStruct((NA_HEADS, n_qt, NA_QT, NA_WIN), F32),
        grid=(NA_HEADS,),
        in_specs=[pl.BlockSpec(memory_space=pltpu.SMEM)],
        out_specs=pl.BlockSpec((1, n_qt, NA_QT, NA_WIN), lambda h: (h, 0, 0, 0)),
        compiler_params=_params("arbitrary"),
        name="na_bias",
    )(rpb.reshape(-1))


def _na_kernel(q_ref, k_ref, v_ref, ck_ref, cv_ref, bias_ref, o_ref, *, rows):
    vecs = _half_vecs()

    def item(j, m):
        qt = pl.program_id(1) * Q_BLOCKS_PER_STEP + j
        ws = jnp.clip(jnp.clip(2 * qt - NA_KH // 2, 0, rows - NA_KH), 0, rows - NA_WIN_ROWS)
        win = pl.ds(pl.multiple_of(ws * GRID_W, GRID_W), NA_WIN)
        qrows = slice(j * NA_QT, (j + 1) * NA_QT)

        def scores():
            q2 = _both_halves(_pair_block(q_ref, m, rows=qrows), vecs)
            return _dot_nt(q2, _pair_block(k_ref, m, rows=win)), _dot_nt(q2, _pair_block(ck_ref, m))

        def finish(s):
            v, cv = _pair_block(v_ref, m, rows=win), _pair_block(cv_ref, m)
            bias = jnp.concatenate([bias_ref[2 * m, j], bias_ref[2 * m + 1, j]], axis=0)
            o = _softmax_pv([(s[0] + bias, v), (s[1], cv)])
            return _merge_pair(o[:NA_QT], o[NA_QT:])
        return scores, finish

    items = [item(j, m) for m in range(NA_HEADS // 2) for j in range(Q_BLOCKS_PER_STEP)]
    res = _pipelined(items, depth=2 * Q_BLOCKS_PER_STEP)
    for j in range(Q_BLOCKS_PER_STEP):
        o_ref[0, j * NA_QT:(j + 1) * NA_QT, :] = jnp.concatenate(res[j::Q_BLOCKS_PER_STEP], axis=1).astype(BF16)


def _na_latent(p, ck, cv, layer, bias, *, nb, t):
    rows = t // GRID_W
    p3 = p.reshape(nb, t, P_TILES * PROJ_TILE)
    out = pl.pallas_call(
        functools.partial(_na_kernel, rows=rows),
        out_shape=jax.ShapeDtypeStruct((nb, t, GROUP_W), BF16),
        grid=(nb, t // (NA_QT * Q_BLOCKS_PER_STEP)),
        in_specs=[
            pl.BlockSpec((1, NA_QT * Q_BLOCKS_PER_STEP, PROJ_TILE), lambda i, n: (i, n, P_COL["nq"])),
            pl.BlockSpec((1, t, PROJ_TILE), lambda i, n: (i, 0, P_COL["nk"])),
            pl.BlockSpec((1, t, PROJ_TILE), lambda i, n: (i, 0, P_COL["nv"])),
            pl.BlockSpec((1,) + ck.shape[1:], lambda i, n: (i * DEPTH + layer, 0, 0)),
            pl.BlockSpec((1,) + cv.shape[1:], lambda i, n: (i * DEPTH + layer, 0, 0)),
            pl.BlockSpec((NA_HEADS, Q_BLOCKS_PER_STEP, NA_QT, NA_WIN), lambda i, n: (0, n, 0, 0)),
        ],
        out_specs=pl.BlockSpec((1, NA_QT * Q_BLOCKS_PER_STEP, GROUP_W), lambda i, n: (i, n, 0)),
        compiler_params=_params("arbitrary", "arbitrary"),
        name="na_latent",
    )(p3, p3, p3, ck, cv, bias)
    return out.reshape(nb * t, GROUP_W)


def _diff_kernel(q_ref, k_ref, v_ref, ck_ref, cv_ref, dl_ref, subln_ref, o_ref, *, lam_init, tq):
    lam = _diff_lambda_val(dl_ref, lam_init)

    vecs = _half_vecs()

    def item(j, h):
        qrows = slice(j * tq, (j + 1) * tq)

        def scores():
            q2 = _both_halves(_pair_block(q_ref, h, rows=qrows), vecs)
            return _dot_nt(q2, _pair_block(k_ref, h)), _dot_nt(q2, _pair_block(ck_ref, h))

        def finish(s):
            o = _softmax_pv([(s[0], _pair_block(v_ref, h)), (s[1], _pair_block(cv_ref, h))])
            return _diff_finish(o[:tq] - lam * o[tq:], subln_ref, lam_init)
        return scores, finish

    items = [item(j, h) for h in range(DIFF_HEADS) for j in range(Q_BLOCKS_PER_STEP)]
    res = _pipelined(items, depth=Q_BLOCKS_PER_STEP)
    for j in range(Q_BLOCKS_PER_STEP):
        o_ref[0, j * tq:(j + 1) * tq, :] = jnp.concatenate(res[j::Q_BLOCKS_PER_STEP], axis=1).astype(BF16)


def _diff_latent(p, ck, cv, layer, dlam, subln, lam_init, *, nb, t):
    tq_sub = 256
    tq = tq_sub * Q_BLOCKS_PER_STEP
    p3 = p.reshape(nb, t, P_TILES * PROJ_TILE)
    out = pl.pallas_call(
        functools.partial(_diff_kernel, lam_init=lam_init, tq=tq_sub),
        out_shape=jax.ShapeDtypeStruct((nb, t, GROUP_W), BF16),
        grid=(nb, t // tq),
        in_specs=[
            pl.BlockSpec((1, tq, PROJ_TILE), lambda i, n: (i, n, P_COL["dq"])),
            pl.BlockSpec((1, t, PROJ_TILE), lambda i, n: (i, 0, P_COL["dk"])),
            pl.BlockSpec((1, t, PROJ_TILE), lambda i, n: (i, 0, P_COL["dv"])),
            pl.BlockSpec((1,) + ck.shape[1:], lambda i, n: (i * DEPTH + layer, 0, 0)),
            pl.BlockSpec((1,) + cv.shape[1:], lambda i, n: (i * DEPTH + layer, 0, 0)),
            pl.BlockSpec((4, HEAD_DIM), lambda i, n: (0, 0)),
            pl.BlockSpec((1, DIFF_VDIM), lambda i, n: (0, 0)),
        ],
        out_specs=pl.BlockSpec((1, tq, GROUP_W), lambda i, n: (i, n, 0)),
        compiler_params=_params("arbitrary", "arbitrary"),
        name="diff_latent",
    )(p3, p3, p3, ck, cv, dlam, subln.reshape(1, DIFF_VDIM))
    return out.reshape(nb * t, GROUP_W)


SUBLANES = 8


def _linear_scan(a, b, row, t, reverse, sa_ref, sb_ref, sc_ref):
    nt = t // SUBLANES
    sub = row & (SUBLANES - 1)

    def doubling(a, b, idx, n, steps):
        for s in steps:
            if reverse:
                ok = idx < n - s
                a_sh = jnp.where(ok, pltpu.roll(a, a.shape[0] - s, 0), 1.0)
                b_sh = jnp.where(ok, pltpu.roll(b, b.shape[0] - s, 0), 0.0)
            else:
                ok = idx >= s
                a_sh = jnp.where(ok, pltpu.roll(a, s, 0), 1.0)
                b_sh = jnp.where(ok, pltpu.roll(b, s, 0), 0.0)
            b = a * b_sh + b
            a = a * a_sh
        return a, b

    tiles = (nt, SUBLANES, LRU_CT)
    sub3 = lax.broadcasted_iota(jnp.int32, tiles, 1)

    def tile_doubling(a, b):
        for s in (1, 2, 4):
            ok = (sub3 < SUBLANES - s) if reverse else (sub3 >= s)
            sh = SUBLANES - s if reverse else s
            a_sh = jnp.where(ok, pltpu.roll(a, sh, 1), 1.0)
            b_sh = jnp.where(ok, pltpu.roll(b, sh, 1), 0.0)
            b = a * b_sh + b
            a = a * a_sh
        return a, b

    a, b = tile_doubling(a.reshape(tiles), b.reshape(tiles))
    a, b = a.reshape(t, LRU_CT), b.reshape(t, LRU_CT)
    sa_ref[...] = a
    sb_ref[...] = b
    last = 0 if reverse else SUBLANES - 1
    at = sa_ref[pl.ds(last, nt, stride=SUBLANES), :]
    bt = sb_ref[pl.ds(last, nt, stride=SUBLANES), :]
    rowt = lax.broadcasted_iota(jnp.int32, (nt, LRU_CT), 0)
    steps, s = [], 1
    while s < nt:
        steps.append(s)
        s *= 2
    _, bt = doubling(at, bt, rowt, nt, steps)
    if reverse:
        carry = jnp.where(rowt < nt - 1, pltpu.roll(bt, nt - 1, 0), 0.0)
    else:
        carry = jnp.where(rowt >= 1, pltpu.roll(bt, 1, 0), 0.0)
    for r in range(SUBLANES):
        sc_ref[pl.ds(r, nt, stride=SUBLANES), :] = carry
    return b + a * sc_ref[...]


def _lru_kernel(x_ref, g_ref, cw_ref, cb_ref, w_ref, gb_ref, lam_ref, h0_ref, y_ref, st_ref, *scratch, t, cpt):
    row = lax.broadcasted_iota(jnp.int32, (t, LRU_CT), 0)
    for ci in range(cpt):
        lanes = slice(ci * LRU_CT, (ci + 1) * LRU_CT)
        x = x_ref[0, :, lanes]
        u = cb_ref[:, lanes] + cw_ref[2:3, lanes] * x
        for tap, off in ((0, -2), (1, -1), (3, 1)):
            xs = pltpu.roll(x, (-off) % t, 0)
            ok = (row + off >= 0) & (row + off < t)
            u = u + cw_ref[tap:tap + 1, lanes] * jnp.where(ok, xs, 0.0)
        gates = _dot(u.astype(BF16), w_ref[ci]) + gb_ref[ci]
        total = None
        for d in range(2):
            r = _sigmoid(gates[:, (2 * d) * LRU_CT:(2 * d + 1) * LRU_CT])
            ig = _sigmoid(gates[:, (2 * d + 1) * LRU_CT:(2 * d + 2) * LRU_CT])
            nl = -lam_ref[d:d + 1, lanes]
            softplus = jnp.maximum(nl, 0.0) + jnp.log1p(jnp.exp(-jnp.abs(nl)))
            log_a = -LRU_C * r * softplus
            a = jnp.exp(log_a)
            bx = jnp.sqrt(jnp.tanh(-log_a) * (a * a + 1.0)) * (ig * u)
            edge = t - 1 if d == 1 else 0
            bx = jnp.where(row == edge, bx + a * h0_ref[0, d:d + 1, lanes], bx)
            sa_ref, sb_ref, sc_ref = scratch[3 * (2 * ci + d):3 * (2 * ci + d) + 3]
            bx = _linear_scan(a, bx, row, t, d == 1, sa_ref, sb_ref, sc_ref)
            fin = t - 1 if d == 0 else 0
            st_ref[0, d:d + 1, lanes] = bx[fin:fin + 1, :]
            total = bx if total is None else total + bx
        g = g_ref[0, :, lanes]
        gelu = 0.5 * g * (1.0 + jnp.tanh(math.sqrt(2.0 / math.pi) * (g + 0.044715 * (g * g * g))))
        y_ref[0, :, lanes] = (total * gelu).astype(BF16)


def _lru_mixer(f, conv_w, conv_b, w_gates, b_gates, lam, h0, *, nb, t, cpt):
    f3 = f.reshape(nb, t, f.shape[1])
    nct = LRU_WIDTH // LRU_CT
    wide = cpt * LRU_CT
    y, st = pl.pallas_call(
        functools.partial(_lru_kernel, t=t, cpt=cpt),
        out_shape=[jax.ShapeDtypeStruct((nb, t, LRU_WIDTH), BF16),
                   jax.ShapeDtypeStruct((nb, 2, LRU_WIDTH), F32)],
        grid=(nb, nct // cpt),
        in_specs=[
            pl.BlockSpec((1, t, wide), lambda i, c: (i, 0, c)),
            pl.BlockSpec((1, t, wide), lambda i, c: (i, 0, nct // cpt + c)),
            pl.BlockSpec((CONV_W, wide), lambda i, c: (0, c)),
            pl.BlockSpec((1, wide), lambda i, c: (0, c)),
            pl.BlockSpec((cpt, LRU_CT, 4 * LRU_CT), lambda i, c: (c, 0, 0)),
            pl.BlockSpec((cpt, 1, 4 * LRU_CT), lambda i, c: (c, 0, 0)),
            pl.BlockSpec((2, wide), lambda i, c: (0, c)),
            pl.BlockSpec((1, 2, wide), lambda i, c: (i, 0, c)),
        ],
        out_specs=[pl.BlockSpec((1, t, wide), lambda i, c: (i, 0, c)),
                   pl.BlockSpec((1, 2, wide), lambda i, c: (i, 0, c))],
        scratch_shapes=[pltpu.VMEM((t, LRU_CT), F32)] * (6 * cpt),
        compiler_params=_params("arbitrary", "arbitrary"),
        name="lru_mixer",
    )(f3, f3, conv_w, conv_b.reshape(1, LRU_WIDTH), w_gates, b_gates, lam, h0)
    return y.reshape(nb * t, LRU_WIDTH), st


def _lru_gate_weights(wa, ba, wx, bx):
    nct = LRU_WIDTH // LRU_CT
    bpt = LRU_CT // LRU_BLOCK_W
    eye = jnp.eye(bpt, dtype=F32)

    def tile_w(w):
        w4 = w.reshape(nct, bpt, LRU_BLOCK_W, LRU_BLOCK_W)
        return jnp.einsum("cnij,nm->cnimj", w4, eye).reshape(nct, LRU_CT, LRU_CT)

    w = jnp.concatenate([tile_w(wa[0]), tile_w(wx[0]), tile_w(wa[1]), tile_w(wx[1])], axis=-1)
    b = jnp.concatenate([v.reshape(nct, 1, LRU_CT) for v in (ba[0], bx[0], ba[1], bx[1])], axis=-1)
    return w.astype(BF16), b


def _outproj_kernel(*refs, n_in):
    ins = refs[:n_in]
    w_ref, x_ref, g_ref, o_ref = refs[n_in:]
    acc = None
    off = 0
    for m_ref in ins:
        width = m_ref.shape[1]
        part = _dot(m_ref[...], w_ref[0, off:off + width, :])
        acc = part if acc is None else acc + part
        off += width
    o_ref[...] = x_ref[...] + g_ref[0] * acc


def _out_projection(mixes, w_out_bf, layer, x, mods_l, mod_row, *, tm):
    n_tok = x.shape[0]
    return pl.pallas_call(
        functools.partial(_outproj_kernel, n_in=len(mixes)),
        out_shape=jax.ShapeDtypeStruct((n_tok, D_MODEL), F32),
        grid=(n_tok // tm,),
        in_specs=[pl.BlockSpec((tm, m.shape[1]), lambda i: (i, 0)) for m in mixes] + [
            pl.BlockSpec((1, D_MODEL, D_MODEL), lambda i: (layer, 0, 0)),
            pl.BlockSpec((tm, D_MODEL), lambda i: (i, 0)),
            pl.BlockSpec((1, 1, D_MODEL), lambda i: (mod_row(i) * MOD_CHUNKS + 2, 0, 0)),
        ],
        out_specs=pl.BlockSpec((tm, D_MODEL), lambda i: (i, 0)),
        compiler_params=_params("arbitrary"),
        name="out_projection",
    )(*mixes, w_out_bf, x, mods_l)


def _ffn_kernel(x_ref, sh_ref, sc_ref, g_ref, nw_ref, wg_ref, wu_ref, wd_ref, o_ref, h_scr, *, tm):
    k = pl.program_id(1)

    def step(first):
        wg = wg_ref[0].astype(BF16)
        wu = wu_ref[0].astype(BF16)
        wd = wd_ref[0].astype(BF16)
        chunk = ROW_CHUNK if first else FFN_ROW_CHUNK

        def norm_chunk(c):
            _modnorm_static(x_ref, nw_ref, sh_ref, sc_ref, h_scr, c * chunk, chunk)

        n_chunks = tm // chunk

        def gate_up(c):
            if first and c + 1 < n_chunks:
                norm_chunk(c + 1)
            h = h_scr[c * chunk:(c + 1) * chunk, :]
            return _dot(h, wg), _dot(h, wu)

        def down(c, gu):
            g, u = gu
            act = (g * _sigmoid(g)) * u
            res = _dot(act.astype(BF16), wd)
            rows = slice(c * chunk, (c + 1) * chunk)
            if first:
                o_ref[rows, :] = res
            else:
                o_ref[rows, :] += res

        if first:
            norm_chunk(0)
        pending = gate_up(0)
        for c in range(1, n_chunks):
            nxt = gate_up(c)
            down(c - 1, pending)
            pending = nxt
        down(n_chunks - 1, pending)

    @pl.when(k == 0)
    def _():
        step(True)

    @pl.when(k > 0)
    def _():
        step(False)

    @pl.when(k == pl.num_programs(1) - 1)
    def _():
        def body(c, carry):
            r = pl.multiple_of(c * 128, 128)
            o_ref[pl.ds(r, 128), :] = x_ref[pl.ds(r, 128), :] + g_ref[0] * o_ref[pl.ds(r, 128), :]
            return carry
        lax.fori_loop(0, tm // 128, body, 0)


def _ffn(x, mods_l, mod_row, norm_w, wg, wu, wd, layer, *, tm, th):
    n_tok = x.shape[0]
    return pl.pallas_call(
        functools.partial(_ffn_kernel, tm=tm),
        out_shape=jax.ShapeDtypeStruct((n_tok, D_MODEL), F32),
        grid=(n_tok // tm, FFN_HIDDEN // th),
        in_specs=[
            pl.BlockSpec((tm, D_MODEL), lambda i, k: (i, 0), pipeline_mode=pl.Buffered(1)),
            pl.BlockSpec((1, 1, D_MODEL), lambda i, k: (mod_row(i) * MOD_CHUNKS + 3, 0, 0)),
            pl.BlockSpec((1, 1, D_MODEL), lambda i, k: (mod_row(i) * MOD_CHUNKS + 4, 0, 0)),
            pl.BlockSpec((1, 1, D_MODEL), lambda i, k: (mod_row(i) * MOD_CHUNKS + 5, 0, 0)),
            pl.BlockSpec((1, D_MODEL), lambda i, k: (0, 0)),
            pl.BlockSpec((1, D_MODEL, th), lambda i, k: (layer, 0, k)),
            pl.BlockSpec((1, D_MODEL, th), lambda i, k: (layer, 0, k)),
            pl.BlockSpec((1, th, D_MODEL), lambda i, k: (layer, k, 0)),
        ],
        out_specs=pl.BlockSpec((tm, D_MODEL), lambda i, k: (i, 0)),
        scratch_shapes=[pltpu.VMEM((tm, D_MODEL), BF16)],
        compiler_params=_params("arbitrary", "arbitrary"),
        name="ffn",
    )(x, mods_l, mods_l, mods_l, norm_w.reshape(1, D_MODEL), wg, wu, wd)


def _rope_tables(t_lat):
    nf = HEAD_DIM // 4
    tok = jnp.arange(t_lat)
    pos = jnp.stack([tok // GRID_W, tok % GRID_W], axis=-1).astype(F32)
    inv = ROPE_BASE ** (-jnp.arange(nf, dtype=F32) / nf)
    ang = pos[:, :, None] * inv
    cos, sin = jnp.cos(ang), jnp.sin(ang)
    cos_h = jnp.concatenate([cos[:, 0], cos[:, 0], cos[:, 1], cos[:, 1]], axis=-1)
    sin_h = jnp.concatenate([-sin[:, 0], sin[:, 0], -sin[:, 1], sin[:, 1]], axis=-1)
    reps = PROJ_HALF // HEAD_DIM
    return jnp.tile(cos_h, (1, reps)), jnp.tile(sin_h, (1, reps))


def _tile_table():
    lo = [_FIRST_BLOCK[n] for n in TILE_NAMES]
    hi = [b + 1 for b in lo]
    cls = [_TILE_CLASS[n] for n in TILE_NAMES]
    return jnp.asarray(np.array(lo + hi + cls, np.int32))


def _tile_gains(qk_gain_l):
    def tiled(g, n):
        return jnp.tile(g, n // HEAD_DIM)
    ones = jnp.ones((PROJ_TILE,), F32)
    per_tile = {
        "lx": ones, "lg": ones, "nv": ones, "dv": ones,
        "nk": tiled(qk_gain_l[1, 1], PROJ_TILE), "dk": tiled(qk_gain_l[2, 1], PROJ_TILE),
        "skv": jnp.concatenate([tiled(qk_gain_l[0, 1], 128), jnp.ones((PROJ_TILE - 128,), F32)]),
        "sq": tiled(qk_gain_l[0, 0], PROJ_TILE) * QK_SCALE,
        "nq": tiled(qk_gain_l[1, 0], PROJ_TILE) * QK_SCALE,
        "dq": tiled(qk_gain_l[2, 0], PROJ_TILE) * QK_SCALE,
    }
    return jnp.stack([per_tile[n] for n in TILE_NAMES]).reshape(N_TILES, 1, PROJ_TILE)


def _lambda_init(layer):
    return 0.8 - 0.6 * math.exp(-0.3 * layer)


def kernel(x_prompt, x_sample, cache_swa_k, cache_swa_v, cache_na_k, cache_na_v, cache_diff_k, cache_diff_v, state_lru, c, c_ctx, norm_mix, norm_ffn, w_mod, b_mod, w_in, w_out, qk_gain, swa_sink, na_rpb, diff_lambda, diff_subln, conv_w, conv_b, lru_wa, lru_ba, lru_wx, lru_bx, lru_L, w_ffn_gate, w_ffn_up, w_ffn_down):
    bc, s_ctx, _ = x_prompt.shape
    bd, t_lat, _ = x_sample.shape
    n_ctx, n_lat = bc * s_ctx, bd * t_lat
    rows = t_lat // GRID_W
    p_ctx = cache_swa_k.shape[2]

    cvecs = jnp.concatenate([c_ctx[None, :], c, jnp.zeros((8 - 1 - bd, D_MODEL), F32)], axis=0)
    mods = _modulation(cvecs, w_mod, b_mod).reshape(DEPTH, 8 * MOD_CHUNKS, 1, D_MODEL)

    hsum = jnp.asarray(np.kron(np.eye(PROJ_HALF // HEAD_DIM, dtype=np.float32),
                               np.full((HEAD_DIM, HEAD_DIM), 1.0 / HEAD_DIM, np.float32)), BF16)
    rope_tabs = _rope_tables(t_lat)
    tab = _tile_table()

    tm = 1024
    tm_out = 512
    ctx_row = lambda tile: (lambda i: 0)
    lat_row = lambda tile: (lambda i: 1 + (i * tile) // t_lat)

    xc = x_prompt.reshape(n_ctx, D_MODEL)
    xs = x_sample.reshape(n_lat, D_MODEL)
    zero_state = jnp.zeros((bc, 2, LRU_WIDTH), F32)
    caches = [jnp.zeros((bc, DEPTH, s_ctx, _CACHE_SRC[name][2]), F32) for name in CACHE_NAMES]
    lru_states = []
    w_in_bf = w_in
    w_out_bf = w_out.astype(BF16)

    def cached(arr):
        return arr.astype(BF16).reshape(bd * DEPTH, p_ctx, -1)

    c_swa_k, c_swa_v = cached(cache_swa_k), cached(cache_swa_v)
    c_na_k, c_na_v = cached(cache_na_k), cached(cache_na_v)
    c_diff_k, c_diff_v = cached(cache_diff_k), cached(cache_diff_v)

    for l in range(DEPTH):
        lam_init = _lambda_init(l)
        mods_l = mods[l]
        gain = _tile_gains(qk_gain[l])
        w_gates, b_gates = _lru_gate_weights(lru_wa[l], lru_ba[l], lru_wx[l], lru_bx[l])
        ffn_w = (w_ffn_gate, w_ffn_up, w_ffn_down, l)

        p, f, caches = _in_projection(xc, mods_l, ctx_row(tm), norm_mix[l], w_in_bf, l, tab, gain, hsum,
                                      None, caches, tm=tm, seq=s_ctx)
        mix3 = _ctx_attention(p, swa_sink[l], diff_lambda[l], diff_subln[l], lam_init, nb=bc, t=s_ctx)
        od, st = _lru_mixer(f, conv_w[l], conv_b[l], w_gates, b_gates, lru_L[l], zero_state,
                            nb=bc, t=s_ctx, cpt=4)
        lru_states.append(st)
        xc = _out_projection([mix3, od], w_out_bf, l, xc, mods_l, ctx_row(tm_out), tm=tm_out)
        xc = _ffn(xc, mods_l, ctx_row(tm), norm_ffn[l], *ffn_w, tm=tm, th=512)

        p, f, _ = _in_projection(xs, mods_l, lat_row(tm), norm_mix[l], w_in_bf, l, tab, gain, hsum,
                                 rope_tabs, None, tm=tm, seq=t_lat)
        oa = _swa_latent(p, c_swa_k, c_swa_v, l, swa_sink[l], nb=bd, t=t_lat)
        bias = _na_bias(na_rpb[l], rows)
        ob = _na_latent(p, c_na_k, c_na_v, l, bias, nb=bd, t=t_lat)
        oc = _diff_latent(p, c_diff_k, c_diff_v, l, diff_lambda[l], diff_subln[l], lam_init, nb=bd, t=t_lat)
        od, _ = _lru_mixer(f, conv_w[l], conv_b[l], w_gates, b_gates, lru_L[l], state_lru[:, l],
                           nb=bd, t=t_lat, cpt=2)
        xs = _out_projection([oa, ob, oc, od], w_out_bf, l, xs, mods_l, lat_row(tm_out), tm=tm_out)
        xs = _ffn(xs, mods_l, lat_row(tm), norm_ffn[l], *ffn_w, tm=tm, th=512)

    swa_k, swa_v, na_k, na_v, diff_k, diff_v = caches
    return (xc.reshape(bc, s_ctx, D_MODEL), xs.reshape(bd, t_lat, D_MODEL),
            swa_k.reshape(bc, DEPTH, s_ctx, SWA_KV_HEADS, HEAD_DIM),
            swa_v.reshape(bc, DEPTH, s_ctx, SWA_KV_HEADS, HEAD_DIM),
            na_k.reshape(bc, DEPTH, s_ctx, NA_HEADS, HEAD_DIM),
            na_v.reshape(bc, DEPTH, s_ctx, NA_HEADS, HEAD_DIM),
            diff_k.reshape(bc, DEPTH, s_ctx, DIFF_HEADS, 2, HEAD_DIM),
            diff_v.reshape(bc, DEPTH, s_ctx, DIFF_HEADS, DIFF_VDIM),
            jnp.stack(lru_states, axis=1))
```

```python
import functools
import math

import jax
import jax.numpy as jnp
import numpy as np
from jax import lax
from jax.experimental import pallas as pl
from jax.experimental.pallas import tpu as pltpu

F32 = jnp.float32
BF16 = jnp.bfloat16

D_MODEL = 2048
DEPTH = 2
GRID_W = 64
HEAD_DIM = 64
GROUP_W = 512
SWA_HEADS = 8
SWA_KV_HEADS = 2
SWA_GROUP = 4
SWA_WINDOW = 128
NA_HEADS = 8
NA_KH = 8
NA_KW = 16
DIFF_HEADS = 4
DIFF_VDIM = 128
LRU_WIDTH = 512
LRU_BLOCKS = 8
LRU_BLOCK_W = 64
LRU_C = 8.0
CONV_W = 4
FFN_HIDDEN = 5632
ROPE_BASE = 10000.0
NORM_EPS = 1e-6
NEG_INF = -1e30
MOD_CHUNKS = 6
PROJ_W = 4864
LOG2E = math.log2(math.e)
QK_SCALE = HEAD_DIM ** -0.5 * LOG2E

V7X_VMEM_LIMIT = 60 * 1024 * 1024

PROJ_HALF = 256
PROJ_TILE = 2 * PROJ_HALF
TILE_NAMES = ("lx", "lg", "nk", "dk", "nv", "dv", "skv", "sq", "nq", "dq")
N_TILES = len(TILE_NAMES)
_FIRST_BLOCK = {"sq": 0, "skv": 2, "nq": 3, "nk": 5, "nv": 7, "dq": 9, "dk": 11, "dv": 13, "lx": 15, "lg": 17}
CLS_PLAIN, CLS_NORM, CLS_ROPE, CLS_MIXED = 0, 1, 2, 3
_TILE_CLASS = {"lx": CLS_PLAIN, "lg": CLS_PLAIN, "nv": CLS_PLAIN, "dv": CLS_PLAIN,
               "nk": CLS_NORM, "nq": CLS_NORM, "dk": CLS_ROPE, "sq": CLS_ROPE, "dq": CLS_ROPE,
               "skv": CLS_MIXED}
P_FIRST = 2
P_TILES = N_TILES - P_FIRST
P_COL = {name: TILE_NAMES.index(name) - P_FIRST for name in TILE_NAMES[P_FIRST:]}
F_TILES = 2

NA_QT = 128
NA_WIN_ROWS = 10
NA_WIN = NA_WIN_ROWS * GRID_W
LRU_CT = 128
ROW_CHUNK = 256
FFN_ROW_CHUNK = 512


def _sigmoid(x):
    return 1.0 / (1.0 + jnp.exp(-x))


def _dot(a, b):
    return jnp.dot(a, b, preferred_element_type=F32)


def _dot_nt(a, b):
    return lax.dot_general(a, b, (((1,), (1,)), ((), ())), preferred_element_type=F32)


def _params(*sem):
    return pltpu.CompilerParams(dimension_semantics=sem, vmem_limit_bytes=V7X_VMEM_LIMIT)


def _mod_kernel(c_ref, w_ref, b_ref, o_ref):
    cv = c_ref[...]
    s = cv * _sigmoid(cv)
    o_ref[0] = _dot(s.astype(BF16), w_ref[0].astype(BF16)) + b_ref[0]


def _modulation(cvecs, w_mod, b_mod):
    tn = 1024
    n = MOD_CHUNKS * D_MODEL
    return pl.pallas_call(
        _mod_kernel,
        out_shape=jax.ShapeDtypeStruct((DEPTH, 8, n), F32),
        grid=(DEPTH, n // tn),
        in_specs=[
            pl.BlockSpec((8, D_MODEL), lambda l, j: (0, 0)),
            pl.BlockSpec((1, D_MODEL, tn), lambda l, j: (l, 0, j)),
            pl.BlockSpec((1, 1, tn), lambda l, j: (l, 0, j)),
        ],
        out_specs=pl.BlockSpec((1, 8, tn), lambda l, j: (l, 0, j)),
        compiler_params=_params("arbitrary", "arbitrary"),
        name="modulation",
    )(cvecs, w_mod, b_mod.reshape(DEPTH, 1, n))


def _modnorm_static(x_ref, nw_ref, sh_ref, sc_ref, h_scr, r0, n_rows, chunk=128):
    for c in range(n_rows // chunk):
        rows = slice(r0 + c * chunk, r0 + (c + 1) * chunk)
        x = x_ref[rows, :]
        ms = jnp.mean(x * x, axis=-1, keepdims=True)
        y = x * lax.rsqrt(ms + NORM_EPS) * nw_ref[...]
        h_scr[rows, :] = (y * (1.0 + sc_ref[0]) + sh_ref[0]).astype(BF16)


CACHE_NAMES = ("swa_k", "swa_v", "na_k", "na_v", "diff_k", "diff_v")
_CACHE_SRC = {"swa_k": ("skv", 0, 128), "swa_v": ("skv", 128, 128), "na_k": ("nk", 0, 512),
              "na_v": ("nv", 0, 512), "diff_k": ("dk", 0, 512), "diff_v": ("dv", 0, 512)}


def _inproj_kernel(*refs, tm, rope, n_cache, seq):
    n_in = 10 if rope else 8
    tab_ref = refs[0]
    (x_ref, sh_ref, sc_ref, nw_ref, wlo_ref, whi_ref, gain_ref, hsum_ref) = refs[1:9]
    cos_ref, sin_ref = (refs[9], refs[10]) if rope else (None, None)
    outs = refs[1 + n_in + n_cache:]
    p_ref, f_ref = outs[0], outs[1]
    cache_refs = outs[2:2 + n_cache]
    h_scr = outs[2 + n_cache]
    j = pl.program_id(1)
    cls = tab_ref[2 * N_TILES + j]

    def run_tile(mode, halves, first=False):
        w = {half: (wlo_ref if half == 0 else whi_ref)[0].astype(BF16) for half in halves}
        n_chunks = tm // ROW_CHUNK
        units = [(half, c) for half in halves for c in range(n_chunks)]

        def norm_chunk(c):
            _modnorm_static(x_ref, nw_ref, sh_ref, sc_ref, h_scr, c * ROW_CHUNK, ROW_CHUNK)

        if first:
            norm_chunk(0)

        def main(unit):
            half, c = unit
            if first and half == halves[0] and c + 1 < n_chunks:
                norm_chunk(c + 1)
            return _dot(h_scr[c * ROW_CHUNK:(c + 1) * ROW_CHUNK, :], w[half])

        def finish(unit, p):
            half, c = unit
            rows = slice(c * ROW_CHUNK, (c + 1) * ROW_CHUNK)
            cols = slice(half * PROJ_HALF, (half + 1) * PROJ_HALF)
            y = p
            if mode != CLS_PLAIN:
                ms = _dot((p * p).astype(BF16), hsum_ref[...])
                y = p * lax.rsqrt(ms + NORM_EPS) * gain_ref[0, :, cols]
                if rope and mode in (CLS_ROPE, CLS_MIXED):
                    lane = lax.broadcasted_iota(jnp.int32, y.shape, 1)
                    up = pltpu.roll(y, PROJ_HALF - 16, 1)
                    down = pltpu.roll(y, 16, 1)
                    partner = jnp.where((lane & 31) < 16, up, down)
                    y = y * cos_ref[rows, :] + partner * sin_ref[rows, :]
                if mode == CLS_MIXED:
                    lane = lax.broadcasted_iota(jnp.int32, y.shape, 1)
                    y = jnp.where(lane < 2 * HEAD_DIM, y, p)
            p_ref[rows, cols] = y.astype(BF16)
            f_ref[rows, cols] = y

        prev, p_prev = units[0], main(units[0])
        for unit in units[1:]:
            p_next = main(unit)
            finish(prev, p_prev)
            prev, p_prev = unit, p_next
        finish(prev, p_prev)

    assert _TILE_CLASS[TILE_NAMES[0]] == CLS_PLAIN

    @pl.when(j == 0)
    def _():
        run_tile(CLS_PLAIN, (0, 1), first=True)

    modes = (CLS_PLAIN, CLS_NORM, CLS_ROPE) if rope else (CLS_PLAIN, CLS_NORM)
    for mode in modes:
        cond = (cls == mode)
        if mode == CLS_PLAIN:
            cond = cond & (j > 0)
        if not rope and mode == CLS_NORM:
            cond = (cls == CLS_NORM) | (cls == CLS_ROPE)

        @pl.when(cond)
        def _(mode=mode):
            run_tile(mode, (0, 1))

    @pl.when(cls == CLS_MIXED)
    def _():
        run_tile(CLS_MIXED, (0,))
        hi = slice(PROJ_HALF, PROJ_TILE)
        p_ref[:, hi] = jnp.zeros((tm, PROJ_HALF), BF16)
        f_ref[:, hi] = jnp.zeros((tm, PROJ_HALF), F32)

    for name, c_ref in zip(CACHE_NAMES, cache_refs):
        tile, off, width = _CACHE_SRC[name]

        @pl.when(j == TILE_NAMES.index(tile))
        def _(c_ref=c_ref, off=off, width=width):
            for b in range(tm // seq):
                c_ref[b, 0] = f_ref[b * seq:(b + 1) * seq, off:off + width]


def _in_projection(x, mods_l, mod_row, norm_w, w_in_bf, layer, tab, gain, hsum, rope_tabs, caches, *, tm, seq):
    n_tok = x.shape[0]
    rope = rope_tabs is not None
    caches = [] if caches is None else list(caches)
    n_cache = len(caches)

    in_specs = [
        pl.BlockSpec((tm, D_MODEL), lambda i, j, t: (i, 0)),
        pl.BlockSpec((1, 1, D_MODEL), lambda i, j, t: (mod_row(i) * MOD_CHUNKS + 0, 0, 0)),
        pl.BlockSpec((1, 1, D_MODEL), lambda i, j, t: (mod_row(i) * MOD_CHUNKS + 1, 0, 0)),
        pl.BlockSpec((1, D_MODEL), lambda i, j, t: (0, 0)),
        pl.BlockSpec((1, D_MODEL, PROJ_HALF), lambda i, j, t: (layer, 0, t[j])),
        pl.BlockSpec((1, D_MODEL, PROJ_HALF), lambda i, j, t: (layer, 0, t[N_TILES + j])),
        pl.BlockSpec((1, 1, PROJ_TILE), lambda i, j, t: (j, 0, 0)),
        pl.BlockSpec((PROJ_HALF, PROJ_HALF), lambda i, j, t: (0, 0)),
    ]
    args = [x, mods_l, mods_l, norm_w.reshape(1, D_MODEL), w_in_bf, w_in_bf, gain, hsum]
    if rope:
        cos_t, sin_t = rope_tabs
        in_specs += [pl.BlockSpec((tm, PROJ_HALF), lambda i, j, t: (0, 0)),
                     pl.BlockSpec((tm, PROJ_HALF), lambda i, j, t: (0, 0))]
        args += [cos_t, sin_t]
    n_in = len(args)
    in_specs += [pl.BlockSpec(memory_space=pl.ANY)] * n_cache
    args += caches
    out_specs = [
        pl.BlockSpec((tm, PROJ_TILE), lambda i, j, t: (i, jnp.maximum(j - P_FIRST, 0))),
        pl.BlockSpec((tm, PROJ_TILE), lambda i, j, t: (i, jnp.minimum(j, F_TILES))),
    ]
    out_shape = [jax.ShapeDtypeStruct((n_tok, P_TILES * PROJ_TILE), BF16),
                 jax.ShapeDtypeStruct((n_tok, (F_TILES + 1) * PROJ_TILE), F32)]
    for arr in caches:
        width = arr.shape[-1]
        out_specs.append(pl.BlockSpec((tm // seq, 1, seq, width), lambda i, j, t: (i, layer, 0, 0)))
        out_shape.append(jax.ShapeDtypeStruct(arr.shape, arr.dtype))
    res = pl.pallas_call(
        functools.partial(_inproj_kernel, tm=tm, rope=rope, n_cache=n_cache, seq=seq),
        out_shape=out_shape,
        grid_spec=pltpu.PrefetchScalarGridSpec(
            num_scalar_prefetch=1,
            grid=(n_tok // tm, N_TILES),
            in_specs=in_specs,
            out_specs=out_specs,
            scratch_shapes=[pltpu.VMEM((tm, D_MODEL), BF16)],
        ),
        input_output_aliases={1 + n_in + k: 2 + k for k in range(n_cache)},
        compiler_params=_params("arbitrary", "arbitrary"),
        name="in_projection_rope" if rope else "in_projection",
    )(tab, *args)
    return res[0], res[1], list(res[2:])


def _joint(scores):
    return scores[0] if len(scores) == 1 else jnp.concatenate(scores, axis=1)


PV_W = 2 * HEAD_DIM


def _weights_pv(segs, extra=None):
    s = _joint([sc for sc, _ in segs])
    m = jnp.max(s, axis=-1, keepdims=True)
    if extra is not None:
        m = jnp.maximum(m, extra)
    e = jnp.exp2(s - m).astype(BF16)
    acc, off = None, 0
    for sc, v in segs:
        n = sc.shape[1]
        v1 = jnp.concatenate([v, jnp.ones((n, PV_W), BF16)], axis=1)
        part = _dot(e[:, off:off + n], v1)
        acc = part if acc is None else acc + part
        off += n
    num, den = acc[:, :PV_W], acc[:, PV_W:]
    if extra is not None:
        den = den + jnp.exp2(extra - m)
    return num, den


def _softmax_pv(segs, extra=None):
    num, den = _weights_pv(segs, extra)
    return num / den


def _half_vecs():
    lane = lax.broadcasted_iota(jnp.int32, (1, 2 * HEAD_DIM), 1)
    lo = (lane < HEAD_DIM).astype(BF16)
    return lo, (1.0 - lo).astype(BF16)


def _pair_block(ref, m, rows=None, b=0):
    if rows is None:
        return ref[b, :, m * 128:(m + 1) * 128]
    return ref[b, rows, m * 128:(m + 1) * 128]


def _both_halves(q, vecs):
    return jnp.concatenate([q * vecs[0], q * vecs[1]], axis=0)


def _merge_pair(o_even, o_odd):
    lane = lax.broadcasted_iota(jnp.int32, o_even.shape, 1)
    return jnp.where(lane < HEAD_DIM, o_even, o_odd)


def _diff_lambda_val(dl_ref, lam_init):
    lp = dl_ref[...]
    s1 = jnp.sum(lp[0:1] * lp[1:2], axis=-1, keepdims=True)
    s2 = jnp.sum(lp[2:3] * lp[3:4], axis=-1, keepdims=True)
    return jnp.exp(s1) - jnp.exp(s2) + lam_init


def _diff_finish(o, subln_ref, lam_init):
    ms = jnp.mean(o * o, axis=-1, keepdims=True)
    return o * lax.rsqrt(ms + NORM_EPS) * subln_ref[...] * (1.0 - lam_init)


def _swap_halves_bf16(x):
    return pltpu.roll(x.astype(F32), HEAD_DIM, 1).astype(BF16)


def _gqa_queries(q_ref, hk, vecs, b=0, rows=None):
    parts = []
    for g in range(SWA_GROUP):
        h = hk * SWA_GROUP + g
        q = _pair_block(q_ref, h // 2, rows=rows, b=b)
        if h % 2 != hk:
            q = _swap_halves_bf16(q)
        parts.append(q * vecs[hk])
    return jnp.concatenate(parts, axis=0)


def _gqa_outputs(o, hk, rows):
    blocks = []
    for j in range(SWA_GROUP // 2):
        halves = []
        for g in (2 * j, 2 * j + 1):
            og = o[g * rows:(g + 1) * rows]
            halves.append(og if g % 2 == hk else pltpu.roll(og, HEAD_DIM, 1))
        blocks.append(_merge_pair(halves[0], halves[1]))
    return blocks


def _pipelined(items, depth=2):
    pending, outs = [], []
    for score_fn, finish_fn in items:
        pending.append((finish_fn, score_fn()))
        if len(pending) > depth:
            fn, s = pending.pop(0)
            outs.append(fn(s))
    for fn, s in pending:
        outs.append(fn(s))
    return outs


def _ctx_attn_kernel(sink_ref, sq_ref, skv_ref, nq_ref, nk_ref, nv_ref, dq_ref, dk_ref, dv_ref,
                     dl_ref, subln_ref, o_ref, *, t, lam_init, bb):
    lam = _diff_lambda_val(dl_ref, lam_init)
    vecs = _half_vecs()

    def swa_item(b, hk):
        def scores():
            return _dot_nt(_gqa_queries(sq_ref, hk, vecs, b=b), _pair_block(skv_ref, 0, b=b))

        def finish(s):
            sink = jnp.concatenate(
                [jnp.full((t, 1), sink_ref[hk * SWA_GROUP + g] * LOG2E, F32) for g in range(SWA_GROUP)], axis=0)
            return _gqa_outputs(_softmax_pv([(s, _pair_block(skv_ref, 1, b=b))], extra=sink), hk, t)
        return scores, finish

    def na_item(b, m):
        def scores():
            return _dot_nt(_both_halves(_pair_block(nq_ref, m, b=b), vecs), _pair_block(nk_ref, m, b=b))

        def finish(s):
            o = _softmax_pv([(s, _pair_block(nv_ref, m, b=b))])
            return [_merge_pair(o[:t], o[t:])]
        return scores, finish

    def diff_item(b, h):
        def scores():
            return _dot_nt(_both_halves(_pair_block(dq_ref, h, b=b), vecs), _pair_block(dk_ref, h, b=b))

        def finish(s):
            o = _softmax_pv([(s, _pair_block(dv_ref, h, b=b))])
            return [_diff_finish(o[:t] - lam * o[t:], subln_ref, lam_init)]
        return scores, finish

    items = []
    for make, n in ((swa_item, SWA_KV_HEADS), (na_item, NA_HEADS // 2), (diff_item, DIFF_HEADS)):
        items += [make(b, i) for i in range(n) for b in range(bb)]
    res = _pipelined(items, depth=2 * bb)
    for b in range(bb):
        pieces = [piece for r in res[b::bb] for piece in r]
        o_ref[b] = jnp.concatenate(pieces, axis=1).astype(BF16)


CTX_ROWS_PER_STEP = 4


def _pcol(name, t, bb=1):
    return pl.BlockSpec((bb, t, PROJ_TILE), lambda i, c=P_COL[name]: (i, 0, c))


def _ctx_attention(p, sink, dlam, subln, lam_init, *, nb, t):
    bb = CTX_ROWS_PER_STEP
    p3 = p.reshape(nb, t, P_TILES * PROJ_TILE)
    names = ("sq", "skv", "nq", "nk", "nv", "dq", "dk", "dv")
    out = pl.pallas_call(
        functools.partial(_ctx_attn_kernel, t=t, lam_init=lam_init, bb=bb),
        out_shape=jax.ShapeDtypeStruct((nb, t, 3 * GROUP_W), BF16),
        grid=(nb // bb,),
        in_specs=[pl.BlockSpec(memory_space=pltpu.SMEM)] + [_pcol(n, t, bb) for n in names] + [
            pl.BlockSpec((4, HEAD_DIM), lambda i: (0, 0)),
            pl.BlockSpec((1, DIFF_VDIM), lambda i: (0, 0)),
        ],
        out_specs=pl.BlockSpec((bb, t, 3 * GROUP_W), lambda i: (i, 0, 0)),
        compiler_params=_params("arbitrary"),
        name="ctx_attention",
    )(sink, *([p3] * len(names)), dlam, subln.reshape(1, DIFF_VDIM))
    return out.reshape(nb * t, 3 * GROUP_W)


Q_BLOCKS_PER_STEP = 4


def _swa_kernel(sink_ref, q_ref, kv_ref, ck_ref, cv_ref, o_ref, *, t, qb):
    span = 3 * qb
    vecs = _half_vecs()

    def item(j, hk):
        n = pl.program_id(1) * Q_BLOCKS_PER_STEP + j
        start = pl.multiple_of(jnp.clip((n - 1) * qb, 0, t - span), qb)
        win = pl.ds(start, span)
        qrows = slice(j * qb, (j + 1) * qb)

        def scores():
            qs = _gqa_queries(q_ref, hk, vecs, rows=qrows)
            return _dot_nt(qs, _pair_block(kv_ref, 0, rows=win)), _dot_nt(qs, ck_ref[0])

        def finish(s):
            row = (lax.broadcasted_iota(jnp.int32, (SWA_GROUP * qb, span), 0) & (qb - 1)) + n * qb
            col = lax.broadcasted_iota(jnp.int32, (SWA_GROUP * qb, span), 1) + start
            dist = row - col
            ok = (dist <= SWA_WINDOW) & (dist >= -SWA_WINDOW)
            sink = jnp.concatenate(
                [jnp.full((qb, 1), sink_ref[hk * SWA_GROUP + g] * LOG2E, F32) for g in range(SWA_GROUP)], axis=0)
            s_loc = jnp.where(ok, s[0], NEG_INF)
            o = _softmax_pv([(s_loc, _pair_block(kv_ref, 1, rows=win)), (s[1], cv_ref[0])], extra=sink)
            return _gqa_outputs(o, hk, qb)
        return scores, finish

    items = [item(j, hk) for hk in range(SWA_KV_HEADS) for j in range(Q_BLOCKS_PER_STEP)]
    res = _pipelined(items, depth=Q_BLOCKS_PER_STEP)
    for j in range(Q_BLOCKS_PER_STEP):
        pieces = [piece for r in res[j::Q_BLOCKS_PER_STEP] for piece in r]
        o_ref[0, j * qb:(j + 1) * qb, :] = jnp.concatenate(pieces, axis=1).astype(BF16)


def _swa_latent(p, ck, cv, layer, sink, *, nb, t):
    qb = 128
    step_rows = qb * Q_BLOCKS_PER_STEP
    p3 = p.reshape(nb, t, P_TILES * PROJ_TILE)
    out = pl.pallas_call(
        functools.partial(_swa_kernel, t=t, qb=qb),
        out_shape=jax.ShapeDtypeStruct((nb, t, GROUP_W), BF16),
        grid=(nb, t // step_rows),
        in_specs=[
            pl.BlockSpec(memory_space=pltpu.SMEM),
            pl.BlockSpec((1, step_rows, PROJ_TILE), lambda b, n: (b, n, P_COL["sq"])),
            pl.BlockSpec((1, t, PROJ_TILE), lambda b, n: (b, 0, P_COL["skv"])),
            pl.BlockSpec((1,) + ck.shape[1:], lambda b, n: (b * DEPTH + layer, 0, 0)),
            pl.BlockSpec((1,) + cv.shape[1:], lambda b, n: (b * DEPTH + layer, 0, 0)),
        ],
        out_specs=pl.BlockSpec((1, step_rows, GROUP_W), lambda b, n: (b, n, 0)),
        compiler_params=_params("arbitrary", "arbitrary"),
        name="swa_latent",
    )(sink, p3, p3, ck, cv)
    return out.reshape(nb * t, GROUP_W)


def _na_row_start(r, rows):
    kh = min(NA_KH, rows)
    return min(max(r - kh // 2, 0), rows - kh)


def _na_win_start(qt, rows):
    return min(max(_na_row_start(2 * qt, rows), 0), rows - NA_WIN_ROWS)


def _na_bias_kernel(rpb_ref, o_ref, *, rows):
    h = pl.program_id(0)
    n_dr, n_dc = 2 * NA_KH - 1, 2 * NA_KW - 1
    qi = lax.broadcasted_iota(jnp.int32, (GRID_W, GRID_W), 0)
    ki = lax.broadcasted_iota(jnp.int32, (GRID_W, GRID_W), 1)
    dc = jnp.clip(ki - qi + (NA_KW - 1), 0, n_dc - 1)
    cs = jnp.clip(qi - NA_KW // 2, 0, GRID_W - NA_KW)
    col_ok = (ki >= cs) & (ki < cs + NA_KW)
    neg = jnp.full((GRID_W, GRID_W), NEG_INF, F32)
    tabs = []
    for dr in range(n_dr):
        acc = jnp.zeros((GRID_W, GRID_W), F32)
        for c in range(n_dc):
            acc = jnp.where(dc == c, rpb_ref[(h * n_dr + dr) * n_dc + c], acc)
        tabs.append(jnp.where(col_ok, acc * LOG2E, neg))
    kh = min(NA_KH, rows)
    for qt in range(rows // 2):
        ws = _na_win_start(qt, rows)
        bands = []
        for qq in range(2):
            qr = 2 * qt + qq
            rs = _na_row_start(qr, rows)
            blks = []
            for kk in range(NA_WIN_ROWS):
                kr = ws + kk
                blks.append(tabs[kr - qr + NA_KH - 1] if rs <= kr < rs + kh else neg)
            bands.append(jnp.concatenate(blks, axis=1))
        o_ref[0, qt] = jnp.concatenate(bands, axis=0)


def _na_bias(rpb, rows):
    n_qt = rows // 2
    n_tab = rpb.shape[0] * rpb.shape[1]
    return pl.pallas_call(
        functools.partial(_na_bias_kernel, rows=rows),
        out_shape=jax.ShapeDtypeStruct((n_tab, n_qt, NA_QT, NA_WIN), F32),
        grid=(n_tab,),
        in_specs=[pl.BlockSpec(memory_space=pltpu.SMEM)],
        out_specs=pl.BlockSpec((1, n_qt, NA_QT, NA_WIN), lambda h: (h, 0, 0, 0)),
        compiler_params=_params("arbitrary"),
        name="na_bias",
    )(rpb.reshape(-1))


def _na_kernel(q_ref, k_ref, v_ref, ck_ref, cv_ref, bias_ref, o_ref, *, rows):
    vecs = _half_vecs()

    def item(j, m):
        qt = pl.program_id(1) * Q_BLOCKS_PER_STEP + j
        ws = jnp.clip(jnp.clip(2 * qt - NA_KH // 2, 0, rows - NA_KH), 0, rows - NA_WIN_ROWS)
        win = pl.ds(pl.multiple_of(ws * GRID_W, GRID_W), NA_WIN)
        qrows = slice(j * NA_QT, (j + 1) * NA_QT)

        def scores():
            q2 = _both_halves(_pair_block(q_ref, m, rows=qrows), vecs)
            return _dot_nt(q2, _pair_block(k_ref, m, rows=win)), _dot_nt(q2, _pair_block(ck_ref, m))

        def finish(s):
            v, cv = _pair_block(v_ref, m, rows=win), _pair_block(cv_ref, m)
            bias = jnp.concatenate([bias_ref[2 * m, j], bias_ref[2 * m + 1, j]], axis=0)
            o = _softmax_pv([(s[0] + bias, v), (s[1], cv)])
            return _merge_pair(o[:NA_QT], o[NA_QT:])
        return scores, finish

    items = [item(j, m) for m in range(NA_HEADS // 2) for j in range(Q_BLOCKS_PER_STEP)]
    res = _pipelined(items, depth=2 * Q_BLOCKS_PER_STEP)
    for j in range(Q_BLOCKS_PER_STEP):
        o_ref[0, j * NA_QT:(j + 1) * NA_QT, :] = jnp.concatenate(res[j::Q_BLOCKS_PER_STEP], axis=1).astype(BF16)


def _na_latent(p, ck, cv, layer, bias, *, nb, t):
    rows = t // GRID_W
    p3 = p.reshape(nb, t, P_TILES * PROJ_TILE)
    out = pl.pallas_call(
        functools.partial(_na_kernel, rows=rows),
        out_shape=jax.ShapeDtypeStruct((nb, t, GROUP_W), BF16),
        grid=(nb, t // (NA_QT * Q_BLOCKS_PER_STEP)),
        in_specs=[
            pl.BlockSpec((1, NA_QT * Q_BLOCKS_PER_STEP, PROJ_TILE), lambda i, n: (i, n, P_COL["nq"])),
            pl.BlockSpec((1, t, PROJ_TILE), lambda i, n: (i, 0, P_COL["nk"])),
            pl.BlockSpec((1, t, PROJ_TILE), lambda i, n: (i, 0, P_COL["nv"])),
            pl.BlockSpec((1,) + ck.shape[1:], lambda i, n: (i * DEPTH + layer, 0, 0)),
            pl.BlockSpec((1,) + cv.shape[1:], lambda i, n: (i * DEPTH + layer, 0, 0)),
            pl.BlockSpec((NA_HEADS, Q_BLOCKS_PER_STEP, NA_QT, NA_WIN), lambda i, n: (layer, n, 0, 0)),
        ],
        out_specs=pl.BlockSpec((1, NA_QT * Q_BLOCKS_PER_STEP, GROUP_W), lambda i, n: (i, n, 0)),
        compiler_params=_params("arbitrary", "arbitrary"),
        name="na_latent",
    )(p3, p3, p3, ck, cv, bias)
    return out.reshape(nb * t, GROUP_W)


def _diff_kernel(q_ref, k_ref, v_ref, ck_ref, cv_ref, dl_ref, subln_ref, o_ref, *, lam_init, tq):
    lam = _diff_lambda_val(dl_ref, lam_init)

    vecs = _half_vecs()

    def item(j, h):
        qrows = slice(j * tq, (j + 1) * tq)

        def scores():
            q2 = _both_halves(_pair_block(q_ref, h, rows=qrows), vecs)
            return _dot_nt(q2, _pair_block(k_ref, h)), _dot_nt(q2, _pair_block(ck_ref, h))

        def finish(s):
            o = _softmax_pv([(s[0], _pair_block(v_ref, h)), (s[1], _pair_block(cv_ref, h))])
            return _diff_finish(o[:tq] - lam * o[tq:], subln_ref, lam_init)
        return scores, finish

    items = [item(j, h) for h in range(DIFF_HEADS) for j in range(Q_BLOCKS_PER_STEP)]
    res = _pipelined(items, depth=Q_BLOCKS_PER_STEP)
    for j in range(Q_BLOCKS_PER_STEP):
        o_ref[0, j * tq:(j + 1) * tq, :] = jnp.concatenate(res[j::Q_BLOCKS_PER_STEP], axis=1).astype(BF16)


def _diff_latent(p, ck, cv, layer, dlam, subln, lam_init, *, nb, t):
    tq_sub = 256
    tq = tq_sub * Q_BLOCKS_PER_STEP
    p3 = p.reshape(nb, t, P_TILES * PROJ_TILE)
    out = pl.pallas_call(
        functools.partial(_diff_kernel, lam_init=lam_init, tq=tq_sub),
        out_shape=jax.ShapeDtypeStruct((nb, t, GROUP_W), BF16),
        grid=(nb, t // tq),
        in_specs=[
            pl.BlockSpec((1, tq, PROJ_TILE), lambda i, n: (i, n, P_COL["dq"])),
            pl.BlockSpec((1, t, PROJ_TILE), lambda i, n: (i, 0, P_COL["dk"])),
            pl.BlockSpec((1, t, PROJ_TILE), lambda i, n: (i, 0, P_COL["dv"])),
            pl.BlockSpec((1,) + ck.shape[1:], lambda i, n: (i * DEPTH + layer, 0, 0)),
            pl.BlockSpec((1,) + cv.shape[1:], lambda i, n: (i * DEPTH + layer, 0, 0)),
            pl.BlockSpec((4, HEAD_DIM), lambda i, n: (0, 0)),
            pl.BlockSpec((1, DIFF_VDIM), lambda i, n: (0, 0)),
        ],
        out_specs=pl.BlockSpec((1, tq, GROUP_W), lambda i, n: (i, n, 0)),
        compiler_params=_params("arbitrary", "arbitrary"),
        name="diff_latent",
    )(p3, p3, p3, ck, cv, dlam, subln.reshape(1, DIFF_VDIM))
    return out.reshape(nb * t, GROUP_W)


SUBLANES = 8


def _linear_scan(a, b, row, t, reverse, sa_ref, sb_ref, sc_ref):
    nt = t // SUBLANES
    sub = row & (SUBLANES - 1)

    def doubling(a, b, idx, n, steps):
        for s in steps:
            if reverse:
                ok = idx < n - s
                a_sh = jnp.where(ok, pltpu.roll(a, a.shape[0] - s, 0), 1.0)
                b_sh = jnp.where(ok, pltpu.roll(b, b.shape[0] - s, 0), 0.0)
            else:
                ok = idx >= s
                a_sh = jnp.where(ok, pltpu.roll(a, s, 0), 1.0)
                b_sh = jnp.where(ok, pltpu.roll(b, s, 0), 0.0)
            b = a * b_sh + b
            a = a * a_sh
        return a, b

    tiles = (nt, SUBLANES, LRU_CT)
    sub3 = lax.broadcasted_iota(jnp.int32, tiles, 1)

    def tile_doubling(a, b):
        for s in (1, 2, 4):
            ok = (sub3 < SUBLANES - s) if reverse else (sub3 >= s)
            sh = SUBLANES - s if reverse else s
            a_sh = jnp.where(ok, pltpu.roll(a, sh, 1), 1.0)
            b_sh = jnp.where(ok, pltpu.roll(b, sh, 1), 0.0)
            b = a * b_sh + b
            a = a * a_sh
        return a, b

    a, b = tile_doubling(a.reshape(tiles), b.reshape(tiles))
    a, b = a.reshape(t, LRU_CT), b.reshape(t, LRU_CT)
    sa_ref[...] = a
    sb_ref[...] = b
    last = 0 if reverse else SUBLANES - 1
    at = sa_ref[pl.ds(last, nt, stride=SUBLANES), :]
    bt = sb_ref[pl.ds(last, nt, stride=SUBLANES), :]
    rowt = lax.broadcasted_iota(jnp.int32, (nt, LRU_CT), 0)
    steps, s = [], 1
    while s < nt:
        steps.append(s)
        s *= 2
    _, bt = doubling(at, bt, rowt, nt, steps)
    if reverse:
        carry = jnp.where(rowt < nt - 1, pltpu.roll(bt, nt - 1, 0), 0.0)
    else:
        carry = jnp.where(rowt >= 1, pltpu.roll(bt, 1, 0), 0.0)
    for r in range(SUBLANES):
        sc_ref[pl.ds(r, nt, stride=SUBLANES), :] = carry
    return b + a * sc_ref[...]


def _lru_kernel(x_ref, g_ref, cw_ref, cb_ref, w_ref, gb_ref, lam_ref, h0_ref, y_ref, st_ref, *scratch, t, cpt,
                has_h0):
    row = lax.broadcasted_iota(jnp.int32, (t, LRU_CT), 0)
    for ci in range(cpt):
        lanes = slice(ci * LRU_CT, (ci + 1) * LRU_CT)
        x = x_ref[0, :, lanes]
        u = cb_ref[:, lanes] + cw_ref[2:3, lanes] * x
        for tap, off in ((0, -2), (1, -1), (3, 1)):
            xs = pltpu.roll(x, (-off) % t, 0)
            ok = (row + off >= 0) & (row + off < t)
            u = u + cw_ref[tap:tap + 1, lanes] * jnp.where(ok, xs, 0.0)
        gates = _dot(u.astype(BF16), w_ref[ci]) + gb_ref[ci]
        total = None
        for d in range(2):
            r = 1.0 / (1.0 + jnp.exp2(gates[:, (2 * d) * LRU_CT:(2 * d + 1) * LRU_CT]))
            ig = 1.0 / (1.0 + jnp.exp2(gates[:, (2 * d + 1) * LRU_CT:(2 * d + 2) * LRU_CT]))
            nl = -lam_ref[d:d + 1, lanes]
            rate = LRU_C * (jnp.maximum(nl, 0.0) + jnp.log1p(jnp.exp(-jnp.abs(nl))))
            a = jnp.exp2(r * (rate * -LOG2E))
            bx = jnp.sqrt(jnp.tanh(r * rate) * (a * a + 1.0)) * (ig * u)
            if has_h0:
                edge = t - 1 if d == 1 else 0
                bx = jnp.where(row == edge, bx + a * h0_ref[0, d:d + 1, lanes], bx)
            sa_ref, sb_ref, sc_ref = scratch[3 * (2 * ci + d):3 * (2 * ci + d) + 3]
            bx = _linear_scan(a, bx, row, t, d == 1, sa_ref, sb_ref, sc_ref)
            fin = t - 1 if d == 0 else 0
            st_ref[0, d:d + 1, lanes] = bx[fin:fin + 1, :]
            total = bx if total is None else total + bx
        g = g_ref[0, :, lanes]
        gelu = 0.5 * g * (1.0 + jnp.tanh(math.sqrt(2.0 / math.pi) * (g + 0.044715 * (g * g * g))))
        y_ref[0, :, lanes] = (total * gelu).astype(BF16)


def _lru_mixer(f, conv_w, conv_b, w_gates, b_gates, lam, h0, *, nb, t, cpt):
    has_h0 = h0 is not None
    if not has_h0:
        h0 = jnp.zeros((nb, 2, LRU_WIDTH), F32)
    f3 = f.reshape(nb, t, f.shape[1])
    nct = LRU_WIDTH // LRU_CT
    wide = cpt * LRU_CT
    y, st = pl.pallas_call(
        functools.partial(_lru_kernel, t=t, cpt=cpt, has_h0=has_h0),
        out_shape=[jax.ShapeDtypeStruct((nb, t, LRU_WIDTH), BF16),
                   jax.ShapeDtypeStruct((nb, 2, LRU_WIDTH), F32)],
        grid=(nb, nct // cpt),
        in_specs=[
            pl.BlockSpec((1, t, wide), lambda i, c: (i, 0, c)),
            pl.BlockSpec((1, t, wide), lambda i, c: (i, 0, nct // cpt + c)),
            pl.BlockSpec((CONV_W, wide), lambda i, c: (0, c)),
            pl.BlockSpec((1, wide), lambda i, c: (0, c)),
            pl.BlockSpec((cpt, LRU_CT, 4 * LRU_CT), lambda i, c: (c, 0, 0)),
            pl.BlockSpec((cpt, 1, 4 * LRU_CT), lambda i, c: (c, 0, 0)),
            pl.BlockSpec((2, wide), lambda i, c: (0, c)),
            pl.BlockSpec((1, 2, wide), lambda i, c: (i, 0, c)),
        ],
        out_specs=[pl.BlockSpec((1, t, wide), lambda i, c: (i, 0, c)),
                   pl.BlockSpec((1, 2, wide), lambda i, c: (i, 0, c))],
        scratch_shapes=[pltpu.VMEM((t, LRU_CT), F32)] * (6 * cpt),
        compiler_params=_params("arbitrary", "arbitrary"),
        name="lru_mixer",
    )(f3, f3, conv_w, conv_b.reshape(1, LRU_WIDTH), w_gates, b_gates, lam, h0)
    return y.reshape(nb * t, LRU_WIDTH), st


def _lru_gate_weights(wa, ba, wx, bx):
    nct = LRU_WIDTH // LRU_CT
    bpt = LRU_CT // LRU_BLOCK_W
    eye = jnp.eye(bpt, dtype=F32)

    def tile_w(w):
        w4 = w.reshape(nct, bpt, LRU_BLOCK_W, LRU_BLOCK_W)
        return jnp.einsum("cnij,nm->cnimj", w4, eye).reshape(nct, LRU_CT, LRU_CT)

    w = jnp.concatenate([tile_w(wa[0]), tile_w(wx[0]), tile_w(wa[1]), tile_w(wx[1])], axis=-1)
    b = jnp.concatenate([v.reshape(nct, 1, LRU_CT) for v in (ba[0], bx[0], ba[1], bx[1])], axis=-1)
    return (w * -LOG2E).astype(BF16), b * -LOG2E


def _outproj_kernel(*refs, n_in):
    ins = refs[:n_in]
    w_ref, x_ref, g_ref, o_ref = refs[n_in:]
    acc = None
    off = 0
    for m_ref in ins:
        width = m_ref.shape[1]
        part = _dot(m_ref[...], w_ref[0, off:off + width, :])
        acc = part if acc is None else acc + part
        off += width
    o_ref[...] = x_ref[...] + g_ref[0] * acc


def _out_projection(mixes, w_out_bf, layer, x, mods_l, mod_row, *, tm):
    n_tok = x.shape[0]
    return pl.pallas_call(
        functools.partial(_outproj_kernel, n_in=len(mixes)),
        out_shape=jax.ShapeDtypeStruct((n_tok, D_MODEL), F32),
        grid=(n_tok // tm,),
        in_specs=[pl.BlockSpec((tm, m.shape[1]), lambda i: (i, 0)) for m in mixes] + [
            pl.BlockSpec((1, D_MODEL, D_MODEL), lambda i: (layer, 0, 0)),
            pl.BlockSpec((tm, D_MODEL), lambda i: (i, 0)),
            pl.BlockSpec((1, 1, D_MODEL), lambda i: (mod_row(i) * MOD_CHUNKS + 2, 0, 0)),
        ],
        out_specs=pl.BlockSpec((tm, D_MODEL), lambda i: (i, 0)),
        compiler_params=_params("arbitrary"),
        name="out_projection",
    )(*mixes, w_out_bf, x, mods_l)


def _ffn_kernel(x_ref, sh_ref, sc_ref, g_ref, nw_ref, wg_ref, wu_ref, wd_ref, o_ref, h_scr, *, tm):
    k = pl.program_id(1)

    def step(first):
        wg = wg_ref[0].astype(BF16)
        wu = wu_ref[0].astype(BF16)
        wd = wd_ref[0].astype(BF16)
        chunk = ROW_CHUNK if first else FFN_ROW_CHUNK

        def norm_chunk(c):
            _modnorm_static(x_ref, nw_ref, sh_ref, sc_ref, h_scr, c * chunk, chunk)

        n_chunks = tm // chunk

        def gate_up(c):
            if first and c + 1 < n_chunks:
                norm_chunk(c + 1)
            h = h_scr[c * chunk:(c + 1) * chunk, :]
            return _dot(h, wg), _dot(h, wu)

        def down(c, gu):
            g, u = gu
            act = (g * _sigmoid(g)) * u
            res = _dot(act.astype(BF16), wd)
            rows = slice(c * chunk, (c + 1) * chunk)
            if first:
                o_ref[rows, :] = res
            else:
                o_ref[rows, :] += res

        if first:
            norm_chunk(0)
        pending = gate_up(0)
        for c in range(1, n_chunks):
            nxt = gate_up(c)
            down(c - 1, pending)
            pending = nxt
        down(n_chunks - 1, pending)

    @pl.when(k == 0)
    def _():
        step(True)

    @pl.when(k > 0)
    def _():
        step(False)

    @pl.when(k == pl.num_programs(1) - 1)
    def _():
        def body(c, carry):
            r = pl.multiple_of(c * 128, 128)
            o_ref[pl.ds(r, 128), :] = x_ref[pl.ds(r, 128), :] + g_ref[0] * o_ref[pl.ds(r, 128), :]
            return carry
        lax.fori_loop(0, tm // 128, body, 0)


def _ffn(x, mods_l, mod_row, norm_w, wg, wu, wd, layer, *, tm, th):
    n_tok = x.shape[0]
    return pl.pallas_call(
        functools.partial(_ffn_kernel, tm=tm),
        out_shape=jax.ShapeDtypeStruct((n_tok, D_MODEL), F32),
        grid=(n_tok // tm, FFN_HIDDEN // th),
        in_specs=[
            pl.BlockSpec((tm, D_MODEL), lambda i, k: (i, 0), pipeline_mode=pl.Buffered(1)),
            pl.BlockSpec((1, 1, D_MODEL), lambda i, k: (mod_row(i) * MOD_CHUNKS + 3, 0, 0)),
            pl.BlockSpec((1, 1, D_MODEL), lambda i, k: (mod_row(i) * MOD_CHUNKS + 4, 0, 0)),
            pl.BlockSpec((1, 1, D_MODEL), lambda i, k: (mod_row(i) * MOD_CHUNKS + 5, 0, 0)),
            pl.BlockSpec((1, D_MODEL), lambda i, k: (0, 0)),
            pl.BlockSpec((1, D_MODEL, th), lambda i, k: (layer, 0, k)),
            pl.BlockSpec((1, D_MODEL, th), lambda i, k: (layer, 0, k)),
            pl.BlockSpec((1, th, D_MODEL), lambda i, k: (layer, k, 0)),
        ],
        out_specs=pl.BlockSpec((tm, D_MODEL), lambda i, k: (i, 0)),
        scratch_shapes=[pltpu.VMEM((tm, D_MODEL), BF16)],
        compiler_params=_params("arbitrary", "arbitrary"),
        name="ffn",
    )(x, mods_l, mods_l, mods_l, norm_w.reshape(1, D_MODEL), wg, wu, wd)


def _rope_tables(t_lat):
    nf = HEAD_DIM // 4
    tok = np.arange(t_lat)
    pos = np.stack([tok // GRID_W, tok % GRID_W], axis=-1).astype(np.float32)
    inv = np.float32(ROPE_BASE) ** (-np.arange(nf, dtype=np.float32) / np.float32(nf))
    ang = (pos[:, :, None] * inv.astype(np.float32)).astype(np.float32)
    cos = np.cos(ang.astype(np.float64)).astype(np.float32)
    sin = np.sin(ang.astype(np.float64)).astype(np.float32)
    cos_h = np.concatenate([cos[:, 0], cos[:, 0], cos[:, 1], cos[:, 1]], axis=-1)
    sin_h = np.concatenate([-sin[:, 0], sin[:, 0], -sin[:, 1], sin[:, 1]], axis=-1)
    reps = PROJ_HALF // HEAD_DIM
    return jnp.asarray(np.tile(cos_h, (1, reps))), jnp.asarray(np.tile(sin_h, (1, reps)))


def _tile_table():
    lo = [_FIRST_BLOCK[n] for n in TILE_NAMES]
    hi = [b + 1 for b in lo]
    cls = [_TILE_CLASS[n] for n in TILE_NAMES]
    return jnp.asarray(np.array(lo + hi + cls, np.int32))


def _tile_gains(qk_gain_l):
    def tiled(g, n):
        return jnp.tile(g, n // HEAD_DIM)
    ones = jnp.ones((PROJ_TILE,), F32)
    per_tile = {
        "lx": ones, "lg": ones, "nv": ones, "dv": ones,
        "nk": tiled(qk_gain_l[1, 1], PROJ_TILE), "dk": tiled(qk_gain_l[2, 1], PROJ_TILE),
        "skv": jnp.concatenate([tiled(qk_gain_l[0, 1], 128), jnp.ones((PROJ_TILE - 128,), F32)]),
        "sq": tiled(qk_gain_l[0, 0], PROJ_TILE) * QK_SCALE,
        "nq": tiled(qk_gain_l[1, 0], PROJ_TILE) * QK_SCALE,
        "dq": tiled(qk_gain_l[2, 0], PROJ_TILE) * QK_SCALE,
    }
    return jnp.stack([per_tile[n] for n in TILE_NAMES]).reshape(N_TILES, 1, PROJ_TILE)


def _lambda_init(layer):
    return 0.8 - 0.6 * math.exp(-0.3 * layer)


def kernel(x_prompt, x_sample, cache_swa_k, cache_swa_v, cache_na_k, cache_na_v, cache_diff_k, cache_diff_v, state_lru, c, c_ctx, norm_mix, norm_ffn, w_mod, b_mod, w_in, w_out, qk_gain, swa_sink, na_rpb, diff_lambda, diff_subln, conv_w, conv_b, lru_wa, lru_ba, lru_wx, lru_bx, lru_L, w_ffn_gate, w_ffn_up, w_ffn_down):
    bc, s_ctx, _ = x_prompt.shape
    bd, t_lat, _ = x_sample.shape
    n_ctx, n_lat = bc * s_ctx, bd * t_lat
    rows = t_lat // GRID_W
    p_ctx = cache_swa_k.shape[2]

    cvecs = jnp.concatenate([c_ctx[None, :], c, jnp.zeros((8 - 1 - bd, D_MODEL), F32)], axis=0)
    mods = _modulation(cvecs, w_mod, b_mod).reshape(DEPTH, 8 * MOD_CHUNKS, 1, D_MODEL)

    hsum = jnp.asarray(np.kron(np.eye(PROJ_HALF // HEAD_DIM, dtype=np.float32),
                               np.full((HEAD_DIM, HEAD_DIM), 1.0 / HEAD_DIM, np.float32)), BF16)
    rope_tabs = _rope_tables(t_lat)
    tab = _tile_table()

    tm = 1024
    tm_out = 512
    ctx_row = lambda tile: (lambda i: 0)
    lat_row = lambda tile: (lambda i: 1 + (i * tile) // t_lat)

    xc = x_prompt.reshape(n_ctx, D_MODEL)
    xs = x_sample.reshape(n_lat, D_MODEL)
    caches = [jnp.zeros((bc, DEPTH, s_ctx, _CACHE_SRC[name][2]), F32) for name in CACHE_NAMES]
    lru_states = []
    w_in_bf = w_in
    w_out_bf = w_out.astype(BF16)

    def cached(arr):
        return arr.astype(BF16).reshape(bd * DEPTH, p_ctx, -1)

    c_swa_k, c_swa_v = cached(cache_swa_k), cached(cache_swa_v)
    c_na_k, c_na_v = cached(cache_na_k), cached(cache_na_v)
    c_diff_k, c_diff_v = cached(cache_diff_k), cached(cache_diff_v)
    na_bias = _na_bias(na_rpb, rows)

    for l in range(DEPTH):
        lam_init = _lambda_init(l)
        mods_l = mods[l]
        gain = _tile_gains(qk_gain[l])
        w_gates, b_gates = _lru_gate_weights(lru_wa[l], lru_ba[l], lru_wx[l], lru_bx[l])
        ffn_w = (w_ffn_gate, w_ffn_up, w_ffn_down, l)

        p, f, caches = _in_projection(xc, mods_l, ctx_row(tm), norm_mix[l], w_in_bf, l, tab, gain, hsum,
                                      None, caches, tm=tm, seq=s_ctx)
        mix3 = _ctx_attention(p, swa_sink[l], diff_lambda[l], diff_subln[l], lam_init, nb=bc, t=s_ctx)
        od, st = _lru_mixer(f, conv_w[l], conv_b[l], w_gates, b_gates, lru_L[l], None,
                            nb=bc, t=s_ctx, cpt=4)
        lru_states.append(st)
        xc = _out_projection([mix3, od], w_out_bf, l, xc, mods_l, ctx_row(tm_out), tm=tm_out)
        xc = _ffn(xc, mods_l, ctx_row(tm), norm_ffn[l], *ffn_w, tm=tm, th=512)

        p, f, _ = _in_projection(xs, mods_l, lat_row(tm), norm_mix[l], w_in_bf, l, tab, gain, hsum,
                                 rope_tabs, None, tm=tm, seq=t_lat)
        oa = _swa_latent(p, c_swa_k, c_swa_v, l, swa_sink[l], nb=bd, t=t_lat)
        ob = _na_latent(p, c_na_k, c_na_v, l, na_bias, nb=bd, t=t_lat)
        oc = _diff_latent(p, c_diff_k, c_diff_v, l, diff_lambda[l], diff_subln[l], lam_init, nb=bd, t=t_lat)
        od, _ = _lru_mixer(f, conv_w[l], conv_b[l], w_gates, b_gates, lru_L[l], state_lru[:, l],
                           nb=bd, t=t_lat, cpt=2)
        xs = _out_projection([oa, ob, oc, od], w_out_bf, l, xs, mods_l, lat_row(tm_out), tm=tm_out)
        xs = _ffn(xs, mods_l, lat_row(tm), norm_ffn[l], *ffn_w, tm=tm, th=512)

    swa_k, swa_v, na_k, na_v, diff_k, diff_v = caches
    return (xc.reshape(bc, s_ctx, D_MODEL), xs.reshape(bd, t_lat, D_MODEL),
            swa_k.reshape(bc, DEPTH, s_ctx, SWA_KV_HEADS, HEAD_DIM),
            swa_v.reshape(bc, DEPTH, s_ctx, SWA_KV_HEADS, HEAD_DIM),
            na_k.reshape(bc, DEPTH, s_ctx, NA_HEADS, HEAD_DIM),
            na_v.reshape(bc, DEPTH, s_ctx, NA_HEADS, HEAD_DIM),
            diff_k.reshape(bc, DEPTH, s_ctx, DIFF_HEADS, 2, HEAD_DIM),
            diff_v.reshape(bc, DEPTH, s_ctx, DIFF_HEADS, DIFF_VDIM),
            jnp.stack(lru_states, axis=1))
```

```python
import functools
import math

import jax
import jax.numpy as jnp
import numpy as np
from jax import lax
from jax.experimental import pallas as pl
from jax.experimental.pallas import tpu as pltpu

F32 = jnp.float32
BF16 = jnp.bfloat16

D_MODEL = 2048
DEPTH = 2
GRID_W = 64
HEAD_DIM = 64
GROUP_W = 512
SWA_HEADS = 8
SWA_KV_HEADS = 2
SWA_GROUP = 4
SWA_WINDOW = 128
NA_HEADS = 8
NA_KH = 8
NA_KW = 16
DIFF_HEADS = 4
DIFF_VDIM = 128
LRU_WIDTH = 512
LRU_BLOCKS = 8
LRU_BLOCK_W = 64
LRU_C = 8.0
CONV_W = 4
FFN_HIDDEN = 5632
ROPE_BASE = 10000.0
NORM_EPS = 1e-6
NEG_INF = -1e30
MOD_CHUNKS = 6
PROJ_W = 4864
LOG2E = math.log2(math.e)
QK_SCALE = HEAD_DIM ** -0.5 * LOG2E

V7X_VMEM_LIMIT = 60 * 1024 * 1024

PROJ_HALF = 256
PROJ_TILE = 2 * PROJ_HALF
TILE_NAMES = ("lx", "lg", "nk", "dk", "nv", "dv", "skv", "sq", "nq", "dq")
N_TILES = len(TILE_NAMES)
_FIRST_BLOCK = {"sq": 0, "skv": 2, "nq": 3, "nk": 5, "nv": 7, "dq": 9, "dk": 11, "dv": 13, "lx": 15, "lg": 17}
CLS_PLAIN, CLS_NORM, CLS_ROPE, CLS_MIXED = 0, 1, 2, 3
_TILE_CLASS = {"lx": CLS_PLAIN, "lg": CLS_PLAIN, "nv": CLS_PLAIN, "dv": CLS_PLAIN,
               "nk": CLS_NORM, "nq": CLS_NORM, "dk": CLS_ROPE, "sq": CLS_ROPE, "dq": CLS_ROPE,
               "skv": CLS_MIXED}
P_FIRST = 2
P_TILES = N_TILES - P_FIRST
P_COL = {name: TILE_NAMES.index(name) - P_FIRST for name in TILE_NAMES[P_FIRST:]}
F_TILES = 2

NA_QT = 128
NA_WIN_ROWS = 10
NA_WIN = NA_WIN_ROWS * GRID_W
LRU_CT = 128
ROW_CHUNK = 256
FFN_ROW_CHUNK = 512


def _sigmoid(x):
    return 1.0 / (1.0 + jnp.exp(-x))


def _dot(a, b):
    return jnp.dot(a, b, preferred_element_type=F32)


def _dot_nt(a, b):
    return lax.dot_general(a, b, (((1,), (1,)), ((), ())), preferred_element_type=F32)


def _params(*sem):
    return pltpu.CompilerParams(dimension_semantics=sem, vmem_limit_bytes=V7X_VMEM_LIMIT)


def _mod_kernel(c_ref, w_ref, b_ref, o_ref):
    cv = c_ref[...]
    s = cv * _sigmoid(cv)
    o_ref[0] = _dot(s.astype(BF16), w_ref[0].astype(BF16)) + b_ref[0]


def _modulation(cvecs, w_mod, b_mod):
    tn = 1024
    n = MOD_CHUNKS * D_MODEL
    return pl.pallas_call(
        _mod_kernel,
        out_shape=jax.ShapeDtypeStruct((DEPTH, 8, n), F32),
        grid=(DEPTH, n // tn),
        in_specs=[
            pl.BlockSpec((8, D_MODEL), lambda l, j: (0, 0)),
            pl.BlockSpec((1, D_MODEL, tn), lambda l, j: (l, 0, j)),
            pl.BlockSpec((1, 1, tn), lambda l, j: (l, 0, j)),
        ],
        out_specs=pl.BlockSpec((1, 8, tn), lambda l, j: (l, 0, j)),
        compiler_params=_params("arbitrary", "arbitrary"),
        name="modulation",
    )(cvecs, w_mod, b_mod.reshape(DEPTH, 1, n))


def _modnorm_static(x_ref, nw_ref, sh_ref, sc_ref, h_scr, r0, n_rows, chunk=128):
    for c in range(n_rows // chunk):
        rows = slice(r0 + c * chunk, r0 + (c + 1) * chunk)
        x = x_ref[rows, :]
        ms = jnp.mean(x * x, axis=-1, keepdims=True)
        y = x * lax.rsqrt(ms + NORM_EPS) * nw_ref[...]
        h_scr[rows, :] = (y * (1.0 + sc_ref[0]) + sh_ref[0]).astype(BF16)


CACHE_NAMES = ("swa_k", "swa_v", "na_k", "na_v", "diff_k", "diff_v")
_CACHE_SRC = {"swa_k": ("skv", 0, 128), "swa_v": ("skv", 128, 128), "na_k": ("nk", 0, 512),
              "na_v": ("nv", 0, 512), "diff_k": ("dk", 0, 512), "diff_v": ("dv", 0, 512)}


def _inproj_kernel(*refs, tm, rope, n_cache, seq):
    n_in = 10 if rope else 8
    tab_ref = refs[0]
    (x_ref, sh_ref, sc_ref, nw_ref, wlo_ref, whi_ref, gain_ref, hsum_ref) = refs[1:9]
    cos_ref, sin_ref = (refs[9], refs[10]) if rope else (None, None)
    outs = refs[1 + n_in + n_cache:]
    p_ref, f_ref = outs[0], outs[1]
    cache_refs = outs[2:2 + n_cache]
    h_scr = outs[2 + n_cache]
    j = pl.program_id(1)
    cls = tab_ref[2 * N_TILES + j]

    def run_tile(mode, halves, first=False):
        w = {half: (wlo_ref if half == 0 else whi_ref)[0].astype(BF16) for half in halves}
        n_chunks = tm // ROW_CHUNK
        units = [(half, c) for half in halves for c in range(n_chunks)]

        def norm_chunk(c):
            _modnorm_static(x_ref, nw_ref, sh_ref, sc_ref, h_scr, c * ROW_CHUNK, ROW_CHUNK)

        if first:
            norm_chunk(0)

        def main(unit):
            half, c = unit
            if first and half == halves[0] and c + 1 < n_chunks:
                norm_chunk(c + 1)
            return _dot(h_scr[c * ROW_CHUNK:(c + 1) * ROW_CHUNK, :], w[half])

        def finish(unit, p):
            half, c = unit
            rows = slice(c * ROW_CHUNK, (c + 1) * ROW_CHUNK)
            cols = slice(half * PROJ_HALF, (half + 1) * PROJ_HALF)
            y = p
            if mode != CLS_PLAIN:
                ms = _dot((p * p).astype(BF16), hsum_ref[...])
                y = p * lax.rsqrt(ms + NORM_EPS) * gain_ref[0, :, cols]
                if rope and mode in (CLS_ROPE, CLS_MIXED):
                    lane = lax.broadcasted_iota(jnp.int32, y.shape, 1)
                    up = pltpu.roll(y, PROJ_HALF - 16, 1)
                    down = pltpu.roll(y, 16, 1)
                    partner = jnp.where((lane & 31) < 16, up, down)
                    y = y * cos_ref[rows, :] + partner * sin_ref[rows, :]
                if mode == CLS_MIXED:
                    lane = lax.broadcasted_iota(jnp.int32, y.shape, 1)
                    y = jnp.where(lane < 2 * HEAD_DIM, y, p)
            p_ref[rows, cols] = y.astype(BF16)
            f_ref[rows, cols] = y

        prev, p_prev = units[0], main(units[0])
        for unit in units[1:]:
            p_next = main(unit)
            finish(prev, p_prev)
            prev, p_prev = unit, p_next
        finish(prev, p_prev)

    assert _TILE_CLASS[TILE_NAMES[0]] == CLS_PLAIN

    @pl.when(j == 0)
    def _():
        run_tile(CLS_PLAIN, (0, 1), first=True)

    modes = (CLS_PLAIN, CLS_NORM, CLS_ROPE) if rope else (CLS_PLAIN, CLS_NORM)
    for mode in modes:
        cond = (cls == mode)
        if mode == CLS_PLAIN:
            cond = cond & (j > 0)
        if not rope and mode == CLS_NORM:
            cond = (cls == CLS_NORM) | (cls == CLS_ROPE)

        @pl.when(cond)
        def _(mode=mode):
            run_tile(mode, (0, 1))

    @pl.when(cls == CLS_MIXED)
    def _():
        run_tile(CLS_MIXED, (0,))
        hi = slice(PROJ_HALF, PROJ_TILE)
        p_ref[:, hi] = jnp.zeros((tm, PROJ_HALF), BF16)
        f_ref[:, hi] = jnp.zeros((tm, PROJ_HALF), F32)

    for name, c_ref in zip(CACHE_NAMES, cache_refs):
        tile, off, width = _CACHE_SRC[name]

        @pl.when(j == TILE_NAMES.index(tile))
        def _(c_ref=c_ref, off=off, width=width):
            for b in range(tm // seq):
                c_ref[b, 0] = f_ref[b * seq:(b + 1) * seq, off:off + width]


def _in_projection(x, mods_l, mod_row, norm_w, w_in_bf, layer, tab, gain, hsum, rope_tabs, caches, *, tm, seq):
    n_tok = x.shape[0]
    rope = rope_tabs is not None
    caches = [] if caches is None else list(caches)
    n_cache = len(caches)

    in_specs = [
        pl.BlockSpec((tm, D_MODEL), lambda i, j, t: (i, 0)),
        pl.BlockSpec((1, 1, D_MODEL), lambda i, j, t: (mod_row(i) * MOD_CHUNKS + 0, 0, 0)),
        pl.BlockSpec((1, 1, D_MODEL), lambda i, j, t: (mod_row(i) * MOD_CHUNKS + 1, 0, 0)),
        pl.BlockSpec((1, D_MODEL), lambda i, j, t: (0, 0)),
        pl.BlockSpec((1, D_MODEL, PROJ_HALF), lambda i, j, t: (layer, 0, t[j])),
        pl.BlockSpec((1, D_MODEL, PROJ_HALF), lambda i, j, t: (layer, 0, t[N_TILES + j])),
        pl.BlockSpec((1, 1, PROJ_TILE), lambda i, j, t: (j, 0, 0)),
        pl.BlockSpec((PROJ_HALF, PROJ_HALF), lambda i, j, t: (0, 0)),
    ]
    args = [x, mods_l, mods_l, norm_w.reshape(1, D_MODEL), w_in_bf, w_in_bf, gain, hsum]
    if rope:
        cos_t, sin_t = rope_tabs
        in_specs += [pl.BlockSpec((tm, PROJ_HALF), lambda i, j, t: (0, 0)),
                     pl.BlockSpec((tm, PROJ_HALF), lambda i, j, t: (0, 0))]
        args += [cos_t, sin_t]
    n_in = len(args)
    in_specs += [pl.BlockSpec(memory_space=pl.ANY)] * n_cache
    args += caches
    out_specs = [
        pl.BlockSpec((tm, PROJ_TILE), lambda i, j, t: (i, jnp.maximum(j - P_FIRST, 0))),
        pl.BlockSpec((tm, PROJ_TILE), lambda i, j, t: (i, jnp.minimum(j, F_TILES))),
    ]
    out_shape = [jax.ShapeDtypeStruct((n_tok, P_TILES * PROJ_TILE), BF16),
                 jax.ShapeDtypeStruct((n_tok, (F_TILES + 1) * PROJ_TILE), F32)]
    for arr in caches:
        width = arr.shape[-1]
        out_specs.append(pl.BlockSpec((tm // seq, 1, seq, width), lambda i, j, t: (i, layer, 0, 0)))
        out_shape.append(jax.ShapeDtypeStruct(arr.shape, arr.dtype))
    res = pl.pallas_call(
        functools.partial(_inproj_kernel, tm=tm, rope=rope, n_cache=n_cache, seq=seq),
        out_shape=out_shape,
        grid_spec=pltpu.PrefetchScalarGridSpec(
            num_scalar_prefetch=1,
            grid=(n_tok // tm, N_TILES),
            in_specs=in_specs,
            out_specs=out_specs,
            scratch_shapes=[pltpu.VMEM((tm, D_MODEL), BF16)],
        ),
        input_output_aliases={1 + n_in + k: 2 + k for k in range(n_cache)},
        compiler_params=_params("arbitrary", "arbitrary"),
        name="in_projection_rope" if rope else "in_projection",
    )(tab, *args)
    return res[0], res[1], list(res[2:])


def _joint(scores):
    return scores[0] if len(scores) == 1 else jnp.concatenate(scores, axis=1)


PV_W = 2 * HEAD_DIM


def _weights_pv(segs, extra=None):
    s = _joint([sc for sc, _ in segs])
    m = jnp.max(s, axis=-1, keepdims=True)
    if extra is not None:
        m = jnp.maximum(m, extra)
    e = jnp.exp2(s - m).astype(BF16)
    acc, off = None, 0
    for sc, v in segs:
        n = sc.shape[1]
        v1 = jnp.concatenate([v, jnp.ones((n, PV_W), BF16)], axis=1)
        part = _dot(e[:, off:off + n], v1)
        acc = part if acc is None else acc + part
        off += n
    num, den = acc[:, :PV_W], acc[:, PV_W:]
    if extra is not None:
        den = den + jnp.exp2(extra - m)
    return num, den


def _softmax_pv(segs, extra=None):
    num, den = _weights_pv(segs, extra)
    return num / den


def _half_vecs():
    lane = lax.broadcasted_iota(jnp.int32, (1, 2 * HEAD_DIM), 1)
    lo = (lane < HEAD_DIM).astype(BF16)
    return lo, (1.0 - lo).astype(BF16)


def _pair_block(ref, m, rows=None, b=0):
    if rows is None:
        return ref[b, :, m * 128:(m + 1) * 128]
    return ref[b, rows, m * 128:(m + 1) * 128]


def _both_halves(q, vecs):
    return jnp.concatenate([q * vecs[0], q * vecs[1]], axis=0)


def _merge_pair(o_even, o_odd):
    lane = lax.broadcasted_iota(jnp.int32, o_even.shape, 1)
    return jnp.where(lane < HEAD_DIM, o_even, o_odd)


def _diff_lambda_val(dl_ref, lam_init):
    lp = dl_ref[...]
    s1 = jnp.sum(lp[0:1] * lp[1:2], axis=-1, keepdims=True)
    s2 = jnp.sum(lp[2:3] * lp[3:4], axis=-1, keepdims=True)
    return jnp.exp(s1) - jnp.exp(s2) + lam_init


def _diff_finish(o, subln_ref, lam_init):
    ms = jnp.mean(o * o, axis=-1, keepdims=True)
    return o * lax.rsqrt(ms + NORM_EPS) * subln_ref[...] * (1.0 - lam_init)


def _swap_halves_bf16(x):
    return pltpu.roll(x.astype(F32), HEAD_DIM, 1).astype(BF16)


def _gqa_queries(q_ref, hk, vecs, b=0, rows=None):
    parts = []
    for g in range(SWA_GROUP):
        h = hk * SWA_GROUP + g
        q = _pair_block(q_ref, h // 2, rows=rows, b=b)
        if h % 2 != hk:
            q = _swap_halves_bf16(q)
        parts.append(q * vecs[hk])
    return jnp.concatenate(parts, axis=0)


def _gqa_outputs(o, hk, rows):
    blocks = []
    for j in range(SWA_GROUP // 2):
        halves = []
        for g in (2 * j, 2 * j + 1):
            og = o[g * rows:(g + 1) * rows]
            halves.append(og if g % 2 == hk else pltpu.roll(og, HEAD_DIM, 1))
        blocks.append(_merge_pair(halves[0], halves[1]))
    return blocks


def _pipelined(items, depth=2):
    pending, outs = [], []
    for score_fn, finish_fn in items:
        pending.append((finish_fn, score_fn()))
        if len(pending) > depth:
            fn, s = pending.pop(0)
            outs.append(fn(s))
    for fn, s in pending:
        outs.append(fn(s))
    return outs


def _ctx_attn_kernel(sink_ref, sq_ref, skv_ref, nq_ref, nk_ref, nv_ref, dq_ref, dk_ref, dv_ref,
                     dl_ref, subln_ref, o_ref, *, t, lam_init, bb):
    lam = _diff_lambda_val(dl_ref, lam_init)
    vecs = _half_vecs()

    def swa_item(b, hk):
        def scores():
            return _dot_nt(_gqa_queries(sq_ref, hk, vecs, b=b), _pair_block(skv_ref, 0, b=b))

        def finish(s):
            sink = jnp.concatenate(
                [jnp.full((t, 1), sink_ref[hk * SWA_GROUP + g] * LOG2E, F32) for g in range(SWA_GROUP)], axis=0)
            return _gqa_outputs(_softmax_pv([(s, _pair_block(skv_ref, 1, b=b))], extra=sink), hk, t)
        return scores, finish

    def na_item(b, m):
        def scores():
            return _dot_nt(_both_halves(_pair_block(nq_ref, m, b=b), vecs), _pair_block(nk_ref, m, b=b))

        def finish(s):
            o = _softmax_pv([(s, _pair_block(nv_ref, m, b=b))])
            return [_merge_pair(o[:t], o[t:])]
        return scores, finish

    def diff_item(b, h):
        def scores():
            return _dot_nt(_both_halves(_pair_block(dq_ref, h, b=b), vecs), _pair_block(dk_ref, h, b=b))

        def finish(s):
            o = _softmax_pv([(s, _pair_block(dv_ref, h, b=b))])
            return [_diff_finish(o[:t] - lam * o[t:], subln_ref, lam_init)]
        return scores, finish

    items = []
    for make, n in ((swa_item, SWA_KV_HEADS), (na_item, NA_HEADS // 2), (diff_item, DIFF_HEADS)):
        items += [make(b, i) for i in range(n) for b in range(bb)]
    res = _pipelined(items, depth=2 * bb)
    for b in range(bb):
        pieces = [piece for r in res[b::bb] for piece in r]
        o_ref[b] = jnp.concatenate(pieces, axis=1).astype(BF16)


CTX_ROWS_PER_STEP = 4


def _pcol(name, t, bb=1):
    return pl.BlockSpec((bb, t, PROJ_TILE), lambda i, c=P_COL[name]: (i, 0, c))


def _ctx_attention(p, sink, dlam, subln, lam_init, *, nb, t):
    bb = CTX_ROWS_PER_STEP
    p3 = p.reshape(nb, t, P_TILES * PROJ_TILE)
    names = ("sq", "skv", "nq", "nk", "nv", "dq", "dk", "dv")
    out = pl.pallas_call(
        functools.partial(_ctx_attn_kernel, t=t, lam_init=lam_init, bb=bb),
        out_shape=jax.ShapeDtypeStruct((nb, t, 3 * GROUP_W), BF16),
        grid=(nb // bb,),
        in_specs=[pl.BlockSpec(memory_space=pltpu.SMEM)] + [_pcol(n, t, bb) for n in names] + [
            pl.BlockSpec((4, HEAD_DIM), lambda i: (0, 0)),
            pl.BlockSpec((1, DIFF_VDIM), lambda i: (0, 0)),
        ],
        out_specs=pl.BlockSpec((bb, t, 3 * GROUP_W), lambda i: (i, 0, 0)),
        compiler_params=_params("arbitrary"),
        name="ctx_attention",
    )(sink, *([p3] * len(names)), dlam, subln.reshape(1, DIFF_VDIM))
    return out.reshape(nb * t, 3 * GROUP_W)


Q_BLOCKS_PER_STEP = 4


def _swa_kernel(sink_ref, q_ref, kv_ref, ck_ref, cv_ref, o_ref, *, t, qb):
    span = 3 * qb
    vecs = _half_vecs()

    def item(j, hk):
        n = pl.program_id(1) * Q_BLOCKS_PER_STEP + j
        start = pl.multiple_of(jnp.clip((n - 1) * qb, 0, t - span), qb)
        win = pl.ds(start, span)
        qrows = slice(j * qb, (j + 1) * qb)

        def scores():
            qs = _gqa_queries(q_ref, hk, vecs, rows=qrows)
            return _dot_nt(qs, _pair_block(kv_ref, 0, rows=win)), _dot_nt(qs, ck_ref[0])

        def finish(s):
            row = (lax.broadcasted_iota(jnp.int32, (SWA_GROUP * qb, span), 0) & (qb - 1)) + n * qb
            col = lax.broadcasted_iota(jnp.int32, (SWA_GROUP * qb, span), 1) + start
            dist = row - col
            ok = (dist <= SWA_WINDOW) & (dist >= -SWA_WINDOW)
            sink = jnp.concatenate(
                [jnp.full((qb, 1), sink_ref[hk * SWA_GROUP + g] * LOG2E, F32) for g in range(SWA_GROUP)], axis=0)
            s_loc = jnp.where(ok, s[0], NEG_INF)
            o = _softmax_pv([(s_loc, _pair_block(kv_ref, 1, rows=win)), (s[1], cv_ref[0])], extra=sink)
            return _gqa_outputs(o, hk, qb)
        return scores, finish

    items = [item(j, hk) for hk in range(SWA_KV_HEADS) for j in range(Q_BLOCKS_PER_STEP)]
    res = _pipelined(items, depth=Q_BLOCKS_PER_STEP)
    for j in range(Q_BLOCKS_PER_STEP):
        pieces = [piece for r in res[j::Q_BLOCKS_PER_STEP] for piece in r]
        o_ref[0, j * qb:(j + 1) * qb, :] = jnp.concatenate(pieces, axis=1).astype(BF16)


def _swa_latent(p, ck, cv, layer, sink, *, nb, t):
    qb = 128
    step_rows = qb * Q_BLOCKS_PER_STEP
    p3 = p.reshape(nb, t, P_TILES * PROJ_TILE)
    out = pl.pallas_call(
        functools.partial(_swa_kernel, t=t, qb=qb),
        out_shape=jax.ShapeDtypeStruct((nb, t, GROUP_W), BF16),
        grid=(nb, t // step_rows),
        in_specs=[
            pl.BlockSpec(memory_space=pltpu.SMEM),
            pl.BlockSpec((1, step_rows, PROJ_TILE), lambda b, n: (b, n, P_COL["sq"])),
            pl.BlockSpec((1, t, PROJ_TILE), lambda b, n: (b, 0, P_COL["skv"])),
            pl.BlockSpec((1,) + ck.shape[1:], lambda b, n: (b * DEPTH + layer, 0, 0)),
            pl.BlockSpec((1,) + cv.shape[1:], lambda b, n: (b * DEPTH + layer, 0, 0)),
        ],
        out_specs=pl.BlockSpec((1, step_rows, GROUP_W), lambda b, n: (b, n, 0)),
        compiler_params=_params("arbitrary", "arbitrary"),
        name="swa_latent",
    )(sink, p3, p3, ck, cv)
    return out.reshape(nb * t, GROUP_W)


def _na_row_start(r, rows):
    kh = min(NA_KH, rows)
    return min(max(r - kh // 2, 0), rows - kh)


def _na_win_start(qt, rows):
    return min(max(_na_row_start(2 * qt, rows), 0), rows - NA_WIN_ROWS)


def _na_bias_kernel(rpb_ref, o_ref, *, rows):
    h = pl.program_id(0)
    n_dr, n_dc = 2 * NA_KH - 1, 2 * NA_KW - 1
    qi = lax.broadcasted_iota(jnp.int32, (GRID_W, GRID_W), 0)
    ki = lax.broadcasted_iota(jnp.int32, (GRID_W, GRID_W), 1)
    dc = jnp.clip(ki - qi + (NA_KW - 1), 0, n_dc - 1)
    cs = jnp.clip(qi - NA_KW // 2, 0, GRID_W - NA_KW)
    col_ok = (ki >= cs) & (ki < cs + NA_KW)
    neg = jnp.full((GRID_W, GRID_W), NEG_INF, F32)
    tabs = []
    for dr in range(n_dr):
        acc = jnp.zeros((GRID_W, GRID_W), F32)
        for c in range(n_dc):
            acc = jnp.where(dc == c, rpb_ref[(h * n_dr + dr) * n_dc + c], acc)
        tabs.append(jnp.where(col_ok, acc * LOG2E, neg))
    kh = min(NA_KH, rows)
    for qt in range(rows // 2):
        ws = _na_win_start(qt, rows)
        bands = []
        for qq in range(2):
            qr = 2 * qt + qq
            rs = _na_row_start(qr, rows)
            blks = []
            for kk in range(NA_WIN_ROWS):
                kr = ws + kk
                blks.append(tabs[kr - qr + NA_KH - 1] if rs <= kr < rs + kh else neg)
            bands.append(jnp.concatenate(blks, axis=1))
        o_ref[0, qt] = jnp.concatenate(bands, axis=0)


def _na_bias(rpb, rows):
    n_qt = rows // 2
    n_tab = rpb.shape[0] * rpb.shape[1]
    return pl.pallas_call(
        functools.partial(_na_bias_kernel, rows=rows),
        out_shape=jax.ShapeDtypeStruct((n_tab, n_qt, NA_QT, NA_WIN), F32),
        grid=(n_tab,),
        in_specs=[pl.BlockSpec(memory_space=pltpu.SMEM)],
        out_specs=pl.BlockSpec((1, n_qt, NA_QT, NA_WIN), lambda h: (h, 0, 0, 0)),
        compiler_params=_params("arbitrary"),
        name="na_bias",
    )(rpb.reshape(-1))


def _na_kernel(q_ref, k_ref, v_ref, ck_ref, cv_ref, bias_ref, o_ref, *, rows):
    vecs = _half_vecs()

    def item(j, m):
        qt = pl.program_id(1) * Q_BLOCKS_PER_STEP + j
        ws = jnp.clip(jnp.clip(2 * qt - NA_KH // 2, 0, rows - NA_KH), 0, rows - NA_WIN_ROWS)
        win = pl.ds(pl.multiple_of(ws * GRID_W, GRID_W), NA_WIN)
        qrows = slice(j * NA_QT, (j + 1) * NA_QT)

        def scores():
            q2 = _both_halves(_pair_block(q_ref, m, rows=qrows), vecs)
            return _dot_nt(q2, _pair_block(k_ref, m, rows=win)), _dot_nt(q2, _pair_block(ck_ref, m))

        def finish(s):
            v, cv = _pair_block(v_ref, m, rows=win), _pair_block(cv_ref, m)
            bias = jnp.concatenate([bias_ref[2 * m, j], bias_ref[2 * m + 1, j]], axis=0)
            o = _softmax_pv([(s[0] + bias, v), (s[1], cv)])
            return _merge_pair(o[:NA_QT], o[NA_QT:])
        return scores, finish

    items = [item(j, m) for m in range(NA_HEADS // 2) for j in range(Q_BLOCKS_PER_STEP)]
    res = _pipelined(items, depth=2 * Q_BLOCKS_PER_STEP)
    for j in range(Q_BLOCKS_PER_STEP):
        o_ref[0, j * NA_QT:(j + 1) * NA_QT, :] = jnp.concatenate(res[j::Q_BLOCKS_PER_STEP], axis=1).astype(BF16)


def _na_latent(p, ck, cv, layer, bias, *, nb, t):
    rows = t // GRID_W
    p3 = p.reshape(nb, t, P_TILES * PROJ_TILE)
    out = pl.pallas_call(
        functools.partial(_na_kernel, rows=rows),
        out_shape=jax.ShapeDtypeStruct((nb, t, GROUP_W), BF16),
        grid=(nb, t // (NA_QT * Q_BLOCKS_PER_STEP)),
        in_specs=[
            pl.BlockSpec((1, NA_QT * Q_BLOCKS_PER_STEP, PROJ_TILE), lambda i, n: (i, n, P_COL["nq"])),
            pl.BlockSpec((1, t, PROJ_TILE), lambda i, n: (i, 0, P_COL["nk"])),
            pl.BlockSpec((1, t, PROJ_TILE), lambda i, n: (i, 0, P_COL["nv"])),
            pl.BlockSpec((1,) + ck.shape[1:], lambda i, n: (i * DEPTH + layer, 0, 0)),
            pl.BlockSpec((1,) + cv.shape[1:], lambda i, n: (i * DEPTH + layer, 0, 0)),
            pl.BlockSpec((NA_HEADS, Q_BLOCKS_PER_STEP, NA_QT, NA_WIN), lambda i, n: (layer, n, 0, 0)),
        ],
        out_specs=pl.BlockSpec((1, NA_QT * Q_BLOCKS_PER_STEP, GROUP_W), lambda i, n: (i, n, 0)),
        compiler_params=_params("arbitrary", "arbitrary"),
        name="na_latent",
    )(p3, p3, p3, ck, cv, bias)
    return out.reshape(nb * t, GROUP_W)


def _diff_kernel(q_ref, k_ref, v_ref, ck_ref, cv_ref, dl_ref, subln_ref, o_ref, *, lam_init, tq):
    lam = _diff_lambda_val(dl_ref, lam_init)

    vecs = _half_vecs()

    def item(j, h):
        qrows = slice(j * tq, (j + 1) * tq)

        def scores():
            q2 = _both_halves(_pair_block(q_ref, h, rows=qrows), vecs)
            return _dot_nt(q2, _pair_block(k_ref, h)), _dot_nt(q2, _pair_block(ck_ref, h))

        def finish(s):
            o = _softmax_pv([(s[0], _pair_block(v_ref, h)), (s[1], _pair_block(cv_ref, h))])
            return _diff_finish(o[:tq] - lam * o[tq:], subln_ref, lam_init)
        return scores, finish

    items = [item(j, h) for h in range(DIFF_HEADS) for j in range(Q_BLOCKS_PER_STEP)]
    res = _pipelined(items, depth=Q_BLOCKS_PER_STEP)
    for j in range(Q_BLOCKS_PER_STEP):
        o_ref[0, j * tq:(j + 1) * tq, :] = jnp.concatenate(res[j::Q_BLOCKS_PER_STEP], axis=1).astype(BF16)


def _diff_latent(p, ck, cv, layer, dlam, subln, lam_init, *, nb, t):
    tq_sub = 256
    tq = tq_sub * Q_BLOCKS_PER_STEP
    p3 = p.reshape(nb, t, P_TILES * PROJ_TILE)
    out = pl.pallas_call(
        functools.partial(_diff_kernel, lam_init=lam_init, tq=tq_sub),
        out_shape=jax.ShapeDtypeStruct((nb, t, GROUP_W), BF16),
        grid=(nb, t // tq),
        in_specs=[
            pl.BlockSpec((1, tq, PROJ_TILE), lambda i, n: (i, n, P_COL["dq"])),
            pl.BlockSpec((1, t, PROJ_TILE), lambda i, n: (i, 0, P_COL["dk"])),
            pl.BlockSpec((1, t, PROJ_TILE), lambda i, n: (i, 0, P_COL["dv"])),
            pl.BlockSpec((1,) + ck.shape[1:], lambda i, n: (i * DEPTH + layer, 0, 0)),
            pl.BlockSpec((1,) + cv.shape[1:], lambda i, n: (i * DEPTH + layer, 0, 0)),
            pl.BlockSpec((4, HEAD_DIM), lambda i, n: (0, 0)),
            pl.BlockSpec((1, DIFF_VDIM), lambda i, n: (0, 0)),
        ],
        out_specs=pl.BlockSpec((1, tq, GROUP_W), lambda i, n: (i, n, 0)),
        compiler_params=_params("arbitrary", "arbitrary"),
        name="diff_latent",
    )(p3, p3, p3, ck, cv, dlam, subln.reshape(1, DIFF_VDIM))
    return out.reshape(nb * t, GROUP_W)


SUBLANES = 8


def _linear_scan(a, b, row, t, reverse, sa_ref, sb_ref, sc_ref):
    nt = t // SUBLANES
    sub = row & (SUBLANES - 1)

    def doubling(a, b, idx, n, steps):
        for s in steps:
            if reverse:
                ok = idx < n - s
                a_sh = jnp.where(ok, pltpu.roll(a, a.shape[0] - s, 0), 1.0)
                b_sh = jnp.where(ok, pltpu.roll(b, b.shape[0] - s, 0), 0.0)
            else:
                ok = idx >= s
                a_sh = jnp.where(ok, pltpu.roll(a, s, 0), 1.0)
                b_sh = jnp.where(ok, pltpu.roll(b, s, 0), 0.0)
            b = a * b_sh + b
            a = a * a_sh
        return a, b

    tiles = (nt, SUBLANES, LRU_CT)
    sub3 = lax.broadcasted_iota(jnp.int32, tiles, 1)

    def tile_doubling(a, b):
        for s in (1, 2, 4):
            ok = (sub3 < SUBLANES - s) if reverse else (sub3 >= s)
            sh = SUBLANES - s if reverse else s
            a_sh = jnp.where(ok, pltpu.roll(a, sh, 1), 1.0)
            b_sh = jnp.where(ok, pltpu.roll(b, sh, 1), 0.0)
            b = a * b_sh + b
            a = a * a_sh
        return a, b

    a, b = tile_doubling(a.reshape(tiles), b.reshape(tiles))
    a, b = a.reshape(t, LRU_CT), b.reshape(t, LRU_CT)
    sa_ref[...] = a
    sb_ref[...] = b
    last = 0 if reverse else SUBLANES - 1
    at = sa_ref[pl.ds(last, nt, stride=SUBLANES), :]
    bt = sb_ref[pl.ds(last, nt, stride=SUBLANES), :]
    rowt = lax.broadcasted_iota(jnp.int32, (nt, LRU_CT), 0)
    steps, s = [], 1
    while s < nt:
        steps.append(s)
        s *= 2
    _, bt = doubling(at, bt, rowt, nt, steps)
    if reverse:
        carry = jnp.where(rowt < nt - 1, pltpu.roll(bt, nt - 1, 0), 0.0)
    else:
        carry = jnp.where(rowt >= 1, pltpu.roll(bt, 1, 0), 0.0)
    for r in range(SUBLANES):
        sc_ref[pl.ds(r, nt, stride=SUBLANES), :] = carry
    return b + a * sc_ref[...]


def _lru_kernel(x_ref, g_ref, cw_ref, cb_ref, w_ref, gb_ref, lam_ref, h0_ref, y_ref, st_ref, *scratch, t, cpt,
                has_h0):
    row = lax.broadcasted_iota(jnp.int32, (t, LRU_CT), 0)
    for ci in range(cpt):
        lanes = slice(ci * LRU_CT, (ci + 1) * LRU_CT)
        x = x_ref[0, :, lanes]
        u = cb_ref[:, lanes] + cw_ref[2:3, lanes] * x
        for tap, off in ((0, -2), (1, -1), (3, 1)):
            xs = pltpu.roll(x, (-off) % t, 0)
            ok = (row + off >= 0) & (row + off < t)
            u = u + cw_ref[tap:tap + 1, lanes] * jnp.where(ok, xs, 0.0)
        gates = _dot(u.astype(BF16), w_ref[ci]) + gb_ref[ci]
        total = None
        for d in range(2):
            r = 1.0 / (1.0 + jnp.exp2(gates[:, (2 * d) * LRU_CT:(2 * d + 1) * LRU_CT]))
            ig = 1.0 / (1.0 + jnp.exp2(gates[:, (2 * d + 1) * LRU_CT:(2 * d + 2) * LRU_CT]))
            nl = -lam_ref[d:d + 1, lanes]
            rate = LRU_C * (jnp.maximum(nl, 0.0) + jnp.log1p(jnp.exp(-jnp.abs(nl))))
            a = jnp.exp2(r * (rate * -LOG2E))
            bx = jnp.sqrt(jnp.tanh(r * rate) * (a * a + 1.0)) * (ig * u)
            if has_h0:
                edge = t - 1 if d == 1 else 0
                bx = jnp.where(row == edge, bx + a * h0_ref[0, d:d + 1, lanes], bx)
            sa_ref, sb_ref, sc_ref = scratch[3 * (2 * ci + d):3 * (2 * ci + d) + 3]
            bx = _linear_scan(a, bx, row, t, d == 1, sa_ref, sb_ref, sc_ref)
            fin = t - 1 if d == 0 else 0
            st_ref[0, d:d + 1, lanes] = bx[fin:fin + 1, :]
            total = bx if total is None else total + bx
        g = g_ref[0, :, lanes]
        gelu = 0.5 * g * (1.0 + jnp.tanh(math.sqrt(2.0 / math.pi) * (g + 0.044715 * (g * g * g))))
        y_ref[0, :, lanes] = (total * gelu).astype(BF16)


def _lru_mixer(f, conv_w, conv_b, w_gates, b_gates, lam, h0, *, nb, t, cpt):
    has_h0 = h0 is not None
    if not has_h0:
        h0 = jnp.zeros((nb, 2, LRU_WIDTH), F32)
    f3 = f.reshape(nb, t, f.shape[1])
    nct = LRU_WIDTH // LRU_CT
    wide = cpt * LRU_CT
    y, st = pl.pallas_call(
        functools.partial(_lru_kernel, t=t, cpt=cpt, has_h0=has_h0),
        out_shape=[jax.ShapeDtypeStruct((nb, t, LRU_WIDTH), BF16),
                   jax.ShapeDtypeStruct((nb, 2, LRU_WIDTH), F32)],
        grid=(nb, nct // cpt),
        in_specs=[
            pl.BlockSpec((1, t, wide), lambda i, c: (i, 0, c)),
            pl.BlockSpec((1, t, wide), lambda i, c: (i, 0, nct // cpt + c)),
            pl.BlockSpec((CONV_W, wide), lambda i, c: (0, c)),
            pl.BlockSpec((1, wide), lambda i, c: (0, c)),
            pl.BlockSpec((cpt, LRU_CT, 4 * LRU_CT), lambda i, c: (c, 0, 0)),
            pl.BlockSpec((cpt, 1, 4 * LRU_CT), lambda i, c: (c, 0, 0)),
            pl.BlockSpec((2, wide), lambda i, c: (0, c)),
            pl.BlockSpec((1, 2, wide), lambda i, c: (i, 0, c)),
        ],
        out_specs=[pl.BlockSpec((1, t, wide), lambda i, c: (i, 0, c)),
                   pl.BlockSpec((1, 2, wide), lambda i, c: (i, 0, c))],
        scratch_shapes=[pltpu.VMEM((t, LRU_CT), F32)] * (6 * cpt),
        compiler_params=_params("arbitrary", "arbitrary"),
        name="lru_mixer",
    )(f3, f3, conv_w, conv_b.reshape(1, LRU_WIDTH), w_gates, b_gates, lam, h0)
    return y.reshape(nb * t, LRU_WIDTH), st


def _lru_gate_weights(wa, ba, wx, bx):
    nct = LRU_WIDTH // LRU_CT
    bpt = LRU_CT // LRU_BLOCK_W
    eye = jnp.eye(bpt, dtype=F32)

    def tile_w(w):
        w4 = w.reshape(nct, bpt, LRU_BLOCK_W, LRU_BLOCK_W)
        return jnp.einsum("cnij,nm->cnimj", w4, eye).reshape(nct, LRU_CT, LRU_CT)

    w = jnp.concatenate([tile_w(wa[0]), tile_w(wx[0]), tile_w(wa[1]), tile_w(wx[1])], axis=-1)
    b = jnp.concatenate([v.reshape(nct, 1, LRU_CT) for v in (ba[0], bx[0], ba[1], bx[1])], axis=-1)
    return (w * -LOG2E).astype(BF16), b * -LOG2E


def _outproj_kernel(*refs, n_in):
    ins = refs[:n_in]
    w_ref, x_ref, g_ref, o_ref = refs[n_in:]
    acc = None
    off = 0
    for m_ref in ins:
        width = m_ref.shape[1]
        part = _dot(m_ref[...], w_ref[0, off:off + width, :])
        acc = part if acc is None else acc + part
        off += width
    o_ref[...] = x_ref[...] + g_ref[0] * acc


def _out_projection(mixes, w_out_bf, layer, x, mods_l, mod_row, *, tm):
    n_tok = x.shape[0]
    return pl.pallas_call(
        functools.partial(_outproj_kernel, n_in=len(mixes)),
        out_shape=jax.ShapeDtypeStruct((n_tok, D_MODEL), F32),
        grid=(n_tok // tm,),
        in_specs=[pl.BlockSpec((tm, m.shape[1]), lambda i: (i, 0)) for m in mixes] + [
            pl.BlockSpec((1, D_MODEL, D_MODEL), lambda i: (layer, 0, 0)),
            pl.BlockSpec((tm, D_MODEL), lambda i: (i, 0)),
            pl.BlockSpec((1, 1, D_MODEL), lambda i: (mod_row(i) * MOD_CHUNKS + 2, 0, 0)),
        ],
        out_specs=pl.BlockSpec((tm, D_MODEL), lambda i: (i, 0)),
        compiler_params=_params("arbitrary"),
        name="out_projection",
    )(*mixes, w_out_bf, x, mods_l)


def _ffn_kernel(x_ref, sh_ref, sc_ref, g_ref, nw_ref, wg_ref, wu_ref, wd_ref, o_ref, h_scr, *, tm):
    k = pl.program_id(1)

    def step(first, last=False):
        wg = wg_ref[0].astype(BF16)
        wu = wu_ref[0].astype(BF16)
        wd = wd_ref[0].astype(BF16)
        chunk = ROW_CHUNK if first else FFN_ROW_CHUNK

        def norm_chunk(c):
            _modnorm_static(x_ref, nw_ref, sh_ref, sc_ref, h_scr, c * chunk, chunk)

        n_chunks = tm // chunk

        def gate_up(c):
            if first and c + 1 < n_chunks:
                norm_chunk(c + 1)
            h = h_scr[c * chunk:(c + 1) * chunk, :]
            return _dot(h, wg), _dot(h, wu)

        def down(c, gu):
            g, u = gu
            act = (g * _sigmoid(g)) * u
            res = _dot(act.astype(BF16), wd)
            rows = slice(c * chunk, (c + 1) * chunk)
            if first:
                o_ref[rows, :] = res
            elif last:
                o_ref[rows, :] = x_ref[rows, :] + g_ref[0] * (o_ref[rows, :] + res)
            else:
                o_ref[rows, :] += res

        if first:
            norm_chunk(0)
        pending = gate_up(0)
        for c in range(1, n_chunks):
            nxt = gate_up(c)
            down(c - 1, pending)
            pending = nxt
        down(n_chunks - 1, pending)

    @pl.when(k == 0)
    def _():
        step(True)

    n_k = pl.num_programs(1)

    @pl.when((k > 0) & (k < n_k - 1))
    def _():
        step(False)

    @pl.when(k == n_k - 1)
    def _():
        step(False, last=True)


def _ffn(x, mods_l, mod_row, norm_w, wg, wu, wd, layer, *, tm, th):
    n_tok = x.shape[0]
    assert FFN_HIDDEN // th >= 2
    return pl.pallas_call(
        functools.partial(_ffn_kernel, tm=tm),
        out_shape=jax.ShapeDtypeStruct((n_tok, D_MODEL), F32),
        grid=(n_tok // tm, FFN_HIDDEN // th),
        in_specs=[
            pl.BlockSpec((tm, D_MODEL), lambda i, k: (i, 0), pipeline_mode=pl.Buffered(1)),
            pl.BlockSpec((1, 1, D_MODEL), lambda i, k: (mod_row(i) * MOD_CHUNKS + 3, 0, 0)),
            pl.BlockSpec((1, 1, D_MODEL), lambda i, k: (mod_row(i) * MOD_CHUNKS + 4, 0, 0)),
            pl.BlockSpec((1, 1, D_MODEL), lambda i, k: (mod_row(i) * MOD_CHUNKS + 5, 0, 0)),
            pl.BlockSpec((1, D_MODEL), lambda i, k: (0, 0)),
            pl.BlockSpec((1, D_MODEL, th), lambda i, k: (layer, 0, k)),
            pl.BlockSpec((1, D_MODEL, th), lambda i, k: (layer, 0, k)),
            pl.BlockSpec((1, th, D_MODEL), lambda i, k: (layer, k, 0)),
        ],
        out_specs=pl.BlockSpec((tm, D_MODEL), lambda i, k: (i, 0)),
        scratch_shapes=[pltpu.VMEM((tm, D_MODEL), BF16)],
        compiler_params=_params("arbitrary", "arbitrary"),
        name="ffn",
    )(x, mods_l, mods_l, mods_l, norm_w.reshape(1, D_MODEL), wg, wu, wd)


def _rope_tables(t_lat):
    nf = HEAD_DIM // 4
    tok = np.arange(t_lat)
    pos = np.stack([tok // GRID_W, tok % GRID_W], axis=-1).astype(np.float32)
    inv = np.float32(ROPE_BASE) ** (-np.arange(nf, dtype=np.float32) / np.float32(nf))
    ang = (pos[:, :, None] * inv.astype(np.float32)).astype(np.float32)
    cos = np.cos(ang.astype(np.float64)).astype(np.float32)
    sin = np.sin(ang.astype(np.float64)).astype(np.float32)
    cos_h = np.concatenate([cos[:, 0], cos[:, 0], cos[:, 1], cos[:, 1]], axis=-1)
    sin_h = np.concatenate([-sin[:, 0], sin[:, 0], -sin[:, 1], sin[:, 1]], axis=-1)
    reps = PROJ_HALF // HEAD_DIM
    return jnp.asarray(np.tile(cos_h, (1, reps))), jnp.asarray(np.tile(sin_h, (1, reps)))


def _tile_table():
    lo = [_FIRST_BLOCK[n] for n in TILE_NAMES]
    hi = [b + 1 for b in lo]
    cls = [_TILE_CLASS[n] for n in TILE_NAMES]
    return jnp.asarray(np.array(lo + hi + cls, np.int32))


def _tile_gains(qk_gain_l):
    def tiled(g, n):
        return jnp.tile(g, n // HEAD_DIM)
    ones = jnp.ones((PROJ_TILE,), F32)
    per_tile = {
        "lx": ones, "lg": ones, "nv": ones, "dv": ones,
        "nk": tiled(qk_gain_l[1, 1], PROJ_TILE), "dk": tiled(qk_gain_l[2, 1], PROJ_TILE),
        "skv": jnp.concatenate([tiled(qk_gain_l[0, 1], 128), jnp.ones((PROJ_TILE - 128,), F32)]),
        "sq": tiled(qk_gain_l[0, 0], PROJ_TILE) * QK_SCALE,
        "nq": tiled(qk_gain_l[1, 0], PROJ_TILE) * QK_SCALE,
        "dq": tiled(qk_gain_l[2, 0], PROJ_TILE) * QK_SCALE,
    }
    return jnp.stack([per_tile[n] for n in TILE_NAMES]).reshape(N_TILES, 1, PROJ_TILE)


def _lambda_init(layer):
    return 0.8 - 0.6 * math.exp(-0.3 * layer)


def kernel(x_prompt, x_sample, cache_swa_k, cache_swa_v, cache_na_k, cache_na_v, cache_diff_k, cache_diff_v, state_lru, c, c_ctx, norm_mix, norm_ffn, w_mod, b_mod, w_in, w_out, qk_gain, swa_sink, na_rpb, diff_lambda, diff_subln, conv_w, conv_b, lru_wa, lru_ba, lru_wx, lru_bx, lru_L, w_ffn_gate, w_ffn_up, w_ffn_down):
    bc, s_ctx, _ = x_prompt.shape
    bd, t_lat, _ = x_sample.shape
    n_ctx, n_lat = bc * s_ctx, bd * t_lat
    rows = t_lat // GRID_W
    p_ctx = cache_swa_k.shape[2]

    cvecs = jnp.concatenate([c_ctx[None, :], c, jnp.zeros((8 - 1 - bd, D_MODEL), F32)], axis=0)
    mods = _modulation(cvecs, w_mod, b_mod).reshape(DEPTH, 8 * MOD_CHUNKS, 1, D_MODEL)

    hsum = jnp.asarray(np.kron(np.eye(PROJ_HALF // HEAD_DIM, dtype=np.float32),
                               np.full((HEAD_DIM, HEAD_DIM), 1.0 / HEAD_DIM, np.float32)), BF16)
    rope_tabs = _rope_tables(t_lat)
    tab = _tile_table()

    tm = 1024
    tm_out = 512
    ctx_row = lambda tile: (lambda i: 0)
    lat_row = lambda tile: (lambda i: 1 + (i * tile) // t_lat)

    xc = x_prompt.reshape(n_ctx, D_MODEL)
    xs = x_sample.reshape(n_lat, D_MODEL)
    caches = [jnp.zeros((bc, DEPTH, s_ctx, _CACHE_SRC[name][2]), F32) for name in CACHE_NAMES]
    lru_states = []
    w_in_bf = w_in
    w_out_bf = w_out.astype(BF16)

    def cached(arr):
        return arr.astype(BF16).reshape(bd * DEPTH, p_ctx, -1)

    c_swa_k, c_swa_v = cached(cache_swa_k), cached(cache_swa_v)
    c_na_k, c_na_v = cached(cache_na_k), cached(cache_na_v)
    c_diff_k, c_diff_v = cached(cache_diff_k), cached(cache_diff_v)
    na_bias = _na_bias(na_rpb, rows)

    for l in range(DEPTH):
        lam_init = _lambda_init(l)
        mods_l = mods[l]
        gain = _tile_gains(qk_gain[l])
        w_gates, b_gates = _lru_gate_weights(lru_wa[l], lru_ba[l], lru_wx[l], lru_bx[l])
        ffn_w = (w_ffn_gate, w_ffn_up, w_ffn_down, l)

        p, f, caches = _in_projection(xc, mods_l, ctx_row(tm), norm_mix[l], w_in_bf, l, tab, gain, hsum,
                                      None, caches, tm=tm, seq=s_ctx)
        mix3 = _ctx_attention(p, swa_sink[l], diff_lambda[l], diff_subln[l], lam_init, nb=bc, t=s_ctx)
        od, st = _lru_mixer(f, conv_w[l], conv_b[l], w_gates, b_gates, lru_L[l], None,
                            nb=bc, t=s_ctx, cpt=4)
        lru_states.append(st)
        xc = _out_projection([mix3, od], w_out_bf, l, xc, mods_l, ctx_row(tm_out), tm=tm_out)
        xc = _ffn(xc, mods_l, ctx_row(tm), norm_ffn[l], *ffn_w, tm=tm, th=512)

        p, f, _ = _in_projection(xs, mods_l, lat_row(tm), norm_mix[l], w_in_bf, l, tab, gain, hsum,
                                 rope_tabs, None, tm=tm, seq=t_lat)
        oa = _swa_latent(p, c_swa_k, c_swa_v, l, swa_sink[l], nb=bd, t=t_lat)
        ob = _na_latent(p, c_na_k, c_na_v, l, na_bias, nb=bd, t=t_lat)
        oc = _diff_latent(p, c_diff_k, c_diff_v, l, diff_lambda[l], diff_subln[l], lam_init, nb=bd, t=t_lat)
        od, _ = _lru_mixer(f, conv_w[l], conv_b[l], w_gates, b_gates, lru_L[l], state_lru[:, l],
                           nb=bd, t=t_lat, cpt=2)
        xs = _out_projection([oa, ob, oc, od], w_out_bf, l, xs, mods_l, lat_row(tm_out), tm=tm_out)
        xs = _ffn(xs, mods_l, lat_row(tm), norm_ffn[l], *ffn_w, tm=tm, th=512)

    swa_k, swa_v, na_k, na_v, diff_k, diff_v = caches
    return (xc.reshape(bc, s_ctx, D_MODEL), xs.reshape(bd, t_lat, D_MODEL),
            swa_k.reshape(bc, DEPTH, s_ctx, SWA_KV_HEADS, HEAD_DIM),
            swa_v.reshape(bc, DEPTH, s_ctx, SWA_KV_HEADS, HEAD_DIM),
            na_k.reshape(bc, DEPTH, s_ctx, NA_HEADS, HEAD_DIM),
            na_v.reshape(bc, DEPTH, s_ctx, NA_HEADS, HEAD_DIM),
            diff_k.reshape(bc, DEPTH, s_ctx, DIFF_HEADS, 2, HEAD_DIM),
            diff_v.reshape(bc, DEPTH, s_ctx, DIFF_HEADS, DIFF_VDIM),
            jnp.stack(lru_states, axis=1))
```

```python
import functools
import math

import jax
import jax.numpy as jnp
import numpy as np
from jax import lax
from jax.experimental import pallas as pl
from jax.experimental.pallas import tpu as pltpu

F32 = jnp.float32
BF16 = jnp.bfloat16

D_MODEL = 2048
DEPTH = 2
GRID_W = 64
HEAD_DIM = 64
GROUP_W = 512
SWA_HEADS = 8
SWA_KV_HEADS = 2
SWA_GROUP = 4
SWA_WINDOW = 128
NA_HEADS = 8
NA_KH = 8
NA_KW = 16
DIFF_HEADS = 4
DIFF_VDIM = 128
LRU_WIDTH = 512
LRU_BLOCKS = 8
LRU_BLOCK_W = 64
LRU_C = 8.0
CONV_W = 4
FFN_HIDDEN = 5632
ROPE_BASE = 10000.0
NORM_EPS = 1e-6
NEG_INF = -1e30
MOD_CHUNKS = 6
PROJ_W = 4864
LOG2E = math.log2(math.e)
QK_SCALE = HEAD_DIM ** -0.5 * LOG2E

V7X_VMEM_LIMIT = 60 * 1024 * 1024

PROJ_HALF = 256
PROJ_TILE = 2 * PROJ_HALF
TILE_NAMES = ("lx", "lg", "nk", "dk", "nv", "dv", "skv", "sq", "nq", "dq")
N_TILES = len(TILE_NAMES)
_FIRST_BLOCK = {"sq": 0, "skv": 2, "nq": 3, "nk": 5, "nv": 7, "dq": 9, "dk": 11, "dv": 13, "lx": 15, "lg": 17}
CLS_PLAIN, CLS_NORM, CLS_ROPE, CLS_MIXED = 0, 1, 2, 3
_TILE_CLASS = {"lx": CLS_PLAIN, "lg": CLS_PLAIN, "nv": CLS_PLAIN, "dv": CLS_PLAIN,
               "nk": CLS_NORM, "nq": CLS_NORM, "dk": CLS_ROPE, "sq": CLS_ROPE, "dq": CLS_ROPE,
               "skv": CLS_MIXED}
P_FIRST = 2
P_TILES = N_TILES - P_FIRST
P_COL = {name: TILE_NAMES.index(name) - P_FIRST for name in TILE_NAMES[P_FIRST:]}
F_TILES = 2

NA_QT = 128
NA_WIN_ROWS = 10
NA_WIN = NA_WIN_ROWS * GRID_W
LRU_CT = 128
ROW_CHUNK = 256
FFN_ROW_CHUNK = 512


def _sigmoid(x):
    return 1.0 / (1.0 + jnp.exp(-x))


def _dot(a, b):
    return jnp.dot(a, b, preferred_element_type=F32)


def _dot_nt(a, b):
    return lax.dot_general(a, b, (((1,), (1,)), ((), ())), preferred_element_type=F32)


def _params(*sem):
    return pltpu.CompilerParams(dimension_semantics=sem, vmem_limit_bytes=V7X_VMEM_LIMIT)


def _mod_kernel(c_ref, w_ref, b_ref, o_ref):
    cv = c_ref[...]
    s = cv * _sigmoid(cv)
    o_ref[0] = _dot(s.astype(BF16), w_ref[0].astype(BF16)) + b_ref[0]


def _modulation(cvecs, w_mod, b_mod):
    tn = 1024
    n = MOD_CHUNKS * D_MODEL
    return pl.pallas_call(
        _mod_kernel,
        out_shape=jax.ShapeDtypeStruct((DEPTH, 8, n), F32),
        grid=(DEPTH, n // tn),
        in_specs=[
            pl.BlockSpec((8, D_MODEL), lambda l, j: (0, 0)),
            pl.BlockSpec((1, D_MODEL, tn), lambda l, j: (l, 0, j)),
            pl.BlockSpec((1, 1, tn), lambda l, j: (l, 0, j)),
        ],
        out_specs=pl.BlockSpec((1, 8, tn), lambda l, j: (l, 0, j)),
        compiler_params=_params("arbitrary", "arbitrary"),
        name="modulation",
    )(cvecs, w_mod, b_mod.reshape(DEPTH, 1, n))


def _modnorm_static(x_ref, nw_ref, sh_ref, sc_ref, h_scr, r0, n_rows, chunk=128):
    for c in range(n_rows // chunk):
        rows = slice(r0 + c * chunk, r0 + (c + 1) * chunk)
        x = x_ref[rows, :]
        ms = jnp.mean(x * x, axis=-1, keepdims=True)
        y = x * lax.rsqrt(ms + NORM_EPS) * nw_ref[...]
        h_scr[rows, :] = (y * (1.0 + sc_ref[0]) + sh_ref[0]).astype(BF16)


CACHE_NAMES = ("swa_k", "swa_v", "na_k", "na_v", "diff_k", "diff_v")
_CACHE_SRC = {"swa_k": ("skv", 0, 128), "swa_v": ("skv", 128, 128), "na_k": ("nk", 0, 512),
              "na_v": ("nv", 0, 512), "diff_k": ("dk", 0, 512), "diff_v": ("dv", 0, 512)}
_CACHE_ROW_SPLIT = {"diff_v": DIFF_HEADS}


def _cache_buffer_shape(name, nb, seq):
    width = _CACHE_SRC[name][2]
    split = _CACHE_ROW_SPLIT.get(name, 1)
    return (nb, DEPTH, seq * split, width // split)


def _inproj_kernel(*refs, tm, rope, n_cache, seq):
    n_in = 10 if rope else 8
    tab_ref = refs[0]
    (x_ref, sh_ref, sc_ref, nw_ref, wlo_ref, whi_ref, gain_ref, hsum_ref) = refs[1:9]
    cos_ref, sin_ref = (refs[9], refs[10]) if rope else (None, None)
    outs = refs[1 + n_in + n_cache:]
    p_ref, f_ref = outs[0], outs[1]
    cache_refs = outs[2:2 + n_cache]
    h_scr = outs[2 + n_cache]
    j = pl.program_id(1)
    cls = tab_ref[2 * N_TILES + j]

    def run_tile(mode, halves, first=False):
        w = {half: (wlo_ref if half == 0 else whi_ref)[0].astype(BF16) for half in halves}
        n_chunks = tm // ROW_CHUNK
        units = [(half, c) for half in halves for c in range(n_chunks)]

        def norm_chunk(c):
            _modnorm_static(x_ref, nw_ref, sh_ref, sc_ref, h_scr, c * ROW_CHUNK, ROW_CHUNK)

        if first:
            norm_chunk(0)

        def main(unit):
            half, c = unit
            if first and half == halves[0] and c + 1 < n_chunks:
                norm_chunk(c + 1)
            return _dot(h_scr[c * ROW_CHUNK:(c + 1) * ROW_CHUNK, :], w[half])

        def finish(unit, p):
            half, c = unit
            rows = slice(c * ROW_CHUNK, (c + 1) * ROW_CHUNK)
            cols = slice(half * PROJ_HALF, (half + 1) * PROJ_HALF)
            y = p
            if mode != CLS_PLAIN:
                ms = _dot((p * p).astype(BF16), hsum_ref[...])
                y = p * lax.rsqrt(ms + NORM_EPS) * gain_ref[0, :, cols]
                if rope and mode in (CLS_ROPE, CLS_MIXED):
                    lane = lax.broadcasted_iota(jnp.int32, y.shape, 1)
                    up = pltpu.roll(y, PROJ_HALF - 16, 1)
                    down = pltpu.roll(y, 16, 1)
                    partner = jnp.where((lane & 31) < 16, up, down)
                    y = y * cos_ref[rows, :] + partner * sin_ref[rows, :]
                if mode == CLS_MIXED:
                    lane = lax.broadcasted_iota(jnp.int32, y.shape, 1)
                    y = jnp.where(lane < 2 * HEAD_DIM, y, p)
            p_ref[rows, cols] = y.astype(BF16)
            f_ref[rows, cols] = y

        prev, p_prev = units[0], main(units[0])
        for unit in units[1:]:
            p_next = main(unit)
            finish(prev, p_prev)
            prev, p_prev = unit, p_next
        finish(prev, p_prev)

    assert _TILE_CLASS[TILE_NAMES[0]] == CLS_PLAIN

    @pl.when(j == 0)
    def _():
        run_tile(CLS_PLAIN, (0, 1), first=True)

    modes = (CLS_PLAIN, CLS_NORM, CLS_ROPE) if rope else (CLS_PLAIN, CLS_NORM)
    for mode in modes:
        cond = (cls == mode)
        if mode == CLS_PLAIN:
            cond = cond & (j > 0)
        if not rope and mode == CLS_NORM:
            cond = (cls == CLS_NORM) | (cls == CLS_ROPE)

        @pl.when(cond)
        def _(mode=mode):
            run_tile(mode, (0, 1))

    @pl.when(cls == CLS_MIXED)
    def _():
        run_tile(CLS_MIXED, (0,))
        hi = slice(PROJ_HALF, PROJ_TILE)
        p_ref[:, hi] = jnp.zeros((tm, PROJ_HALF), BF16)
        f_ref[:, hi] = jnp.zeros((tm, PROJ_HALF), F32)

    for name, c_ref in zip(CACHE_NAMES, cache_refs):
        tile, off, width = _CACHE_SRC[name]

        split = _CACHE_ROW_SPLIT.get(name, 1)

        @pl.when(j == TILE_NAMES.index(tile))
        def _(c_ref=c_ref, off=off, width=width, split=split):
            for b in range(tm // seq):
                rows = slice(b * seq, (b + 1) * seq)
                if split == 1:
                    c_ref[b, 0] = f_ref[rows, off:off + width]
                else:
                    piece = width // split
                    for h in range(split):
                        c_ref[b, 0, pl.ds(h, seq, stride=split), :] = f_ref[rows, off + h * piece:off + (h + 1) * piece]


def _in_projection(x, mods_l, mod_row, norm_w, w_in_bf, layer, tab, gain, hsum, rope_tabs, caches, *, tm, seq):
    n_tok = x.shape[0]
    rope = rope_tabs is not None
    caches = [] if caches is None else list(caches)
    n_cache = len(caches)

    in_specs = [
        pl.BlockSpec((tm, D_MODEL), lambda i, j, t: (i, 0)),
        pl.BlockSpec((1, 1, D_MODEL), lambda i, j, t: (mod_row(i) * MOD_CHUNKS + 0, 0, 0)),
        pl.BlockSpec((1, 1, D_MODEL), lambda i, j, t: (mod_row(i) * MOD_CHUNKS + 1, 0, 0)),
        pl.BlockSpec((1, D_MODEL), lambda i, j, t: (0, 0)),
        pl.BlockSpec((1, D_MODEL, PROJ_HALF), lambda i, j, t: (layer, 0, t[j])),
        pl.BlockSpec((1, D_MODEL, PROJ_HALF), lambda i, j, t: (layer, 0, t[N_TILES + j])),
        pl.BlockSpec((1, 1, PROJ_TILE), lambda i, j, t: (j, 0, 0)),
        pl.BlockSpec((PROJ_HALF, PROJ_HALF), lambda i, j, t: (0, 0)),
    ]
    args = [x, mods_l, mods_l, norm_w.reshape(1, D_MODEL), w_in_bf, w_in_bf, gain, hsum]
    if rope:
        cos_t, sin_t = rope_tabs
        in_specs += [pl.BlockSpec((tm, PROJ_HALF), lambda i, j, t: (0, 0)),
                     pl.BlockSpec((tm, PROJ_HALF), lambda i, j, t: (0, 0))]
        args += [cos_t, sin_t]
    n_in = len(args)
    in_specs += [pl.BlockSpec(memory_space=pl.ANY)] * n_cache
    args += caches
    out_specs = [
        pl.BlockSpec((tm, PROJ_TILE), lambda i, j, t: (i, jnp.maximum(j - P_FIRST, 0))),
        pl.BlockSpec((tm, PROJ_TILE), lambda i, j, t: (i, jnp.minimum(j, F_TILES))),
    ]
    out_shape = [jax.ShapeDtypeStruct((n_tok, P_TILES * PROJ_TILE), BF16),
                 jax.ShapeDtypeStruct((n_tok, (F_TILES + 1) * PROJ_TILE), F32)]
    for arr in caches:
        out_specs.append(pl.BlockSpec((tm // seq, 1) + arr.shape[2:], lambda i, j, t: (i, layer, 0, 0)))
        out_shape.append(jax.ShapeDtypeStruct(arr.shape, arr.dtype))
    res = pl.pallas_call(
        functools.partial(_inproj_kernel, tm=tm, rope=rope, n_cache=n_cache, seq=seq),
        out_shape=out_shape,
        grid_spec=pltpu.PrefetchScalarGridSpec(
            num_scalar_prefetch=1,
            grid=(n_tok // tm, N_TILES),
            in_specs=in_specs,
            out_specs=out_specs,
            scratch_shapes=[pltpu.VMEM((tm, D_MODEL), BF16)],
        ),
        input_output_aliases={1 + n_in + k: 2 + k for k in range(n_cache)},
        compiler_params=_params("arbitrary", "arbitrary"),
        name="in_projection_rope" if rope else "in_projection",
    )(tab, *args)
    return res[0], res[1], list(res[2:])


def _joint(scores):
    return scores[0] if len(scores) == 1 else jnp.concatenate(scores, axis=1)


PV_W = 2 * HEAD_DIM


def _weights_pv(segs, extra=None):
    s = _joint([sc for sc, _ in segs])
    m = jnp.max(s, axis=-1, keepdims=True)
    if extra is not None:
        m = jnp.maximum(m, extra)
    e = jnp.exp2(s - m).astype(BF16)
    acc, off = None, 0
    for sc, v in segs:
        n = sc.shape[1]
        v1 = jnp.concatenate([v, jnp.ones((n, PV_W), BF16)], axis=1)
        part = _dot(e[:, off:off + n], v1)
        acc = part if acc is None else acc + part
        off += n
    num, den = acc[:, :PV_W], acc[:, PV_W:]
    if extra is not None:
        den = den + jnp.exp2(extra - m)
    return num, den


def _softmax_pv(segs, extra=None):
    num, den = _weights_pv(segs, extra)
    return num / den


def _half_vecs():
    lane = lax.broadcasted_iota(jnp.int32, (1, 2 * HEAD_DIM), 1)
    lo = (lane < HEAD_DIM).astype(BF16)
    return lo, (1.0 - lo).astype(BF16)


def _pair_block(ref, m, rows=None, b=0):
    if rows is None:
        return ref[b, :, m * 128:(m + 1) * 128]
    return ref[b, rows, m * 128:(m + 1) * 128]


def _both_halves(q, vecs):
    return jnp.concatenate([q * vecs[0], q * vecs[1]], axis=0)


def _merge_pair(o_even, o_odd):
    lane = lax.broadcasted_iota(jnp.int32, o_even.shape, 1)
    return jnp.where(lane < HEAD_DIM, o_even, o_odd)


def _diff_lambda_val(dl_ref, lam_init):
    lp = dl_ref[...]
    s1 = jnp.sum(lp[0:1] * lp[1:2], axis=-1, keepdims=True)
    s2 = jnp.sum(lp[2:3] * lp[3:4], axis=-1, keepdims=True)
    return jnp.exp(s1) - jnp.exp(s2) + lam_init


def _diff_finish(o, subln_ref, lam_init):
    ms = jnp.mean(o * o, axis=-1, keepdims=True)
    return o * lax.rsqrt(ms + NORM_EPS) * subln_ref[...] * (1.0 - lam_init)


def _swap_halves_bf16(x):
    return pltpu.roll(x.astype(F32), HEAD_DIM, 1).astype(BF16)


def _gqa_queries(q_ref, hk, vecs, b=0, rows=None):
    parts = []
    for g in range(SWA_GROUP):
        h = hk * SWA_GROUP + g
        q = _pair_block(q_ref, h // 2, rows=rows, b=b)
        if h % 2 != hk:
            q = _swap_halves_bf16(q)
        parts.append(q * vecs[hk])
    return jnp.concatenate(parts, axis=0)


def _gqa_outputs(o, hk, rows):
    blocks = []
    for j in range(SWA_GROUP // 2):
        halves = []
        for g in (2 * j, 2 * j + 1):
            og = o[g * rows:(g + 1) * rows]
            halves.append(og if g % 2 == hk else pltpu.roll(og, HEAD_DIM, 1))
        blocks.append(_merge_pair(halves[0], halves[1]))
    return blocks


def _pipelined(items, depth=2):
    pending, outs = [], []
    for score_fn, finish_fn in items:
        pending.append((finish_fn, score_fn()))
        if len(pending) > depth:
            fn, s = pending.pop(0)
            outs.append(fn(s))
    for fn, s in pending:
        outs.append(fn(s))
    return outs


def _ctx_attn_kernel(sink_ref, sq_ref, skv_ref, nq_ref, nk_ref, nv_ref, dq_ref, dk_ref, dv_ref,
                     dl_ref, subln_ref, o_ref, *, t, lam_init, bb):
    lam = _diff_lambda_val(dl_ref, lam_init)
    vecs = _half_vecs()

    def swa_item(b, hk):
        def scores():
            return _dot_nt(_gqa_queries(sq_ref, hk, vecs, b=b), _pair_block(skv_ref, 0, b=b))

        def finish(s):
            sink = jnp.concatenate(
                [jnp.full((t, 1), sink_ref[hk * SWA_GROUP + g] * LOG2E, F32) for g in range(SWA_GROUP)], axis=0)
            return _gqa_outputs(_softmax_pv([(s, _pair_block(skv_ref, 1, b=b))], extra=sink), hk, t)
        return scores, finish

    def na_item(b, m):
        def scores():
            return _dot_nt(_both_halves(_pair_block(nq_ref, m, b=b), vecs), _pair_block(nk_ref, m, b=b))

        def finish(s):
            o = _softmax_pv([(s, _pair_block(nv_ref, m, b=b))])
            return [_merge_pair(o[:t], o[t:])]
        return scores, finish

    def diff_item(b, h):
        def scores():
            return _dot_nt(_both_halves(_pair_block(dq_ref, h, b=b), vecs), _pair_block(dk_ref, h, b=b))

        def finish(s):
            o = _softmax_pv([(s, _pair_block(dv_ref, h, b=b))])
            return [_diff_finish(o[:t] - lam * o[t:], subln_ref, lam_init)]
        return scores, finish

    items = []
    for make, n in ((swa_item, SWA_KV_HEADS), (na_item, NA_HEADS // 2), (diff_item, DIFF_HEADS)):
        items += [make(b, i) for i in range(n) for b in range(bb)]
    res = _pipelined(items, depth=2 * bb)
    for b in range(bb):
        pieces = [piece for r in res[b::bb] for piece in r]
        o_ref[b] = jnp.concatenate(pieces, axis=1).astype(BF16)


CTX_ROWS_PER_STEP = 4


def _pcol(name, t, bb=1):
    return pl.BlockSpec((bb, t, PROJ_TILE), lambda i, c=P_COL[name]: (i, 0, c))


def _ctx_attention(p, sink, dlam, subln, lam_init, *, nb, t):
    bb = CTX_ROWS_PER_STEP
    p3 = p.reshape(nb, t, P_TILES * PROJ_TILE)
    names = ("sq", "skv", "nq", "nk", "nv", "dq", "dk", "dv")
    out = pl.pallas_call(
        functools.partial(_ctx_attn_kernel, t=t, lam_init=lam_init, bb=bb),
        out_shape=jax.ShapeDtypeStruct((nb, t, 3 * GROUP_W), BF16),
        grid=(nb // bb,),
        in_specs=[pl.BlockSpec(memory_space=pltpu.SMEM)] + [_pcol(n, t, bb) for n in names] + [
            pl.BlockSpec((4, HEAD_DIM), lambda i: (0, 0)),
            pl.BlockSpec((1, DIFF_VDIM), lambda i: (0, 0)),
        ],
        out_specs=pl.BlockSpec((bb, t, 3 * GROUP_W), lambda i: (i, 0, 0)),
        compiler_params=_params("arbitrary"),
        name="ctx_attention",
    )(sink, *([p3] * len(names)), dlam, subln.reshape(1, DIFF_VDIM))
    return out.reshape(nb * t, 3 * GROUP_W)


Q_BLOCKS_PER_STEP = 4


def _swa_kernel(sink_ref, q_ref, kv_ref, ck_ref, cv_ref, o_ref, *, t, qb):
    span = 3 * qb
    vecs = _half_vecs()

    def item(j, hk):
        n = pl.program_id(1) * Q_BLOCKS_PER_STEP + j
        start = pl.multiple_of(jnp.clip((n - 1) * qb, 0, t - span), qb)
        win = pl.ds(start, span)
        qrows = slice(j * qb, (j + 1) * qb)

        def scores():
            qs = _gqa_queries(q_ref, hk, vecs, rows=qrows)
            return _dot_nt(qs, _pair_block(kv_ref, 0, rows=win)), _dot_nt(qs, ck_ref[0])

        def finish(s):
            row = (lax.broadcasted_iota(jnp.int32, (SWA_GROUP * qb, span), 0) & (qb - 1)) + n * qb
            col = lax.broadcasted_iota(jnp.int32, (SWA_GROUP * qb, span), 1) + start
            dist = row - col
            ok = (dist <= SWA_WINDOW) & (dist >= -SWA_WINDOW)
            sink = jnp.concatenate(
                [jnp.full((qb, 1), sink_ref[hk * SWA_GROUP + g] * LOG2E, F32) for g in range(SWA_GROUP)], axis=0)
            s_loc = jnp.where(ok, s[0], NEG_INF)
            o = _softmax_pv([(s_loc, _pair_block(kv_ref, 1, rows=win)), (s[1], cv_ref[0])], extra=sink)
            return _gqa_outputs(o, hk, qb)
        return scores, finish

    items = [item(j, hk) for hk in range(SWA_KV_HEADS) for j in range(Q_BLOCKS_PER_STEP)]
    res = _pipelined(items, depth=Q_BLOCKS_PER_STEP)
    for j in range(Q_BLOCKS_PER_STEP):
        pieces = [piece for r in res[j::Q_BLOCKS_PER_STEP] for piece in r]
        o_ref[0, j * qb:(j + 1) * qb, :] = jnp.concatenate(pieces, axis=1).astype(BF16)


def _swa_latent(p, ck, cv, layer, sink, *, nb, t):
    qb = 128
    step_rows = qb * Q_BLOCKS_PER_STEP
    p3 = p.reshape(nb, t, P_TILES * PROJ_TILE)
    out = pl.pallas_call(
        functools.partial(_swa_kernel, t=t, qb=qb),
        out_shape=jax.ShapeDtypeStruct((nb, t, GROUP_W), BF16),
        grid=(nb, t // step_rows),
        in_specs=[
            pl.BlockSpec(memory_space=pltpu.SMEM),
            pl.BlockSpec((1, step_rows, PROJ_TILE), lambda b, n: (b, n, P_COL["sq"])),
            pl.BlockSpec((1, t, PROJ_TILE), lambda b, n: (b, 0, P_COL["skv"])),
            pl.BlockSpec((1,) + ck.shape[1:], lambda b, n: (b * DEPTH + layer, 0, 0)),
            pl.BlockSpec((1,) + cv.shape[1:], lambda b, n: (b * DEPTH + layer, 0, 0)),
        ],
        out_specs=pl.BlockSpec((1, step_rows, GROUP_W), lambda b, n: (b, n, 0)),
        compiler_params=_params("arbitrary", "arbitrary"),
        name="swa_latent",
    )(sink, p3, p3, ck, cv)
    return out.reshape(nb * t, GROUP_W)


def _na_row_start(r, rows):
    kh = min(NA_KH, rows)
    return min(max(r - kh // 2, 0), rows - kh)


def _na_win_start(qt, rows):
    return min(max(_na_row_start(2 * qt, rows), 0), rows - NA_WIN_ROWS)


def _na_bias_kernel(rpb_ref, o_ref, *, rows):
    h = pl.program_id(0)
    n_dr, n_dc = 2 * NA_KH - 1, 2 * NA_KW - 1
    qi = lax.broadcasted_iota(jnp.int32, (GRID_W, GRID_W), 0)
    ki = lax.broadcasted_iota(jnp.int32, (GRID_W, GRID_W), 1)
    dc = jnp.clip(ki - qi + (NA_KW - 1), 0, n_dc - 1)
    cs = jnp.clip(qi - NA_KW // 2, 0, GRID_W - NA_KW)
    col_ok = (ki >= cs) & (ki < cs + NA_KW)
    neg = jnp.full((GRID_W, GRID_W), NEG_INF, F32)
    tabs = []
    for dr in range(n_dr):
        acc = jnp.zeros((GRID_W, GRID_W), F32)
        for c in range(n_dc):
            acc = jnp.where(dc == c, rpb_ref[(h * n_dr + dr) * n_dc + c], acc)
        tabs.append(jnp.where(col_ok, acc * LOG2E, neg))
    kh = min(NA_KH, rows)
    for qt in range(rows // 2):
        ws = _na_win_start(qt, rows)
        bands = []
        for qq in range(2):
            qr = 2 * qt + qq
            rs = _na_row_start(qr, rows)
            blks = []
            for kk in range(NA_WIN_ROWS):
                kr = ws + kk
                blks.append(tabs[kr - qr + NA_KH - 1] if rs <= kr < rs + kh else neg)
            bands.append(jnp.concatenate(blks, axis=1))
        o_ref[0, qt] = jnp.concatenate(bands, axis=0)


def _na_bias(rpb, rows):
    n_qt = rows // 2
    n_tab = rpb.shape[0] * rpb.shape[1]
    return pl.pallas_call(
        functools.partial(_na_bias_kernel, rows=rows),
        out_shape=jax.ShapeDtypeStruct((n_tab, n_qt, NA_QT, NA_WIN), F32),
        grid=(n_tab,),
        in_specs=[pl.BlockSpec(memory_space=pltpu.SMEM)],
        out_specs=pl.BlockSpec((1, n_qt, NA_QT, NA_WIN), lambda h: (h, 0, 0, 0)),
        compiler_params=_params("arbitrary"),
        name="na_bias",
    )(rpb.reshape(-1))


def _na_kernel(q_ref, k_ref, v_ref, ck_ref, cv_ref, bias_ref, o_ref, *, rows):
    vecs = _half_vecs()

    def item(j, m):
        qt = pl.program_id(1) * Q_BLOCKS_PER_STEP + j
        ws = jnp.clip(jnp.clip(2 * qt - NA_KH // 2, 0, rows - NA_KH), 0, rows - NA_WIN_ROWS)
        win = pl.ds(pl.multiple_of(ws * GRID_W, GRID_W), NA_WIN)
        qrows = slice(j * NA_QT, (j + 1) * NA_QT)

        def scores():
            q2 = _both_halves(_pair_block(q_ref, m, rows=qrows), vecs)
            return _dot_nt(q2, _pair_block(k_ref, m, rows=win)), _dot_nt(q2, _pair_block(ck_ref, m))

        def finish(s):
            v, cv = _pair_block(v_ref, m, rows=win), _pair_block(cv_ref, m)
            bias = jnp.concatenate([bias_ref[2 * m, j], bias_ref[2 * m + 1, j]], axis=0)
            o = _softmax_pv([(s[0] + bias, v), (s[1], cv)])
            return _merge_pair(o[:NA_QT], o[NA_QT:])
        return scores, finish

    items = [item(j, m) for m in range(NA_HEADS // 2) for j in range(Q_BLOCKS_PER_STEP)]
    res = _pipelined(items, depth=2 * Q_BLOCKS_PER_STEP)
    for j in range(Q_BLOCKS_PER_STEP):
        o_ref[0, j * NA_QT:(j + 1) * NA_QT, :] = jnp.concatenate(res[j::Q_BLOCKS_PER_STEP], axis=1).astype(BF16)


def _na_latent(p, ck, cv, layer, bias, *, nb, t):
    rows = t // GRID_W
    p3 = p.reshape(nb, t, P_TILES * PROJ_TILE)
    out = pl.pallas_call(
        functools.partial(_na_kernel, rows=rows),
        out_shape=jax.ShapeDtypeStruct((nb, t, GROUP_W), BF16),
        grid=(nb, t // (NA_QT * Q_BLOCKS_PER_STEP)),
        in_specs=[
            pl.BlockSpec((1, NA_QT * Q_BLOCKS_PER_STEP, PROJ_TILE), lambda i, n: (i, n, P_COL["nq"])),
            pl.BlockSpec((1, t, PROJ_TILE), lambda i, n: (i, 0, P_COL["nk"])),
            pl.BlockSpec((1, t, PROJ_TILE), lambda i, n: (i, 0, P_COL["nv"])),
            pl.BlockSpec((1,) + ck.shape[1:], lambda i, n: (i * DEPTH + layer, 0, 0)),
            pl.BlockSpec((1,) + cv.shape[1:], lambda i, n: (i * DEPTH + layer, 0, 0)),
            pl.BlockSpec((NA_HEADS, Q_BLOCKS_PER_STEP, NA_QT, NA_WIN), lambda i, n: (layer, n, 0, 0)),
        ],
        out_specs=pl.BlockSpec((1, NA_QT * Q_BLOCKS_PER_STEP, GROUP_W), lambda i, n: (i, n, 0)),
        compiler_params=_params("arbitrary", "arbitrary"),
        name="na_latent",
    )(p3, p3, p3, ck, cv, bias)
    return out.reshape(nb * t, GROUP_W)


def _diff_kernel(q_ref, k_ref, v_ref, ck_ref, cv_ref, dl_ref, subln_ref, o_ref, *, lam_init, tq):
    lam = _diff_lambda_val(dl_ref, lam_init)

    vecs = _half_vecs()

    def item(j, h):
        qrows = slice(j * tq, (j + 1) * tq)

        def scores():
            q2 = _both_halves(_pair_block(q_ref, h, rows=qrows), vecs)
            return _dot_nt(q2, _pair_block(k_ref, h)), _dot_nt(q2, _pair_block(ck_ref, h))

        def finish(s):
            o = _softmax_pv([(s[0], _pair_block(v_ref, h)), (s[1], _pair_block(cv_ref, h))])
            return _diff_finish(o[:tq] - lam * o[tq:], subln_ref, lam_init)
        return scores, finish

    items = [item(j, h) for h in range(DIFF_HEADS) for j in range(Q_BLOCKS_PER_STEP)]
    res = _pipelined(items, depth=Q_BLOCKS_PER_STEP)
    for j in range(Q_BLOCKS_PER_STEP):
        o_ref[0, j * tq:(j + 1) * tq, :] = jnp.concatenate(res[j::Q_BLOCKS_PER_STEP], axis=1).astype(BF16)


def _diff_latent(p, ck, cv, layer, dlam, subln, lam_init, *, nb, t):
    tq_sub = 256
    tq = tq_sub * Q_BLOCKS_PER_STEP
    p3 = p.reshape(nb, t, P_TILES * PROJ_TILE)
    out = pl.pallas_call(
        functools.partial(_diff_kernel, lam_init=lam_init, tq=tq_sub),
        out_shape=jax.ShapeDtypeStruct((nb, t, GROUP_W), BF16),
        grid=(nb, t // tq),
        in_specs=[
            pl.BlockSpec((1, tq, PROJ_TILE), lambda i, n: (i, n, P_COL["dq"])),
            pl.BlockSpec((1, t, PROJ_TILE), lambda i, n: (i, 0, P_COL["dk"])),
            pl.BlockSpec((1, t, PROJ_TILE), lambda i, n: (i, 0, P_COL["dv"])),
            pl.BlockSpec((1,) + ck.shape[1:], lambda i, n: (i * DEPTH + layer, 0, 0)),
            pl.BlockSpec((1,) + cv.shape[1:], lambda i, n: (i * DEPTH + layer, 0, 0)),
            pl.BlockSpec((4, HEAD_DIM), lambda i, n: (0, 0)),
            pl.BlockSpec((1, DIFF_VDIM), lambda i, n: (0, 0)),
        ],
        out_specs=pl.BlockSpec((1, tq, GROUP_W), lambda i, n: (i, n, 0)),
        compiler_params=_params("arbitrary", "arbitrary"),
        name="diff_latent",
    )(p3, p3, p3, ck, cv, dlam, subln.reshape(1, DIFF_VDIM))
    return out.reshape(nb * t, GROUP_W)


SUBLANES = 8


def _linear_scan(a, b, row, t, reverse, sa_ref, sb_ref, sc_ref):
    nt = t // SUBLANES
    sub = row & (SUBLANES - 1)

    def doubling(a, b, idx, n, steps):
        for s in steps:
            if reverse:
                ok = idx < n - s
                a_sh = jnp.where(ok, pltpu.roll(a, a.shape[0] - s, 0), 1.0)
                b_sh = jnp.where(ok, pltpu.roll(b, b.shape[0] - s, 0), 0.0)
            else:
                ok = idx >= s
                a_sh = jnp.where(ok, pltpu.roll(a, s, 0), 1.0)
                b_sh = jnp.where(ok, pltpu.roll(b, s, 0), 0.0)
            b = a * b_sh + b
            a = a * a_sh
        return a, b

    tiles = (nt, SUBLANES, LRU_CT)
    sub3 = lax.broadcasted_iota(jnp.int32, tiles, 1)

    def tile_doubling(a, b):
        for s in (1, 2, 4):
            ok = (sub3 < SUBLANES - s) if reverse else (sub3 >= s)
            sh = SUBLANES - s if reverse else s
            a_sh = jnp.where(ok, pltpu.roll(a, sh, 1), 1.0)
            b_sh = jnp.where(ok, pltpu.roll(b, sh, 1), 0.0)
            b = a * b_sh + b
            a = a * a_sh
        return a, b

    a, b = tile_doubling(a.reshape(tiles), b.reshape(tiles))
    a, b = a.reshape(t, LRU_CT), b.reshape(t, LRU_CT)
    sa_ref[...] = a
    sb_ref[...] = b
    last = 0 if reverse else SUBLANES - 1
    at = sa_ref[pl.ds(last, nt, stride=SUBLANES), :]
    bt = sb_ref[pl.ds(last, nt, stride=SUBLANES), :]
    rowt = lax.broadcasted_iota(jnp.int32, (nt, LRU_CT), 0)
    steps, s = [], 1
    while s < nt:
        steps.append(s)
        s *= 2
    _, bt = doubling(at, bt, rowt, nt, steps)
    if reverse:
        carry = jnp.where(rowt < nt - 1, pltpu.roll(bt, nt - 1, 0), 0.0)
    else:
        carry = jnp.where(rowt >= 1, pltpu.roll(bt, 1, 0), 0.0)
    for r in range(SUBLANES):
        sc_ref[pl.ds(r, nt, stride=SUBLANES), :] = carry
    return b + a * sc_ref[...]


def _lru_kernel(x_ref, g_ref, cw_ref, cb_ref, w_ref, gb_ref, lam_ref, h0_ref, y_ref, st_ref, *scratch, t, cpt,
                has_h0):
    row = lax.broadcasted_iota(jnp.int32, (t, LRU_CT), 0)
    for ci in range(cpt):
        lanes = slice(ci * LRU_CT, (ci + 1) * LRU_CT)
        x = x_ref[0, :, lanes]
        u = cb_ref[:, lanes] + cw_ref[2:3, lanes] * x
        for tap, off in ((0, -2), (1, -1), (3, 1)):
            xs = pltpu.roll(x, (-off) % t, 0)
            ok = (row + off >= 0) & (row + off < t)
            u = u + cw_ref[tap:tap + 1, lanes] * jnp.where(ok, xs, 0.0)
        gates = _dot(u.astype(BF16), w_ref[ci]) + gb_ref[ci]
        total = None
        for d in range(2):
            r = 1.0 / (1.0 + jnp.exp2(gates[:, (2 * d) * LRU_CT:(2 * d + 1) * LRU_CT]))
            ig = 1.0 / (1.0 + jnp.exp2(gates[:, (2 * d + 1) * LRU_CT:(2 * d + 2) * LRU_CT]))
            nl = -lam_ref[d:d + 1, lanes]
            rate = LRU_C * (jnp.maximum(nl, 0.0) + jnp.log1p(jnp.exp(-jnp.abs(nl))))
            a = jnp.exp2(r * (rate * -LOG2E))
            bx = jnp.sqrt(jnp.tanh(r * rate) * (a * a + 1.0)) * (ig * u)
            if has_h0:
                edge = t - 1 if d == 1 else 0
                bx = jnp.where(row == edge, bx + a * h0_ref[0, d:d + 1, lanes], bx)
            sa_ref, sb_ref, sc_ref = scratch[3 * (2 * ci + d):3 * (2 * ci + d) + 3]
            bx = _linear_scan(a, bx, row, t, d == 1, sa_ref, sb_ref, sc_ref)
            fin = t - 1 if d == 0 else 0
            st_ref[0, d:d + 1, lanes] = bx[fin:fin + 1, :]
            total = bx if total is None else total + bx
        g = g_ref[0, :, lanes]
        gelu = 0.5 * g * (1.0 + jnp.tanh(math.sqrt(2.0 / math.pi) * (g + 0.044715 * (g * g * g))))
        y_ref[0, :, lanes] = (total * gelu).astype(BF16)


def _lru_mixer(f, conv_w, conv_b, w_gates, b_gates, lam, h0, *, nb, t, cpt):
    has_h0 = h0 is not None
    if not has_h0:
        h0 = jnp.zeros((nb, 2, LRU_WIDTH), F32)
    f3 = f.reshape(nb, t, f.shape[1])
    nct = LRU_WIDTH // LRU_CT
    wide = cpt * LRU_CT
    y, st = pl.pallas_call(
        functools.partial(_lru_kernel, t=t, cpt=cpt, has_h0=has_h0),
        out_shape=[jax.ShapeDtypeStruct((nb, t, LRU_WIDTH), BF16),
                   jax.ShapeDtypeStruct((nb, 2, LRU_WIDTH), F32)],
        grid=(nb, nct // cpt),
        in_specs=[
            pl.BlockSpec((1, t, wide), lambda i, c: (i, 0, c)),
            pl.BlockSpec((1, t, wide), lambda i, c: (i, 0, nct // cpt + c)),
            pl.BlockSpec((CONV_W, wide), lambda i, c: (0, c)),
            pl.BlockSpec((1, wide), lambda i, c: (0, c)),
            pl.BlockSpec((cpt, LRU_CT, 4 * LRU_CT), lambda i, c: (c, 0, 0)),
            pl.BlockSpec((cpt, 1, 4 * LRU_CT), lambda i, c: (c, 0, 0)),
            pl.BlockSpec((2, wide), lambda i, c: (0, c)),
            pl.BlockSpec((1, 2, wide), lambda i, c: (i, 0, c)),
        ],
        out_specs=[pl.BlockSpec((1, t, wide), lambda i, c: (i, 0, c)),
                   pl.BlockSpec((1, 2, wide), lambda i, c: (i, 0, c))],
        scratch_shapes=[pltpu.VMEM((t, LRU_CT), F32)] * (6 * cpt),
        compiler_params=_params("arbitrary", "arbitrary"),
        name="lru_mixer",
    )(f3, f3, conv_w, conv_b.reshape(1, LRU_WIDTH), w_gates, b_gates, lam, h0)
    return y.reshape(nb * t, LRU_WIDTH), st


def _lru_gate_weights(wa, ba, wx, bx):
    nct = LRU_WIDTH // LRU_CT
    bpt = LRU_CT // LRU_BLOCK_W
    eye = jnp.eye(bpt, dtype=F32)

    def tile_w(w):
        w4 = w.reshape(nct, bpt, LRU_BLOCK_W, LRU_BLOCK_W)
        return jnp.einsum("cnij,nm->cnimj", w4, eye).reshape(nct, LRU_CT, LRU_CT)

    w = jnp.concatenate([tile_w(wa[0]), tile_w(wx[0]), tile_w(wa[1]), tile_w(wx[1])], axis=-1)
    b = jnp.concatenate([v.reshape(nct, 1, LRU_CT) for v in (ba[0], bx[0], ba[1], bx[1])], axis=-1)
    return (w * -LOG2E).astype(BF16), b * -LOG2E


def _outproj_kernel(*refs, n_in):
    ins = refs[:n_in]
    w_ref, x_ref, g_ref, o_ref = refs[n_in:]
    acc = None
    off = 0
    for m_ref in ins:
        width = m_ref.shape[1]
        part = _dot(m_ref[...], w_ref[0, off:off + width, :])
        acc = part if acc is None else acc + part
        off += width
    o_ref[...] = x_ref[...] + g_ref[0] * acc


def _out_projection(mixes, w_out_bf, layer, x, mods_l, mod_row, *, tm):
    n_tok = x.shape[0]
    return pl.pallas_call(
        functools.partial(_outproj_kernel, n_in=len(mixes)),
        out_shape=jax.ShapeDtypeStruct((n_tok, D_MODEL), F32),
        grid=(n_tok // tm,),
        in_specs=[pl.BlockSpec((tm, m.shape[1]), lambda i: (i, 0)) for m in mixes] + [
            pl.BlockSpec((1, D_MODEL, D_MODEL), lambda i: (layer, 0, 0)),
            pl.BlockSpec((tm, D_MODEL), lambda i: (i, 0)),
            pl.BlockSpec((1, 1, D_MODEL), lambda i: (mod_row(i) * MOD_CHUNKS + 2, 0, 0)),
        ],
        out_specs=pl.BlockSpec((tm, D_MODEL), lambda i: (i, 0)),
        compiler_params=_params("arbitrary"),
        name="out_projection",
    )(*mixes, w_out_bf, x, mods_l)


def _ffn_kernel(x_ref, sh_ref, sc_ref, g_ref, nw_ref, wg_ref, wu_ref, wd_ref, o_ref, h_scr, *, tm):
    k = pl.program_id(1)

    def step(first, last=False):
        wg = wg_ref[0].astype(BF16)
        wu = wu_ref[0].astype(BF16)
        wd = wd_ref[0].astype(BF16)
        chunk = ROW_CHUNK if first else FFN_ROW_CHUNK

        def norm_chunk(c):
            _modnorm_static(x_ref, nw_ref, sh_ref, sc_ref, h_scr, c * chunk, chunk)

        n_chunks = tm // chunk

        def gate_up(c):
            if first and c + 1 < n_chunks:
                norm_chunk(c + 1)
            h = h_scr[c * chunk:(c + 1) * chunk, :]
            return _dot(h, wg), _dot(h, wu)

        def down(c, gu):
            g, u = gu
            act = (g * _sigmoid(g)) * u
            res = _dot(act.astype(BF16), wd)
            rows = slice(c * chunk, (c + 1) * chunk)
            if first:
                o_ref[rows, :] = res
            elif last:
                o_ref[rows, :] = x_ref[rows, :] + g_ref[0] * (o_ref[rows, :] + res)
            else:
                o_ref[rows, :] += res

        if first:
            norm_chunk(0)
        pending = gate_up(0)
        for c in range(1, n_chunks):
            nxt = gate_up(c)
            down(c - 1, pending)
            pending = nxt
        down(n_chunks - 1, pending)

    @pl.when(k == 0)
    def _():
        step(True)

    n_k = pl.num_programs(1)

    @pl.when((k > 0) & (k < n_k - 1))
    def _():
        step(False)

    @pl.when(k == n_k - 1)
    def _():
        step(False, last=True)


def _ffn(x, mods_l, mod_row, norm_w, wg, wu, wd, layer, *, tm, th):
    n_tok = x.shape[0]
    assert FFN_HIDDEN // th >= 2
    return pl.pallas_call(
        functools.partial(_ffn_kernel, tm=tm),
        out_shape=jax.ShapeDtypeStruct((n_tok, D_MODEL), F32),
        grid=(n_tok // tm, FFN_HIDDEN // th),
        in_specs=[
            pl.BlockSpec((tm, D_MODEL), lambda i, k: (i, 0), pipeline_mode=pl.Buffered(1)),
            pl.BlockSpec((1, 1, D_MODEL), lambda i, k: (mod_row(i) * MOD_CHUNKS + 3, 0, 0)),
            pl.BlockSpec((1, 1, D_MODEL), lambda i, k: (mod_row(i) * MOD_CHUNKS + 4, 0, 0)),
            pl.BlockSpec((1, 1, D_MODEL), lambda i, k: (mod_row(i) * MOD_CHUNKS + 5, 0, 0)),
            pl.BlockSpec((1, D_MODEL), lambda i, k: (0, 0)),
            pl.BlockSpec((1, D_MODEL, th), lambda i, k: (layer, 0, k)),
            pl.BlockSpec((1, D_MODEL, th), lambda i, k: (layer, 0, k)),
            pl.BlockSpec((1, th, D_MODEL), lambda i, k: (layer, k, 0)),
        ],
        out_specs=pl.BlockSpec((tm, D_MODEL), lambda i, k: (i, 0)),
        scratch_shapes=[pltpu.VMEM((tm, D_MODEL), BF16)],
        compiler_params=_params("arbitrary", "arbitrary"),
        name="ffn",
    )(x, mods_l, mods_l, mods_l, norm_w.reshape(1, D_MODEL), wg, wu, wd)


def _rope_tables(t_lat):
    nf = HEAD_DIM // 4
    tok = np.arange(t_lat)
    pos = np.stack([tok // GRID_W, tok % GRID_W], axis=-1).astype(np.float32)
    inv = np.float32(ROPE_BASE) ** (-np.arange(nf, dtype=np.float32) / np.float32(nf))
    ang = (pos[:, :, None] * inv.astype(np.float32)).astype(np.float32)
    cos = np.cos(ang.astype(np.float64)).astype(np.float32)
    sin = np.sin(ang.astype(np.float64)).astype(np.float32)
    cos_h = np.concatenate([cos[:, 0], cos[:, 0], cos[:, 1], cos[:, 1]], axis=-1)
    sin_h = np.concatenate([-sin[:, 0], sin[:, 0], -sin[:, 1], sin[:, 1]], axis=-1)
    reps = PROJ_HALF // HEAD_DIM
    return jnp.asarray(np.tile(cos_h, (1, reps))), jnp.asarray(np.tile(sin_h, (1, reps)))


def _tile_table():
    lo = [_FIRST_BLOCK[n] for n in TILE_NAMES]
    hi = [b + 1 for b in lo]
    cls = [_TILE_CLASS[n] for n in TILE_NAMES]
    return jnp.asarray(np.array(lo + hi + cls, np.int32))


def _tile_gains(qk_gain_l):
    def tiled(g, n):
        return jnp.tile(g, n // HEAD_DIM)
    ones = jnp.ones((PROJ_TILE,), F32)
    per_tile = {
        "lx": ones, "lg": ones, "nv": ones, "dv": ones,
        "nk": tiled(qk_gain_l[1, 1], PROJ_TILE), "dk": tiled(qk_gain_l[2, 1], PROJ_TILE),
        "skv": jnp.concatenate([tiled(qk_gain_l[0, 1], 128), jnp.ones((PROJ_TILE - 128,), F32)]),
        "sq": tiled(qk_gain_l[0, 0], PROJ_TILE) * QK_SCALE,
        "nq": tiled(qk_gain_l[1, 0], PROJ_TILE) * QK_SCALE,
        "dq": tiled(qk_gain_l[2, 0], PROJ_TILE) * QK_SCALE,
    }
    return jnp.stack([per_tile[n] for n in TILE_NAMES]).reshape(N_TILES, 1, PROJ_TILE)


def _lambda_init(layer):
    return 0.8 - 0.6 * math.exp(-0.3 * layer)


def kernel(x_prompt, x_sample, cache_swa_k, cache_swa_v, cache_na_k, cache_na_v, cache_diff_k, cache_diff_v, state_lru, c, c_ctx, norm_mix, norm_ffn, w_mod, b_mod, w_in, w_out, qk_gain, swa_sink, na_rpb, diff_lambda, diff_subln, conv_w, conv_b, lru_wa, lru_ba, lru_wx, lru_bx, lru_L, w_ffn_gate, w_ffn_up, w_ffn_down):
    bc, s_ctx, _ = x_prompt.shape
    bd, t_lat, _ = x_sample.shape
    n_ctx, n_lat = bc * s_ctx, bd * t_lat
    rows = t_lat // GRID_W
    p_ctx = cache_swa_k.shape[2]

    cvecs = jnp.concatenate([c_ctx[None, :], c, jnp.zeros((8 - 1 - bd, D_MODEL), F32)], axis=0)
    mods = _modulation(cvecs, w_mod, b_mod).reshape(DEPTH, 8 * MOD_CHUNKS, 1, D_MODEL)

    hsum = jnp.asarray(np.kron(np.eye(PROJ_HALF // HEAD_DIM, dtype=np.float32),
                               np.full((HEAD_DIM, HEAD_DIM), 1.0 / HEAD_DIM, np.float32)), BF16)
    rope_tabs = _rope_tables(t_lat)
    tab = _tile_table()

    tm = 1024
    tm_out = 512
    ctx_row = lambda tile: (lambda i: 0)
    lat_row = lambda tile: (lambda i: 1 + (i * tile) // t_lat)

    xc = x_prompt.reshape(n_ctx, D_MODEL)
    xs = x_sample.reshape(n_lat, D_MODEL)
    caches = [jnp.zeros(_cache_buffer_shape(name, bc, s_ctx), F32) for name in CACHE_NAMES]
    lru_states = []
    w_in_bf = w_in
    w_out_bf = w_out.astype(BF16)

    def cached(arr):
        return arr.astype(BF16).reshape(bd * DEPTH, p_ctx, -1)

    c_swa_k, c_swa_v = cached(cache_swa_k), cached(cache_swa_v)
    c_na_k, c_na_v = cached(cache_na_k), cached(cache_na_v)
    c_diff_k, c_diff_v = cached(cache_diff_k), cached(cache_diff_v)
    na_bias = _na_bias(na_rpb, rows)

    for l in range(DEPTH):
        lam_init = _lambda_init(l)
        mods_l = mods[l]
        gain = _tile_gains(qk_gain[l])
        w_gates, b_gates = _lru_gate_weights(lru_wa[l], lru_ba[l], lru_wx[l], lru_bx[l])
        ffn_w = (w_ffn_gate, w_ffn_up, w_ffn_down, l)

        p, f, caches = _in_projection(xc, mods_l, ctx_row(tm), norm_mix[l], w_in_bf, l, tab, gain, hsum,
                                      None, caches, tm=tm, seq=s_ctx)
        mix3 = _ctx_attention(p, swa_sink[l], diff_lambda[l], diff_subln[l], lam_init, nb=bc, t=s_ctx)
        od, st = _lru_mixer(f, conv_w[l], conv_b[l], w_gates, b_gates, lru_L[l], None,
                            nb=bc, t=s_ctx, cpt=4)
        lru_states.append(st)
        xc = _out_projection([mix3, od], w_out_bf, l, xc, mods_l, ctx_row(tm_out), tm=tm_out)
        xc = _ffn(xc, mods_l, ctx_row(tm), norm_ffn[l], *ffn_w, tm=tm, th=512)

        p, f, _ = _in_projection(xs, mods_l, lat_row(tm), norm_mix[l], w_in_bf, l, tab, gain, hsum,
                                 rope_tabs, None, tm=tm, seq=t_lat)
        oa = _swa_latent(p, c_swa_k, c_swa_v, l, swa_sink[l], nb=bd, t=t_lat)
        ob = _na_latent(p, c_na_k, c_na_v, l, na_bias, nb=bd, t=t_lat)
        oc = _diff_latent(p, c_diff_k, c_diff_v, l, diff_lambda[l], diff_subln[l], lam_init, nb=bd, t=t_lat)
        od, _ = _lru_mixer(f, conv_w[l], conv_b[l], w_gates, b_gates, lru_L[l], state_lru[:, l],
                           nb=bd, t=t_lat, cpt=2)
        xs = _out_projection([oa, ob, oc, od], w_out_bf, l, xs, mods_l, lat_row(tm_out), tm=tm_out)
        xs = _ffn(xs, mods_l, lat_row(tm), norm_ffn[l], *ffn_w, tm=tm, th=512)

    swa_k, swa_v, na_k, na_v, diff_k, diff_v = caches
    return (xc.reshape(bc, s_ctx, D_MODEL), xs.reshape(bd, t_lat, D_MODEL),
            swa_k.reshape(bc, DEPTH, s_ctx, SWA_KV_HEADS, HEAD_DIM),
            swa_v.reshape(bc, DEPTH, s_ctx, SWA_KV_HEADS, HEAD_DIM),
            na_k.reshape(bc, DEPTH, s_ctx, NA_HEADS, HEAD_DIM),
            na_v.reshape(bc, DEPTH, s_ctx, NA_HEADS, HEAD_DIM),
            diff_k.reshape(bc, DEPTH, s_ctx, DIFF_HEADS, 2, HEAD_DIM),
            diff_v.reshape(bc, DEPTH, s_ctx, DIFF_HEADS, DIFF_VDIM),
            jnp.stack(lru_states, axis=1))
```

```python
import functools
import math

import jax
import jax.numpy as jnp
import numpy as np
from jax import lax
from jax.experimental import pallas as pl
from jax.experimental.pallas import tpu as pltpu

F32 = jnp.float32
BF16 = jnp.bfloat16

D_MODEL = 2048
DEPTH = 2
GRID_W = 64
HEAD_DIM = 64
GROUP_W = 512
SWA_HEADS = 8
SWA_KV_HEADS = 2
SWA_GROUP = 4
SWA_WINDOW = 128
NA_HEADS = 8
NA_KH = 8
NA_KW = 16
DIFF_HEADS = 4
DIFF_VDIM = 128
LRU_WIDTH = 512
LRU_BLOCKS = 8
LRU_BLOCK_W = 64
LRU_C = 8.0
CONV_W = 4
FFN_HIDDEN = 5632
ROPE_BASE = 10000.0
NORM_EPS = 1e-6
NEG_INF = -1e30
MOD_CHUNKS = 6
PROJ_W = 4864
LOG2E = math.log2(math.e)
QK_SCALE = HEAD_DIM ** -0.5 * LOG2E

V7X_VMEM_LIMIT = 60 * 1024 * 1024

PROJ_HALF = 256
PROJ_TILE = 2 * PROJ_HALF
TILE_NAMES = ("lx", "lg", "nk", "dk", "nv", "dv", "skv", "sq", "nq", "dq")
N_TILES = len(TILE_NAMES)
_FIRST_BLOCK = {"sq": 0, "skv": 2, "nq": 3, "nk": 5, "nv": 7, "dq": 9, "dk": 11, "dv": 13, "lx": 15, "lg": 17}
CLS_PLAIN, CLS_NORM, CLS_ROPE, CLS_MIXED = 0, 1, 2, 3
_TILE_CLASS = {"lx": CLS_PLAIN, "lg": CLS_PLAIN, "nv": CLS_PLAIN, "dv": CLS_PLAIN,
               "nk": CLS_NORM, "nq": CLS_NORM, "dk": CLS_ROPE, "sq": CLS_ROPE, "dq": CLS_ROPE,
               "skv": CLS_MIXED}
P_FIRST = 2
P_TILES = N_TILES - P_FIRST
P_COL = {name: TILE_NAMES.index(name) - P_FIRST for name in TILE_NAMES[P_FIRST:]}
F_TILES = 2

NA_QT = 128
NA_WIN_ROWS = 10
NA_WIN = NA_WIN_ROWS * GRID_W
LRU_CT = 128
ROW_CHUNK = 256
FFN_ROW_CHUNK = 512


def _sigmoid(x):
    return 1.0 / (1.0 + jnp.exp(-x))


def _dot(a, b):
    return jnp.dot(a, b, preferred_element_type=F32)


def _dot_nt(a, b):
    return lax.dot_general(a, b, (((1,), (1,)), ((), ())), preferred_element_type=F32)


def _params(*sem):
    return pltpu.CompilerParams(dimension_semantics=sem, vmem_limit_bytes=V7X_VMEM_LIMIT)


def _mod_kernel(c_ref, w_ref, b_ref, o_ref):
    cv = c_ref[...]
    s = cv * _sigmoid(cv)
    o_ref[0] = _dot(s.astype(BF16), w_ref[0].astype(BF16)) + b_ref[0]


def _modulation(cvecs, w_mod, b_mod):
    tn = 1024
    n = MOD_CHUNKS * D_MODEL
    return pl.pallas_call(
        _mod_kernel,
        out_shape=jax.ShapeDtypeStruct((DEPTH, 8, n), F32),
        grid=(DEPTH, n // tn),
        in_specs=[
            pl.BlockSpec((8, D_MODEL), lambda l, j: (0, 0)),
            pl.BlockSpec((1, D_MODEL, tn), lambda l, j: (l, 0, j)),
            pl.BlockSpec((1, 1, tn), lambda l, j: (l, 0, j)),
        ],
        out_specs=pl.BlockSpec((1, 8, tn), lambda l, j: (l, 0, j)),
        compiler_params=_params("arbitrary", "arbitrary"),
        name="modulation",
    )(cvecs, w_mod, b_mod.reshape(DEPTH, 1, n))


def _modnorm_static(x_ref, nw_ref, sh_ref, sc_ref, h_scr, r0, n_rows, chunk=128):
    for c in range(n_rows // chunk):
        rows = slice(r0 + c * chunk, r0 + (c + 1) * chunk)
        x = x_ref[rows, :]
        ms = jnp.mean(x * x, axis=-1, keepdims=True)
        y = x * lax.rsqrt(ms + NORM_EPS) * nw_ref[...]
        h_scr[rows, :] = (y * (1.0 + sc_ref[0]) + sh_ref[0]).astype(BF16)


CACHE_NAMES = ("swa_k", "swa_v", "na_k", "na_v", "diff_k", "diff_v")
_CACHE_SRC = {"swa_k": ("skv", 0, 128), "swa_v": ("skv", 128, 128), "na_k": ("nk", 0, 512),
              "na_v": ("nv", 0, 512), "diff_k": ("dk", 0, 512), "diff_v": ("dv", 0, 512)}
_CACHE_ROW_SPLIT = {"diff_v": DIFF_HEADS}


def _cache_buffer_shape(name, nb, seq):
    width = _CACHE_SRC[name][2]
    split = _CACHE_ROW_SPLIT.get(name, 1)
    return (nb, DEPTH, seq * split, width // split)


def _inproj_kernel(*refs, tm, rope, n_cache, seq):
    n_in = 10 if rope else 8
    tab_ref = refs[0]
    (x_ref, sh_ref, sc_ref, nw_ref, wlo_ref, whi_ref, gain_ref, hsum_ref) = refs[1:9]
    cos_ref, sin_ref = (refs[9], refs[10]) if rope else (None, None)
    outs = refs[1 + n_in + n_cache:]
    p_ref, f_ref = outs[0], outs[1]
    cache_refs = outs[2:2 + n_cache]
    h_scr = outs[2 + n_cache]
    j = pl.program_id(1)
    cls = tab_ref[2 * N_TILES + j]

    def run_tile(mode, halves, first=False):
        w = {half: (wlo_ref if half == 0 else whi_ref)[0].astype(BF16) for half in halves}
        n_chunks = tm // ROW_CHUNK
        units = [(half, c) for half in halves for c in range(n_chunks)]

        def norm_chunk(c):
            _modnorm_static(x_ref, nw_ref, sh_ref, sc_ref, h_scr, c * ROW_CHUNK, ROW_CHUNK)

        if first:
            norm_chunk(0)

        def main(unit):
            half, c = unit
            if first and half == halves[0] and c + 1 < n_chunks:
                norm_chunk(c + 1)
            return _dot(h_scr[c * ROW_CHUNK:(c + 1) * ROW_CHUNK, :], w[half])

        def finish(unit, p):
            half, c = unit
            rows = slice(c * ROW_CHUNK, (c + 1) * ROW_CHUNK)
            cols = slice(half * PROJ_HALF, (half + 1) * PROJ_HALF)
            y = p
            if mode != CLS_PLAIN:
                ms = _dot((p * p).astype(BF16), hsum_ref[...])
                y = p * lax.rsqrt(ms + NORM_EPS) * gain_ref[0, :, cols]
                if rope and mode in (CLS_ROPE, CLS_MIXED):
                    lane = lax.broadcasted_iota(jnp.int32, y.shape, 1)
                    up = pltpu.roll(y, PROJ_HALF - 16, 1)
                    down = pltpu.roll(y, 16, 1)
                    partner = jnp.where((lane & 31) < 16, up, down)
                    y = y * cos_ref[rows, :] + partner * sin_ref[rows, :]
                if mode == CLS_MIXED:
                    lane = lax.broadcasted_iota(jnp.int32, y.shape, 1)
                    y = jnp.where(lane < 2 * HEAD_DIM, y, p)
            p_ref[rows, cols] = y.astype(BF16)
            f_ref[rows, cols] = y

        prev, p_prev = units[0], main(units[0])
        for unit in units[1:]:
            p_next = main(unit)
            finish(prev, p_prev)
            prev, p_prev = unit, p_next
        finish(prev, p_prev)

    assert _TILE_CLASS[TILE_NAMES[0]] == CLS_PLAIN

    @pl.when(j == 0)
    def _():
        run_tile(CLS_PLAIN, (0, 1), first=True)

    modes = (CLS_PLAIN, CLS_NORM, CLS_ROPE) if rope else (CLS_PLAIN, CLS_NORM)
    for mode in modes:
        cond = (cls == mode)
        if mode == CLS_PLAIN:
            cond = cond & (j > 0)
        if not rope and mode == CLS_NORM:
            cond = (cls == CLS_NORM) | (cls == CLS_ROPE)

        @pl.when(cond)
        def _(mode=mode):
            run_tile(mode, (0, 1))

    @pl.when(cls == CLS_MIXED)
    def _():
        run_tile(CLS_MIXED, (0,))
        hi = slice(PROJ_HALF, PROJ_TILE)
        p_ref[:, hi] = jnp.zeros((tm, PROJ_HALF), BF16)
        f_ref[:, hi] = jnp.zeros((tm, PROJ_HALF), F32)

    for name, c_ref in zip(CACHE_NAMES, cache_refs):
        tile, off, width = _CACHE_SRC[name]

        split = _CACHE_ROW_SPLIT.get(name, 1)

        @pl.when(j == TILE_NAMES.index(tile))
        def _(c_ref=c_ref, off=off, width=width, split=split):
            for b in range(tm // seq):
                rows = slice(b * seq, (b + 1) * seq)
                if split == 1:
                    c_ref[b, 0] = f_ref[rows, off:off + width]
                else:
                    piece = width // split
                    for h in range(split):
                        c_ref[b, 0, pl.ds(h, seq, stride=split), :] = f_ref[rows, off + h * piece:off + (h + 1) * piece]


def _in_projection(x, mods_l, mod_row, norm_w, w_in_bf, layer, tab, gain, hsum, rope_tabs, caches, *, tm, seq):
    n_tok = x.shape[0]
    rope = rope_tabs is not None
    caches = [] if caches is None else list(caches)
    n_cache = len(caches)

    in_specs = [
        pl.BlockSpec((tm, D_MODEL), lambda i, j, t: (i, 0)),
        pl.BlockSpec((1, 1, D_MODEL), lambda i, j, t: (mod_row(i) * MOD_CHUNKS + 0, 0, 0)),
        pl.BlockSpec((1, 1, D_MODEL), lambda i, j, t: (mod_row(i) * MOD_CHUNKS + 1, 0, 0)),
        pl.BlockSpec((1, D_MODEL), lambda i, j, t: (0, 0)),
        pl.BlockSpec((1, D_MODEL, PROJ_HALF), lambda i, j, t: (layer, 0, t[j])),
        pl.BlockSpec((1, D_MODEL, PROJ_HALF), lambda i, j, t: (layer, 0, t[N_TILES + j])),
        pl.BlockSpec((1, 1, PROJ_TILE), lambda i, j, t: (j, 0, 0)),
        pl.BlockSpec((PROJ_HALF, PROJ_HALF), lambda i, j, t: (0, 0)),
    ]
    args = [x, mods_l, mods_l, norm_w.reshape(1, D_MODEL), w_in_bf, w_in_bf, gain, hsum]
    if rope:
        cos_t, sin_t = rope_tabs
        in_specs += [pl.BlockSpec((tm, PROJ_HALF), lambda i, j, t: (0, 0)),
                     pl.BlockSpec((tm, PROJ_HALF), lambda i, j, t: (0, 0))]
        args += [cos_t, sin_t]
    n_in = len(args)
    in_specs += [pl.BlockSpec(memory_space=pl.ANY)] * n_cache
    args += caches
    out_specs = [
        pl.BlockSpec((tm, PROJ_TILE), lambda i, j, t: (i, jnp.maximum(j - P_FIRST, 0))),
        pl.BlockSpec((tm, PROJ_TILE), lambda i, j, t: (i, jnp.minimum(j, F_TILES))),
    ]
    out_shape = [jax.ShapeDtypeStruct((n_tok, P_TILES * PROJ_TILE), BF16),
                 jax.ShapeDtypeStruct((n_tok, (F_TILES + 1) * PROJ_TILE), F32)]
    for arr in caches:
        out_specs.append(pl.BlockSpec((tm // seq, 1) + arr.shape[2:], lambda i, j, t: (i, layer, 0, 0)))
        out_shape.append(jax.ShapeDtypeStruct(arr.shape, arr.dtype))
    res = pl.pallas_call(
        functools.partial(_inproj_kernel, tm=tm, rope=rope, n_cache=n_cache, seq=seq),
        out_shape=out_shape,
        grid_spec=pltpu.PrefetchScalarGridSpec(
            num_scalar_prefetch=1,
            grid=(n_tok // tm, N_TILES),
            in_specs=in_specs,
            out_specs=out_specs,
            scratch_shapes=[pltpu.VMEM((tm, D_MODEL), BF16)],
        ),
        input_output_aliases={1 + n_in + k: 2 + k for k in range(n_cache)},
        compiler_params=_params("arbitrary", "arbitrary"),
        name="in_projection_rope" if rope else "in_projection",
    )(tab, *args)
    return res[0], res[1], list(res[2:])


def _joint(scores):
    return scores[0] if len(scores) == 1 else jnp.concatenate(scores, axis=1)


PV_W = 2 * HEAD_DIM


def _weights_pv(segs, extra=None):
    s = _joint([sc for sc, _ in segs])
    m = jnp.max(s, axis=-1, keepdims=True)
    if extra is not None:
        m = jnp.maximum(m, extra)
    e = jnp.exp2(s - m).astype(BF16)
    acc, off = None, 0
    for sc, v in segs:
        n = sc.shape[1]
        v1 = jnp.concatenate([v, jnp.ones((n, PV_W), BF16)], axis=1)
        part = _dot(e[:, off:off + n], v1)
        acc = part if acc is None else acc + part
        off += n
    num, den = acc[:, :PV_W], acc[:, PV_W:]
    if extra is not None:
        den = den + jnp.exp2(extra - m)
    return num, den


def _softmax_pv(segs, extra=None):
    num, den = _weights_pv(segs, extra)
    return num / den


def _half_vecs():
    lane = lax.broadcasted_iota(jnp.int32, (1, 2 * HEAD_DIM), 1)
    lo = (lane < HEAD_DIM).astype(BF16)
    return lo, (1.0 - lo).astype(BF16)


def _pair_block(ref, m, rows=None, b=0):
    if rows is None:
        return ref[b, :, m * 128:(m + 1) * 128]
    return ref[b, rows, m * 128:(m + 1) * 128]


def _both_halves(q, vecs):
    return jnp.concatenate([q * vecs[0], q * vecs[1]], axis=0)


def _merge_pair(o_even, o_odd):
    lane = lax.broadcasted_iota(jnp.int32, o_even.shape, 1)
    return jnp.where(lane < HEAD_DIM, o_even, o_odd)


def _diff_lambda_val(dl_ref, lam_init):
    lp = dl_ref[...]
    s1 = jnp.sum(lp[0:1] * lp[1:2], axis=-1, keepdims=True)
    s2 = jnp.sum(lp[2:3] * lp[3:4], axis=-1, keepdims=True)
    return jnp.exp(s1) - jnp.exp(s2) + lam_init


def _diff_finish(o, subln_ref, lam_init):
    ms = jnp.mean(o * o, axis=-1, keepdims=True)
    return o * lax.rsqrt(ms + NORM_EPS) * subln_ref[...] * (1.0 - lam_init)


def _swap_halves_bf16(x):
    return pltpu.roll(x.astype(F32), HEAD_DIM, 1).astype(BF16)


def _gqa_queries(q_ref, hk, vecs, b=0, rows=None):
    parts = []
    for g in range(SWA_GROUP):
        h = hk * SWA_GROUP + g
        q = _pair_block(q_ref, h // 2, rows=rows, b=b)
        if h % 2 != hk:
            q = _swap_halves_bf16(q)
        parts.append(q * vecs[hk])
    return jnp.concatenate(parts, axis=0)


def _gqa_outputs(o, hk, rows):
    blocks = []
    for j in range(SWA_GROUP // 2):
        halves = []
        for g in (2 * j, 2 * j + 1):
            og = o[g * rows:(g + 1) * rows]
            halves.append(og if g % 2 == hk else pltpu.roll(og, HEAD_DIM, 1))
        blocks.append(_merge_pair(halves[0], halves[1]))
    return blocks


def _pipelined(items, depth=2):
    pending, outs = [], []
    for score_fn, finish_fn in items:
        pending.append((finish_fn, score_fn()))
        if len(pending) > depth:
            fn, s = pending.pop(0)
            outs.append(fn(s))
    for fn, s in pending:
        outs.append(fn(s))
    return outs


def _ctx_attn_kernel(sink_ref, sq_ref, skv_ref, nq_ref, nk_ref, nv_ref, dq_ref, dk_ref, dv_ref,
                     dl_ref, subln_ref, o_ref, *, t, lam_init, bb):
    lam = _diff_lambda_val(dl_ref, lam_init)
    vecs = _half_vecs()

    def swa_item(b, hk):
        def scores():
            return _dot_nt(_gqa_queries(sq_ref, hk, vecs, b=b), _pair_block(skv_ref, 0, b=b))

        def finish(s):
            sink = jnp.concatenate(
                [jnp.full((t, 1), sink_ref[hk * SWA_GROUP + g] * LOG2E, F32) for g in range(SWA_GROUP)], axis=0)
            return _gqa_outputs(_softmax_pv([(s, _pair_block(skv_ref, 1, b=b))], extra=sink), hk, t)
        return scores, finish

    def na_item(b, m):
        def scores():
            return _dot_nt(_both_halves(_pair_block(nq_ref, m, b=b), vecs), _pair_block(nk_ref, m, b=b))

        def finish(s):
            o = _softmax_pv([(s, _pair_block(nv_ref, m, b=b))])
            return [_merge_pair(o[:t], o[t:])]
        return scores, finish

    def diff_item(b, h):
        def scores():
            return _dot_nt(_both_halves(_pair_block(dq_ref, h, b=b), vecs), _pair_block(dk_ref, h, b=b))

        def finish(s):
            o = _softmax_pv([(s, _pair_block(dv_ref, h, b=b))])
            return [_diff_finish(o[:t] - lam * o[t:], subln_ref, lam_init)]
        return scores, finish

    items = []
    for make, n in ((swa_item, SWA_KV_HEADS), (na_item, NA_HEADS // 2), (diff_item, DIFF_HEADS)):
        items += [make(b, i) for i in range(n) for b in range(bb)]
    res = _pipelined(items, depth=2 * bb)
    for b in range(bb):
        pieces = [piece for r in res[b::bb] for piece in r]
        o_ref[b] = jnp.concatenate(pieces, axis=1).astype(BF16)


CTX_ROWS_PER_STEP = 4


def _pcol(name, t, bb=1):
    return pl.BlockSpec((bb, t, PROJ_TILE), lambda i, c=P_COL[name]: (i, 0, c))


def _ctx_attention(p, sink, dlam, subln, lam_init, *, nb, t):
    bb = CTX_ROWS_PER_STEP
    p3 = p.reshape(nb, t, P_TILES * PROJ_TILE)
    names = ("sq", "skv", "nq", "nk", "nv", "dq", "dk", "dv")
    out = pl.pallas_call(
        functools.partial(_ctx_attn_kernel, t=t, lam_init=lam_init, bb=bb),
        out_shape=jax.ShapeDtypeStruct((nb, t, 3 * GROUP_W), BF16),
        grid=(nb // bb,),
        in_specs=[pl.BlockSpec(memory_space=pltpu.SMEM)] + [_pcol(n, t, bb) for n in names] + [
            pl.BlockSpec((4, HEAD_DIM), lambda i: (0, 0)),
            pl.BlockSpec((1, DIFF_VDIM), lambda i: (0, 0)),
        ],
        out_specs=pl.BlockSpec((bb, t, 3 * GROUP_W), lambda i: (i, 0, 0)),
        compiler_params=_params("arbitrary"),
        name="ctx_attention",
    )(sink, *([p3] * len(names)), dlam, subln.reshape(1, DIFF_VDIM))
    return out.reshape(nb * t, 3 * GROUP_W)


Q_BLOCKS_PER_STEP = 4


def _swa_kernel(sink_ref, q_ref, kv_ref, ck_ref, cv_ref, o_ref, *, t, qb):
    span = 3 * qb
    vecs = _half_vecs()

    def item(j, hk):
        n = pl.program_id(1) * Q_BLOCKS_PER_STEP + j
        start = pl.multiple_of(jnp.clip((n - 1) * qb, 0, t - span), qb)
        win = pl.ds(start, span)
        qrows = slice(j * qb, (j + 1) * qb)

        def scores():
            qs = _gqa_queries(q_ref, hk, vecs, rows=qrows)
            return _dot_nt(qs, _pair_block(kv_ref, 0, rows=win)), _dot_nt(qs, ck_ref[0])

        def finish(s):
            row = (lax.broadcasted_iota(jnp.int32, (SWA_GROUP * qb, span), 0) & (qb - 1)) + n * qb
            col = lax.broadcasted_iota(jnp.int32, (SWA_GROUP * qb, span), 1) + start
            dist = row - col
            ok = (dist <= SWA_WINDOW) & (dist >= -SWA_WINDOW)
            sink = jnp.concatenate(
                [jnp.full((qb, 1), sink_ref[hk * SWA_GROUP + g] * LOG2E, F32) for g in range(SWA_GROUP)], axis=0)
            s_loc = jnp.where(ok, s[0], NEG_INF)
            o = _softmax_pv([(s_loc, _pair_block(kv_ref, 1, rows=win)), (s[1], cv_ref[0])], extra=sink)
            return _gqa_outputs(o, hk, qb)
        return scores, finish

    items = [item(j, hk) for hk in range(SWA_KV_HEADS) for j in range(Q_BLOCKS_PER_STEP)]
    res = _pipelined(items, depth=Q_BLOCKS_PER_STEP)
    for j in range(Q_BLOCKS_PER_STEP):
        pieces = [piece for r in res[j::Q_BLOCKS_PER_STEP] for piece in r]
        o_ref[0, j * qb:(j + 1) * qb, :] = jnp.concatenate(pieces, axis=1).astype(BF16)


def _swa_latent(p, ck, cv, layer, sink, *, nb, t):
    qb = 128
    step_rows = qb * Q_BLOCKS_PER_STEP
    p3 = p.reshape(nb, t, P_TILES * PROJ_TILE)
    out = pl.pallas_call(
        functools.partial(_swa_kernel, t=t, qb=qb),
        out_shape=jax.ShapeDtypeStruct((nb, t, GROUP_W), BF16),
        grid=(nb, t // step_rows),
        in_specs=[
            pl.BlockSpec(memory_space=pltpu.SMEM),
            pl.BlockSpec((1, step_rows, PROJ_TILE), lambda b, n: (b, n, P_COL["sq"])),
            pl.BlockSpec((1, t, PROJ_TILE), lambda b, n: (b, 0, P_COL["skv"])),
            pl.BlockSpec((1,) + ck.shape[1:], lambda b, n: (b * DEPTH + layer, 0, 0)),
            pl.BlockSpec((1,) + cv.shape[1:], lambda b, n: (b * DEPTH + layer, 0, 0)),
        ],
        out_specs=pl.BlockSpec((1, step_rows, GROUP_W), lambda b, n: (b, n, 0)),
        compiler_params=_params("arbitrary", "arbitrary"),
        name="swa_latent",
    )(sink, p3, p3, ck, cv)
    return out.reshape(nb * t, GROUP_W)


def _na_row_start(r, rows):
    kh = min(NA_KH, rows)
    return min(max(r - kh // 2, 0), rows - kh)


def _na_win_start(qt, rows):
    return min(max(_na_row_start(2 * qt, rows), 0), rows - NA_WIN_ROWS)


def _na_bias_kernel(rpb_ref, o_ref, *, rows):
    h = pl.program_id(0)
    n_dr, n_dc = 2 * NA_KH - 1, 2 * NA_KW - 1
    qi = lax.broadcasted_iota(jnp.int32, (GRID_W, GRID_W), 0)
    ki = lax.broadcasted_iota(jnp.int32, (GRID_W, GRID_W), 1)
    dc = jnp.clip(ki - qi + (NA_KW - 1), 0, n_dc - 1)
    cs = jnp.clip(qi - NA_KW // 2, 0, GRID_W - NA_KW)
    col_ok = (ki >= cs) & (ki < cs + NA_KW)
    neg = jnp.full((GRID_W, GRID_W), NEG_INF, F32)
    tabs = []
    for dr in range(n_dr):
        acc = jnp.zeros((GRID_W, GRID_W), F32)
        for c in range(n_dc):
            acc = jnp.where(dc == c, rpb_ref[(h * n_dr + dr) * n_dc + c], acc)
        tabs.append(jnp.where(col_ok, acc * LOG2E, neg))
    kh = min(NA_KH, rows)
    built = {}
    for qt in range(rows // 2):
        ws = _na_win_start(qt, rows)
        pattern = []
        for qq in range(2):
            qr = 2 * qt + qq
            rs = _na_row_start(qr, rows)
            pattern.append(tuple(ws + kk - qr + NA_KH - 1 if rs <= ws + kk < rs + kh else None
                                 for kk in range(NA_WIN_ROWS)))
        pattern = tuple(pattern)
        if pattern not in built:
            bands = [jnp.concatenate([neg if dr is None else tabs[dr] for dr in band], axis=1)
                     for band in pattern]
            built[pattern] = jnp.concatenate(bands, axis=0)
        o_ref[0, qt] = built[pattern]


def _na_bias(rpb, rows):
    n_qt = rows // 2
    n_tab = rpb.shape[0] * rpb.shape[1]
    return pl.pallas_call(
        functools.partial(_na_bias_kernel, rows=rows),
        out_shape=jax.ShapeDtypeStruct((n_tab, n_qt, NA_QT, NA_WIN), F32),
        grid=(n_tab,),
        in_specs=[pl.BlockSpec(memory_space=pltpu.SMEM)],
        out_specs=pl.BlockSpec((1, n_qt, NA_QT, NA_WIN), lambda h: (h, 0, 0, 0)),
        compiler_params=_params("arbitrary"),
        name="na_bias",
    )(rpb.reshape(-1))


def _na_kernel(q_ref, k_ref, v_ref, ck_ref, cv_ref, bias_ref, o_ref, *, rows):
    vecs = _half_vecs()

    def item(j, m):
        qt = pl.program_id(1) * Q_BLOCKS_PER_STEP + j
        ws = jnp.clip(jnp.clip(2 * qt - NA_KH // 2, 0, rows - NA_KH), 0, rows - NA_WIN_ROWS)
        win = pl.ds(pl.multiple_of(ws * GRID_W, GRID_W), NA_WIN)
        qrows = slice(j * NA_QT, (j + 1) * NA_QT)

        def scores():
            q2 = _both_halves(_pair_block(q_ref, m, rows=qrows), vecs)
            return _dot_nt(q2, _pair_block(k_ref, m, rows=win)), _dot_nt(q2, _pair_block(ck_ref, m))

        def finish(s):
            v, cv = _pair_block(v_ref, m, rows=win), _pair_block(cv_ref, m)
            bias = jnp.concatenate([bias_ref[2 * m, j], bias_ref[2 * m + 1, j]], axis=0)
            o = _softmax_pv([(s[0] + bias, v), (s[1], cv)])
            return _merge_pair(o[:NA_QT], o[NA_QT:])
        return scores, finish

    items = [item(j, m) for m in range(NA_HEADS // 2) for j in range(Q_BLOCKS_PER_STEP)]
    res = _pipelined(items, depth=2 * Q_BLOCKS_PER_STEP)
    for j in range(Q_BLOCKS_PER_STEP):
        o_ref[0, j * NA_QT:(j + 1) * NA_QT, :] = jnp.concatenate(res[j::Q_BLOCKS_PER_STEP], axis=1).astype(BF16)


def _na_latent(p, ck, cv, layer, bias, *, nb, t):
    rows = t // GRID_W
    p3 = p.reshape(nb, t, P_TILES * PROJ_TILE)
    out = pl.pallas_call(
        functools.partial(_na_kernel, rows=rows),
        out_shape=jax.ShapeDtypeStruct((nb, t, GROUP_W), BF16),
        grid=(nb, t // (NA_QT * Q_BLOCKS_PER_STEP)),
        in_specs=[
            pl.BlockSpec((1, NA_QT * Q_BLOCKS_PER_STEP, PROJ_TILE), lambda i, n: (i, n, P_COL["nq"])),
            pl.BlockSpec((1, t, PROJ_TILE), lambda i, n: (i, 0, P_COL["nk"])),
            pl.BlockSpec((1, t, PROJ_TILE), lambda i, n: (i, 0, P_COL["nv"])),
            pl.BlockSpec((1,) + ck.shape[1:], lambda i, n: (i * DEPTH + layer, 0, 0)),
            pl.BlockSpec((1,) + cv.shape[1:], lambda i, n: (i * DEPTH + layer, 0, 0)),
            pl.BlockSpec((NA_HEADS, Q_BLOCKS_PER_STEP, NA_QT, NA_WIN), lambda i, n: (layer, n, 0, 0)),
        ],
        out_specs=pl.BlockSpec((1, NA_QT * Q_BLOCKS_PER_STEP, GROUP_W), lambda i, n: (i, n, 0)),
        compiler_params=_params("arbitrary", "arbitrary"),
        name="na_latent",
    )(p3, p3, p3, ck, cv, bias)
    return out.reshape(nb * t, GROUP_W)


def _diff_kernel(q_ref, k_ref, v_ref, ck_ref, cv_ref, dl_ref, subln_ref, o_ref, *, lam_init, tq):
    lam = _diff_lambda_val(dl_ref, lam_init)

    vecs = _half_vecs()

    def item(j, h):
        qrows = slice(j * tq, (j + 1) * tq)

        def scores():
            q2 = _both_halves(_pair_block(q_ref, h, rows=qrows), vecs)
            return _dot_nt(q2, _pair_block(k_ref, h)), _dot_nt(q2, _pair_block(ck_ref, h))

        def finish(s):
            o = _softmax_pv([(s[0], _pair_block(v_ref, h)), (s[1], _pair_block(cv_ref, h))])
            return _diff_finish(o[:tq] - lam * o[tq:], subln_ref, lam_init)
        return scores, finish

    items = [item(j, h) for h in range(DIFF_HEADS) for j in range(Q_BLOCKS_PER_STEP)]
    res = _pipelined(items, depth=Q_BLOCKS_PER_STEP)
    for j in range(Q_BLOCKS_PER_STEP):
        o_ref[0, j * tq:(j + 1) * tq, :] = jnp.concatenate(res[j::Q_BLOCKS_PER_STEP], axis=1).astype(BF16)


def _diff_latent(p, ck, cv, layer, dlam, subln, lam_init, *, nb, t):
    tq_sub = 256
    tq = tq_sub * Q_BLOCKS_PER_STEP
    p3 = p.reshape(nb, t, P_TILES * PROJ_TILE)
    out = pl.pallas_call(
        functools.partial(_diff_kernel, lam_init=lam_init, tq=tq_sub),
        out_shape=jax.ShapeDtypeStruct((nb, t, GROUP_W), BF16),
        grid=(nb, t // tq),
        in_specs=[
            pl.BlockSpec((1, tq, PROJ_TILE), lambda i, n: (i, n, P_COL["dq"])),
            pl.BlockSpec((1, t, PROJ_TILE), lambda i, n: (i, 0, P_COL["dk"])),
            pl.BlockSpec((1, t, PROJ_TILE), lambda i, n: (i, 0, P_COL["dv"])),
            pl.BlockSpec((1,) + ck.shape[1:], lambda i, n: (i * DEPTH + layer, 0, 0)),
            pl.BlockSpec((1,) + cv.shape[1:], lambda i, n: (i * DEPTH + layer, 0, 0)),
            pl.BlockSpec((4, HEAD_DIM), lambda i, n: (0, 0)),
            pl.BlockSpec((1, DIFF_VDIM), lambda i, n: (0, 0)),
        ],
        out_specs=pl.BlockSpec((1, tq, GROUP_W), lambda i, n: (i, n, 0)),
        compiler_params=_params("arbitrary", "arbitrary"),
        name="diff_latent",
    )(p3, p3, p3, ck, cv, dlam, subln.reshape(1, DIFF_VDIM))
    return out.reshape(nb * t, GROUP_W)


SUBLANES = 8


def _linear_scan(a, b, row, t, reverse, sa_ref, sb_ref, sc_ref):
    nt = t // SUBLANES
    sub = row & (SUBLANES - 1)

    def doubling(a, b, idx, n, steps):
        for s in steps:
            if reverse:
                ok = idx < n - s
                a_sh = jnp.where(ok, pltpu.roll(a, a.shape[0] - s, 0), 1.0)
                b_sh = jnp.where(ok, pltpu.roll(b, b.shape[0] - s, 0), 0.0)
            else:
                ok = idx >= s
                a_sh = jnp.where(ok, pltpu.roll(a, s, 0), 1.0)
                b_sh = jnp.where(ok, pltpu.roll(b, s, 0), 0.0)
            b = a * b_sh + b
            a = a * a_sh
        return a, b

    tiles = (nt, SUBLANES, LRU_CT)
    sub3 = lax.broadcasted_iota(jnp.int32, tiles, 1)

    def tile_doubling(a, b):
        for s in (1, 2, 4):
            ok = (sub3 < SUBLANES - s) if reverse else (sub3 >= s)
            sh = SUBLANES - s if reverse else s
            a_sh = jnp.where(ok, pltpu.roll(a, sh, 1), 1.0)
            b_sh = jnp.where(ok, pltpu.roll(b, sh, 1), 0.0)
            b = a * b_sh + b
            a = a * a_sh
        return a, b

    a, b = tile_doubling(a.reshape(tiles), b.reshape(tiles))
    a, b = a.reshape(t, LRU_CT), b.reshape(t, LRU_CT)
    sa_ref[...] = a
    sb_ref[...] = b
    last = 0 if reverse else SUBLANES - 1
    at = sa_ref[pl.ds(last, nt, stride=SUBLANES), :]
    bt = sb_ref[pl.ds(last, nt, stride=SUBLANES), :]
    rowt = lax.broadcasted_iota(jnp.int32, (nt, LRU_CT), 0)
    steps, s = [], 1
    while s < nt:
        steps.append(s)
        s *= 2
    _, bt = doubling(at, bt, rowt, nt, steps)
    if reverse:
        carry = jnp.where(rowt < nt - 1, pltpu.roll(bt, nt - 1, 0), 0.0)
    else:
        carry = jnp.where(rowt >= 1, pltpu.roll(bt, 1, 0), 0.0)
    for r in range(SUBLANES):
        sc_ref[pl.ds(r, nt, stride=SUBLANES), :] = carry
    return b + a * sc_ref[...]


def _lru_kernel(x_ref, g_ref, cw_ref, cb_ref, w_ref, gb_ref, lam_ref, h0_ref, y_ref, st_ref, *scratch, t, cpt,
                has_h0):
    row = lax.broadcasted_iota(jnp.int32, (t, LRU_CT), 0)
    for ci in range(cpt):
        lanes = slice(ci * LRU_CT, (ci + 1) * LRU_CT)
        x = x_ref[0, :, lanes]
        u = cb_ref[:, lanes] + cw_ref[2:3, lanes] * x
        for tap, off in ((0, -2), (1, -1), (3, 1)):
            xs = pltpu.roll(x, (-off) % t, 0)
            ok = (row + off >= 0) & (row + off < t)
            u = u + cw_ref[tap:tap + 1, lanes] * jnp.where(ok, xs, 0.0)
        gates = _dot(u.astype(BF16), w_ref[ci]) + gb_ref[ci]
        total = None
        for d in range(2):
            r = 1.0 / (1.0 + jnp.exp2(gates[:, (2 * d) * LRU_CT:(2 * d + 1) * LRU_CT]))
            ig = 1.0 / (1.0 + jnp.exp2(gates[:, (2 * d + 1) * LRU_CT:(2 * d + 2) * LRU_CT]))
            nl = -lam_ref[d:d + 1, lanes]
            rate = LRU_C * (jnp.maximum(nl, 0.0) + jnp.log1p(jnp.exp(-jnp.abs(nl))))
            a = jnp.exp2(r * (rate * -LOG2E))
            bx = jnp.sqrt(jnp.tanh(r * rate) * (a * a + 1.0)) * (ig * u)
            if has_h0:
                edge = t - 1 if d == 1 else 0
                bx = jnp.where(row == edge, bx + a * h0_ref[0, d:d + 1, lanes], bx)
            sa_ref, sb_ref, sc_ref = scratch[3 * (2 * ci + d):3 * (2 * ci + d) + 3]
            bx = _linear_scan(a, bx, row, t, d == 1, sa_ref, sb_ref, sc_ref)
            fin = t - 1 if d == 0 else 0
            st_ref[0, d:d + 1, lanes] = bx[fin:fin + 1, :]
            total = bx if total is None else total + bx
        g = g_ref[0, :, lanes]
        gelu = 0.5 * g * (1.0 + jnp.tanh(math.sqrt(2.0 / math.pi) * (g + 0.044715 * (g * g * g))))
        y_ref[0, :, lanes] = (total * gelu).astype(BF16)


def _lru_mixer(f, conv_w, conv_b, w_gates, b_gates, lam, h0, *, nb, t, cpt):
    has_h0 = h0 is not None
    if not has_h0:
        h0 = jnp.zeros((nb, 2, LRU_WIDTH), F32)
    f3 = f.reshape(nb, t, f.shape[1])
    nct = LRU_WIDTH // LRU_CT
    wide = cpt * LRU_CT
    y, st = pl.pallas_call(
        functools.partial(_lru_kernel, t=t, cpt=cpt, has_h0=has_h0),
        out_shape=[jax.ShapeDtypeStruct((nb, t, LRU_WIDTH), BF16),
                   jax.ShapeDtypeStruct((nb, 2, LRU_WIDTH), F32)],
        grid=(nb, nct // cpt),
        in_specs=[
            pl.BlockSpec((1, t, wide), lambda i, c: (i, 0, c)),
            pl.BlockSpec((1, t, wide), lambda i, c: (i, 0, nct // cpt + c)),
            pl.BlockSpec((CONV_W, wide), lambda i, c: (0, c)),
            pl.BlockSpec((1, wide), lambda i, c: (0, c)),
            pl.BlockSpec((cpt, LRU_CT, 4 * LRU_CT), lambda i, c: (c, 0, 0)),
            pl.BlockSpec((cpt, 1, 4 * LRU_CT), lambda i, c: (c, 0, 0)),
            pl.BlockSpec((2, wide), lambda i, c: (0, c)),
            pl.BlockSpec((1, 2, wide), lambda i, c: (i, 0, c)),
        ],
        out_specs=[pl.BlockSpec((1, t, wide), lambda i, c: (i, 0, c)),
                   pl.BlockSpec((1, 2, wide), lambda i, c: (i, 0, c))],
        scratch_shapes=[pltpu.VMEM((t, LRU_CT), F32)] * (6 * cpt),
        compiler_params=_params("arbitrary", "arbitrary"),
        name="lru_mixer",
    )(f3, f3, conv_w, conv_b.reshape(1, LRU_WIDTH), w_gates, b_gates, lam, h0)
    return y.reshape(nb * t, LRU_WIDTH), st


def _lru_gate_weights(wa, ba, wx, bx):
    nct = LRU_WIDTH // LRU_CT
    bpt = LRU_CT // LRU_BLOCK_W
    eye = jnp.eye(bpt, dtype=F32)

    def tile_w(w):
        w4 = w.reshape(nct, bpt, LRU_BLOCK_W, LRU_BLOCK_W)
        return jnp.einsum("cnij,nm->cnimj", w4, eye).reshape(nct, LRU_CT, LRU_CT)

    w = jnp.concatenate([tile_w(wa[0]), tile_w(wx[0]), tile_w(wa[1]), tile_w(wx[1])], axis=-1)
    b = jnp.concatenate([v.reshape(nct, 1, LRU_CT) for v in (ba[0], bx[0], ba[1], bx[1])], axis=-1)
    return (w * -LOG2E).astype(BF16), b * -LOG2E


def _outproj_kernel(*refs, n_in):
    ins = refs[:n_in]
    w_ref, x_ref, g_ref, o_ref = refs[n_in:]
    acc = None
    off = 0
    for m_ref in ins:
        width = m_ref.shape[1]
        part = _dot(m_ref[...], w_ref[0, off:off + width, :])
        acc = part if acc is None else acc + part
        off += width
    o_ref[...] = x_ref[...] + g_ref[0] * acc


def _out_projection(mixes, w_out_bf, layer, x, mods_l, mod_row, *, tm):
    n_tok = x.shape[0]
    return pl.pallas_call(
        functools.partial(_outproj_kernel, n_in=len(mixes)),
        out_shape=jax.ShapeDtypeStruct((n_tok, D_MODEL), F32),
        grid=(n_tok // tm,),
        in_specs=[pl.BlockSpec((tm, m.shape[1]), lambda i: (i, 0)) for m in mixes] + [
            pl.BlockSpec((1, D_MODEL, D_MODEL), lambda i: (layer, 0, 0)),
            pl.BlockSpec((tm, D_MODEL), lambda i: (i, 0)),
            pl.BlockSpec((1, 1, D_MODEL), lambda i: (mod_row(i) * MOD_CHUNKS + 2, 0, 0)),
        ],
        out_specs=pl.BlockSpec((tm, D_MODEL), lambda i: (i, 0)),
        compiler_params=_params("arbitrary"),
        name="out_projection",
    )(*mixes, w_out_bf, x, mods_l)


def _ffn_kernel(x_ref, sh_ref, sc_ref, g_ref, nw_ref, wg_ref, wu_ref, wd_ref, o_ref, h_scr, *, tm):
    k = pl.program_id(1)

    def step(first, last=False):
        wg = wg_ref[0].astype(BF16)
        wu = wu_ref[0].astype(BF16)
        wd = wd_ref[0].astype(BF16)
        chunk = ROW_CHUNK if first else FFN_ROW_CHUNK

        def norm_chunk(c):
            _modnorm_static(x_ref, nw_ref, sh_ref, sc_ref, h_scr, c * chunk, chunk)

        n_chunks = tm // chunk

        def gate_up(c):
            if first and c + 1 < n_chunks:
                norm_chunk(c + 1)
            h = h_scr[c * chunk:(c + 1) * chunk, :]
            return _dot(h, wg), _dot(h, wu)

        def down(c, gu):
            g, u = gu
            act = (g * _sigmoid(g)) * u
            res = _dot(act.astype(BF16), wd)
            rows = slice(c * chunk, (c + 1) * chunk)
            if first:
                o_ref[rows, :] = res
            elif last:
                o_ref[rows, :] = x_ref[rows, :] + g_ref[0] * (o_ref[rows, :] + res)
            else:
                o_ref[rows, :] += res

        if first:
            norm_chunk(0)
        pending = gate_up(0)
        for c in range(1, n_chunks):
            nxt = gate_up(c)
            down(c - 1, pending)
            pending = nxt
        down(n_chunks - 1, pending)

    @pl.when(k == 0)
    def _():
        step(True)

    n_k = pl.num_programs(1)

    @pl.when((k > 0) & (k < n_k - 1))
    def _():
        step(False)

    @pl.when(k == n_k - 1)
    def _():
        step(False, last=True)


def _ffn(x, mods_l, mod_row, norm_w, wg, wu, wd, layer, *, tm, th):
    n_tok = x.shape[0]
    assert FFN_HIDDEN // th >= 2
    return pl.pallas_call(
        functools.partial(_ffn_kernel, tm=tm),
        out_shape=jax.ShapeDtypeStruct((n_tok, D_MODEL), F32),
        grid=(n_tok // tm, FFN_HIDDEN // th),
        in_specs=[
            pl.BlockSpec((tm, D_MODEL), lambda i, k: (i, 0), pipeline_mode=pl.Buffered(1)),
            pl.BlockSpec((1, 1, D_MODEL), lambda i, k: (mod_row(i) * MOD_CHUNKS + 3, 0, 0)),
            pl.BlockSpec((1, 1, D_MODEL), lambda i, k: (mod_row(i) * MOD_CHUNKS + 4, 0, 0)),
            pl.BlockSpec((1, 1, D_MODEL), lambda i, k: (mod_row(i) * MOD_CHUNKS + 5, 0, 0)),
            pl.BlockSpec((1, D_MODEL), lambda i, k: (0, 0)),
            pl.BlockSpec((1, D_MODEL, th), lambda i, k: (layer, 0, k)),
            pl.BlockSpec((1, D_MODEL, th), lambda i, k: (layer, 0, k)),
            pl.BlockSpec((1, th, D_MODEL), lambda i, k: (layer, k, 0)),
        ],
        out_specs=pl.BlockSpec((tm, D_MODEL), lambda i, k: (i, 0)),
        scratch_shapes=[pltpu.VMEM((tm, D_MODEL), BF16)],
        compiler_params=_params("arbitrary", "arbitrary"),
        name="ffn",
    )(x, mods_l, mods_l, mods_l, norm_w.reshape(1, D_MODEL), wg, wu, wd)


def _rope_tables(t_lat):
    nf = HEAD_DIM // 4
    tok = np.arange(t_lat)
    pos = np.stack([tok // GRID_W, tok % GRID_W], axis=-1).astype(np.float32)
    inv = np.float32(ROPE_BASE) ** (-np.arange(nf, dtype=np.float32) / np.float32(nf))
    ang = (pos[:, :, None] * inv.astype(np.float32)).astype(np.float32)
    cos = np.cos(ang.astype(np.float64)).astype(np.float32)
    sin = np.sin(ang.astype(np.float64)).astype(np.float32)
    cos_h = np.concatenate([cos[:, 0], cos[:, 0], cos[:, 1], cos[:, 1]], axis=-1)
    sin_h = np.concatenate([-sin[:, 0], sin[:, 0], -sin[:, 1], sin[:, 1]], axis=-1)
    reps = PROJ_HALF // HEAD_DIM
    return jnp.asarray(np.tile(cos_h, (1, reps))), jnp.asarray(np.tile(sin_h, (1, reps)))


def _tile_table():
    lo = [_FIRST_BLOCK[n] for n in TILE_NAMES]
    hi = [b + 1 for b in lo]
    cls = [_TILE_CLASS[n] for n in TILE_NAMES]
    return jnp.asarray(np.array(lo + hi + cls, np.int32))


def _tile_gains(qk_gain_l):
    def tiled(g, n):
        return jnp.tile(g, n // HEAD_DIM)
    ones = jnp.ones((PROJ_TILE,), F32)
    per_tile = {
        "lx": ones, "lg": ones, "nv": ones, "dv": ones,
        "nk": tiled(qk_gain_l[1, 1], PROJ_TILE), "dk": tiled(qk_gain_l[2, 1], PROJ_TILE),
        "skv": jnp.concatenate([tiled(qk_gain_l[0, 1], 128), jnp.ones((PROJ_TILE - 128,), F32)]),
        "sq": tiled(qk_gain_l[0, 0], PROJ_TILE) * QK_SCALE,
        "nq": tiled(qk_gain_l[1, 0], PROJ_TILE) * QK_SCALE,
        "dq": tiled(qk_gain_l[2, 0], PROJ_TILE) * QK_SCALE,
    }
    return jnp.stack([per_tile[n] for n in TILE_NAMES]).reshape(N_TILES, 1, PROJ_TILE)


def _lambda_init(layer):
    return 0.8 - 0.6 * math.exp(-0.3 * layer)


def kernel(x_prompt, x_sample, cache_swa_k, cache_swa_v, cache_na_k, cache_na_v, cache_diff_k, cache_diff_v, state_lru, c, c_ctx, norm_mix, norm_ffn, w_mod, b_mod, w_in, w_out, qk_gain, swa_sink, na_rpb, diff_lambda, diff_subln, conv_w, conv_b, lru_wa, lru_ba, lru_wx, lru_bx, lru_L, w_ffn_gate, w_ffn_up, w_ffn_down):
    bc, s_ctx, _ = x_prompt.shape
    bd, t_lat, _ = x_sample.shape
    n_ctx, n_lat = bc * s_ctx, bd * t_lat
    rows = t_lat // GRID_W
    p_ctx = cache_swa_k.shape[2]

    cvecs = jnp.concatenate([c_ctx[None, :], c, jnp.zeros((8 - 1 - bd, D_MODEL), F32)], axis=0)
    mods = _modulation(cvecs, w_mod, b_mod).reshape(DEPTH, 8 * MOD_CHUNKS, 1, D_MODEL)

    hsum = jnp.asarray(np.kron(np.eye(PROJ_HALF // HEAD_DIM, dtype=np.float32),
                               np.full((HEAD_DIM, HEAD_DIM), 1.0 / HEAD_DIM, np.float32)), BF16)
    rope_tabs = _rope_tables(t_lat)
    tab = _tile_table()

    tm = 1024
    tm_out = 512
    ctx_row = lambda tile: (lambda i: 0)
    lat_row = lambda tile: (lambda i: 1 + (i * tile) // t_lat)

    xc = x_prompt.reshape(n_ctx, D_MODEL)
    xs = x_sample.reshape(n_lat, D_MODEL)
    caches = [jnp.zeros(_cache_buffer_shape(name, bc, s_ctx), F32) for name in CACHE_NAMES]
    lru_states = []
    w_in_bf = w_in
    w_out_bf = w_out.astype(BF16)

    def cached(arr):
        return arr.astype(BF16).reshape(bd * DEPTH, p_ctx, -1)

    c_swa_k, c_swa_v = cached(cache_swa_k), cached(cache_swa_v)
    c_na_k, c_na_v = cached(cache_na_k), cached(cache_na_v)
    c_diff_k, c_diff_v = cached(cache_diff_k), cached(cache_diff_v)
    na_bias = _na_bias(na_rpb, rows)

    for l in range(DEPTH):
        lam_init = _lambda_init(l)
        mods_l = mods[l]
        gain = _tile_gains(qk_gain[l])
        w_gates, b_gates = _lru_gate_weights(lru_wa[l], lru_ba[l], lru_wx[l], lru_bx[l])
        ffn_w = (w_ffn_gate, w_ffn_up, w_ffn_down, l)

        p, f, caches = _in_projection(xc, mods_l, ctx_row(tm), norm_mix[l], w_in_bf, l, tab, gain, hsum,
                                      None, caches, tm=tm, seq=s_ctx)
        mix3 = _ctx_attention(p, swa_sink[l], diff_lambda[l], diff_subln[l], lam_init, nb=bc, t=s_ctx)
        od, st = _lru_mixer(f, conv_w[l], conv_b[l], w_gates, b_gates, lru_L[l], None,
                            nb=bc, t=s_ctx, cpt=4)
        lru_states.append(st)
        xc = _out_projection([mix3, od], w_out_bf, l, xc, mods_l, ctx_row(tm_out), tm=tm_out)
        xc = _ffn(xc, mods_l, ctx_row(tm), norm_ffn[l], *ffn_w, tm=tm, th=512)

        p, f, _ = _in_projection(xs, mods_l, lat_row(tm), norm_mix[l], w_in_bf, l, tab, gain, hsum,
                                 rope_tabs, None, tm=tm, seq=t_lat)
        oa = _swa_latent(p, c_swa_k, c_swa_v, l, swa_sink[l], nb=bd, t=t_lat)
        ob = _na_latent(p, c_na_k, c_na_v, l, na_bias, nb=bd, t=t_lat)
        oc = _diff_latent(p, c_diff_k, c_diff_v, l, diff_lambda[l], diff_subln[l], lam_init, nb=bd, t=t_lat)
        od, _ = _lru_mixer(f, conv_w[l], conv_b[l], w_gates, b_gates, lru_L[l], state_lru[:, l],
                           nb=bd, t=t_lat, cpt=2)
        xs = _out_projection([oa, ob, oc, od], w_out_bf, l, xs, mods_l, lat_row(tm_out), tm=tm_out)
        xs = _ffn(xs, mods_l, lat_row(tm), norm_ffn[l], *ffn_w, tm=tm, th=512)

    swa_k, swa_v, na_k, na_v, diff_k, diff_v = caches
    return (xc.reshape(bc, s_ctx, D_MODEL), xs.reshape(bd, t_lat, D_MODEL),
            swa_k.reshape(bc, DEPTH, s_ctx, SWA_KV_HEADS, HEAD_DIM),
            swa_v.reshape(bc, DEPTH, s_ctx, SWA_KV_HEADS, HEAD_DIM),
            na_k.reshape(bc, DEPTH, s_ctx, NA_HEADS, HEAD_DIM),
            na_v.reshape(bc, DEPTH, s_ctx, NA_HEADS, HEAD_DIM),
            diff_k.reshape(bc, DEPTH, s_ctx, DIFF_HEADS, 2, HEAD_DIM),
            diff_v.reshape(bc, DEPTH, s_ctx, DIFF_HEADS, DIFF_VDIM),
            jnp.stack(lru_states, axis=1))
```

```python
import functools
import math

import jax
import jax.numpy as jnp
import numpy as np
from jax import lax
from jax.experimental import pallas as pl
from jax.experimental.pallas import tpu as pltpu

F32 = jnp.float32
BF16 = jnp.bfloat16

D_MODEL = 2048
DEPTH = 2
GRID_W = 64
HEAD_DIM = 64
GROUP_W = 512
SWA_HEADS = 8
SWA_KV_HEADS = 2
SWA_GROUP = 4
SWA_WINDOW = 128
NA_HEADS = 8
NA_KH = 8
NA_KW = 16
DIFF_HEADS = 4
DIFF_VDIM = 128
LRU_WIDTH = 512
LRU_BLOCKS = 8
LRU_BLOCK_W = 64
LRU_C = 8.0
CONV_W = 4
FFN_HIDDEN = 5632
ROPE_BASE = 10000.0
NORM_EPS = 1e-6
NEG_INF = -1e30
MOD_CHUNKS = 6
PROJ_W = 4864
LOG2E = math.log2(math.e)
QK_SCALE = HEAD_DIM ** -0.5 * LOG2E

V7X_VMEM_LIMIT = 60 * 1024 * 1024

PROJ_HALF = 256
PROJ_TILE = 2 * PROJ_HALF
TILE_NAMES = ("lx", "lg", "nk", "dk", "nv", "dv", "skv", "sq", "nq", "dq")
N_TILES = len(TILE_NAMES)
_FIRST_BLOCK = {"sq": 0, "skv": 2, "nq": 3, "nk": 5, "nv": 7, "dq": 9, "dk": 11, "dv": 13, "lx": 15, "lg": 17}
CLS_PLAIN, CLS_NORM, CLS_ROPE, CLS_MIXED = 0, 1, 2, 3
_TILE_CLASS = {"lx": CLS_PLAIN, "lg": CLS_PLAIN, "nv": CLS_PLAIN, "dv": CLS_PLAIN,
               "nk": CLS_NORM, "nq": CLS_NORM, "dk": CLS_ROPE, "sq": CLS_ROPE, "dq": CLS_ROPE,
               "skv": CLS_MIXED}
P_FIRST = 2
P_TILES = N_TILES - P_FIRST
P_COL = {name: TILE_NAMES.index(name) - P_FIRST for name in TILE_NAMES[P_FIRST:]}
F_TILES = 2

NA_QT = 128
NA_WIN_ROWS = 10
NA_WIN = NA_WIN_ROWS * GRID_W
LRU_CT = 128
ROW_CHUNK = 512
FFN_ROW_CHUNK = 512
FFN_FIRST_ROW_CHUNK = 256


def _sigmoid(x):
    return 1.0 / (1.0 + jnp.exp(-x))


def _dot(a, b):
    return jnp.dot(a, b, preferred_element_type=F32)


def _dot_nt(a, b):
    return lax.dot_general(a, b, (((1,), (1,)), ((), ())), preferred_element_type=F32)


def _params(*sem):
    return pltpu.CompilerParams(dimension_semantics=sem, vmem_limit_bytes=V7X_VMEM_LIMIT)


def _mod_kernel(c_ref, w_ref, b_ref, o_ref):
    cv = c_ref[...]
    s = cv * _sigmoid(cv)
    o_ref[0] = _dot(s.astype(BF16), w_ref[0].astype(BF16)) + b_ref[0]


def _modulation(cvecs, w_mod, b_mod):
    tn = 1024
    n = MOD_CHUNKS * D_MODEL
    return pl.pallas_call(
        _mod_kernel,
        out_shape=jax.ShapeDtypeStruct((DEPTH, 8, n), F32),
        grid=(DEPTH, n // tn),
        in_specs=[
            pl.BlockSpec((8, D_MODEL), lambda l, j: (0, 0)),
            pl.BlockSpec((1, D_MODEL, tn), lambda l, j: (l, 0, j)),
            pl.BlockSpec((1, 1, tn), lambda l, j: (l, 0, j)),
        ],
        out_specs=pl.BlockSpec((1, 8, tn), lambda l, j: (l, 0, j)),
        compiler_params=_params("arbitrary", "arbitrary"),
        name="modulation",
    )(cvecs, w_mod, b_mod.reshape(DEPTH, 1, n))


def _modnorm_static(x_ref, nw_ref, sh_ref, sc_ref, h_scr, r0, n_rows, chunk=128):
    for c in range(n_rows // chunk):
        rows = slice(r0 + c * chunk, r0 + (c + 1) * chunk)
        x = x_ref[rows, :]
        ms = jnp.mean(x * x, axis=-1, keepdims=True)
        y = x * lax.rsqrt(ms + NORM_EPS) * nw_ref[...]
        h_scr[rows, :] = (y * (1.0 + sc_ref[0]) + sh_ref[0]).astype(BF16)


CACHE_NAMES = ("swa_k", "swa_v", "na_k", "na_v", "diff_k", "diff_v")
_CACHE_SRC = {"swa_k": ("skv", 0, 128), "swa_v": ("skv", 128, 128), "na_k": ("nk", 0, 512),
              "na_v": ("nv", 0, 512), "diff_k": ("dk", 0, 512), "diff_v": ("dv", 0, 512)}
_CACHE_ROW_SPLIT = {"diff_v": DIFF_HEADS}


def _cache_buffer_shape(name, nb, seq):
    width = _CACHE_SRC[name][2]
    split = _CACHE_ROW_SPLIT.get(name, 1)
    return (nb, DEPTH, seq * split, width // split)


def _zero_fill_kernel(*o_refs):
    for o_ref in o_refs:
        o_ref[...] = jnp.zeros(o_ref.shape, o_ref.dtype)


def _zero_cache_buffers(nb, seq, rows_per_step=4):
    shapes = [_cache_buffer_shape(name, nb, seq) for name in CACHE_NAMES]
    return pl.pallas_call(
        _zero_fill_kernel,
        out_shape=[jax.ShapeDtypeStruct(s, F32) for s in shapes],
        grid=(nb // rows_per_step,),
        out_specs=[pl.BlockSpec((rows_per_step,) + s[1:], lambda i: (i, 0, 0, 0)) for s in shapes],
        compiler_params=_params("arbitrary"),
        name="zero_cache_buffers",
    )()


def _inproj_kernel(*refs, tm, rope, n_cache, seq):
    n_in = 10 if rope else 8
    tab_ref = refs[0]
    (x_ref, sh_ref, sc_ref, nw_ref, wlo_ref, whi_ref, gain_ref, hsum_ref) = refs[1:9]
    cos_ref, sin_ref = (refs[9], refs[10]) if rope else (None, None)
    outs = refs[1 + n_in + n_cache:]
    p_ref, f_ref = outs[0], outs[1]
    cache_refs = outs[2:2 + n_cache]
    h_scr = outs[2 + n_cache]
    j = pl.program_id(1)
    cls = tab_ref[2 * N_TILES + j]

    def run_tile(mode, halves, first=False):
        w = {half: (wlo_ref if half == 0 else whi_ref)[0].astype(BF16) for half in halves}
        n_chunks = tm // ROW_CHUNK
        units = [(half, c) for half in halves for c in range(n_chunks)]

        def norm_chunk(c):
            _modnorm_static(x_ref, nw_ref, sh_ref, sc_ref, h_scr, c * ROW_CHUNK, ROW_CHUNK)

        if first:
            norm_chunk(0)

        def main(unit):
            half, c = unit
            if first and half == halves[0] and c + 1 < n_chunks:
                norm_chunk(c + 1)
            return _dot(h_scr[c * ROW_CHUNK:(c + 1) * ROW_CHUNK, :], w[half])

        def finish(unit, p):
            half, c = unit
            rows = slice(c * ROW_CHUNK, (c + 1) * ROW_CHUNK)
            cols = slice(half * PROJ_HALF, (half + 1) * PROJ_HALF)
            y = p
            if mode != CLS_PLAIN:
                ms = _dot((p * p).astype(BF16), hsum_ref[...])
                y = p * lax.rsqrt(ms + NORM_EPS) * gain_ref[0, :, cols]
                if rope and mode in (CLS_ROPE, CLS_MIXED):
                    lane = lax.broadcasted_iota(jnp.int32, y.shape, 1)
                    up = pltpu.roll(y, PROJ_HALF - 16, 1)
                    down = pltpu.roll(y, 16, 1)
                    partner = jnp.where((lane & 31) < 16, up, down)
                    y = y * cos_ref[rows, :] + partner * sin_ref[rows, :]
                if mode == CLS_MIXED:
                    lane = lax.broadcasted_iota(jnp.int32, y.shape, 1)
                    y = jnp.where(lane < 2 * HEAD_DIM, y, p)
            p_ref[rows, cols] = y.astype(BF16)
            f_ref[rows, cols] = y

        prev, p_prev = units[0], main(units[0])
        for unit in units[1:]:
            p_next = main(unit)
            finish(prev, p_prev)
            prev, p_prev = unit, p_next
        finish(prev, p_prev)

    assert _TILE_CLASS[TILE_NAMES[0]] == CLS_PLAIN

    @pl.when(j == 0)
    def _():
        run_tile(CLS_PLAIN, (0, 1), first=True)

    modes = (CLS_PLAIN, CLS_NORM, CLS_ROPE) if rope else (CLS_PLAIN, CLS_NORM)
    for mode in modes:
        cond = (cls == mode)
        if mode == CLS_PLAIN:
            cond = cond & (j > 0)
        if not rope and mode == CLS_NORM:
            cond = (cls == CLS_NORM) | (cls == CLS_ROPE)

        @pl.when(cond)
        def _(mode=mode):
            run_tile(mode, (0, 1))

    @pl.when(cls == CLS_MIXED)
    def _():
        run_tile(CLS_MIXED, (0,))
        hi = slice(PROJ_HALF, PROJ_TILE)
        p_ref[:, hi] = jnp.zeros((tm, PROJ_HALF), BF16)
        f_ref[:, hi] = jnp.zeros((tm, PROJ_HALF), F32)

    for name, c_ref in zip(CACHE_NAMES, cache_refs):
        tile, off, width = _CACHE_SRC[name]

        split = _CACHE_ROW_SPLIT.get(name, 1)

        @pl.when(j == TILE_NAMES.index(tile))
        def _(c_ref=c_ref, off=off, width=width, split=split):
            for b in range(tm // seq):
                rows = slice(b * seq, (b + 1) * seq)
                if split == 1:
                    c_ref[b, 0] = f_ref[rows, off:off + width]
                else:
                    piece = width // split
                    for h in range(split):
                        c_ref[b, 0, pl.ds(h, seq, stride=split), :] = f_ref[rows, off + h * piece:off + (h + 1) * piece]


def _in_projection(x, mods_l, mod_row, norm_w, w_in_bf, layer, tab, gain, hsum, rope_tabs, caches, *, tm, seq):
    n_tok = x.shape[0]
    rope = rope_tabs is not None
    caches = [] if caches is None else list(caches)
    n_cache = len(caches)

    in_specs = [
        pl.BlockSpec((tm, D_MODEL), lambda i, j, t: (i, 0)),
        pl.BlockSpec((1, 1, D_MODEL), lambda i, j, t: (mod_row(i) * MOD_CHUNKS + 0, 0, 0)),
        pl.BlockSpec((1, 1, D_MODEL), lambda i, j, t: (mod_row(i) * MOD_CHUNKS + 1, 0, 0)),
        pl.BlockSpec((1, D_MODEL), lambda i, j, t: (0, 0)),
        pl.BlockSpec((1, D_MODEL, PROJ_HALF), lambda i, j, t: (layer, 0, t[j])),
        pl.BlockSpec((1, D_MODEL, PROJ_HALF), lambda i, j, t: (layer, 0, t[N_TILES + j])),
        pl.BlockSpec((1, 1, PROJ_TILE), lambda i, j, t: (j, 0, 0)),
        pl.BlockSpec((PROJ_HALF, PROJ_HALF), lambda i, j, t: (0, 0)),
    ]
    args = [x, mods_l, mods_l, norm_w.reshape(1, D_MODEL), w_in_bf, w_in_bf, gain, hsum]
    if rope:
        cos_t, sin_t = rope_tabs
        in_specs += [pl.BlockSpec((tm, PROJ_HALF), lambda i, j, t: (0, 0)),
                     pl.BlockSpec((tm, PROJ_HALF), lambda i, j, t: (0, 0))]
        args += [cos_t, sin_t]
    n_in = len(args)
    in_specs += [pl.BlockSpec(memory_space=pl.ANY)] * n_cache
    args += caches
    out_specs = [
        pl.BlockSpec((tm, PROJ_TILE), lambda i, j, t: (i, jnp.maximum(j - P_FIRST, 0))),
        pl.BlockSpec((tm, PROJ_TILE), lambda i, j, t: (i, jnp.minimum(j, F_TILES))),
    ]
    out_shape = [jax.ShapeDtypeStruct((n_tok, P_TILES * PROJ_TILE), BF16),
                 jax.ShapeDtypeStruct((n_tok, (F_TILES + 1) * PROJ_TILE), F32)]
    for arr in caches:
        out_specs.append(pl.BlockSpec((tm // seq, 1) + arr.shape[2:], lambda i, j, t: (i, layer, 0, 0)))
        out_shape.append(jax.ShapeDtypeStruct(arr.shape, arr.dtype))
    res = pl.pallas_call(
        functools.partial(_inproj_kernel, tm=tm, rope=rope, n_cache=n_cache, seq=seq),
        out_shape=out_shape,
        grid_spec=pltpu.PrefetchScalarGridSpec(
            num_scalar_prefetch=1,
            grid=(n_tok // tm, N_TILES),
            in_specs=in_specs,
            out_specs=out_specs,
            scratch_shapes=[pltpu.VMEM((tm, D_MODEL), BF16)],
        ),
        input_output_aliases={1 + n_in + k: 2 + k for k in range(n_cache)},
        compiler_params=_params("arbitrary", "arbitrary"),
        name="in_projection_rope" if rope else "in_projection",
    )(tab, *args)
    return res[0], res[1], list(res[2:])


def _joint(scores):
    return scores[0] if len(scores) == 1 else jnp.concatenate(scores, axis=1)


PV_W = 2 * HEAD_DIM


def _weights_pv(segs, extra=None):
    s = _joint([sc for sc, _ in segs])
    m = jnp.max(s, axis=-1, keepdims=True)
    if extra is not None:
        m = jnp.maximum(m, extra)
    e = jnp.exp2(s - m).astype(BF16)
    acc, off = None, 0
    for sc, v in segs:
        n = sc.shape[1]
        v1 = jnp.concatenate([v, jnp.ones((n, PV_W), BF16)], axis=1)
        part = _dot(e[:, off:off + n], v1)
        acc = part if acc is None else acc + part
        off += n
    num, den = acc[:, :PV_W], acc[:, PV_W:]
    if extra is not None:
        den = den + jnp.exp2(extra - m)
    return num, den


def _softmax_pv(segs, extra=None):
    num, den = _weights_pv(segs, extra)
    return num / den


def _half_vecs():
    lane = lax.broadcasted_iota(jnp.int32, (1, 2 * HEAD_DIM), 1)
    lo = (lane < HEAD_DIM).astype(BF16)
    return lo, (1.0 - lo).astype(BF16)


def _pair_block(ref, m, rows=None, b=0):
    if rows is None:
        return ref[b, :, m * 128:(m + 1) * 128]
    return ref[b, rows, m * 128:(m + 1) * 128]


def _both_halves(q, vecs):
    return jnp.concatenate([q * vecs[0], q * vecs[1]], axis=0)


def _merge_pair(o_even, o_odd):
    lane = lax.broadcasted_iota(jnp.int32, o_even.shape, 1)
    return jnp.where(lane < HEAD_DIM, o_even, o_odd)


def _diff_lambda_val(dl_ref, lam_init):
    lp = dl_ref[...]
    s1 = jnp.sum(lp[0:1] * lp[1:2], axis=-1, keepdims=True)
    s2 = jnp.sum(lp[2:3] * lp[3:4], axis=-1, keepdims=True)
    return jnp.exp(s1) - jnp.exp(s2) + lam_init


def _diff_finish(o, subln_ref, lam_init):
    ms = jnp.mean(o * o, axis=-1, keepdims=True)
    return o * lax.rsqrt(ms + NORM_EPS) * subln_ref[...] * (1.0 - lam_init)


def _swap_halves_bf16(x):
    return pltpu.roll(x.astype(F32), HEAD_DIM, 1).astype(BF16)


def _gqa_queries(q_ref, hk, vecs, b=0, rows=None):
    parts = []
    for g in range(SWA_GROUP):
        h = hk * SWA_GROUP + g
        q = _pair_block(q_ref, h // 2, rows=rows, b=b)
        if h % 2 != hk:
            q = _swap_halves_bf16(q)
        parts.append(q * vecs[hk])
    return jnp.concatenate(parts, axis=0)


def _gqa_outputs(o, hk, rows):
    blocks = []
    for j in range(SWA_GROUP // 2):
        halves = []
        for g in (2 * j, 2 * j + 1):
            og = o[g * rows:(g + 1) * rows]
            halves.append(og if g % 2 == hk else pltpu.roll(og, HEAD_DIM, 1))
        blocks.append(_merge_pair(halves[0], halves[1]))
    return blocks


def _pipelined(items, depth=2):
    pending, outs = [], []
    for score_fn, finish_fn in items:
        pending.append((finish_fn, score_fn()))
        if len(pending) > depth:
            fn, s = pending.pop(0)
            outs.append(fn(s))
    for fn, s in pending:
        outs.append(fn(s))
    return outs


def _ctx_attn_kernel(sink_ref, sq_ref, skv_ref, nq_ref, nk_ref, nv_ref, dq_ref, dk_ref, dv_ref,
                     dl_ref, subln_ref, o_ref, *, t, lam_init, bb):
    lam = _diff_lambda_val(dl_ref, lam_init)
    vecs = _half_vecs()

    def swa_item(b, hk):
        def scores():
            return _dot_nt(_gqa_queries(sq_ref, hk, vecs, b=b), _pair_block(skv_ref, 0, b=b))

        def finish(s):
            sink = jnp.concatenate(
                [jnp.full((t, 1), sink_ref[hk * SWA_GROUP + g] * LOG2E, F32) for g in range(SWA_GROUP)], axis=0)
            return _gqa_outputs(_softmax_pv([(s, _pair_block(skv_ref, 1, b=b))], extra=sink), hk, t)
        return scores, finish

    def na_item(b, m):
        def scores():
            return _dot_nt(_both_halves(_pair_block(nq_ref, m, b=b), vecs), _pair_block(nk_ref, m, b=b))

        def finish(s):
            o = _softmax_pv([(s, _pair_block(nv_ref, m, b=b))])
            return [_merge_pair(o[:t], o[t:])]
        return scores, finish

    def diff_item(b, h):
        def scores():
            return _dot_nt(_both_halves(_pair_block(dq_ref, h, b=b), vecs), _pair_block(dk_ref, h, b=b))

        def finish(s):
            o = _softmax_pv([(s, _pair_block(dv_ref, h, b=b))])
            return [_diff_finish(o[:t] - lam * o[t:], subln_ref, lam_init)]
        return scores, finish

    items = []
    for make, n in ((swa_item, SWA_KV_HEADS), (na_item, NA_HEADS // 2), (diff_item, DIFF_HEADS)):
        items += [make(b, i) for i in range(n) for b in range(bb)]
    res = _pipelined(items, depth=2 * bb)
    for b in range(bb):
        pieces = [piece for r in res[b::bb] for piece in r]
        o_ref[b] = jnp.concatenate(pieces, axis=1).astype(BF16)


CTX_ROWS_PER_STEP = 4


def _pcol(name, t, bb=1):
    return pl.BlockSpec((bb, t, PROJ_TILE), lambda i, c=P_COL[name]: (i, 0, c))


def _ctx_attention(p, sink, dlam, subln, lam_init, *, nb, t):
    bb = CTX_ROWS_PER_STEP
    p3 = p.reshape(nb, t, P_TILES * PROJ_TILE)
    names = ("sq", "skv", "nq", "nk", "nv", "dq", "dk", "dv")
    out = pl.pallas_call(
        functools.partial(_ctx_attn_kernel, t=t, lam_init=lam_init, bb=bb),
        out_shape=jax.ShapeDtypeStruct((nb, t, 3 * GROUP_W), BF16),
        grid=(nb // bb,),
        in_specs=[pl.BlockSpec(memory_space=pltpu.SMEM)] + [_pcol(n, t, bb) for n in names] + [
            pl.BlockSpec((4, HEAD_DIM), lambda i: (0, 0)),
            pl.BlockSpec((1, DIFF_VDIM), lambda i: (0, 0)),
        ],
        out_specs=pl.BlockSpec((bb, t, 3 * GROUP_W), lambda i: (i, 0, 0)),
        compiler_params=_params("arbitrary"),
        name="ctx_attention",
    )(sink, *([p3] * len(names)), dlam, subln.reshape(1, DIFF_VDIM))
    return out.reshape(nb * t, 3 * GROUP_W)


Q_BLOCKS_PER_STEP = 4


def _swa_kernel(sink_ref, q_ref, kv_ref, ck_ref, cv_ref, o_ref, *, t, qb):
    span = 3 * qb
    vecs = _half_vecs()

    def item(j, hk):
        n = pl.program_id(1) * Q_BLOCKS_PER_STEP + j
        start = pl.multiple_of(jnp.clip((n - 1) * qb, 0, t - span), qb)
        win = pl.ds(start, span)
        qrows = slice(j * qb, (j + 1) * qb)

        def scores():
            qs = _gqa_queries(q_ref, hk, vecs, rows=qrows)
            return _dot_nt(qs, _pair_block(kv_ref, 0, rows=win)), _dot_nt(qs, ck_ref[0])

        def finish(s):
            row = (lax.broadcasted_iota(jnp.int32, (SWA_GROUP * qb, span), 0) & (qb - 1)) + n * qb
            col = lax.broadcasted_iota(jnp.int32, (SWA_GROUP * qb, span), 1) + start
            dist = row - col
            ok = (dist <= SWA_WINDOW) & (dist >= -SWA_WINDOW)
            sink = jnp.concatenate(
                [jnp.full((qb, 1), sink_ref[hk * SWA_GROUP + g] * LOG2E, F32) for g in range(SWA_GROUP)], axis=0)
            s_loc = jnp.where(ok, s[0], NEG_INF)
            o = _softmax_pv([(s_loc, _pair_block(kv_ref, 1, rows=win)), (s[1], cv_ref[0])], extra=sink)
            return _gqa_outputs(o, hk, qb)
        return scores, finish

    items = [item(j, hk) for hk in range(SWA_KV_HEADS) for j in range(Q_BLOCKS_PER_STEP)]
    res = _pipelined(items, depth=Q_BLOCKS_PER_STEP)
    for j in range(Q_BLOCKS_PER_STEP):
        pieces = [piece for r in res[j::Q_BLOCKS_PER_STEP] for piece in r]
        o_ref[0, j * qb:(j + 1) * qb, :] = jnp.concatenate(pieces, axis=1).astype(BF16)


def _swa_latent(p, ck, cv, layer, sink, *, nb, t):
    qb = 128
    step_rows = qb * Q_BLOCKS_PER_STEP
    p3 = p.reshape(nb, t, P_TILES * PROJ_TILE)
    out = pl.pallas_call(
        functools.partial(_swa_kernel, t=t, qb=qb),
        out_shape=jax.ShapeDtypeStruct((nb, t, GROUP_W), BF16),
        grid=(nb, t // step_rows),
        in_specs=[
            pl.BlockSpec(memory_space=pltpu.SMEM),
            pl.BlockSpec((1, step_rows, PROJ_TILE), lambda b, n: (b, n, P_COL["sq"])),
            pl.BlockSpec((1, t, PROJ_TILE), lambda b, n: (b, 0, P_COL["skv"])),
            pl.BlockSpec((1,) + ck.shape[1:], lambda b, n: (b * DEPTH + layer, 0, 0)),
            pl.BlockSpec((1,) + cv.shape[1:], lambda b, n: (b * DEPTH + layer, 0, 0)),
        ],
        out_specs=pl.BlockSpec((1, step_rows, GROUP_W), lambda b, n: (b, n, 0)),
        compiler_params=_params("arbitrary", "arbitrary"),
        name="swa_latent",
    )(sink, p3, p3, ck, cv)
    return out.reshape(nb * t, GROUP_W)


def _na_row_start(r, rows):
    kh = min(NA_KH, rows)
    return min(max(r - kh // 2, 0), rows - kh)


def _na_win_start(qt, rows):
    return min(max(_na_row_start(2 * qt, rows), 0), rows - NA_WIN_ROWS)


def _na_bias_kernel(rpb_ref, o_ref, *, rows):
    h = pl.program_id(0)
    n_dr, n_dc = 2 * NA_KH - 1, 2 * NA_KW - 1
    qi = lax.broadcasted_iota(jnp.int32, (GRID_W, GRID_W), 0)
    ki = lax.broadcasted_iota(jnp.int32, (GRID_W, GRID_W), 1)
    dc = jnp.clip(ki - qi + (NA_KW - 1), 0, n_dc - 1)
    cs = jnp.clip(qi - NA_KW // 2, 0, GRID_W - NA_KW)
    col_ok = (ki >= cs) & (ki < cs + NA_KW)
    neg = jnp.full((GRID_W, GRID_W), NEG_INF, F32)
    tabs = []
    for dr in range(n_dr):
        acc = jnp.zeros((GRID_W, GRID_W), F32)
        for c in range(n_dc):
            acc = jnp.where(dc == c, rpb_ref[(h * n_dr + dr) * n_dc + c], acc)
        tabs.append(jnp.where(col_ok, acc * LOG2E, neg))
    kh = min(NA_KH, rows)
    built = {}
    for qt in range(rows // 2):
        ws = _na_win_start(qt, rows)
        pattern = []
        for qq in range(2):
            qr = 2 * qt + qq
            rs = _na_row_start(qr, rows)
            pattern.append(tuple(ws + kk - qr + NA_KH - 1 if rs <= ws + kk < rs + kh else None
                                 for kk in range(NA_WIN_ROWS)))
        pattern = tuple(pattern)
        if pattern not in built:
            bands = [jnp.concatenate([neg if dr is None else tabs[dr] for dr in band], axis=1)
                     for band in pattern]
            built[pattern] = jnp.concatenate(bands, axis=0)
        o_ref[0, qt] = built[pattern]


def _na_bias(rpb, rows):
    n_qt = rows // 2
    n_tab = rpb.shape[0] * rpb.shape[1]
    return pl.pallas_call(
        functools.partial(_na_bias_kernel, rows=rows),
        out_shape=jax.ShapeDtypeStruct((n_tab, n_qt, NA_QT, NA_WIN), F32),
        grid=(n_tab,),
        in_specs=[pl.BlockSpec(memory_space=pltpu.SMEM)],
        out_specs=pl.BlockSpec((1, n_qt, NA_QT, NA_WIN), lambda h: (h, 0, 0, 0)),
        compiler_params=_params("arbitrary"),
        name="na_bias",
    )(rpb.reshape(-1))


def _na_kernel(q_ref, k_ref, v_ref, ck_ref, cv_ref, bias_ref, o_ref, *, rows):
    vecs = _half_vecs()

    def item(j, m):
        qt = pl.program_id(1) * Q_BLOCKS_PER_STEP + j
        ws = jnp.clip(jnp.clip(2 * qt - NA_KH // 2, 0, rows - NA_KH), 0, rows - NA_WIN_ROWS)
        win = pl.ds(pl.multiple_of(ws * GRID_W, GRID_W), NA_WIN)
        qrows = slice(j * NA_QT, (j + 1) * NA_QT)

        def scores():
            q2 = _both_halves(_pair_block(q_ref, m, rows=qrows), vecs)
            return _dot_nt(q2, _pair_block(k_ref, m, rows=win)), _dot_nt(q2, _pair_block(ck_ref, m))

        def finish(s):
            v, cv = _pair_block(v_ref, m, rows=win), _pair_block(cv_ref, m)
            bias = jnp.concatenate([bias_ref[2 * m, j], bias_ref[2 * m + 1, j]], axis=0)
            o = _softmax_pv([(s[0] + bias, v), (s[1], cv)])
            return _merge_pair(o[:NA_QT], o[NA_QT:])
        return scores, finish

    items = [item(j, m) for m in range(NA_HEADS // 2) for j in range(Q_BLOCKS_PER_STEP)]
    res = _pipelined(items, depth=2 * Q_BLOCKS_PER_STEP)
    for j in range(Q_BLOCKS_PER_STEP):
        o_ref[0, j * NA_QT:(j + 1) * NA_QT, :] = jnp.concatenate(res[j::Q_BLOCKS_PER_STEP], axis=1).astype(BF16)


def _na_latent(p, ck, cv, layer, bias, *, nb, t):
    rows = t // GRID_W
    p3 = p.reshape(nb, t, P_TILES * PROJ_TILE)
    out = pl.pallas_call(
        functools.partial(_na_kernel, rows=rows),
        out_shape=jax.ShapeDtypeStruct((nb, t, GROUP_W), BF16),
        grid=(nb, t // (NA_QT * Q_BLOCKS_PER_STEP)),
        in_specs=[
            pl.BlockSpec((1, NA_QT * Q_BLOCKS_PER_STEP, PROJ_TILE), lambda i, n: (i, n, P_COL["nq"])),
            pl.BlockSpec((1, t, PROJ_TILE), lambda i, n: (i, 0, P_COL["nk"])),
            pl.BlockSpec((1, t, PROJ_TILE), lambda i, n: (i, 0, P_COL["nv"])),
            pl.BlockSpec((1,) + ck.shape[1:], lambda i, n: (i * DEPTH + layer, 0, 0)),
            pl.BlockSpec((1,) + cv.shape[1:], lambda i, n: (i * DEPTH + layer, 0, 0)),
            pl.BlockSpec((NA_HEADS, Q_BLOCKS_PER_STEP, NA_QT, NA_WIN), lambda i, n: (layer, n, 0, 0)),
        ],
        out_specs=pl.BlockSpec((1, NA_QT * Q_BLOCKS_PER_STEP, GROUP_W), lambda i, n: (i, n, 0)),
        compiler_params=_params("arbitrary", "arbitrary"),
        name="na_latent",
    )(p3, p3, p3, ck, cv, bias)
    return out.reshape(nb * t, GROUP_W)


def _diff_kernel(q_ref, k_ref, v_ref, ck_ref, cv_ref, dl_ref, subln_ref, o_ref, *, lam_init, tq):
    lam = _diff_lambda_val(dl_ref, lam_init)

    vecs = _half_vecs()

    def item(j, h):
        qrows = slice(j * tq, (j + 1) * tq)

        def scores():
            q2 = _both_halves(_pair_block(q_ref, h, rows=qrows), vecs)
            return _dot_nt(q2, _pair_block(k_ref, h)), _dot_nt(q2, _pair_block(ck_ref, h))

        def finish(s):
            o = _softmax_pv([(s[0], _pair_block(v_ref, h)), (s[1], _pair_block(cv_ref, h))])
            return _diff_finish(o[:tq] - lam * o[tq:], subln_ref, lam_init)
        return scores, finish

    items = [item(j, h) for h in range(DIFF_HEADS) for j in range(Q_BLOCKS_PER_STEP)]
    res = _pipelined(items, depth=Q_BLOCKS_PER_STEP)
    for j in range(Q_BLOCKS_PER_STEP):
        o_ref[0, j * tq:(j + 1) * tq, :] = jnp.concatenate(res[j::Q_BLOCKS_PER_STEP], axis=1).astype(BF16)


def _diff_latent(p, ck, cv, layer, dlam, subln, lam_init, *, nb, t):
    tq_sub = 256
    tq = tq_sub * Q_BLOCKS_PER_STEP
    p3 = p.reshape(nb, t, P_TILES * PROJ_TILE)
    out = pl.pallas_call(
        functools.partial(_diff_kernel, lam_init=lam_init, tq=tq_sub),
        out_shape=jax.ShapeDtypeStruct((nb, t, GROUP_W), BF16),
        grid=(nb, t // tq),
        in_specs=[
            pl.BlockSpec((1, tq, PROJ_TILE), lambda i, n: (i, n, P_COL["dq"])),
            pl.BlockSpec((1, t, PROJ_TILE), lambda i, n: (i, 0, P_COL["dk"])),
            pl.BlockSpec((1, t, PROJ_TILE), lambda i, n: (i, 0, P_COL["dv"])),
            pl.BlockSpec((1,) + ck.shape[1:], lambda i, n: (i * DEPTH + layer, 0, 0)),
            pl.BlockSpec((1,) + cv.shape[1:], lambda i, n: (i * DEPTH + layer, 0, 0)),
            pl.BlockSpec((4, HEAD_DIM), lambda i, n: (0, 0)),
            pl.BlockSpec((1, DIFF_VDIM), lambda i, n: (0, 0)),
        ],
        out_specs=pl.BlockSpec((1, tq, GROUP_W), lambda i, n: (i, n, 0)),
        compiler_params=_params("arbitrary", "arbitrary"),
        name="diff_latent",
    )(p3, p3, p3, ck, cv, dlam, subln.reshape(1, DIFF_VDIM))
    return out.reshape(nb * t, GROUP_W)


SUBLANES = 8


def _linear_scan(a, b, row, t, reverse, sa_ref, sb_ref, sc_ref):
    nt = t // SUBLANES
    sub = row & (SUBLANES - 1)

    def doubling(a, b, idx, n, steps):
        for s in steps:
            if reverse:
                ok = idx < n - s
                a_sh = jnp.where(ok, pltpu.roll(a, a.shape[0] - s, 0), 1.0)
                b_sh = jnp.where(ok, pltpu.roll(b, b.shape[0] - s, 0), 0.0)
            else:
                ok = idx >= s
                a_sh = jnp.where(ok, pltpu.roll(a, s, 0), 1.0)
                b_sh = jnp.where(ok, pltpu.roll(b, s, 0), 0.0)
            b = a * b_sh + b
            a = a * a_sh
        return a, b

    tiles = (nt, SUBLANES, LRU_CT)
    sub3 = lax.broadcasted_iota(jnp.int32, tiles, 1)

    def tile_doubling(a, b):
        for s in (1, 2, 4):
            ok = (sub3 < SUBLANES - s) if reverse else (sub3 >= s)
            sh = SUBLANES - s if reverse else s
            a_sh = jnp.where(ok, pltpu.roll(a, sh, 1), 1.0)
            b_sh = jnp.where(ok, pltpu.roll(b, sh, 1), 0.0)
            b = a * b_sh + b
            a = a * a_sh
        return a, b

    a, b = tile_doubling(a.reshape(tiles), b.reshape(tiles))
    a, b = a.reshape(t, LRU_CT), b.reshape(t, LRU_CT)
    sa_ref[...] = a
    sb_ref[...] = b
    last = 0 if reverse else SUBLANES - 1
    at = sa_ref[pl.ds(last, nt, stride=SUBLANES), :]
    bt = sb_ref[pl.ds(last, nt, stride=SUBLANES), :]
    rowt = lax.broadcasted_iota(jnp.int32, (nt, LRU_CT), 0)
    steps, s = [], 1
    while s < nt:
        steps.append(s)
        s *= 2
    _, bt = doubling(at, bt, rowt, nt, steps)
    if reverse:
        carry = jnp.where(rowt < nt - 1, pltpu.roll(bt, nt - 1, 0), 0.0)
    else:
        carry = jnp.where(rowt >= 1, pltpu.roll(bt, 1, 0), 0.0)
    for r in range(SUBLANES):
        sc_ref[pl.ds(r, nt, stride=SUBLANES), :] = carry
    return b + a * sc_ref[...]


def _lru_kernel(x_ref, g_ref, cw_ref, cb_ref, w_ref, gb_ref, lam_ref, h0_ref, y_ref, st_ref, *scratch, t, cpt,
                has_h0):
    row = lax.broadcasted_iota(jnp.int32, (t, LRU_CT), 0)
    for ci in range(cpt):
        lanes = slice(ci * LRU_CT, (ci + 1) * LRU_CT)
        x = x_ref[0, :, lanes]
        u = cb_ref[:, lanes] + cw_ref[2:3, lanes] * x
        for tap, off in ((0, -2), (1, -1), (3, 1)):
            xs = pltpu.roll(x, (-off) % t, 0)
            ok = (row + off >= 0) & (row + off < t)
            u = u + cw_ref[tap:tap + 1, lanes] * jnp.where(ok, xs, 0.0)
        gates = _dot(u.astype(BF16), w_ref[ci]) + gb_ref[ci]
        total = None
        for d in range(2):
            r = 1.0 / (1.0 + jnp.exp2(gates[:, (2 * d) * LRU_CT:(2 * d + 1) * LRU_CT]))
            ig = 1.0 / (1.0 + jnp.exp2(gates[:, (2 * d + 1) * LRU_CT:(2 * d + 2) * LRU_CT]))
            nl = -lam_ref[d:d + 1, lanes]
            rate = LRU_C * (jnp.maximum(nl, 0.0) + jnp.log1p(jnp.exp(-jnp.abs(nl))))
            a = jnp.exp2(r * (rate * -LOG2E))
            bx = jnp.sqrt(jnp.tanh(r * rate) * (a * a + 1.0)) * (ig * u)
            if has_h0:
                edge = t - 1 if d == 1 else 0
                bx = jnp.where(row == edge, bx + a * h0_ref[0, d:d + 1, lanes], bx)
            sa_ref, sb_ref, sc_ref = scratch[3 * (2 * ci + d):3 * (2 * ci + d) + 3]
            bx = _linear_scan(a, bx, row, t, d == 1, sa_ref, sb_ref, sc_ref)
            fin = t - 1 if d == 0 else 0
            st_ref[0, d:d + 1, lanes] = bx[fin:fin + 1, :]
            total = bx if total is None else total + bx
        g = g_ref[0, :, lanes]
        gelu = 0.5 * g * (1.0 + jnp.tanh(math.sqrt(2.0 / math.pi) * (g + 0.044715 * (g * g * g))))
        y_ref[0, :, lanes] = (total * gelu).astype(BF16)


def _lru_mixer(f, conv_w, conv_b, w_gates, b_gates, lam, h0, *, nb, t, cpt):
    has_h0 = h0 is not None
    if not has_h0:
        h0 = jnp.zeros((nb, 2, LRU_WIDTH), F32)
    f3 = f.reshape(nb, t, f.shape[1])
    nct = LRU_WIDTH // LRU_CT
    wide = cpt * LRU_CT
    y, st = pl.pallas_call(
        functools.partial(_lru_kernel, t=t, cpt=cpt, has_h0=has_h0),
        out_shape=[jax.ShapeDtypeStruct((nb, t, LRU_WIDTH), BF16),
                   jax.ShapeDtypeStruct((nb, 2, LRU_WIDTH), F32)],
        grid=(nb, nct // cpt),
        in_specs=[
            pl.BlockSpec((1, t, wide), lambda i, c: (i, 0, c)),
            pl.BlockSpec((1, t, wide), lambda i, c: (i, 0, nct // cpt + c)),
            pl.BlockSpec((CONV_W, wide), lambda i, c: (0, c)),
            pl.BlockSpec((1, wide), lambda i, c: (0, c)),
            pl.BlockSpec((cpt, LRU_CT, 4 * LRU_CT), lambda i, c: (c, 0, 0)),
            pl.BlockSpec((cpt, 1, 4 * LRU_CT), lambda i, c: (c, 0, 0)),
            pl.BlockSpec((2, wide), lambda i, c: (0, c)),
            pl.BlockSpec((1, 2, wide), lambda i, c: (i, 0, c)),
        ],
        out_specs=[pl.BlockSpec((1, t, wide), lambda i, c: (i, 0, c)),
                   pl.BlockSpec((1, 2, wide), lambda i, c: (i, 0, c))],
        scratch_shapes=[pltpu.VMEM((t, LRU_CT), F32)] * (6 * cpt),
        compiler_params=_params("arbitrary", "arbitrary"),
        name="lru_mixer",
    )(f3, f3, conv_w, conv_b.reshape(1, LRU_WIDTH), w_gates, b_gates, lam, h0)
    return y.reshape(nb * t, LRU_WIDTH), st


def _lru_gate_weights(wa, ba, wx, bx):
    nct = LRU_WIDTH // LRU_CT
    bpt = LRU_CT // LRU_BLOCK_W
    eye = jnp.eye(bpt, dtype=F32)

    def tile_w(w):
        w4 = w.reshape(nct, bpt, LRU_BLOCK_W, LRU_BLOCK_W)
        return jnp.einsum("cnij,nm->cnimj", w4, eye).reshape(nct, LRU_CT, LRU_CT)

    w = jnp.concatenate([tile_w(wa[0]), tile_w(wx[0]), tile_w(wa[1]), tile_w(wx[1])], axis=-1)
    b = jnp.concatenate([v.reshape(nct, 1, LRU_CT) for v in (ba[0], bx[0], ba[1], bx[1])], axis=-1)
    return (w * -LOG2E).astype(BF16), b * -LOG2E


def _outproj_kernel(*refs, n_in):
    ins = refs[:n_in]
    w_ref, x_ref, g_ref, o_ref = refs[n_in:]
    acc = None
    off = 0
    for m_ref in ins:
        width = m_ref.shape[1]
        part = _dot(m_ref[...], w_ref[0, off:off + width, :])
        acc = part if acc is None else acc + part
        off += width
    o_ref[...] = x_ref[...] + g_ref[0] * acc


def _out_projection(mixes, w_out_bf, layer, x, mods_l, mod_row, *, tm):
    n_tok = x.shape[0]
    return pl.pallas_call(
        functools.partial(_outproj_kernel, n_in=len(mixes)),
        out_shape=jax.ShapeDtypeStruct((n_tok, D_MODEL), F32),
        grid=(n_tok // tm,),
        in_specs=[pl.BlockSpec((tm, m.shape[1]), lambda i: (i, 0)) for m in mixes] + [
            pl.BlockSpec((1, D_MODEL, D_MODEL), lambda i: (layer, 0, 0)),
            pl.BlockSpec((tm, D_MODEL), lambda i: (i, 0)),
            pl.BlockSpec((1, 1, D_MODEL), lambda i: (mod_row(i) * MOD_CHUNKS + 2, 0, 0)),
        ],
        out_specs=pl.BlockSpec((tm, D_MODEL), lambda i: (i, 0)),
        compiler_params=_params("arbitrary"),
        name="out_projection",
    )(*mixes, w_out_bf, x, mods_l)


def _ffn_kernel(x_ref, sh_ref, sc_ref, g_ref, nw_ref, wg_ref, wu_ref, wd_ref, o_ref, h_scr, *, tm):
    k = pl.program_id(1)

    def step(first, last=False):
        wg = wg_ref[0].astype(BF16)
        wu = wu_ref[0].astype(BF16)
        wd = wd_ref[0].astype(BF16)
        chunk = FFN_FIRST_ROW_CHUNK if first else FFN_ROW_CHUNK

        def norm_chunk(c):
            _modnorm_static(x_ref, nw_ref, sh_ref, sc_ref, h_scr, c * chunk, chunk)

        n_chunks = tm // chunk

        def gate_up(c):
            if first and c + 1 < n_chunks:
                norm_chunk(c + 1)
            h = h_scr[c * chunk:(c + 1) * chunk, :]
            return _dot(h, wg), _dot(h, wu)

        def down(c, gu):
            g, u = gu
            act = (g * _sigmoid(g)) * u
            res = _dot(act.astype(BF16), wd)
            rows = slice(c * chunk, (c + 1) * chunk)
            if first:
                o_ref[rows, :] = res
            elif last:
                o_ref[rows, :] = x_ref[rows, :] + g_ref[0] * (o_ref[rows, :] + res)
            else:
                o_ref[rows, :] += res

        if first:
            norm_chunk(0)
        pending = gate_up(0)
        for c in range(1, n_chunks):
            nxt = gate_up(c)
            down(c - 1, pending)
            pending = nxt
        down(n_chunks - 1, pending)

    @pl.when(k == 0)
    def _():
        step(True)

    n_k = pl.num_programs(1)

    @pl.when((k > 0) & (k < n_k - 1))
    def _():
        step(False)

    @pl.when(k == n_k - 1)
    def _():
        step(False, last=True)


def _ffn(x, mods_l, mod_row, norm_w, wg, wu, wd, layer, *, tm, th):
    n_tok = x.shape[0]
    assert FFN_HIDDEN // th >= 2
    return pl.pallas_call(
        functools.partial(_ffn_kernel, tm=tm),
        out_shape=jax.ShapeDtypeStruct((n_tok, D_MODEL), F32),
        grid=(n_tok // tm, FFN_HIDDEN // th),
        in_specs=[
            pl.BlockSpec((tm, D_MODEL), lambda i, k: (i, 0), pipeline_mode=pl.Buffered(1)),
            pl.BlockSpec((1, 1, D_MODEL), lambda i, k: (mod_row(i) * MOD_CHUNKS + 3, 0, 0)),
            pl.BlockSpec((1, 1, D_MODEL), lambda i, k: (mod_row(i) * MOD_CHUNKS + 4, 0, 0)),
            pl.BlockSpec((1, 1, D_MODEL), lambda i, k: (mod_row(i) * MOD_CHUNKS + 5, 0, 0)),
            pl.BlockSpec((1, D_MODEL), lambda i, k: (0, 0)),
            pl.BlockSpec((1, D_MODEL, th), lambda i, k: (layer, 0, k)),
            pl.BlockSpec((1, D_MODEL, th), lambda i, k: (layer, 0, k)),
            pl.BlockSpec((1, th, D_MODEL), lambda i, k: (layer, k, 0)),
        ],
        out_specs=pl.BlockSpec((tm, D_MODEL), lambda i, k: (i, 0)),
        scratch_shapes=[pltpu.VMEM((tm, D_MODEL), BF16)],
        compiler_params=_params("arbitrary", "arbitrary"),
        name="ffn",
    )(x, mods_l, mods_l, mods_l, norm_w.reshape(1, D_MODEL), wg, wu, wd)


def _rope_tables(t_lat):
    nf = HEAD_DIM // 4
    tok = np.arange(t_lat)
    pos = np.stack([tok // GRID_W, tok % GRID_W], axis=-1).astype(np.float32)
    inv = np.float32(ROPE_BASE) ** (-np.arange(nf, dtype=np.float32) / np.float32(nf))
    ang = (pos[:, :, None] * inv.astype(np.float32)).astype(np.float32)
    cos = np.cos(ang.astype(np.float64)).astype(np.float32)
    sin = np.sin(ang.astype(np.float64)).astype(np.float32)
    cos_h = np.concatenate([cos[:, 0], cos[:, 0], cos[:, 1], cos[:, 1]], axis=-1)
    sin_h = np.concatenate([-sin[:, 0], sin[:, 0], -sin[:, 1], sin[:, 1]], axis=-1)
    reps = PROJ_HALF // HEAD_DIM
    return jnp.asarray(np.tile(cos_h, (1, reps))), jnp.asarray(np.tile(sin_h, (1, reps)))


def _tile_table():
    lo = [_FIRST_BLOCK[n] for n in TILE_NAMES]
    hi = [b + 1 for b in lo]
    cls = [_TILE_CLASS[n] for n in TILE_NAMES]
    return jnp.asarray(np.array(lo + hi + cls, np.int32))


def _tile_gains(qk_gain_l):
    def tiled(g, n):
        return jnp.tile(g, n // HEAD_DIM)
    ones = jnp.ones((PROJ_TILE,), F32)
    per_tile = {
        "lx": ones, "lg": ones, "nv": ones, "dv": ones,
        "nk": tiled(qk_gain_l[1, 1], PROJ_TILE), "dk": tiled(qk_gain_l[2, 1], PROJ_TILE),
        "skv": jnp.concatenate([tiled(qk_gain_l[0, 1], 128), jnp.ones((PROJ_TILE - 128,), F32)]),
        "sq": tiled(qk_gain_l[0, 0], PROJ_TILE) * QK_SCALE,
        "nq": tiled(qk_gain_l[1, 0], PROJ_TILE) * QK_SCALE,
        "dq": tiled(qk_gain_l[2, 0], PROJ_TILE) * QK_SCALE,
    }
    return jnp.stack([per_tile[n] for n in TILE_NAMES]).reshape(N_TILES, 1, PROJ_TILE)


def _lambda_init(layer):
    return 0.8 - 0.6 * math.exp(-0.3 * layer)


def kernel(x_prompt, x_sample, cache_swa_k, cache_swa_v, cache_na_k, cache_na_v, cache_diff_k, cache_diff_v, state_lru, c, c_ctx, norm_mix, norm_ffn, w_mod, b_mod, w_in, w_out, qk_gain, swa_sink, na_rpb, diff_lambda, diff_subln, conv_w, conv_b, lru_wa, lru_ba, lru_wx, lru_bx, lru_L, w_ffn_gate, w_ffn_up, w_ffn_down):
    bc, s_ctx, _ = x_prompt.shape
    bd, t_lat, _ = x_sample.shape
    n_ctx, n_lat = bc * s_ctx, bd * t_lat
    rows = t_lat // GRID_W
    p_ctx = cache_swa_k.shape[2]

    cvecs = jnp.concatenate([c_ctx[None, :], c, jnp.zeros((8 - 1 - bd, D_MODEL), F32)], axis=0)
    mods = _modulation(cvecs, w_mod, b_mod).reshape(DEPTH, 8 * MOD_CHUNKS, 1, D_MODEL)

    hsum = jnp.asarray(np.kron(np.eye(PROJ_HALF // HEAD_DIM, dtype=np.float32),
                               np.full((HEAD_DIM, HEAD_DIM), 1.0 / HEAD_DIM, np.float32)), BF16)
    rope_tabs = _rope_tables(t_lat)
    tab = _tile_table()

    tm = 1024
    tm_out = 512
    ctx_row = lambda tile: (lambda i: 0)
    lat_row = lambda tile: (lambda i: 1 + (i * tile) // t_lat)

    xc = x_prompt.reshape(n_ctx, D_MODEL)
    xs = x_sample.reshape(n_lat, D_MODEL)
    caches = _zero_cache_buffers(bc, s_ctx)
    lru_states = []
    w_in_bf = w_in
    w_out_bf = w_out.astype(BF16)

    def cached(arr):
        return arr.astype(BF16).reshape(bd * DEPTH, p_ctx, -1)

    c_swa_k, c_swa_v = cached(cache_swa_k), cached(cache_swa_v)
    c_na_k, c_na_v = cached(cache_na_k), cached(cache_na_v)
    c_diff_k, c_diff_v = cached(cache_diff_k), cached(cache_diff_v)
    na_bias = _na_bias(na_rpb, rows)

    for l in range(DEPTH):
        lam_init = _lambda_init(l)
        mods_l = mods[l]
        gain = _tile_gains(qk_gain[l])
        w_gates, b_gates = _lru_gate_weights(lru_wa[l], lru_ba[l], lru_wx[l], lru_bx[l])
        ffn_w = (w_ffn_gate, w_ffn_up, w_ffn_down, l)

        p, f, caches = _in_projection(xc, mods_l, ctx_row(tm), norm_mix[l], w_in_bf, l, tab, gain, hsum,
                                      None, caches, tm=tm, seq=s_ctx)
        mix3 = _ctx_attention(p, swa_sink[l], diff_lambda[l], diff_subln[l], lam_init, nb=bc, t=s_ctx)
        od, st = _lru_mixer(f, conv_w[l], conv_b[l], w_gates, b_gates, lru_L[l], None,
                            nb=bc, t=s_ctx, cpt=4)
        lru_states.append(st)
        xc = _out_projection([mix3, od], w_out_bf, l, xc, mods_l, ctx_row(tm_out), tm=tm_out)
        xc = _ffn(xc, mods_l, ctx_row(tm), norm_ffn[l], *ffn_w, tm=tm, th=512)

        p, f, _ = _in_projection(xs, mods_l, lat_row(tm), norm_mix[l], w_in_bf, l, tab, gain, hsum,
                                 rope_tabs, None, tm=tm, seq=t_lat)
        oa = _swa_latent(p, c_swa_k, c_swa_v, l, swa_sink[l], nb=bd, t=t_lat)
        ob = _na_latent(p, c_na_k, c_na_v, l, na_bias, nb=bd, t=t_lat)
        oc = _diff_latent(p, c_diff_k, c_diff_v, l, diff_lambda[l], diff_subln[l], lam_init, nb=bd, t=t_lat)
        od, _ = _lru_mixer(f, conv_w[l], conv_b[l], w_gates, b_gates, lru_L[l], state_lru[:, l],
                           nb=bd, t=t_lat, cpt=2)
        xs = _out_projection([oa, ob, oc, od], w_out_bf, l, xs, mods_l, lat_row(tm_out), tm=tm_out)
        xs = _ffn(xs, mods_l, lat_row(tm), norm_ffn[l], *ffn_w, tm=tm, th=512)

    swa_k, swa_v, na_k, na_v, diff_k, diff_v = caches
    return (xc.reshape(bc, s_ctx, D_MODEL), xs.reshape(bd, t_lat, D_MODEL),
            swa_k.reshape(bc, DEPTH, s_ctx, SWA_KV_HEADS, HEAD_DIM),
            swa_v.reshape(bc, DEPTH, s_ctx, SWA_KV_HEADS, HEAD_DIM),
            na_k.reshape(bc, DEPTH, s_ctx, NA_HEADS, HEAD_DIM),
            na_v.reshape(bc, DEPTH, s_ctx, NA_HEADS, HEAD_DIM),
            diff_k.reshape(bc, DEPTH, s_ctx, DIFF_HEADS, 2, HEAD_DIM),
            diff_v.reshape(bc, DEPTH, s_ctx, DIFF_HEADS, DIFF_VDIM),
            jnp.stack(lru_states, axis=1))
```
